```python
import math
import jax
import jax.numpy as jnp
from jax import lax
import numpy as np

D_MODEL = 1024
BATCH = 32
SEQ = 256
DEPTH = 4
DEC_BATCH = 2
DEC_SEQ = 2048
PAST_LEN = 256

GRID_W = 64
N_MIXERS = 3
N_GLA = (DEPTH + 2) // 3
N_FN = (DEPTH + 1) // 3
N_HY = DEPTH // 3
EPS = 1e-6

GLA_HEADS = 4
GLA_KEY = D_MODEL // 2
GLA_VAL = D_MODEL
GLA_DK = GLA_KEY // GLA_HEADS
GLA_DV = GLA_VAL // GLA_HEADS
GLA_RANK = 16
GLA_GATE_NORM = 16.0
GLA_CHUNK = 32
GLA_IN = 2 * GLA_KEY + 2 * GLA_VAL + 2 * GLA_RANK

FN_WIDTH = D_MODEL
FN_GROUPS = 4
FN_GC = FN_WIDTH // FN_GROUPS

HY_WIDTH = D_MODEL
HY_ORDER = 2
HY_SHORT = 3
HY_EMB = 33
HY_BANDS = (HY_EMB - 1) // 2
HY_FFN = 64
HY_TARGET = 1e-2
HY_MIN_DECAY = math.log(HY_TARGET) / 1.5
HY_MAX_DECAY = math.log(HY_TARGET) / 0.3

kernel_name = 'hybrid_gla_fnet_hyena_prefix_dit_step'

F32 = jnp.float32


def rms_norm(x, g):
    xf = x.astype(F32)
    y = xf * lax.rsqrt(jnp.mean(xf * xf, axis=-1, keepdims=True) + EPS)
    return (y * g.astype(F32)).astype(x.dtype)


def gla_scan(q, k, v, logg, s0):
    bsz, L, H, _ = q.shape
    dv = v.shape[-1]
    n = L // GLA_CHUNK

    def chunks(a):
        return a.reshape(bsz, n, GLA_CHUNK, *a.shape[2:]).swapaxes(0, 1)

    tri = jnp.tril(jnp.ones((GLA_CHUNK, GLA_CHUNK), dtype=bool))[None, :, :, None, None]

    def step(s, inp):
        qi, ki, vi, gi = inp
        bcum = jnp.cumsum(gi, axis=1)
        o_inter = jnp.einsum('bchk,bhkv->bchv', qi * jnp.exp(bcum), s)
        diff = bcum[:, :, None] - bcum[:, None, :]
        dec = jnp.where(tri, jnp.exp(jnp.where(tri, diff, 0.0)), 0.0)
        att = jnp.einsum('bqhk,bshk,bqshk->bhqs', qi.astype(F32), ki.astype(F32), dec)
        o_intra = jnp.einsum('bhqs,bshv->bqhv', att, vi.astype(F32))
        blast = bcum[:, -1]
        s_new = jnp.exp(blast)[..., None] * s + jnp.einsum(
            'bshk,bshv->bhkv', ki * jnp.exp(blast[:, None] - bcum), vi.astype(F32))
        return s_new, o_inter + o_intra

    s_fin, o = lax.scan(step, s0.astype(F32), (chunks(q), chunks(k), chunks(v), chunks(logg)))
    return o.swapaxes(0, 1).reshape(bsz, L, H, dv), s_fin


def gla_mixer(h, w_in, w_dec, b_dec, onorm_g, w_out, s0_f, s0_b):
    bsz, L, _ = h.shape
    proj = h @ w_in
    o1 = GLA_KEY
    o2 = 2 * GLA_KEY
    o3 = o2 + GLA_VAL
    o4 = o3 + GLA_VAL
    o5 = o4 + GLA_RANK
    q = (proj[..., :o1] * GLA_DK ** -0.5).reshape(bsz, L, GLA_HEADS, GLA_DK)
    k = proj[..., o1:o2].reshape(bsz, L, GLA_HEADS, GLA_DK)
    v = proj[..., o2:o3].reshape(bsz, L, GLA_HEADS, GLA_DV)
    r = proj[..., o3:o4]

    def log_decay(lr, d):
        logit = (lr @ w_dec[d] + b_dec[d]).astype(F32)
        return (jax.nn.log_sigmoid(logit) / GLA_GATE_NORM).reshape(bsz, L, GLA_HEADS, GLA_DK)

    g_f = log_decay(proj[..., o4:o5], 0)
    g_b = log_decay(proj[..., o5:], 1)
    o_f, s_f = gla_scan(q, k, v, g_f, s0_f)
    o_b, s_b = gla_scan(jnp.flip(q, 1), jnp.flip(k, 1), jnp.flip(v, 1), jnp.flip(g_b, 1), s0_b)
    o = o_f + jnp.flip(o_b, 1)
    o = o * lax.rsqrt(jnp.mean(o * o, axis=-1, keepdims=True) + EPS) * onorm_g.astype(F32)
    o = o.reshape(bsz, L, GLA_VAL) * jax.nn.silu(r.astype(F32))
    return o.astype(h.dtype) @ w_out, s_f, s_b


def fnet_mixer(h, w_in, w_out):
    bsz, L, _ = h.shape
    proj = h @ w_in
    u, z = proj[..., :FN_WIDTH], proj[..., FN_WIDTH:]
    ug = u.astype(F32).reshape(bsz, L, FN_GROUPS, FN_GC)
    f = jnp.fft.fft2(ug, axes=(1, 3), norm='ortho').real.reshape(bsz, L, FN_WIDTH)
    y = f * jax.nn.silu(z.astype(F32))
    return y.astype(h.dtype) @ w_out


def hyena_filters(L, w1, b1, w2, b2, w3, b3, w4, freq):
    t = jnp.linspace(0.0, 1.0, L, dtype=F32)[:, None]
    w = 2.0 * math.pi * jnp.arange(L, dtype=F32)[:, None] / L
    f = jnp.linspace(1e-4, HY_BANDS - 1, HY_BANDS, dtype=F32)[None, :]
    zpos = jnp.concatenate([t, jnp.cos(f * w), -jnp.sin(f * w)], axis=-1)
    fr = freq.astype(F32)
    a = jnp.sin(fr * (zpos @ w1 + b1))
    a = jnp.sin(fr * (a @ w2 + b2))
    a = jnp.sin(fr * (a @ w3 + b3))
    hf = (a @ w4).astype(F32).reshape(L, HY_ORDER, 2, HY_WIDTH)
    deltas = jnp.abs(jnp.linspace(HY_MIN_DECAY, HY_MAX_DECAY, HY_WIDTH, dtype=F32))
    hf = hf * jnp.exp(-t * deltas)[:, None, None, :]
    fwd, bwd = hf[:, :, 0], hf[:, :, 1]
    two_sided = jnp.concatenate(
        [fwd, jnp.zeros((1, HY_ORDER, HY_WIDTH), F32), jnp.flip(bwd[1:], axis=0)], axis=0)
    return jnp.fft.rfft(two_sided, axis=0)


def long_conv(u, filt_f):
    L = u.shape[1]
    U = jnp.fft.rfft(u, n=2 * L, axis=1)
    return jnp.fft.irfft(U * filt_f[None], n=2 * L, axis=1)[:, :L]


def hyena_mixer(h, w_in, conv_w, conv_b, w1, b1, w2, b2, w3, b3, w4, freq, d_skip, w_out):
    L = h.shape[1]
    proj = h @ w_in
    u, z = proj[..., :3 * HY_WIDTH], proj[..., 3 * HY_WIDTH:]
    up = jnp.pad(u, ((0, 0), (1, 1), (0, 0)))
    u = up[:, :-2] * conv_w[0] + up[:, 1:-1] * conv_w[1] + up[:, 2:] * conv_w[2] + conv_b
    x1, x2, v = jnp.split(u.astype(F32), 3, axis=-1)
    filt_f = hyena_filters(L, w1, b1, w2, b2, w3, b3, w4, freq)
    y = v
    for n, gate in enumerate((x1, x2)):
        y = gate * (long_conv(y, filt_f[:, n]) + y * d_skip[n].astype(F32))
    y = y * jax.nn.silu(z.astype(F32))
    return y.astype(h.dtype) @ w_out


def setup_inputs(seed: int = 0) -> dict:
    key = jax.random.key(seed)
    ks = iter(jax.random.split(key, 40))

    def nrm(shape, scale):
        return jax.random.normal(next(ks), shape, F32) * scale

    D = D_MODEL
    return {
        'x_prompt': nrm((BATCH, SEQ, D), 1.0),
        'x_sample': nrm((DEC_BATCH, DEC_SEQ, D), 1.0),
        'state_gla': nrm((DEC_BATCH, N_GLA, 2, GLA_HEADS, GLA_DK, GLA_DV), 1.0),
        'c': nrm((DEC_BATCH, D), 1.0),
        'c_ctx': nrm((D,), 1.0),
        'mod_w': nrm((DEPTH, D, 3 * D), 0.5 * D ** -0.5),
        'mod_b': nrm((DEPTH, 3 * D), 0.01),
        'norm_g': 1.0 + nrm((DEPTH, D), 0.05),
        'final_norm_g': 1.0 + nrm((D,), 0.05),
        'gla_w_in': nrm((N_GLA, D, GLA_IN), D ** -0.5),
        'gla_w_dec': nrm((N_GLA, 2, GLA_RANK, GLA_KEY), GLA_RANK ** -0.5),
        'gla_b_dec': nrm((N_GLA, 2, GLA_KEY), 0.5),
        'gla_onorm_g': 1.0 + nrm((N_GLA, GLA_DV), 0.05),
        'gla_w_out': nrm((N_GLA, GLA_VAL, D), GLA_VAL ** -0.5),
        'fn_w_in': nrm((N_FN, D, 2 * FN_WIDTH), D ** -0.5),
        'fn_w_out': nrm((N_FN, FN_WIDTH, D), FN_WIDTH ** -0.5),
        'hy_w_in': nrm((N_HY, D, 4 * HY_WIDTH), D ** -0.5),
        'hy_conv_w': nrm((N_HY, HY_SHORT, 3 * HY_WIDTH), HY_SHORT ** -0.5),
        'hy_conv_b': nrm((N_HY, 3 * HY_WIDTH), 0.02),
        'hy_ffn_w1': nrm((N_HY, HY_EMB, HY_FFN), HY_EMB ** -0.5),
        'hy_ffn_b1': nrm((N_HY, HY_FFN), 0.1),
        'hy_ffn_w2': nrm((N_HY, HY_FFN, HY_FFN), HY_FFN ** -0.5),
        'hy_ffn_b2': nrm((N_HY, HY_FFN), 0.1),
        'hy_ffn_w3': nrm((N_HY, HY_FFN, HY_FFN), HY_FFN ** -0.5),
        'hy_ffn_b3': nrm((N_HY, HY_FFN), 0.1),
        'hy_ffn_w4': nrm((N_HY, HY_FFN, HY_ORDER * 2 * HY_WIDTH), 0.1 * HY_FFN ** -0.5),
        'hy_freq': 1.0 + nrm((N_HY, HY_FFN), 0.1),
        'hy_d': nrm((N_HY, HY_ORDER, HY_WIDTH), 0.5),
        'hy_w_out': nrm((N_HY, HY_WIDTH, D), HY_WIDTH ** -0.5),
    }


def reference(x_prompt, x_sample, state_gla, c, c_ctx, mod_w, mod_b, norm_g, final_norm_g,
              gla_w_in, gla_w_dec, gla_b_dec, gla_onorm_g, gla_w_out,
              fn_w_in, fn_w_out,
              hy_w_in, hy_conv_w, hy_conv_b, hy_ffn_w1, hy_ffn_b1, hy_ffn_w2, hy_ffn_b2,
              hy_ffn_w3, hy_ffn_b3, hy_ffn_w4, hy_freq, hy_d, hy_w_out):
    ctx_mod = jnp.einsum('d,lde->le', jax.nn.silu(c_ctx), mod_w) + mod_b
    lat_mod = jnp.einsum('bd,lde->lbe', jax.nn.silu(c), mod_w) + mod_b[:, None]
    xp, xs = x_prompt, x_sample
    new_states = []
    for i in range(DEPTH):
        kind, j = i % N_MIXERS, i // N_MIXERS
        sh_c, sc_c, g_c = jnp.split(ctx_mod[i], 3, axis=-1)
        sh_s, sc_s, g_s = jnp.split(lat_mod[i][:, None, :], 3, axis=-1)
        hp = rms_norm(xp, norm_g[i]) * (1.0 + sc_c) + sh_c
        hs = rms_norm(xs, norm_g[i]) * (1.0 + sc_s) + sh_s
        if kind == 0:
            zero = jnp.zeros((xp.shape[0], GLA_HEADS, GLA_DK, GLA_DV), F32)
            prm = (gla_w_in[j], gla_w_dec[j], gla_b_dec[j], gla_onorm_g[j], gla_w_out[j])
            op, s_f, s_b = gla_mixer(hp, *prm, zero, zero)
            os_, _, _ = gla_mixer(hs, *prm, state_gla[:, j, 0], state_gla[:, j, 1])
            new_states.append(jnp.stack([s_f, s_b], axis=1))
        elif kind == 1:
            op = fnet_mixer(hp, fn_w_in[j], fn_w_out[j])
            os_ = fnet_mixer(hs, fn_w_in[j], fn_w_out[j])
        else:
            prm = (hy_w_in[j], hy_conv_w[j], hy_conv_b[j], hy_ffn_w1[j], hy_ffn_b1[j],
                   hy_ffn_w2[j], hy_ffn_b2[j], hy_ffn_w3[j], hy_ffn_b3[j], hy_ffn_w4[j],
                   hy_freq[j], hy_d[j], hy_w_out[j])
            op = hyena_mixer(hp, *prm)
            os_ = hyena_mixer(hs, *prm)
        xp = xp + g_c * op
        xs = xs + g_s * os_
    y_prompt = rms_norm(xp, final_norm_g)
    y_sample = rms_norm(xs, final_norm_g)
    new_state_gla = jnp.stack(new_states, axis=1).astype(x_prompt.dtype)
    return (y_prompt, y_sample, new_state_gla)
```

```python
import functools
import math

import jax
import jax.numpy as jnp
import numpy as np
from jax import lax
from jax.experimental import pallas as pl
from jax.experimental.pallas import tpu as pltpu

F32 = jnp.float32
BF16 = jnp.bfloat16

EPS = 1e-6
N_MIXERS = 3
GLA_HEADS = 4
GLA_RANK = 16
GLA_GATE_NORM = 16.0
FN_GROUPS = 4
HY_ORDER = 2
HY_TARGET = 1e-2
HY_MIN_DECAY = math.log(HY_TARGET) / 1.5
HY_MAX_DECAY = math.log(HY_TARGET) / 0.3

V7X_VMEM_BYTES = 64 * 1024 * 1024
VMEM_LIMIT_BYTES = V7X_VMEM_BYTES - 8 * 1024 * 1024
MOD_ROWS = 8

_NT = (((1,), (1,)), ((), ()))
_TN = (((0,), (0,)), ((), ()))


def _params(*sem):
    return pltpu.CompilerParams(dimension_semantics=sem, vmem_limit_bytes=VMEM_LIMIT_BYTES)


def _dot(a, b):
    return jnp.dot(a, b, preferred_element_type=F32)


def _split3(x):
    hi = x.astype(BF16)
    r1 = x - hi.astype(F32)
    mid = r1.astype(BF16)
    lo = (r1 - mid.astype(F32)).astype(BF16)
    return hi, mid, lo


def _dot3(a, b):
    ah = a.astype(BF16)
    al = (a - ah.astype(F32)).astype(BF16)
    bh = b.astype(BF16)
    bl = (b - bh.astype(F32)).astype(BF16)
    return _dot(ah, bh) + (_dot(ah, bl) + _dot(al, bh))


def _dot_exact_lhs(a_bf16, b):
    hi, mid, lo = _split3(b)
    return _dot(a_bf16, hi) + (_dot(a_bf16, mid) + _dot(a_bf16, lo))


def _trig_bf16(ang):
    return (jnp.asarray(np.cos(ang), F32).astype(BF16), jnp.asarray(np.sin(ang), F32).astype(BF16))


def _silu(x):
    return x / (1.0 + jnp.exp(-x))


def _log_sigmoid(x):
    return jnp.minimum(x, 0.0) - jnp.log(1.0 + jnp.exp(-jnp.abs(x)))


def _mod_body(c_ref, w_ref, b_ref, o_ref):
    o_ref[...] = _dot3(_silu(c_ref[...]), w_ref[...]) + b_ref[...]


def _modulation(cvec, mod_w, mod_b):
    depth, d, e = mod_w.shape
    tn = d
    return pl.pallas_call(
        _mod_body,
        grid=(depth, e // tn),
        in_specs=[
            pl.BlockSpec((MOD_ROWS, d), lambda l, j: (0, 0)),
            pl.BlockSpec((None, d, tn), lambda l, j: (l, 0, j)),
            pl.BlockSpec((None, 1, tn), lambda l, j: (l, 0, j)),
        ],
        out_specs=pl.BlockSpec((None, MOD_ROWS, tn), lambda l, j: (l, 0, j)),
        out_shape=jax.ShapeDtypeStruct((depth, MOD_ROWS, e), F32),
        compiler_params=_params("parallel", "parallel"),
        name="modulation",
    )(cvec, mod_w, mod_b.reshape(depth, 1, e))


def _inproj_body(x_ref, g_ref, mod_ref, *refs):
    n_w = len(refs) // 2
    x = x_ref[...]
    d = x.shape[-1]
    h = x * lax.rsqrt(jnp.mean(x * x, axis=-1, keepdims=True) + EPS) * g_ref[...]
    mod = mod_ref[...]
    hb = (h * (1.0 + mod[:, d:2 * d]) + mod[:, :d]).astype(BF16)
    for w_ref, o_ref in zip(refs[:n_w], refs[n_w:]):
        o_ref[...] = _dot(hb, w_ref[...])


def _inproj(x, norm_g3, mod4, layer, mod_row, weights, tm, name):
    t, d = x.shape
    w_specs = [pl.BlockSpec(bs, functools.partial(lambda idx, i: idx, idx)) for _, bs, idx in weights]
    widths = [bs[-1] for _, bs, _ in weights]
    return pl.pallas_call(
        _inproj_body,
        grid=(t // tm,),
        in_specs=[
            pl.BlockSpec((tm, d), lambda i: (i, 0)),
            pl.BlockSpec((None, 1, d), lambda i: (layer, 0, 0)),
            pl.BlockSpec((None, None, 1, mod4.shape[-1]), lambda i: (layer, mod_row(i), 0, 0)),
        ] + w_specs,
        out_specs=[pl.BlockSpec((tm, n), lambda i: (i, 0)) for n in widths],
        out_shape=[jax.ShapeDtypeStruct((t, n), F32) for n in widths],
        compiler_params=_params("parallel"),
        name=name,
    )(x, norm_g3, mod4, *[w for w, _, _ in weights])


def _outproj_body(y_ref, w_ref, x_ref, mod_ref, *rest, final):
    o = _dot(y_ref[...], w_ref[...])
    d = o.shape[-1]
    xn = x_ref[...] + mod_ref[...][:, 2 * d:] * o
    if final:
        fg_ref, out_ref = rest
        xn = xn * lax.rsqrt(jnp.mean(xn * xn, axis=-1, keepdims=True) + EPS) * fg_ref[...]
    else:
        (out_ref,) = rest
    out_ref[...] = xn


def _outproj(y, w_out, j, x, mod4, layer, mod_row, final_g, tm, name):
    t, d = x.shape
    final = final_g is not None
    in_specs = [
        pl.BlockSpec((tm, y.shape[-1]), lambda i: (i, 0)),
        pl.BlockSpec((None,) + w_out.shape[1:], lambda i: (j, 0, 0)),
        pl.BlockSpec((tm, d), lambda i: (i, 0)),
        pl.BlockSpec((None, None, 1, mod4.shape[-1]), lambda i: (layer, mod_row(i), 0, 0)),
    ]
    args = [y, w_out, x, mod4]
    if final:
        in_specs.append(pl.BlockSpec((1, d), lambda i: (0, 0)))
        args.append(final_g.reshape(1, d))
    return pl.pallas_call(
        functools.partial(_outproj_body, final=final),
        grid=(t // tm,),
        in_specs=in_specs,
        out_specs=pl.BlockSpec((tm, d), lambda i: (i, 0)),
        out_shape=jax.ShapeDtypeStruct((t, d), F32),
        compiler_params=_params("parallel"),
        name=name,
    )(*args)


def _gla_body(q_ref, k_ref, v_ref, r_ref, lr_ref, wdec_ref, bdec_ref, og_ref, *rest,
              seq, chunk, has_s0, want_state):
    rest = list(rest)
    s0_ref = rest.pop(0) if has_s0 else None
    y_ref = rest.pop(0)
    sfin_ref = rest.pop(0) if want_state else None
    of_ref, ob_ref, st_ref = rest
    n_chunks = seq // chunk
    dk = q_ref.shape[-1]
    dv = v_ref.shape[-1]
    scale = dk ** -0.5

    row = lax.broadcasted_iota(jnp.int32, (chunk, chunk), 0)
    col = lax.broadcasted_iota(jnp.int32, (chunk, chunk), 1)
    masks = (row >= col, row <= col)
    tris = tuple(jnp.where(m, 1.0, 0.0).astype(BF16) for m in masks)

    for d in range(2):
        if has_s0:
            st_ref[d] = s0_ref[d].T
        else:
            st_ref[d] = jnp.zeros((dv, dk), F32)

    def one_chunk(i, d):
        sl = pl.ds(pl.multiple_of(i * chunk, chunk), chunk)
        q = q_ref[sl, :] * scale
        k = k_ref[sl, :]
        vb = v_ref[sl, :].astype(BF16)
        lr = lr_ref[sl, :][:, d * GLA_RANK:(d + 1) * GLA_RANK]
        logit = _dot3(lr, wdec_ref[d]) + bdec_ref[d]
        g = _log_sigmoid(logit) * (1.0 / GLA_GATE_NORM)
        cum = _dot_exact_lhs(tris[d], g)
        blast = cum[chunk - 1:chunk] if d == 0 else cum[0:1]
        qt = (q * jnp.exp(cum)).astype(BF16)
        kt = (k * jnp.exp(-cum)).astype(BF16)
        kh = (k * jnp.exp(blast - cum)).astype(BF16)
        att = lax.dot_general(qt, kt, _NT, preferred_element_type=F32)
        att = jnp.where(masks[d], att, 0.0).astype(BF16)
        st = st_ref[d]
        o = _dot(att, vb) + lax.dot_general(qt, st.astype(BF16), _NT, preferred_element_type=F32)
        (of_ref, ob_ref)[d][sl, :] = o
        st_ref[d] = jnp.exp(blast) * st + lax.dot_general(vb, kh, _TN, preferred_element_type=F32)

    def scan_step(i, carry):
        one_chunk(i, 0)
        one_chunk(n_chunks - 1 - i, 1)
        return carry

    lax.fori_loop(0, n_chunks, scan_step, 0)

    rows = min(seq, 256)

    def finish(jb, carry):
        sl = pl.ds(pl.multiple_of(jb * rows, rows), rows)
        o = of_ref[sl, :] + ob_ref[sl, :]
        o = o * lax.rsqrt(jnp.mean(o * o, axis=-1, keepdims=True) + EPS) * og_ref[...]
        y_ref[sl, :] = (o * _silu(r_ref[sl, :])).astype(BF16)
        return carry

    lax.fori_loop(0, seq // rows, finish, 0)
    if want_state:
        for d in range(2):
            sfin_ref[d] = st_ref[d].T


def _gla_mixer(proj, lr, w_dec, b_dec, onorm_g, s0, j, n_seq, seq, chunk, want_state, name):
    t = proj.shape[0]
    h = GLA_HEADS
    key = w_dec.shape[-1]
    dk = key // h
    val = (proj.shape[1] - 2 * key) // 2
    dv = val // h
    has_s0 = s0 is not None
    in_specs = [
        pl.BlockSpec((seq, dk), lambda b, hh: (b, hh)),
        pl.BlockSpec((seq, dk), lambda b, hh: (b, h + hh)),
        pl.BlockSpec((seq, dv), lambda b, hh: (b, (2 * key) // dv + hh)),
        pl.BlockSpec((seq, dv), lambda b, hh: (b, (2 * key + val) // dv + hh)),
        pl.BlockSpec((seq, 2 * GLA_RANK), lambda b, hh: (b, 0)),
        pl.BlockSpec((None, 2, GLA_RANK, dk), lambda b, hh: (j, 0, 0, hh)),
        pl.BlockSpec((None, 2, 1, dk), lambda b, hh: (j, 0, 0, hh)),
        pl.BlockSpec((None, 1, dv), lambda b, hh: (j, 0, 0)),
    ]
    args = [proj, proj, proj, proj, lr, w_dec, b_dec.reshape(b_dec.shape[0], 2, 1, key),
            onorm_g.reshape(onorm_g.shape[0], 1, dv)]
    if has_s0:
        in_specs.append(pl.BlockSpec((None, None, 2, None, dk, dv), lambda b, hh: (b, j, 0, hh, 0, 0)))
        args.append(s0)
    out_specs = [pl.BlockSpec((seq, dv), lambda b, hh: (b, hh))]
    out_shape = [jax.ShapeDtypeStruct((t, val), BF16)]
    if want_state:
        out_specs.append(pl.BlockSpec((None, 2, None, dk, dv), lambda b, hh: (b, 0, hh, 0, 0)))
        out_shape.append(jax.ShapeDtypeStruct((n_seq, 2, h, dk, dv), F32))
    return pl.pallas_call(
        functools.partial(_gla_body, seq=seq, chunk=chunk, has_s0=has_s0, want_state=want_state),
        grid=(n_seq, h),
        in_specs=in_specs,
        out_specs=out_specs,
        out_shape=out_shape,
        scratch_shapes=[pltpu.VMEM((seq, dv), F32), pltpu.VMEM((seq, dv), F32),
                        pltpu.VMEM((2, dv, dk), F32)],
        compiler_params=_params("parallel", "parallel"),
        name=name,
    )(*args)


def _dft_tables(n):
    f = np.arange(n, dtype=np.int64)
    ang = (np.outer(f, f) % n).astype(np.float64) * (2.0 * math.pi / n)
    return _trig_bf16(ang)


def _fnet_body(x_ref, z_ref, cc_ref, sc_ref, cl_ref, sl_ref, y_ref, *, seq, n_seq):
    xb = x_ref[...].astype(BF16)
    a = _dot(xb, cc_ref[...]).astype(BF16)
    bm = _dot(xb, sc_ref[...]).astype(BF16)
    inv = 1.0 / math.sqrt(seq * x_ref.shape[-1])
    for i in range(n_seq):
        rows = slice(i * seq, (i + 1) * seq)
        f = _dot(cl_ref[...], a[rows]) - _dot(sl_ref[...], bm[rows])
        y_ref[rows, :] = (f * inv * _silu(z_ref[rows, :])).astype(BF16)


def _fnet_mixer(proj, n_seq_total, seq, seqs_per_step, name):
    t = proj.shape[0]
    width = proj.shape[1] // 2
    gc = width // FN_GROUPS
    cc, sc = _dft_tables(gc)
    cl, sl = _dft_tables(seq)
    rows = seq * seqs_per_step
    const = pl.Buffered(1)
    return pl.pallas_call(
        functools.partial(_fnet_body, seq=seq, n_seq=seqs_per_step),
        grid=(n_seq_total // seqs_per_step, FN_GROUPS),
        in_specs=[
            pl.BlockSpec((rows, gc), lambda i, g: (i, g)),
            pl.BlockSpec((rows, gc), lambda i, g: (i, FN_GROUPS + g)),
            pl.BlockSpec((gc, gc), lambda i, g: (0, 0), pipeline_mode=const),
            pl.BlockSpec((gc, gc), lambda i, g: (0, 0), pipeline_mode=const),
            pl.BlockSpec((seq, seq), lambda i, g: (0, 0), pipeline_mode=const),
            pl.BlockSpec((seq, seq), lambda i, g: (0, 0), pipeline_mode=const),
        ],
        out_specs=pl.BlockSpec((rows, gc), lambda i, g: (i, g)),
        out_shape=jax.ShapeDtypeStruct((t, width), BF16),
        compiler_params=_params("parallel", "parallel"),
        name=name,
    )(proj, proj, cc, sc, cl, sl)


def _signal_tables(n):
    o = 2 * np.arange(n, dtype=np.int64) + 1
    ang = (np.outer(o, o) % (8 * n)).astype(np.float64) * (math.pi / (4 * n))
    return _trig_bf16(ang)


def _filter_tables(n):
    f = 2 * np.arange(n, dtype=np.int64) + 1
    o = np.arange(n, dtype=np.int64)
    ang = (np.outer(f, o) % (4 * n)).astype(np.float64) * (math.pi / (2 * n))
    return _trig_bf16(ang)


def _hyena_positions(seq, bands, pad_to):
    t = np.linspace(0.0, 1.0, seq)[:, None]
    w = 2.0 * math.pi * np.arange(seq)[:, None] / seq
    f = np.linspace(1e-4, bands - 1, bands)[None, :]
    zpos = np.concatenate([t, np.cos(f * w), -np.sin(f * w)], axis=-1)
    zpos = np.pad(zpos, ((0, 0), (0, pad_to - zpos.shape[1])))
    return jnp.asarray(zpos, F32), jnp.asarray(t, F32)


def _hyfilt_body(zpos_ref, w1_ref, b1_ref, w2_ref, b2_ref, w3_ref, b3_ref, fr_ref, w4f_ref,
                 w4b_ref, t_ref, del_ref, ch_ref, sh_ref, hre_ref, him_ref, a3_ref):
    @pl.when(pl.program_id(0) == 0)
    def _():
        fr = fr_ref[...]
        a = jnp.sin(fr * (_dot3(zpos_ref[...], w1_ref[...]) + b1_ref[...]))
        a = jnp.sin(fr * (_dot3(a, w2_ref[...]) + b2_ref[...]))
        a3_ref[...] = jnp.sin(fr * (_dot3(a, w3_ref[...]) + b3_ref[...]))

    a3 = a3_ref[...]
    dec = jnp.exp(-t_ref[...] * del_ref[...])
    fwd = _dot3(a3, w4f_ref[...]) * dec
    bwd = _dot3(a3, w4b_ref[...]) * dec
    first = lax.broadcasted_iota(jnp.int32, bwd.shape, 0) == 0
    bwd = jnp.where(first, 0.0, bwd)
    hre_ref[...] = _dot(ch_ref[...], (fwd + bwd).astype(BF16))
    him_ref[...] = _dot(sh_ref[...], (bwd - fwd).astype(BF16))


def _hyena_filters(seq, j, emb, w1p, b1, w2, b2, w3, b3, w4, freq, ce, name):
    ffn = w2.shape[-1]
    width = w4.shape[-1] // (2 * HY_ORDER)
    emb_pad = w1p.shape[1]
    zpos, tcol = _hyena_positions(seq, (emb - 1) // 2, emb_pad)
    deltas = jnp.asarray(np.abs(np.linspace(HY_MIN_DECAY, HY_MAX_DECAY, width, dtype=np.float32))[None, :])
    ch, sh = _filter_tables(seq)
    nb = width // ce
    const = pl.Buffered(1)
    vec = lambda: pl.BlockSpec((None, 1, ffn), lambda c: (j, 0, 0))
    return pl.pallas_call(
        _hyfilt_body,
        grid=(HY_ORDER * nb,),
        in_specs=[
            pl.BlockSpec((seq, emb_pad), lambda c: (0, 0)),
            pl.BlockSpec((None, emb_pad, ffn), lambda c: (j, 0, 0)),
            vec(),
            pl.BlockSpec((None, ffn, ffn), lambda c: (j, 0, 0)),
            vec(),
            pl.BlockSpec((None, ffn, ffn), lambda c: (j, 0, 0)),
            vec(),
            vec(),
            pl.BlockSpec((None, ffn, ce), lambda c: (j, 0, (c // nb) * 2 * nb + c % nb)),
            pl.BlockSpec((None, ffn, ce), lambda c: (j, 0, (c // nb) * 2 * nb + nb + c % nb)),
            pl.BlockSpec((seq, 1), lambda c: (0, 0)),
            pl.BlockSpec((1, ce), lambda c: (0, c % nb)),
            pl.BlockSpec((seq, seq), lambda c: (0, 0), pipeline_mode=const),
            pl.BlockSpec((seq, seq), lambda c: (0, 0), pipeline_mode=const),
        ],
        out_specs=[pl.BlockSpec((seq, ce), lambda c: (0, c)), pl.BlockSpec((seq, ce), lambda c: (0, c))],
        out_shape=[jax.ShapeDtypeStruct((seq, HY_ORDER * width), F32)] * 2,
        scratch_shapes=[pltpu.VMEM((seq, ffn), F32)],
        compiler_params=_params("arbitrary"),
        name=name,
    )(zpos, w1p, b1, w2, b2, w3, b3, freq, w4, w4, tcol, deltas, ch, sh)


def _hyena_body(x1_ref, x2_ref, v_ref, z_ref, cw1_ref, cw2_ref, cwv_ref, cb1_ref, cb2_ref, cbv_ref,
                hre0_ref, him0_ref, hre1_ref, him1_ref, d_ref, cq_ref, sq_ref, y_ref,
                cur_ref, cb_ref, yr_ref, yi_ref, *, seq, rows):
    n_rb = seq // rows
    inv_len = 1.0 / seq
    rid = lax.broadcasted_iota(jnp.int32, (rows, 1), 0)

    def block(rb):
        return pl.ds(pl.multiple_of(rb * rows, rows), rows)

    def short_conv(u_ref, w_ref, b_ref, rb):
        u = u_ref[block(rb), :]
        before = u_ref[pl.ds(jnp.maximum(rb * rows - 1, 0), 1), :]
        after = u_ref[pl.ds(jnp.minimum((rb + 1) * rows, seq - 1), 1), :]
        before = jnp.where(rb == 0, 0.0, before)
        after = jnp.where(rb == n_rb - 1, 0.0, after)
        prev = jnp.where(rid == 0, before, pltpu.roll(u, 1, 0))
        nxt = jnp.where(rid == rows - 1, after, pltpu.roll(u, rows - 1, 0))
        return prev * w_ref[0:1] + u * w_ref[1:2] + nxt * w_ref[2:3] + b_ref[...]

    def long_conv_gate(x_ref, cw_ref, cbias_ref, hre_ref, him_ref, dn):
        def fwd(rb, carry):
            sl = block(rb)
            ur = _dot(cq_ref[sl, :], cb_ref[...])
            ui = -_dot(sq_ref[sl, :], cb_ref[...])
            hr = hre_ref[sl, :]
            hi = him_ref[sl, :]
            yr_ref[sl, :] = (ur * hr - ui * hi).astype(BF16)
            yi_ref[sl, :] = (ur * hi + ui * hr).astype(BF16)
            return carry

        lax.fori_loop(0, n_rb, fwd, 0)

        def inv(rb, carry):
            sl = block(rb)
            c = (_dot(cq_ref[sl, :], yr_ref[...]) - _dot(sq_ref[sl, :], yi_ref[...])) * inv_len
            new = short_conv(x_ref, cw_ref, cbias_ref, rb) * (c + cur_ref[sl, :] * dn)
            cur_ref[sl, :] = new
            cb_ref[sl, :] = new.astype(BF16)
            return carry

        lax.fori_loop(0, n_rb, inv, 0)

    def load_v(rb, carry):
        v = short_conv(v_ref, cwv_ref, cbv_ref, rb)
        cur_ref[block(rb), :] = v
        cb_ref[block(rb), :] = v.astype(BF16)
        return carry

    lax.fori_loop(0, n_rb, load_v, 0)
    long_conv_gate(x1_ref, cw1_ref, cb1_ref, hre0_ref, him0_ref, d_ref[0:1])
    long_conv_gate(x2_ref, cw2_ref, cb2_ref, hre1_ref, him1_ref, d_ref[1:2])

    def store(rb, carry):
        sl = block(rb)
        y_ref[sl, :] = (cur_ref[sl, :] * _silu(z_ref[sl, :])).astype(BF16)
        return carry

    lax.fori_loop(0, n_rb, store, 0)


def _hyena_mixer(proj, conv_w, conv_b, hre, him, d_skip, j, n_seq, seq, ce, name):
    t = proj.shape[0]
    width = proj.shape[1] // 4
    nb = width // ce
    cq, sq = _signal_tables(seq)
    rows = min(seq, 256)
    const = pl.Buffered(1)
    seg = lambda s: pl.BlockSpec((seq, ce), lambda e, b: (b, s * nb + e))
    cw = lambda s: pl.BlockSpec((None, conv_w.shape[1], ce), lambda e, b: (j, 0, s * nb + e))
    cb = lambda s: pl.BlockSpec((None, 1, ce), lambda e, b: (j, 0, s * nb + e))
    hs = lambda n: pl.BlockSpec((seq, ce), lambda e, b: (0, n * nb + e), pipeline_mode=const)
    return pl.pallas_call(
        functools.partial(_hyena_body, seq=seq, rows=rows),
        grid=(nb, n_seq),
        in_specs=[seg(0), seg(1), seg(2), seg(3), cw(0), cw(1), cw(2), cb(0), cb(1), cb(2),
                  hs(0), hs(0), hs(1), hs(1),
                  pl.BlockSpec((None, HY_ORDER, ce), lambda e, b: (j, 0, e)),
                  pl.BlockSpec((seq, seq), lambda e, b: (0, 0), pipeline_mode=const),
                  pl.BlockSpec((seq, seq), lambda e, b: (0, 0), pipeline_mode=const)],
        out_specs=pl.BlockSpec((seq, ce), lambda e, b: (b, e)),
        out_shape=jax.ShapeDtypeStruct((t, width), BF16),
        scratch_shapes=[pltpu.VMEM((seq, ce), F32), pltpu.VMEM((seq, ce), BF16),
                        pltpu.VMEM((seq, ce), BF16), pltpu.VMEM((seq, ce), BF16)],
        compiler_params=_params("parallel", "parallel"),
        name=name,
    )(proj, proj, proj, proj, conv_w, conv_w, conv_w,
      conv_b.reshape(conv_b.shape[0], 1, -1), conv_b.reshape(conv_b.shape[0], 1, -1),
      conv_b.reshape(conv_b.shape[0], 1, -1), hre, him, hre, him, d_skip, cq, sq)


def kernel(x_prompt, x_sample, state_gla, c, c_ctx, mod_w, mod_b, norm_g, final_norm_g, gla_w_in, gla_w_dec, gla_b_dec, gla_onorm_g, gla_w_out, fn_w_in, fn_w_out, hy_w_in, hy_conv_w, hy_conv_b, hy_ffn_w1, hy_ffn_b1, hy_ffn_w2, hy_ffn_b2, hy_ffn_w3, hy_ffn_b3, hy_ffn_w4, hy_freq, hy_d, hy_w_out):
    n_p, l_p, d = x_prompt.shape
    n_s, l_s, _ = x_sample.shape
    depth = mod_w.shape[0]
    key = gla_w_dec.shape[-1]
    n_main = gla_w_in.shape[-1] - 2 * GLA_RANK

    cvec = jnp.concatenate([c_ctx[None], c, jnp.zeros((MOD_ROWS - 1 - n_s, d), F32)], axis=0)
    mod4 = _modulation(cvec, mod_w, mod_b).reshape(depth, MOD_ROWS, 1, 3 * d)
    norm_g3 = norm_g.reshape(depth, 1, d)

    gla_w_main = gla_w_in.astype(BF16)
    gla_w_lr = gla_w_in[:, :, n_main:].astype(BF16)
    gla_w_out_b = gla_w_out.astype(BF16)
    fn_w_in_b = fn_w_in.astype(BF16)
    fn_w_out_b = fn_w_out.astype(BF16)
    hy_w_in_b = hy_w_in.astype(BF16)
    hy_w_out_b = hy_w_out.astype(BF16)
    emb = hy_ffn_w1.shape[1]
    emb_pad = -(-emb // 128) * 128
    hy_w1p = jnp.pad(hy_ffn_w1, ((0, 0), (0, emb_pad - emb), (0, 0)))
    ffn = hy_ffn_w2.shape[-1]
    hy_vecs = [a.reshape(a.shape[0], 1, ffn) for a in (hy_ffn_b1, hy_ffn_b2, hy_ffn_b3, hy_freq)]

    tm = 512
    tiles_per_sample = l_s // tm
    streams = {
        "p": dict(x=x_prompt.reshape(n_p * l_p, d), n=n_p, l=l_p, row=lambda i: 0),
        "s": dict(x=x_sample.reshape(n_s * l_s, d), n=n_s, l=l_s, row=lambda i: 1 + i // tiles_per_sample),
    }
    new_states = []
    for i in range(depth):
        kind, j = i % N_MIXERS, i // N_MIXERS
        final_g = final_norm_g if i == depth - 1 else None
        if kind == 2:
            filt = {}
            for tag, st in streams.items():
                ce = min(d, 256 if st["l"] > 256 else d)
                filt[tag] = _hyena_filters(st["l"], j, emb, hy_w1p, hy_vecs[0], hy_ffn_w2, hy_vecs[1], hy_ffn_w3,
                                           hy_vecs[2], hy_ffn_w4, hy_vecs[3], ce, f"hyfilt_{tag}{i}")
        for tag, st in streams.items():
            x, n_seq, seq, row = st["x"], st["n"], st["l"], st["row"]
            if kind == 0:
                proj, lr = _inproj(
                    x, norm_g3, mod4, i, row,
                    [(gla_w_main, (None, d, n_main), (j, 0, 0)), (gla_w_lr, (None, d, 2 * GLA_RANK), (j, 0, 0))],
                    tm, f"inproj_{tag}{i}")
                is_ctx = tag == "p"
                res = _gla_mixer(proj, lr, gla_w_dec, gla_b_dec, gla_onorm_g,
                                 None if is_ctx else state_gla, j, n_seq, seq, 64, is_ctx, f"gla_{tag}{i}")
                y = res[0]
                if is_ctx:
                    new_states.append(res[1])
                w_out = gla_w_out_b
            elif kind == 1:
                (proj,) = _inproj(x, norm_g3, mod4, i, row,
                                  [(fn_w_in_b, (None,) + fn_w_in.shape[1:], (j, 0, 0))], tm, f"inproj_{tag}{i}")
                y = _fnet_mixer(proj, n_seq, seq, 4 if seq <= 256 else 1, f"fnet_{tag}{i}")
                w_out = fn_w_out_b
            else:
                (proj,) = _inproj(x, norm_g3, mod4, i, row,
                                  [(hy_w_in_b, (None,) + hy_w_in.shape[1:], (j, 0, 0))], tm, f"inproj_{tag}{i}")
                ce = min(d, 256 if seq > 256 else d)
                hre, him = filt[tag]
                y = _hyena_mixer(proj, hy_conv_w, hy_conv_b, hre, him, hy_d, j, n_seq, seq, ce, f"hyena_{tag}{i}")
                w_out = hy_w_out_b
            st["x"] = _outproj(y, w_out, j, x, mod4, i, row, final_g, tm, f"outproj_{tag}{i}")
    y_prompt = streams["p"]["x"].reshape(n_p, l_p, d)
    y_sample = streams["s"]["x"].reshape(n_s, l_s, d)
    new_state_gla = jnp.stack(new_states, axis=1).astype(x_prompt.dtype)
    return (y_prompt, y_sample, new_state_gla)
```

```python
import functools
import math

import jax
import jax.numpy as jnp
import numpy as np
from jax import lax
from jax.experimental import pallas as pl
from jax.experimental.pallas import tpu as pltpu

F32 = jnp.float32
BF16 = jnp.bfloat16

EPS = 1e-6
N_MIXERS = 3
GLA_HEADS = 4
GLA_RANK = 16
GLA_GATE_NORM = 16.0
FN_GROUPS = 4
HY_ORDER = 2
HY_TARGET = 1e-2
HY_MIN_DECAY = math.log(HY_TARGET) / 1.5
HY_MAX_DECAY = math.log(HY_TARGET) / 0.3

V7X_VMEM_BYTES = 64 * 1024 * 1024
VMEM_LIMIT_BYTES = V7X_VMEM_BYTES - 8 * 1024 * 1024
MOD_ROWS = 8

_NT = (((1,), (1,)), ((), ()))
_TN = (((0,), (0,)), ((), ()))


def _params(*sem):
    return pltpu.CompilerParams(dimension_semantics=sem, vmem_limit_bytes=VMEM_LIMIT_BYTES)


def _dot(a, b):
    return jnp.dot(a, b, preferred_element_type=F32)


def _split3(x):
    hi = x.astype(BF16)
    r1 = x - hi.astype(F32)
    mid = r1.astype(BF16)
    lo = (r1 - mid.astype(F32)).astype(BF16)
    return hi, mid, lo


def _dot3(a, b):
    ah = a.astype(BF16)
    al = (a - ah.astype(F32)).astype(BF16)
    bh = b.astype(BF16)
    bl = (b - bh.astype(F32)).astype(BF16)
    return _dot(ah, bh) + (_dot(ah, bl) + _dot(al, bh))


def _dot_exact_lhs(a_bf16, b):
    hi, mid, lo = _split3(b)
    return _dot(a_bf16, hi) + (_dot(a_bf16, mid) + _dot(a_bf16, lo))


def _trig_bf16(ang):
    return (jnp.asarray(np.cos(ang), F32).astype(BF16), jnp.asarray(np.sin(ang), F32).astype(BF16))


def _silu(x):
    return x / (1.0 + jnp.exp(-x))


def _log_sigmoid(x):
    return jnp.minimum(x, 0.0) - jnp.log(1.0 + jnp.exp(-jnp.abs(x)))


def _mod_body(c_ref, w_ref, b_ref, o_ref):
    o_ref[...] = _dot3(_silu(c_ref[...]), w_ref[...]) + b_ref[...]


def _modulation(cvec, mod_w, mod_b):
    depth, d, e = mod_w.shape
    tn = d
    return pl.pallas_call(
        _mod_body,
        grid=(depth, e // tn),
        in_specs=[
            pl.BlockSpec((MOD_ROWS, d), lambda l, j: (0, 0)),
            pl.BlockSpec((None, d, tn), lambda l, j: (l, 0, j)),
            pl.BlockSpec((None, 1, tn), lambda l, j: (l, 0, j)),
        ],
        out_specs=pl.BlockSpec((None, MOD_ROWS, tn), lambda l, j: (l, 0, j)),
        out_shape=jax.ShapeDtypeStruct((depth, MOD_ROWS, e), F32),
        compiler_params=_params("parallel", "parallel"),
        name="modulation",
    )(cvec, mod_w, mod_b.reshape(depth, 1, e))


def _inproj_body(x_ref, g_ref, mod_ref, *refs):
    n_w = len(refs) // 2
    x = x_ref[...]
    d = x.shape[-1]
    h = x * lax.rsqrt(jnp.mean(x * x, axis=-1, keepdims=True) + EPS) * g_ref[...]
    mod = mod_ref[...]
    hb = (h * (1.0 + mod[:, d:2 * d]) + mod[:, :d]).astype(BF16)
    for w_ref, o_ref in zip(refs[:n_w], refs[n_w:]):
        o_ref[...] = _dot(hb, w_ref[...])


def _inproj(x, norm_g3, mod4, layer, mod_row, weights, tm, name):
    t, d = x.shape
    w_specs = [pl.BlockSpec(bs, functools.partial(lambda idx, i: idx, idx)) for _, bs, idx in weights]
    widths = [bs[-1] for _, bs, _ in weights]
    return pl.pallas_call(
        _inproj_body,
        grid=(t // tm,),
        in_specs=[
            pl.BlockSpec((tm, d), lambda i: (i, 0)),
            pl.BlockSpec((None, 1, d), lambda i: (layer, 0, 0)),
            pl.BlockSpec((None, None, 1, mod4.shape[-1]), lambda i: (layer, mod_row(i), 0, 0)),
        ] + w_specs,
        out_specs=[pl.BlockSpec((tm, n), lambda i: (i, 0)) for n in widths],
        out_shape=[jax.ShapeDtypeStruct((t, n), F32) for n in widths],
        compiler_params=_params("parallel"),
        name=name,
    )(x, norm_g3, mod4, *[w for w, _, _ in weights])


def _outproj_body(y_ref, w_ref, x_ref, mod_ref, *rest, final):
    o = _dot(y_ref[...], w_ref[...])
    d = o.shape[-1]
    xn = x_ref[...] + mod_ref[...][:, 2 * d:] * o
    if final:
        fg_ref, out_ref = rest
        xn = xn * lax.rsqrt(jnp.mean(xn * xn, axis=-1, keepdims=True) + EPS) * fg_ref[...]
    else:
        (out_ref,) = rest
    out_ref[...] = xn


def _outproj(y, w_out, j, x, mod4, layer, mod_row, final_g, tm, name):
    t, d = x.shape
    final = final_g is not None
    in_specs = [
        pl.BlockSpec((tm, y.shape[-1]), lambda i: (i, 0)),
        pl.BlockSpec((None,) + w_out.shape[1:], lambda i: (j, 0, 0)),
        pl.BlockSpec((tm, d), lambda i: (i, 0)),
        pl.BlockSpec((None, None, 1, mod4.shape[-1]), lambda i: (layer, mod_row(i), 0, 0)),
    ]
    args = [y, w_out, x, mod4]
    if final:
        in_specs.append(pl.BlockSpec((1, d), lambda i: (0, 0)))
        args.append(final_g.reshape(1, d))
    return pl.pallas_call(
        functools.partial(_outproj_body, final=final),
        grid=(t // tm,),
        in_specs=in_specs,
        out_specs=pl.BlockSpec((tm, d), lambda i: (i, 0)),
        out_shape=jax.ShapeDtypeStruct((t, d), F32),
        compiler_params=_params("parallel"),
        name=name,
    )(*args)


GLA_CHUNK = 64
GLA_BLOCK = 256
GLA_CLAMP_SPLIT = 8.0
GLA_CLAMP_HALF = 80.0


def _gla_tables(rows):
    c, hc = GLA_CHUNK, GLA_CHUNK // 2
    r = np.arange(rows)[:, None]
    s = np.arange(rows)[None, :]
    same_chunk = (r // c) == (s // c)
    same_half = (r // hc) == (s // hc)
    tri, sel, m_split, m_half = [], [], [], []
    for d in range(2):
        before = (s <= r) if d == 0 else (s >= r)
        late_half = (r // hc) % 2 == (1 - d)
        early_half_key = (s // hc) % 2 == d
        ref_split = (r // c) * c + (hc if d == 0 else hc - 1)
        ref_half = (r // hc) * hc + (hc // 2 if d == 0 else hc // 2 - 1)
        tri.append(same_chunk & before)
        sel.append(np.concatenate([s == ref_split, s == ref_half], axis=0))
        m_split.append(same_chunk & late_half & early_half_key)
        m_half.append(same_half & before)
    as_const = lambda m, dt: jnp.asarray(np.stack(m).astype(np.float32), dt)
    return as_const(tri, BF16), as_const(sel, BF16), as_const(m_split, F32), as_const(m_half, F32)


def _gla_body(q_ref, k_ref, v_ref, r_ref, lr_ref, wdec_ref, bdec_ref, og_ref, tri_ref, sel_ref,
              msplit_ref, mhalf_ref, *rest, seq, has_s0, want_state):
    rest = list(rest)
    s0_ref = rest.pop(0) if has_s0 else None
    y_ref = rest.pop(0)
    sfin_ref = rest.pop(0) if want_state else None
    o_ref, qi_ref, cum_ref, st_ref = rest
    chunk = GLA_CHUNK
    n_chunks = seq // chunk
    blk = tri_ref.shape[-1]
    dk = q_ref.shape[-1]
    dv = v_ref.shape[-1]
    scale = dk ** -0.5

    def nt(a, b):
        return lax.dot_general(a, b, _NT, preferred_element_type=F32)

    def block_pass(bi, carry):
        rows = pl.ds(pl.multiple_of(bi * blk, blk), blk)
        q = q_ref[rows, :] * scale
        k = k_ref[rows, :]
        dirs = range(2)
        logit = _dot3(lr_ref[rows, :], wdec_ref[...]) + bdec_ref[...]
        g = _log_sigmoid(logit) * (1.0 / GLA_GATE_NORM)
        cum = [_dot_exact_lhs(tri_ref[d], g[:, d * dk:(d + 1) * dk]) for d in dirs]
        refs = [_dot(sel_ref[d], cum[d].astype(BF16)) for d in dirs]
        prods = []
        for d in dirs:
            d_split = cum[d] - refs[d][:blk]
            d_half = cum[d] - refs[d][blk:]
            q1 = (q * jnp.exp(jnp.minimum(d_split, GLA_CLAMP_SPLIT))).astype(BF16)
            k1 = (k * jnp.exp(jnp.minimum(-d_split, GLA_CLAMP_SPLIT))).astype(BF16)
            q0 = (q * jnp.exp(jnp.minimum(d_half, GLA_CLAMP_HALF))).astype(BF16)
            k0 = (k * jnp.exp(jnp.minimum(-d_half, GLA_CLAMP_HALF))).astype(BF16)
            prods.append((nt(q1, k1), nt(q0, k0)))
            qi_ref[d, rows, :] = (q * jnp.exp(cum[d])).astype(BF16)
            cum_ref[d, rows, :] = cum[d]
        att = [p1 * msplit_ref[d] + jnp.where(mhalf_ref[d] > 0.5, p0, 0.0)
               for d, (p1, p0) in zip(dirs, prods)]
        o_ref[rows, :] = _dot((att[0] + att[1]).astype(BF16), v_ref[rows, :].astype(BF16))
        return carry

    lax.fori_loop(0, seq // blk, block_pass, 0)

    for d in range(2):
        if has_s0:
            st_ref[d] = s0_ref[d].T
        else:
            st_ref[d] = jnp.zeros((dv, dk), F32)

    def chunk_step(i, carry):
        for d in range(2):
            c = i if d == 0 else n_chunks - 1 - i
            sl = pl.ds(pl.multiple_of(c * chunk, chunk), chunk)
            cum = cum_ref[d, sl, :]
            blast = cum[chunk - 1:chunk] if d == 0 else cum[0:1]
            kh = (k_ref[sl, :] * jnp.exp(blast - cum)).astype(BF16)
            vb = v_ref[sl, :].astype(BF16)
            st = st_ref[d]
            o_ref[sl, :] += nt(qi_ref[d, sl, :], st.astype(BF16))
            st_ref[d] = jnp.exp(blast) * st + lax.dot_general(vb, kh, _TN, preferred_element_type=F32)
        return carry

    lax.fori_loop(0, n_chunks, chunk_step, 0, unroll=4)

    def finish(jb, carry):
        sl = pl.ds(pl.multiple_of(jb * blk, blk), blk)
        o = o_ref[sl, :]
        o = o * lax.rsqrt(jnp.mean(o * o, axis=-1, keepdims=True) + EPS) * og_ref[...]
        y_ref[sl, :] = (o * _silu(r_ref[sl, :])).astype(BF16)
        return carry

    lax.fori_loop(0, seq // blk, finish, 0)
    if want_state:
        for d in range(2):
            sfin_ref[d] = st_ref[d].T


def _gla_mixer(proj, lr, w_dec, b_dec, onorm_g, s0, j, n_seq, seq, want_state, name):
    t = proj.shape[0]
    h = GLA_HEADS
    key = w_dec.shape[-1]
    dk = key // h
    val = (proj.shape[1] - 2 * key) // 2
    dv = val // h
    has_s0 = s0 is not None
    tables = _gla_tables(min(seq, GLA_BLOCK))
    fixed = lambda a: pl.BlockSpec(a.shape, lambda b, hh: (0, 0, 0))
    in_specs = [
        pl.BlockSpec((seq, dk), lambda b, hh: (b, hh)),
        pl.BlockSpec((seq, dk), lambda b, hh: (b, h + hh)),
        pl.BlockSpec((seq, dv), lambda b, hh: (b, (2 * key) // dv + hh)),
        pl.BlockSpec((seq, dv), lambda b, hh: (b, (2 * key + val) // dv + hh)),
        pl.BlockSpec((seq, 2 * GLA_RANK), lambda b, hh: (b, 0)),
        pl.BlockSpec((None, None, 2 * GLA_RANK, 2 * dk), lambda b, hh: (j, hh, 0, 0)),
        pl.BlockSpec((None, None, 1, 2 * dk), lambda b, hh: (j, hh, 0, 0)),
        pl.BlockSpec((None, 1, dv), lambda b, hh: (j, 0, 0)),
    ] + [fixed(a) for a in tables]
    n_gla = w_dec.shape[0]
    w5 = w_dec.reshape(n_gla, 2, GLA_RANK, h, dk)
    zero = jnp.zeros_like(w5[:, 0])
    w_blk = jnp.concatenate([jnp.concatenate([w5[:, 0], zero], axis=-1),
                             jnp.concatenate([zero, w5[:, 1]], axis=-1)], axis=1).transpose(0, 2, 1, 3)
    b_cat = b_dec.reshape(n_gla, 2, h, dk).transpose(0, 2, 1, 3).reshape(n_gla, h, 1, 2 * dk)
    args = [proj, proj, proj, proj, lr, w_blk, b_cat, onorm_g.reshape(n_gla, 1, dv), *tables]
    if has_s0:
        in_specs.append(pl.BlockSpec((None, None, 2, None, dk, dv), lambda b, hh: (b, j, 0, hh, 0, 0)))
        args.append(s0)
    out_specs = [pl.BlockSpec((seq, dv), lambda b, hh: (b, hh))]
    out_shape = [jax.ShapeDtypeStruct((t, val), BF16)]
    if want_state:
        out_specs.append(pl.BlockSpec((None, 2, None, dk, dv), lambda b, hh: (b, 0, hh, 0, 0)))
        out_shape.append(jax.ShapeDtypeStruct((n_seq, 2, h, dk, dv), F32))
    return pl.pallas_call(
        functools.partial(_gla_body, seq=seq, has_s0=has_s0, want_state=want_state),
        grid=(n_seq, h),
        in_specs=in_specs,
        out_specs=out_specs,
        out_shape=out_shape,
        scratch_shapes=[pltpu.VMEM((seq, dv), F32), pltpu.VMEM((2, seq, dk), BF16),
                        pltpu.VMEM((2, seq, dk), F32), pltpu.VMEM((2, dv, dk), F32)],
        compiler_params=_params("parallel", "parallel"),
        name=name,
    )(*args)


def _dft_tables(n):
    f = np.arange(n, dtype=np.int64)
    ang = (np.outer(f, f) % n).astype(np.float64) * (2.0 * math.pi / n)
    return _trig_bf16(ang)


def _fnet_body(x_ref, z_ref, cc_ref, sc_ref, cl_ref, sl_ref, y_ref, *, seq, n_seq):
    xb = x_ref[...].astype(BF16)
    a = _dot(xb, cc_ref[...]).astype(BF16)
    bm = _dot(xb, sc_ref[...]).astype(BF16)
    inv = 1.0 / math.sqrt(seq * x_ref.shape[-1])
    for i in range(n_seq):
        rows = slice(i * seq, (i + 1) * seq)
        f = _dot(cl_ref[...], a[rows]) - _dot(sl_ref[...], bm[rows])
        y_ref[rows, :] = (f * inv * _silu(z_ref[rows, :])).astype(BF16)


def _fnet_mixer(proj, n_seq_total, seq, seqs_per_step, name):
    t = proj.shape[0]
    width = proj.shape[1] // 2
    gc = width // FN_GROUPS
    cc, sc = _dft_tables(gc)
    cl, sl = _dft_tables(seq)
    rows = seq * seqs_per_step
    const = pl.Buffered(1)
    return pl.pallas_call(
        functools.partial(_fnet_body, seq=seq, n_seq=seqs_per_step),
        grid=(n_seq_total // seqs_per_step, FN_GROUPS),
        in_specs=[
            pl.BlockSpec((rows, gc), lambda i, g: (i, g)),
            pl.BlockSpec((rows, gc), lambda i, g: (i, FN_GROUPS + g)),
            pl.BlockSpec((gc, gc), lambda i, g: (0, 0), pipeline_mode=const),
            pl.BlockSpec((gc, gc), lambda i, g: (0, 0), pipeline_mode=const),
            pl.BlockSpec((seq, seq), lambda i, g: (0, 0), pipeline_mode=const),
            pl.BlockSpec((seq, seq), lambda i, g: (0, 0), pipeline_mode=const),
        ],
        out_specs=pl.BlockSpec((rows, gc), lambda i, g: (i, g)),
        out_shape=jax.ShapeDtypeStruct((t, width), BF16),
        compiler_params=_params("parallel", "parallel"),
        name=name,
    )(proj, proj, cc, sc, cl, sl)


def _signal_tables(n):
    o = 2 * np.arange(n, dtype=np.int64) + 1
    ang = (np.outer(o, o) % (8 * n)).astype(np.float64) * (math.pi / (4 * n))
    return _trig_bf16(ang)


def _filter_tables(n):
    f = 2 * np.arange(n, dtype=np.int64) + 1
    o = np.arange(n, dtype=np.int64)
    ang = (np.outer(f, o) % (4 * n)).astype(np.float64) * (math.pi / (2 * n))
    return _trig_bf16(ang)


def _hyena_positions(seq, bands, pad_to):
    t = np.linspace(0.0, 1.0, seq)[:, None]
    w = 2.0 * math.pi * np.arange(seq)[:, None] / seq
    f = np.linspace(1e-4, bands - 1, bands)[None, :]
    zpos = np.concatenate([t, np.cos(f * w), -np.sin(f * w)], axis=-1)
    zpos = np.pad(zpos, ((0, 0), (0, pad_to - zpos.shape[1])))
    return jnp.asarray(zpos, F32), jnp.asarray(t, F32)


def _hyfilt_body(zpos_ref, w1_ref, b1_ref, w2_ref, b2_ref, w3_ref, b3_ref, fr_ref, w4f_ref,
                 w4b_ref, t_ref, del_ref, ch_ref, sh_ref, hre_ref, him_ref, a3_ref):
    @pl.when(pl.program_id(0) == 0)
    def _():
        fr = fr_ref[...]
        a = jnp.sin(fr * (_dot3(zpos_ref[...], w1_ref[...]) + b1_ref[...]))
        a = jnp.sin(fr * (_dot3(a, w2_ref[...]) + b2_ref[...]))
        a3_ref[...] = jnp.sin(fr * (_dot3(a, w3_ref[...]) + b3_ref[...]))

    a3 = a3_ref[...]
    dec = jnp.exp(-t_ref[...] * del_ref[...])
    fwd = _dot3(a3, w4f_ref[...]) * dec
    bwd = _dot3(a3, w4b_ref[...]) * dec
    first = lax.broadcasted_iota(jnp.int32, bwd.shape, 0) == 0
    bwd = jnp.where(first, 0.0, bwd)
    hre_ref[...] = _dot(ch_ref[...], (fwd + bwd).astype(BF16))
    him_ref[...] = _dot(sh_ref[...], (bwd - fwd).astype(BF16))


def _hyena_filters(seq, j, emb, w1p, b1, w2, b2, w3, b3, w4, freq, ce, name):
    ffn = w2.shape[-1]
    width = w4.shape[-1] // (2 * HY_ORDER)
    emb_pad = w1p.shape[1]
    zpos, tcol = _hyena_positions(seq, (emb - 1) // 2, emb_pad)
    deltas = jnp.asarray(np.abs(np.linspace(HY_MIN_DECAY, HY_MAX_DECAY, width, dtype=np.float32))[None, :])
    ch, sh = _filter_tables(seq)
    nb = width // ce
    const = pl.Buffered(1)
    vec = lambda: pl.BlockSpec((None, 1, ffn), lambda c: (j, 0, 0))
    return pl.pallas_call(
        _hyfilt_body,
        grid=(HY_ORDER * nb,),
        in_specs=[
            pl.BlockSpec((seq, emb_pad), lambda c: (0, 0)),
            pl.BlockSpec((None, emb_pad, ffn), lambda c: (j, 0, 0)),
            vec(),
            pl.BlockSpec((None, ffn, ffn), lambda c: (j, 0, 0)),
            vec(),
            pl.BlockSpec((None, ffn, ffn), lambda c: (j, 0, 0)),
            vec(),
            vec(),
            pl.BlockSpec((None, ffn, ce), lambda c: (j, 0, (c // nb) * 2 * nb + c % nb)),
            pl.BlockSpec((None, ffn, ce), lambda c: (j, 0, (c // nb) * 2 * nb + nb + c % nb)),
            pl.BlockSpec((seq, 1), lambda c: (0, 0)),
            pl.BlockSpec((1, ce), lambda c: (0, c % nb)),
            pl.BlockSpec((seq, seq), lambda c: (0, 0), pipeline_mode=const),
            pl.BlockSpec((seq, seq), lambda c: (0, 0), pipeline_mode=const),
        ],
        out_specs=[pl.BlockSpec((seq, ce), lambda c: (0, c)), pl.BlockSpec((seq, ce), lambda c: (0, c))],
        out_shape=[jax.ShapeDtypeStruct((seq, HY_ORDER * width), F32)] * 2,
        scratch_shapes=[pltpu.VMEM((seq, ffn), F32)],
        compiler_params=_params("arbitrary"),
        name=name,
    )(zpos, w1p, b1, w2, b2, w3, b3, freq, w4, w4, tcol, deltas, ch, sh)


def _hyena_body(x1_ref, x2_ref, v_ref, z_ref, cw1_ref, cw2_ref, cwv_ref, cb1_ref, cb2_ref, cbv_ref,
                hre0_ref, him0_ref, hre1_ref, him1_ref, d_ref, cq_ref, sq_ref, y_ref,
                cur_ref, cb_ref, yr_ref, yi_ref, *, seq, rows):
    n_rb = seq // rows
    inv_len = 1.0 / seq
    rid = lax.broadcasted_iota(jnp.int32, (rows, 1), 0)

    def block(rb):
        return pl.ds(pl.multiple_of(rb * rows, rows), rows)

    def short_conv(u_ref, w_ref, b_ref, rb):
        u = u_ref[block(rb), :]
        before = u_ref[pl.ds(jnp.maximum(rb * rows - 1, 0), 1), :]
        after = u_ref[pl.ds(jnp.minimum((rb + 1) * rows, seq - 1), 1), :]
        before = jnp.where(rb == 0, 0.0, before)
        after = jnp.where(rb == n_rb - 1, 0.0, after)
        prev = jnp.where(rid == 0, before, pltpu.roll(u, 1, 0))
        nxt = jnp.where(rid == rows - 1, after, pltpu.roll(u, rows - 1, 0))
        return prev * w_ref[0:1] + u * w_ref[1:2] + nxt * w_ref[2:3] + b_ref[...]

    def long_conv_gate(x_ref, cw_ref, cbias_ref, hre_ref, him_ref, dn):
        def fwd(rb, carry):
            sl = block(rb)
            ur = _dot(cq_ref[sl, :], cb_ref[...])
            ui = -_dot(sq_ref[sl, :], cb_ref[...])
            hr = hre_ref[sl, :]
            hi = him_ref[sl, :]
            yr_ref[sl, :] = (ur * hr - ui * hi).astype(BF16)
            yi_ref[sl, :] = (ur * hi + ui * hr).astype(BF16)
            return carry

        lax.fori_loop(0, n_rb, fwd, 0)

        def inv(rb, carry):
            sl = block(rb)
            c = (_dot(cq_ref[sl, :], yr_ref[...]) - _dot(sq_ref[sl, :], yi_ref[...])) * inv_len
            new = short_conv(x_ref, cw_ref, cbias_ref, rb) * (c + cur_ref[sl, :] * dn)
            cur_ref[sl, :] = new
            cb_ref[sl, :] = new.astype(BF16)
            return carry

        lax.fori_loop(0, n_rb, inv, 0)

    def load_v(rb, carry):
        v = short_conv(v_ref, cwv_ref, cbv_ref, rb)
        cur_ref[block(rb), :] = v
        cb_ref[block(rb), :] = v.astype(BF16)
        return carry

    lax.fori_loop(0, n_rb, load_v, 0)
    long_conv_gate(x1_ref, cw1_ref, cb1_ref, hre0_ref, him0_ref, d_ref[0:1])
    long_conv_gate(x2_ref, cw2_ref, cb2_ref, hre1_ref, him1_ref, d_ref[1:2])

    def store(rb, carry):
        sl = block(rb)
        y_ref[sl, :] = (cur_ref[sl, :] * _silu(z_ref[sl, :])).astype(BF16)
        return carry

    lax.fori_loop(0, n_rb, store, 0)


def _hyena_mixer(proj, conv_w, conv_b, hre, him, d_skip, j, n_seq, seq, ce, name):
    t = proj.shape[0]
    width = proj.shape[1] // 4
    nb = width // ce
    cq, sq = _signal_tables(seq)
    rows = min(seq, 256)
    const = pl.Buffered(1)
    seg = lambda s: pl.BlockSpec((seq, ce), lambda e, b: (b, s * nb + e))
    cw = lambda s: pl.BlockSpec((None, conv_w.shape[1], ce), lambda e, b: (j, 0, s * nb + e))
    cb = lambda s: pl.BlockSpec((None, 1, ce), lambda e, b: (j, 0, s * nb + e))
    hs = lambda n: pl.BlockSpec((seq, ce), lambda e, b: (0, n * nb + e), pipeline_mode=const)
    return pl.pallas_call(
        functools.partial(_hyena_body, seq=seq, rows=rows),
        grid=(nb, n_seq),
        in_specs=[seg(0), seg(1), seg(2), seg(3), cw(0), cw(1), cw(2), cb(0), cb(1), cb(2),
                  hs(0), hs(0), hs(1), hs(1),
                  pl.BlockSpec((None, HY_ORDER, ce), lambda e, b: (j, 0, e)),
                  pl.BlockSpec((seq, seq), lambda e, b: (0, 0), pipeline_mode=const),
                  pl.BlockSpec((seq, seq), lambda e, b: (0, 0), pipeline_mode=const)],
        out_specs=pl.BlockSpec((seq, ce), lambda e, b: (b, e)),
        out_shape=jax.ShapeDtypeStruct((t, width), BF16),
        scratch_shapes=[pltpu.VMEM((seq, ce), F32), pltpu.VMEM((seq, ce), BF16),
                        pltpu.VMEM((seq, ce), BF16), pltpu.VMEM((seq, ce), BF16)],
        compiler_params=_params("parallel", "parallel"),
        name=name,
    )(proj, proj, proj, proj, conv_w, conv_w, conv_w,
      conv_b.reshape(conv_b.shape[0], 1, -1), conv_b.reshape(conv_b.shape[0], 1, -1),
      conv_b.reshape(conv_b.shape[0], 1, -1), hre, him, hre, him, d_skip, cq, sq)


def kernel(x_prompt, x_sample, state_gla, c, c_ctx, mod_w, mod_b, norm_g, final_norm_g, gla_w_in, gla_w_dec, gla_b_dec, gla_onorm_g, gla_w_out, fn_w_in, fn_w_out, hy_w_in, hy_conv_w, hy_conv_b, hy_ffn_w1, hy_ffn_b1, hy_ffn_w2, hy_ffn_b2, hy_ffn_w3, hy_ffn_b3, hy_ffn_w4, hy_freq, hy_d, hy_w_out):
    n_p, l_p, d = x_prompt.shape
    n_s, l_s, _ = x_sample.shape
    depth = mod_w.shape[0]
    key = gla_w_dec.shape[-1]
    n_main = gla_w_in.shape[-1] - 2 * GLA_RANK

    cvec = jnp.concatenate([c_ctx[None], c, jnp.zeros((MOD_ROWS - 1 - n_s, d), F32)], axis=0)
    mod4 = _modulation(cvec, mod_w, mod_b).reshape(depth, MOD_ROWS, 1, 3 * d)
    norm_g3 = norm_g.reshape(depth, 1, d)

    gla_w_main = gla_w_in.astype(BF16)
    gla_w_lr = gla_w_in[:, :, n_main:].astype(BF16)
    gla_w_out_b = gla_w_out.astype(BF16)
    fn_w_in_b = fn_w_in.astype(BF16)
    fn_w_out_b = fn_w_out.astype(BF16)
    hy_w_in_b = hy_w_in.astype(BF16)
    hy_w_out_b = hy_w_out.astype(BF16)
    emb = hy_ffn_w1.shape[1]
    emb_pad = -(-emb // 128) * 128
    hy_w1p = jnp.pad(hy_ffn_w1, ((0, 0), (0, emb_pad - emb), (0, 0)))
    ffn = hy_ffn_w2.shape[-1]
    hy_vecs = [a.reshape(a.shape[0], 1, ffn) for a in (hy_ffn_b1, hy_ffn_b2, hy_ffn_b3, hy_freq)]

    tm = 512
    tiles_per_sample = l_s // tm
    streams = {
        "p": dict(x=x_prompt.reshape(n_p * l_p, d), n=n_p, l=l_p, row=lambda i: 0),
        "s": dict(x=x_sample.reshape(n_s * l_s, d), n=n_s, l=l_s, row=lambda i: 1 + i // tiles_per_sample),
    }
    new_states = []
    for i in range(depth):
        kind, j = i % N_MIXERS, i // N_MIXERS
        final_g = final_norm_g if i == depth - 1 else None
        if kind == 2:
            filt = {}
            for tag, st in streams.items():
                ce = min(d, 256 if st["l"] > 256 else d)
                filt[tag] = _hyena_filters(st["l"], j, emb, hy_w1p, hy_vecs[0], hy_ffn_w2, hy_vecs[1], hy_ffn_w3,
                                           hy_vecs[2], hy_ffn_w4, hy_vecs[3], ce, f"hyfilt_{tag}{i}")
        for tag, st in streams.items():
            x, n_seq, seq, row = st["x"], st["n"], st["l"], st["row"]
            if kind == 0:
                proj, lr = _inproj(
                    x, norm_g3, mod4, i, row,
                    [(gla_w_main, (None, d, n_main), (j, 0, 0)), (gla_w_lr, (None, d, 2 * GLA_RANK), (j, 0, 0))],
                    tm, f"inproj_{tag}{i}")
                is_ctx = tag == "p"
                res = _gla_mixer(proj, lr, gla_w_dec, gla_b_dec, gla_onorm_g,
                                 None if is_ctx else state_gla, j, n_seq, seq, is_ctx, f"gla_{tag}{i}")
                y = res[0]
                if is_ctx:
                    new_states.append(res[1])
                w_out = gla_w_out_b
            elif kind == 1:
                (proj,) = _inproj(x, norm_g3, mod4, i, row,
                                  [(fn_w_in_b, (None,) + fn_w_in.shape[1:], (j, 0, 0))], tm, f"inproj_{tag}{i}")
                y = _fnet_mixer(proj, n_seq, seq, 4 if seq <= 256 else 1, f"fnet_{tag}{i}")
                w_out = fn_w_out_b
            else:
                (proj,) = _inproj(x, norm_g3, mod4, i, row,
                                  [(hy_w_in_b, (None,) + hy_w_in.shape[1:], (j, 0, 0))], tm, f"inproj_{tag}{i}")
                ce = min(d, 256 if seq > 256 else d)
                hre, him = filt[tag]
                y = _hyena_mixer(proj, hy_conv_w, hy_conv_b, hre, him, hy_d, j, n_seq, seq, ce, f"hyena_{tag}{i}")
                w_out = hy_w_out_b
            st["x"] = _outproj(y, w_out, j, x, mod4, i, row, final_g, tm, f"outproj_{tag}{i}")
    y_prompt = streams["p"]["x"].reshape(n_p, l_p, d)
    y_sample = streams["s"]["x"].reshape(n_s, l_s, d)
    new_state_gla = jnp.stack(new_states, axis=1).astype(x_prompt.dtype)
    return (y_prompt, y_sample, new_state_gla)
```

```python
import functools
import math

import jax
import jax.numpy as jnp
import numpy as np
from jax import lax
from jax.experimental import pallas as pl
from jax.experimental.pallas import tpu as pltpu

F32 = jnp.float32
BF16 = jnp.bfloat16

EPS = 1e-6
N_MIXERS = 3
GLA_HEADS = 4
GLA_RANK = 16
GLA_GATE_NORM = 16.0
FN_GROUPS = 4
HY_ORDER = 2
HY_TARGET = 1e-2
HY_MIN_DECAY = math.log(HY_TARGET) / 1.5
HY_MAX_DECAY = math.log(HY_TARGET) / 0.3

V7X_VMEM_BYTES = 64 * 1024 * 1024
VMEM_LIMIT_BYTES = V7X_VMEM_BYTES - 8 * 1024 * 1024
MOD_ROWS = 8

_NT = (((1,), (1,)), ((), ()))
_TN = (((0,), (0,)), ((), ()))


def _params(*sem):
    return pltpu.CompilerParams(dimension_semantics=sem, vmem_limit_bytes=VMEM_LIMIT_BYTES)


def _dot(a, b):
    return jnp.dot(a, b, preferred_element_type=F32)


def _split3(x):
    hi = x.astype(BF16)
    r1 = x - hi.astype(F32)
    mid = r1.astype(BF16)
    lo = (r1 - mid.astype(F32)).astype(BF16)
    return hi, mid, lo


def _dot3(a, b):
    ah = a.astype(BF16)
    al = (a - ah.astype(F32)).astype(BF16)
    bh = b.astype(BF16)
    bl = (b - bh.astype(F32)).astype(BF16)
    return _dot(ah, bh) + (_dot(ah, bl) + _dot(al, bh))


def _dot_exact_lhs(a_bf16, b):
    hi, mid, lo = _split3(b)
    return _dot(a_bf16, hi) + (_dot(a_bf16, mid) + _dot(a_bf16, lo))


def _trig_bf16(ang):
    return (jnp.asarray(np.cos(ang), F32).astype(BF16), jnp.asarray(np.sin(ang), F32).astype(BF16))


def _silu(x):
    return x / (1.0 + jnp.exp(-x))


def _log_sigmoid(x):
    return jnp.minimum(x, 0.0) - jnp.log(1.0 + jnp.exp(-jnp.abs(x)))


def _mod_body(c_ref, w_ref, b_ref, o_ref):
    o_ref[...] = _dot3(_silu(c_ref[...]), w_ref[...]) + b_ref[...]


def _modulation(cvec, mod_w, mod_b):
    depth, d, e = mod_w.shape
    tn = d
    return pl.pallas_call(
        _mod_body,
        grid=(depth, e // tn),
        in_specs=[
            pl.BlockSpec((MOD_ROWS, d), lambda l, j: (0, 0)),
            pl.BlockSpec((None, d, tn), lambda l, j: (l, 0, j)),
            pl.BlockSpec((None, 1, tn), lambda l, j: (l, 0, j)),
        ],
        out_specs=pl.BlockSpec((None, MOD_ROWS, tn), lambda l, j: (l, 0, j)),
        out_shape=jax.ShapeDtypeStruct((depth, MOD_ROWS, e), F32),
        compiler_params=_params("parallel", "parallel"),
        name="modulation",
    )(cvec, mod_w, mod_b.reshape(depth, 1, e))


def _inproj_body(x_ref, g_ref, mod_ref, *refs):
    n_w = len(refs) // 2
    x = x_ref[...]
    d = x.shape[-1]
    h = x * lax.rsqrt(jnp.mean(x * x, axis=-1, keepdims=True) + EPS) * g_ref[...]
    mod = mod_ref[...]
    hb = (h * (1.0 + mod[:, d:2 * d]) + mod[:, :d]).astype(BF16)
    for w_ref, o_ref in zip(refs[:n_w], refs[n_w:]):
        o_ref[...] = _dot(hb, w_ref[...])


def _inproj(x, norm_g3, mod4, layer, mod_row, weights, tm, name):
    t, d = x.shape
    w_specs = [pl.BlockSpec(bs, functools.partial(lambda idx, i: idx, idx)) for _, bs, idx in weights]
    widths = [bs[-1] for _, bs, _ in weights]
    return pl.pallas_call(
        _inproj_body,
        grid=(t // tm,),
        in_specs=[
            pl.BlockSpec((tm, d), lambda i: (i, 0)),
            pl.BlockSpec((None, 1, d), lambda i: (layer, 0, 0)),
            pl.BlockSpec((None, None, 1, mod4.shape[-1]), lambda i: (layer, mod_row(i), 0, 0)),
        ] + w_specs,
        out_specs=[pl.BlockSpec((tm, n), lambda i: (i, 0)) for n in widths],
        out_shape=[jax.ShapeDtypeStruct((t, n), F32) for n in widths],
        compiler_params=_params("parallel"),
        name=name,
    )(x, norm_g3, mod4, *[w for w, _, _ in weights])


def _outproj_body(y_ref, w_ref, x_ref, mod_ref, *rest, final):
    o = _dot(y_ref[...], w_ref[...])
    d = o.shape[-1]
    xn = x_ref[...] + mod_ref[...][:, 2 * d:] * o
    if final:
        fg_ref, out_ref = rest
        xn = xn * lax.rsqrt(jnp.mean(xn * xn, axis=-1, keepdims=True) + EPS) * fg_ref[...]
    else:
        (out_ref,) = rest
    out_ref[...] = xn


def _outproj(y, w_out, j, x, mod4, layer, mod_row, final_g, tm, name):
    t, d = x.shape
    final = final_g is not None
    in_specs = [
        pl.BlockSpec((tm, y.shape[-1]), lambda i: (i, 0)),
        pl.BlockSpec((None,) + w_out.shape[1:], lambda i: (j, 0, 0)),
        pl.BlockSpec((tm, d), lambda i: (i, 0)),
        pl.BlockSpec((None, None, 1, mod4.shape[-1]), lambda i: (layer, mod_row(i), 0, 0)),
    ]
    args = [y, w_out, x, mod4]
    if final:
        in_specs.append(pl.BlockSpec((1, d), lambda i: (0, 0)))
        args.append(final_g.reshape(1, d))
    return pl.pallas_call(
        functools.partial(_outproj_body, final=final),
        grid=(t // tm,),
        in_specs=in_specs,
        out_specs=pl.BlockSpec((tm, d), lambda i: (i, 0)),
        out_shape=jax.ShapeDtypeStruct((t, d), F32),
        compiler_params=_params("parallel"),
        name=name,
    )(*args)


GLA_CHUNK = 64
GLA_BLOCK = 256
GLA_HEADS_PER_STEP = 2
GLA_CLAMP_SPLIT = 8.0
GLA_CLAMP_HALF = 80.0


def _gla_tables(rows):
    c, hc = GLA_CHUNK, GLA_CHUNK // 2
    r = np.arange(rows)[:, None]
    s = np.arange(rows)[None, :]
    same_chunk = (r // c) == (s // c)
    same_half = (r // hc) == (s // hc)
    tri, sel, m_split, m_half = [], [], [], []
    for d in range(2):
        before = (s <= r) if d == 0 else (s >= r)
        late_half = (r // hc) % 2 == (1 - d)
        early_half_key = (s // hc) % 2 == d
        ref_split = (r // c) * c + (hc if d == 0 else hc - 1)
        ref_half = (r // hc) * hc + (hc // 2 if d == 0 else hc // 2 - 1)
        tri.append(same_chunk & before)
        sel.append(np.concatenate([s == ref_split, s == ref_half], axis=0))
        m_split.append(same_chunk & late_half & early_half_key)
        m_half.append(same_half & before)
    as_const = lambda m, dt: jnp.asarray(np.stack(m).astype(np.float32), dt)
    return as_const(tri, BF16), as_const(sel, BF16), as_const(m_split, F32), as_const(m_half, F32)


def _gla_body(q_ref, k_ref, v_ref, r_ref, lr_ref, wdec_ref, bdec_ref, og_ref, tri_ref, sel_ref,
              msplit_ref, mhalf_ref, *rest, seq, has_s0, want_state):
    rest = list(rest)
    s0_ref = rest.pop(0) if has_s0 else None
    y_ref = rest.pop(0)
    sfin_ref = rest.pop(0) if want_state else None
    o_ref, qi_ref, cum_ref, st_ref = rest
    chunk = GLA_CHUNK
    n_chunks = seq // chunk
    blk = tri_ref.shape[-1]
    n_heads = wdec_ref.shape[0]
    dk = q_ref.shape[-1] // n_heads
    dv = v_ref.shape[-1] // n_heads
    scale = dk ** -0.5
    heads = range(n_heads)
    dirs = range(2)
    chains = [(hh, d) for hh in heads for d in dirs]

    def nt(a, b):
        return lax.dot_general(a, b, _NT, preferred_element_type=F32)

    def cols(hh, width):
        return slice(hh * width, (hh + 1) * width)

    def block_pass(bi, carry):
        rows = pl.ds(pl.multiple_of(bi * blk, blk), blk)
        q = [q_ref[rows, cols(hh, dk)] * scale for hh in heads]
        k = [k_ref[rows, cols(hh, dk)] for hh in heads]
        lr = lr_ref[rows, :]
        lr_hi = lr.astype(BF16)
        lr_lo = (lr - lr_hi.astype(F32)).astype(BF16)
        g = []
        for hh in heads:
            w = wdec_ref[hh]
            w_hi = w.astype(BF16)
            w_lo = (w - w_hi.astype(F32)).astype(BF16)
            logit = _dot(lr_hi, w_hi) + (_dot(lr_hi, w_lo) + _dot(lr_lo, w_hi)) + bdec_ref[hh]
            g.append(_log_sigmoid(logit) * (1.0 / GLA_GATE_NORM))
        cum = {(hh, d): _dot_exact_lhs(tri_ref[d], g[hh][:, cols(d, dk)]) for hh, d in chains}
        refs = {c: _dot(sel_ref[c[1]], cum[c].astype(BF16)) for c in chains}
        prods = {}
        for hh, d in chains:
            c = (hh, d)
            d_split = cum[c] - refs[c][:blk]
            d_half = cum[c] - refs[c][blk:]
            q1 = (q[hh] * jnp.exp(jnp.minimum(d_split, GLA_CLAMP_SPLIT))).astype(BF16)
            k1 = (k[hh] * jnp.exp(jnp.minimum(-d_split, GLA_CLAMP_SPLIT))).astype(BF16)
            q0 = (q[hh] * jnp.exp(jnp.minimum(d_half, GLA_CLAMP_HALF))).astype(BF16)
            k0 = (k[hh] * jnp.exp(jnp.minimum(-d_half, GLA_CLAMP_HALF))).astype(BF16)
            prods[c] = (nt(q1, k1), nt(q0, k0))
            qi_ref[hh, d, rows, :] = (q[hh] * jnp.exp(cum[c])).astype(BF16)
            cum_ref[hh, d, rows, :] = cum[c]
        for hh in heads:
            att = [prods[hh, d][0] * msplit_ref[d] + jnp.where(mhalf_ref[d] > 0.5, prods[hh, d][1], 0.0)
                   for d in dirs]
            o_ref[rows, cols(hh, dv)] = _dot((att[0] + att[1]).astype(BF16),
                                            v_ref[rows, cols(hh, dv)].astype(BF16))
        return carry

    lax.fori_loop(0, seq // blk, block_pass, 0)

    for hh, d in chains:
        if has_s0:
            st_ref[hh, d] = s0_ref[d, hh].T
        else:
            st_ref[hh, d] = jnp.zeros((dv, dk), F32)

    def chunk_step(i, carry):
        for hh, d in chains:
            c = i if d == 0 else n_chunks - 1 - i
            sl = pl.ds(pl.multiple_of(c * chunk, chunk), chunk)
            cum = cum_ref[hh, d, sl, :]
            blast = cum[chunk - 1:chunk] if d == 0 else cum[0:1]
            kh = (k_ref[sl, cols(hh, dk)] * jnp.exp(blast - cum)).astype(BF16)
            vb = v_ref[sl, cols(hh, dv)].astype(BF16)
            st = st_ref[hh, d]
            o_ref[sl, cols(hh, dv)] += nt(qi_ref[hh, d, sl, :], st.astype(BF16))
            st_ref[hh, d] = jnp.exp(blast) * st + lax.dot_general(vb, kh, _TN, preferred_element_type=F32)
        return carry

    lax.fori_loop(0, n_chunks, chunk_step, 0, unroll=n_chunks if n_chunks <= 4 else 2)

    def finish(jb, carry):
        sl = pl.ds(pl.multiple_of(jb * blk, blk), blk)
        for hh in heads:
            o = o_ref[sl, cols(hh, dv)]
            o = o * lax.rsqrt(jnp.mean(o * o, axis=-1, keepdims=True) + EPS) * og_ref[...]
            y_ref[sl, cols(hh, dv)] = (o * _silu(r_ref[sl, cols(hh, dv)])).astype(BF16)
        return carry

    lax.fori_loop(0, seq // blk, finish, 0)
    if want_state:
        for hh, d in chains:
            sfin_ref[d, hh] = st_ref[hh, d].T


def _gla_mixer(proj, lr, w_dec, b_dec, onorm_g, s0, j, n_seq, seq, want_state, name):
    t = proj.shape[0]
    h = GLA_HEADS
    key = w_dec.shape[-1]
    dk = key // h
    val = (proj.shape[1] - 2 * key) // 2
    dv = val // h
    has_s0 = s0 is not None
    tables = _gla_tables(min(seq, GLA_BLOCK))
    fixed = lambda a: pl.BlockSpec(a.shape, lambda b, hp: (0, 0, 0))
    hps = GLA_HEADS_PER_STEP
    wk, wv = hps * dk, hps * dv
    in_specs = [
        pl.BlockSpec((seq, wk), lambda b, hp: (b, hp)),
        pl.BlockSpec((seq, wk), lambda b, hp: (b, key // wk + hp)),
        pl.BlockSpec((seq, wv), lambda b, hp: (b, (2 * key) // wv + hp)),
        pl.BlockSpec((seq, wv), lambda b, hp: (b, (2 * key + val) // wv + hp)),
        pl.BlockSpec((seq, 2 * GLA_RANK), lambda b, hp: (b, 0)),
        pl.BlockSpec((None, hps, 2 * GLA_RANK, 2 * dk), lambda b, hp: (j, hp, 0, 0)),
        pl.BlockSpec((None, hps, 1, 2 * dk), lambda b, hp: (j, hp, 0, 0)),
        pl.BlockSpec((None, 1, dv), lambda b, hp: (j, 0, 0)),
    ] + [fixed(a) for a in tables]
    n_gla = w_dec.shape[0]
    w5 = w_dec.reshape(n_gla, 2, GLA_RANK, h, dk)
    zero = jnp.zeros_like(w5[:, 0])
    w_blk = jnp.concatenate([jnp.concatenate([w5[:, 0], zero], axis=-1),
                             jnp.concatenate([zero, w5[:, 1]], axis=-1)], axis=1).transpose(0, 2, 1, 3)
    b_cat = b_dec.reshape(n_gla, 2, h, dk).transpose(0, 2, 1, 3).reshape(n_gla, h, 1, 2 * dk)
    args = [proj, proj, proj, proj, lr, w_blk, b_cat, onorm_g.reshape(n_gla, 1, dv), *tables]
    if has_s0:
        in_specs.append(pl.BlockSpec((None, None, 2, hps, dk, dv), lambda b, hp: (b, j, 0, hp, 0, 0)))
        args.append(s0)
    out_specs = [pl.BlockSpec((seq, wv), lambda b, hp: (b, hp))]
    out_shape = [jax.ShapeDtypeStruct((t, val), BF16)]
    if want_state:
        out_specs.append(pl.BlockSpec((None, 2, hps, dk, dv), lambda b, hp: (b, 0, hp, 0, 0)))
        out_shape.append(jax.ShapeDtypeStruct((n_seq, 2, h, dk, dv), F32))
    return pl.pallas_call(
        functools.partial(_gla_body, seq=seq, has_s0=has_s0, want_state=want_state),
        grid=(n_seq, h // hps),
        in_specs=in_specs,
        out_specs=out_specs,
        out_shape=out_shape,
        scratch_shapes=[pltpu.VMEM((seq, wv), F32), pltpu.VMEM((hps, 2, seq, dk), BF16),
                        pltpu.VMEM((hps, 2, seq, dk), F32), pltpu.VMEM((hps, 2, dv, dk), F32)],
        compiler_params=_params("parallel", "parallel"),
        name=name,
    )(*args)


def _dft_tables(n):
    f = np.arange(n, dtype=np.int64)
    ang = (np.outer(f, f) % n).astype(np.float64) * (2.0 * math.pi / n)
    return _trig_bf16(ang)


def _fnet_body(x_ref, z_ref, cc_ref, sc_ref, cl_ref, sl_ref, y_ref, *, seq, n_seq):
    xb = x_ref[...].astype(BF16)
    a = _dot(xb, cc_ref[...]).astype(BF16)
    bm = _dot(xb, sc_ref[...]).astype(BF16)
    inv = 1.0 / math.sqrt(seq * x_ref.shape[-1])
    for i in range(n_seq):
        rows = slice(i * seq, (i + 1) * seq)
        f = _dot(cl_ref[...], a[rows]) - _dot(sl_ref[...], bm[rows])
        y_ref[rows, :] = (f * inv * _silu(z_ref[rows, :])).astype(BF16)


def _fnet_mixer(proj, n_seq_total, seq, seqs_per_step, name):
    t = proj.shape[0]
    width = proj.shape[1] // 2
    gc = width // FN_GROUPS
    cc, sc = _dft_tables(gc)
    cl, sl = _dft_tables(seq)
    rows = seq * seqs_per_step
    const = pl.Buffered(1)
    return pl.pallas_call(
        functools.partial(_fnet_body, seq=seq, n_seq=seqs_per_step),
        grid=(n_seq_total // seqs_per_step, FN_GROUPS),
        in_specs=[
            pl.BlockSpec((rows, gc), lambda i, g: (i, g)),
            pl.BlockSpec((rows, gc), lambda i, g: (i, FN_GROUPS + g)),
            pl.BlockSpec((gc, gc), lambda i, g: (0, 0), pipeline_mode=const),
            pl.BlockSpec((gc, gc), lambda i, g: (0, 0), pipeline_mode=const),
            pl.BlockSpec((seq, seq), lambda i, g: (0, 0), pipeline_mode=const),
            pl.BlockSpec((seq, seq), lambda i, g: (0, 0), pipeline_mode=const),
        ],
        out_specs=pl.BlockSpec((rows, gc), lambda i, g: (i, g)),
        out_shape=jax.ShapeDtypeStruct((t, width), BF16),
        compiler_params=_params("parallel", "parallel"),
        name=name,
    )(proj, proj, cc, sc, cl, sl)


def _signal_tables(n):
    o = 2 * np.arange(n, dtype=np.int64) + 1
    ang = (np.outer(o, o) % (8 * n)).astype(np.float64) * (math.pi / (4 * n))
    return _trig_bf16(ang)


def _filter_tables(n):
    f = 2 * np.arange(n, dtype=np.int64) + 1
    o = np.arange(n, dtype=np.int64)
    ang = (np.outer(f, o) % (4 * n)).astype(np.float64) * (math.pi / (2 * n))
    return _trig_bf16(ang)


def _hyena_positions(seq, bands, pad_to):
    t = np.linspace(0.0, 1.0, seq)[:, None]
    w = 2.0 * math.pi * np.arange(seq)[:, None] / seq
    f = np.linspace(1e-4, bands - 1, bands)[None, :]
    zpos = np.concatenate([t, np.cos(f * w), -np.sin(f * w)], axis=-1)
    zpos = np.pad(zpos, ((0, 0), (0, pad_to - zpos.shape[1])))
    return jnp.asarray(zpos, F32), jnp.asarray(t, F32)


def _hyfilt_body(zpos_ref, w1_ref, b1_ref, w2_ref, b2_ref, w3_ref, b3_ref, fr_ref, w4f_ref,
                 w4b_ref, t_ref, del_ref, ch_ref, sh_ref, hre_ref, him_ref, a3_ref):
    @pl.when(pl.program_id(0) == 0)
    def _():
        fr = fr_ref[...]
        a = jnp.sin(fr * (_dot3(zpos_ref[...], w1_ref[...]) + b1_ref[...]))
        a = jnp.sin(fr * (_dot3(a, w2_ref[...]) + b2_ref[...]))
        a3_ref[...] = jnp.sin(fr * (_dot3(a, w3_ref[...]) + b3_ref[...]))

    a3 = a3_ref[...]
    dec = jnp.exp(-t_ref[...] * del_ref[...])
    fwd = _dot3(a3, w4f_ref[...]) * dec
    bwd = _dot3(a3, w4b_ref[...]) * dec
    first = lax.broadcasted_iota(jnp.int32, bwd.shape, 0) == 0
    bwd = jnp.where(first, 0.0, bwd)
    hre_ref[...] = _dot(ch_ref[...], (fwd + bwd).astype(BF16))
    him_ref[...] = _dot(sh_ref[...], (bwd - fwd).astype(BF16))


def _hyena_filters(seq, j, emb, w1p, b1, w2, b2, w3, b3, w4, freq, ce, name):
    ffn = w2.shape[-1]
    width = w4.shape[-1] // (2 * HY_ORDER)
    emb_pad = w1p.shape[1]
    zpos, tcol = _hyena_positions(seq, (emb - 1) // 2, emb_pad)
    deltas = jnp.asarray(np.abs(np.linspace(HY_MIN_DECAY, HY_MAX_DECAY, width, dtype=np.float32))[None, :])
    ch, sh = _filter_tables(seq)
    nb = width // ce
    const = pl.Buffered(1)
    vec = lambda: pl.BlockSpec((None, 1, ffn), lambda c: (j, 0, 0))
    return pl.pallas_call(
        _hyfilt_body,
        grid=(HY_ORDER * nb,),
        in_specs=[
            pl.BlockSpec((seq, emb_pad), lambda c: (0, 0)),
            pl.BlockSpec((None, emb_pad, ffn), lambda c: (j, 0, 0)),
            vec(),
            pl.BlockSpec((None, ffn, ffn), lambda c: (j, 0, 0)),
            vec(),
            pl.BlockSpec((None, ffn, ffn), lambda c: (j, 0, 0)),
            vec(),
            vec(),
            pl.BlockSpec((None, ffn, ce), lambda c: (j, 0, (c // nb) * 2 * nb + c % nb)),
            pl.BlockSpec((None, ffn, ce), lambda c: (j, 0, (c // nb) * 2 * nb + nb + c % nb)),
            pl.BlockSpec((seq, 1), lambda c: (0, 0)),
            pl.BlockSpec((1, ce), lambda c: (0, c % nb)),
            pl.BlockSpec((seq, seq), lambda c: (0, 0), pipeline_mode=const),
            pl.BlockSpec((seq, seq), lambda c: (0, 0), pipeline_mode=const),
        ],
        out_specs=[pl.BlockSpec((seq, ce), lambda c: (0, c)), pl.BlockSpec((seq, ce), lambda c: (0, c))],
        out_shape=[jax.ShapeDtypeStruct((seq, HY_ORDER * width), F32)] * 2,
        scratch_shapes=[pltpu.VMEM((seq, ffn), F32)],
        compiler_params=_params("arbitrary"),
        name=name,
    )(zpos, w1p, b1, w2, b2, w3, b3, freq, w4, w4, tcol, deltas, ch, sh)


def _hyena_body(x1_ref, x2_ref, v_ref, z_ref, cw1_ref, cw2_ref, cwv_ref, cb1_ref, cb2_ref, cbv_ref,
                hre0_ref, him0_ref, hre1_ref, him1_ref, d_ref, cq_ref, sq_ref, y_ref,
                cur_ref, cb_ref, yr_ref, yi_ref, *, seq, rows):
    n_rb = seq // rows
    inv_len = 1.0 / seq
    rid = lax.broadcasted_iota(jnp.int32, (rows, 1), 0)

    def block(rb):
        return pl.ds(pl.multiple_of(rb * rows, rows), rows)

    def short_conv(u_ref, w_ref, b_ref, rb):
        u = u_ref[block(rb), :]
        before = u_ref[pl.ds(jnp.maximum(rb * rows - 1, 0), 1), :]
        after = u_ref[pl.ds(jnp.minimum((rb + 1) * rows, seq - 1), 1), :]
        before = jnp.where(rb == 0, 0.0, before)
        after = jnp.where(rb == n_rb - 1, 0.0, after)
        prev = jnp.where(rid == 0, before, pltpu.roll(u, 1, 0))
        nxt = jnp.where(rid == rows - 1, after, pltpu.roll(u, rows - 1, 0))
        return prev * w_ref[0:1] + u * w_ref[1:2] + nxt * w_ref[2:3] + b_ref[...]

    def long_conv_gate(x_ref, cw_ref, cbias_ref, hre_ref, him_ref, dn):
        def fwd(rb, carry):
            sl = block(rb)
            ur = _dot(cq_ref[sl, :], cb_ref[...])
            ui = -_dot(sq_ref[sl, :], cb_ref[...])
            hr = hre_ref[sl, :]
            hi = him_ref[sl, :]
            yr_ref[sl, :] = (ur * hr - ui * hi).astype(BF16)
            yi_ref[sl, :] = (ur * hi + ui * hr).astype(BF16)
            return carry

        lax.fori_loop(0, n_rb, fwd, 0)

        def inv(rb, carry):
            sl = block(rb)
            c = (_dot(cq_ref[sl, :], yr_ref[...]) - _dot(sq_ref[sl, :], yi_ref[...])) * inv_len
            new = short_conv(x_ref, cw_ref, cbias_ref, rb) * (c + cur_ref[sl, :] * dn)
            cur_ref[sl, :] = new
            cb_ref[sl, :] = new.astype(BF16)
            return carry

        lax.fori_loop(0, n_rb, inv, 0)

    def load_v(rb, carry):
        v = short_conv(v_ref, cwv_ref, cbv_ref, rb)
        cur_ref[block(rb), :] = v
        cb_ref[block(rb), :] = v.astype(BF16)
        return carry

    lax.fori_loop(0, n_rb, load_v, 0)
    long_conv_gate(x1_ref, cw1_ref, cb1_ref, hre0_ref, him0_ref, d_ref[0:1])
    long_conv_gate(x2_ref, cw2_ref, cb2_ref, hre1_ref, him1_ref, d_ref[1:2])

    def store(rb, carry):
        sl = block(rb)
        y_ref[sl, :] = (cur_ref[sl, :] * _silu(z_ref[sl, :])).astype(BF16)
        return carry

    lax.fori_loop(0, n_rb, store, 0)


def _hyena_mixer(proj, conv_w, conv_b, hre, him, d_skip, j, n_seq, seq, ce, name):
    t = proj.shape[0]
    width = proj.shape[1] // 4
    nb = width // ce
    cq, sq = _signal_tables(seq)
    rows = min(seq, 256)
    const = pl.Buffered(1)
    seg = lambda s: pl.BlockSpec((seq, ce), lambda e, b: (b, s * nb + e))
    cw = lambda s: pl.BlockSpec((None, conv_w.shape[1], ce), lambda e, b: (j, 0, s * nb + e))
    cb = lambda s: pl.BlockSpec((None, 1, ce), lambda e, b: (j, 0, s * nb + e))
    hs = lambda n: pl.BlockSpec((seq, ce), lambda e, b: (0, n * nb + e), pipeline_mode=const)
    return pl.pallas_call(
        functools.partial(_hyena_body, seq=seq, rows=rows),
        grid=(nb, n_seq),
        in_specs=[seg(0), seg(1), seg(2), seg(3), cw(0), cw(1), cw(2), cb(0), cb(1), cb(2),
                  hs(0), hs(0), hs(1), hs(1),
                  pl.BlockSpec((None, HY_ORDER, ce), lambda e, b: (j, 0, e)),
                  pl.BlockSpec((seq, seq), lambda e, b: (0, 0), pipeline_mode=const),
                  pl.BlockSpec((seq, seq), lambda e, b: (0, 0), pipeline_mode=const)],
        out_specs=pl.BlockSpec((seq, ce), lambda e, b: (b, e)),
        out_shape=jax.ShapeDtypeStruct((t, width), BF16),
        scratch_shapes=[pltpu.VMEM((seq, ce), F32), pltpu.VMEM((seq, ce), BF16),
                        pltpu.VMEM((seq, ce), BF16), pltpu.VMEM((seq, ce), BF16)],
        compiler_params=_params("parallel", "parallel"),
        name=name,
    )(proj, proj, proj, proj, conv_w, conv_w, conv_w,
      conv_b.reshape(conv_b.shape[0], 1, -1), conv_b.reshape(conv_b.shape[0], 1, -1),
      conv_b.reshape(conv_b.shape[0], 1, -1), hre, him, hre, him, d_skip, cq, sq)


def kernel(x_prompt, x_sample, state_gla, c, c_ctx, mod_w, mod_b, norm_g, final_norm_g, gla_w_in, gla_w_dec, gla_b_dec, gla_onorm_g, gla_w_out, fn_w_in, fn_w_out, hy_w_in, hy_conv_w, hy_conv_b, hy_ffn_w1, hy_ffn_b1, hy_ffn_w2, hy_ffn_b2, hy_ffn_w3, hy_ffn_b3, hy_ffn_w4, hy_freq, hy_d, hy_w_out):
    n_p, l_p, d = x_prompt.shape
    n_s, l_s, _ = x_sample.shape
    depth = mod_w.shape[0]
    key = gla_w_dec.shape[-1]
    n_main = gla_w_in.shape[-1] - 2 * GLA_RANK

    cvec = jnp.concatenate([c_ctx[None], c, jnp.zeros((MOD_ROWS - 1 - n_s, d), F32)], axis=0)
    mod4 = _modulation(cvec, mod_w, mod_b).reshape(depth, MOD_ROWS, 1, 3 * d)
    norm_g3 = norm_g.reshape(depth, 1, d)

    gla_w_main = gla_w_in.astype(BF16)
    gla_w_lr = gla_w_in[:, :, n_main:].astype(BF16)
    gla_w_out_b = gla_w_out.astype(BF16)
    fn_w_in_b = fn_w_in.astype(BF16)
    fn_w_out_b = fn_w_out.astype(BF16)
    hy_w_in_b = hy_w_in.astype(BF16)
    hy_w_out_b = hy_w_out.astype(BF16)
    emb = hy_ffn_w1.shape[1]
    emb_pad = -(-emb // 128) * 128
    hy_w1p = jnp.pad(hy_ffn_w1, ((0, 0), (0, emb_pad - emb), (0, 0)))
    ffn = hy_ffn_w2.shape[-1]
    hy_vecs = [a.reshape(a.shape[0], 1, ffn) for a in (hy_ffn_b1, hy_ffn_b2, hy_ffn_b3, hy_freq)]

    tm = 512
    tiles_per_sample = l_s // tm
    streams = {
        "p": dict(x=x_prompt.reshape(n_p * l_p, d), n=n_p, l=l_p, row=lambda i: 0),
        "s": dict(x=x_sample.reshape(n_s * l_s, d), n=n_s, l=l_s, row=lambda i: 1 + i // tiles_per_sample),
    }
    new_states = []
    for i in range(depth):
        kind, j = i % N_MIXERS, i // N_MIXERS
        final_g = final_norm_g if i == depth - 1 else None
        if kind == 2:
            filt = {}
            for tag, st in streams.items():
                ce = min(d, 256 if st["l"] > 256 else d)
                filt[tag] = _hyena_filters(st["l"], j, emb, hy_w1p, hy_vecs[0], hy_ffn_w2, hy_vecs[1], hy_ffn_w3,
                                           hy_vecs[2], hy_ffn_w4, hy_vecs[3], ce, f"hyfilt_{tag}{i}")
        for tag, st in streams.items():
            x, n_seq, seq, row = st["x"], st["n"], st["l"], st["row"]
            if kind == 0:
                proj, lr = _inproj(
                    x, norm_g3, mod4, i, row,
                    [(gla_w_main, (None, d, n_main), (j, 0, 0)), (gla_w_lr, (None, d, 2 * GLA_RANK), (j, 0, 0))],
                    tm, f"inproj_{tag}{i}")
                is_ctx = tag == "p"
                res = _gla_mixer(proj, lr, gla_w_dec, gla_b_dec, gla_onorm_g,
                                 None if is_ctx else state_gla, j, n_seq, seq, is_ctx, f"gla_{tag}{i}")
                y = res[0]
                if is_ctx:
                    new_states.append(res[1])
                w_out = gla_w_out_b
            elif kind == 1:
                (proj,) = _inproj(x, norm_g3, mod4, i, row,
                                  [(fn_w_in_b, (None,) + fn_w_in.shape[1:], (j, 0, 0))], tm, f"inproj_{tag}{i}")
                y = _fnet_mixer(proj, n_seq, seq, 4 if seq <= 256 else 1, f"fnet_{tag}{i}")
                w_out = fn_w_out_b
            else:
                (proj,) = _inproj(x, norm_g3, mod4, i, row,
                                  [(hy_w_in_b, (None,) + hy_w_in.shape[1:], (j, 0, 0))], tm, f"inproj_{tag}{i}")
                ce = min(d, 256 if seq > 256 else d)
                hre, him = filt[tag]
                y = _hyena_mixer(proj, hy_conv_w, hy_conv_b, hre, him, hy_d, j, n_seq, seq, ce, f"hyena_{tag}{i}")
                w_out = hy_w_out_b
            st["x"] = _outproj(y, w_out, j, x, mod4, i, row, final_g, tm, f"outproj_{tag}{i}")
    y_prompt = streams["p"]["x"].reshape(n_p, l_p, d)
    y_sample = streams["s"]["x"].reshape(n_s, l_s, d)
    new_state_gla = jnp.stack(new_states, axis=1).astype(x_prompt.dtype)
    return (y_prompt, y_sample, new_state_gla)
```

```python
import functools
import math

import jax
import jax.numpy as jnp
import numpy as np
from jax import lax
from jax.experimental import pallas as pl
from jax.experimental.pallas import tpu as pltpu

F32 = jnp.float32
BF16 = jnp.bfloat16

EPS = 1e-6
N_MIXERS = 3
GLA_HEADS = 4
GLA_RANK = 16
GLA_GATE_NORM = 16.0
FN_GROUPS = 4
HY_ORDER = 2
HY_TARGET = 1e-2
HY_MIN_DECAY = math.log(HY_TARGET) / 1.5
HY_MAX_DECAY = math.log(HY_TARGET) / 0.3
HY_ROW_BLOCK = 512

V7X_VMEM_BYTES = 64 * 1024 * 1024
VMEM_LIMIT_BYTES = V7X_VMEM_BYTES - 8 * 1024 * 1024
MOD_ROWS = 8

_NT = (((1,), (1,)), ((), ()))
_TN = (((0,), (0,)), ((), ()))


def _params(*sem):
    return pltpu.CompilerParams(dimension_semantics=sem, vmem_limit_bytes=VMEM_LIMIT_BYTES)


def _dot(a, b):
    return jnp.dot(a, b, preferred_element_type=F32)


def _split3(x):
    hi = x.astype(BF16)
    r1 = x - hi.astype(F32)
    mid = r1.astype(BF16)
    lo = (r1 - mid.astype(F32)).astype(BF16)
    return hi, mid, lo


def _dot3(a, b):
    ah = a.astype(BF16)
    al = (a - ah.astype(F32)).astype(BF16)
    bh = b.astype(BF16)
    bl = (b - bh.astype(F32)).astype(BF16)
    return _dot(ah, bh) + (_dot(ah, bl) + _dot(al, bh))


def _dot_exact_lhs(a_bf16, b):
    hi, mid, lo = _split3(b)
    return _dot(a_bf16, hi) + (_dot(a_bf16, mid) + _dot(a_bf16, lo))


def _trig_bf16(ang):
    return (jnp.asarray(np.cos(ang), F32).astype(BF16), jnp.asarray(np.sin(ang), F32).astype(BF16))


def _silu(x):
    return x / (1.0 + jnp.exp(-x))


def _log_sigmoid(x):
    return jnp.minimum(x, 0.0) - jnp.log(1.0 + jnp.exp(-jnp.abs(x)))


def _mod_body(c_ref, w_ref, b_ref, o_ref):
    o_ref[...] = _dot3(_silu(c_ref[...]), w_ref[...]) + b_ref[...]


def _modulation(cvec, mod_w, mod_b):
    depth, d, e = mod_w.shape
    tn = d
    return pl.pallas_call(
        _mod_body,
        grid=(depth, e // tn),
        in_specs=[
            pl.BlockSpec((MOD_ROWS, d), lambda l, j: (0, 0)),
            pl.BlockSpec((None, d, tn), lambda l, j: (l, 0, j)),
            pl.BlockSpec((None, 1, tn), lambda l, j: (l, 0, j)),
        ],
        out_specs=pl.BlockSpec((None, MOD_ROWS, tn), lambda l, j: (l, 0, j)),
        out_shape=jax.ShapeDtypeStruct((depth, MOD_ROWS, e), F32),
        compiler_params=_params("parallel", "parallel"),
        name="modulation",
    )(cvec, mod_w, mod_b.reshape(depth, 1, e))


def _inproj_body(x_ref, g_ref, mod_ref, *refs):
    n_w = len(refs) // 2
    x = x_ref[...]
    d = x.shape[-1]
    h = x * lax.rsqrt(jnp.mean(x * x, axis=-1, keepdims=True) + EPS) * g_ref[...]
    mod = mod_ref[...]
    hb = (h * (1.0 + mod[:, d:2 * d]) + mod[:, :d]).astype(BF16)
    for w_ref, o_ref in zip(refs[:n_w], refs[n_w:]):
        o_ref[...] = _dot(hb, w_ref[...])


def _inproj(x, norm_g3, mod4, layer, mod_row, weights, tm, name):
    t, d = x.shape
    w_specs = [pl.BlockSpec(bs, functools.partial(lambda idx, i: idx, idx)) for _, bs, idx in weights]
    widths = [bs[-1] for _, bs, _ in weights]
    return pl.pallas_call(
        _inproj_body,
        grid=(t // tm,),
        in_specs=[
            pl.BlockSpec((tm, d), lambda i: (i, 0)),
            pl.BlockSpec((None, 1, d), lambda i: (layer, 0, 0)),
            pl.BlockSpec((None, None, 1, mod4.shape[-1]), lambda i: (layer, mod_row(i), 0, 0)),
        ] + w_specs,
        out_specs=[pl.BlockSpec((tm, n), lambda i: (i, 0)) for n in widths],
        out_shape=[jax.ShapeDtypeStruct((t, n), F32) for n in widths],
        compiler_params=_params("parallel"),
        name=name,
    )(x, norm_g3, mod4, *[w for w, _, _ in weights])


def _outproj_body(y_ref, w_ref, x_ref, mod_ref, *rest, final):
    o = _dot(y_ref[...], w_ref[...])
    d = o.shape[-1]
    xn = x_ref[...] + mod_ref[...][:, 2 * d:] * o
    if final:
        fg_ref, out_ref = rest
        xn = xn * lax.rsqrt(jnp.mean(xn * xn, axis=-1, keepdims=True) + EPS) * fg_ref[...]
    else:
        (out_ref,) = rest
    out_ref[...] = xn


def _outproj(y, w_out, j, x, mod4, layer, mod_row, final_g, tm, name):
    t, d = x.shape
    final = final_g is not None
    in_specs = [
        pl.BlockSpec((tm, y.shape[-1]), lambda i: (i, 0)),
        pl.BlockSpec((None,) + w_out.shape[1:], lambda i: (j, 0, 0)),
        pl.BlockSpec((tm, d), lambda i: (i, 0)),
        pl.BlockSpec((None, None, 1, mod4.shape[-1]), lambda i: (layer, mod_row(i), 0, 0)),
    ]
    args = [y, w_out, x, mod4]
    if final:
        in_specs.append(pl.BlockSpec((1, d), lambda i: (0, 0)))
        args.append(final_g.reshape(1, d))
    return pl.pallas_call(
        functools.partial(_outproj_body, final=final),
        grid=(t // tm,),
        in_specs=in_specs,
        out_specs=pl.BlockSpec((tm, d), lambda i: (i, 0)),
        out_shape=jax.ShapeDtypeStruct((t, d), F32),
        compiler_params=_params("parallel"),
        name=name,
    )(*args)


GLA_CHUNK = 64
GLA_BLOCK = 256
GLA_HEADS_PER_STEP = 2
GLA_CLAMP_SPLIT = 8.0
GLA_CLAMP_HALF = 80.0


def _gla_tables(rows, dk):
    c, hc = GLA_CHUNK, GLA_CHUNK // 2
    r = np.arange(rows)[:, None]
    s = np.arange(rows)[None, :]
    same_chunk = (r // c) == (s // c)
    same_half = (r // hc) == (s // hc)
    before = [s <= r, s >= r]
    tri = np.stack([same_chunk & b for b in before])
    m_half = np.stack([same_half & b for b in before])
    sel = np.concatenate([s == (r // c) * c + hc, s == (r // hc) * hc + hc // 2], axis=0)
    second = np.broadcast_to((r // hc) % 2 == 1, (rows, dk))
    as_const = lambda m, dt: jnp.asarray(m.astype(np.float32), dt)
    return (as_const(tri, BF16), as_const(sel, BF16), as_const(same_chunk, F32), as_const(m_half, F32),
            as_const(second, F32))


def _gla_body(q_ref, k_ref, v_ref, r_ref, lr_ref, wdec_ref, bdec_ref, og_ref, tri_ref, sel_ref,
              mchunk_ref, mhalf_ref, second_ref, *rest, seq, layer, has_s0, want_state, has_states_in):
    rest = list(rest)
    s0_ref = rest.pop(0) if has_s0 else None
    if has_states_in:
        rest.pop(0)
    y_ref = rest.pop(0)
    sfin_ref = rest.pop(0) if want_state else None
    o_ref, qi_ref, cum_ref, st_ref = rest
    chunk = GLA_CHUNK
    n_chunks = seq // chunk
    blk = tri_ref.shape[-1]
    n_heads = wdec_ref.shape[0]
    dk = q_ref.shape[-1] // n_heads
    dv = v_ref.shape[-1] // n_heads
    scale = dk ** -0.5
    heads = range(n_heads)
    dirs = range(2)
    chains = [(hh, d) for hh in heads for d in dirs]

    def nt(a, b):
        return lax.dot_general(a, b, _NT, preferred_element_type=F32)

    def cols(hh, width):
        return slice(hh * width, (hh + 1) * width)

    def block_pass(bi, carry):
        rows = pl.ds(pl.multiple_of(bi * blk, blk), blk)
        q = [q_ref[rows, cols(hh, dk)] * scale for hh in heads]
        k = [k_ref[rows, cols(hh, dk)] for hh in heads]
        lr = lr_ref[rows, :]
        lr_hi = lr.astype(BF16)
        lr_lo = (lr - lr_hi.astype(F32)).astype(BF16)
        g = []
        for hh in heads:
            w = wdec_ref[hh]
            w_hi = w.astype(BF16)
            w_lo = (w - w_hi.astype(F32)).astype(BF16)
            logit = _dot(lr_hi, w_hi) + (_dot(lr_hi, w_lo) + _dot(lr_lo, w_hi)) + bdec_ref[hh]
            g.append(_log_sigmoid(logit) * (1.0 / GLA_GATE_NORM))
        cum = {(hh, d): _dot_exact_lhs(tri_ref[d], g[hh][:, cols(d, dk)]) for hh, d in chains}
        refs = [_dot(sel_ref[...], jnp.concatenate([cum[hh, 0], cum[hh, 1]], axis=1).astype(BF16))
                for hh in heads]
        second = second_ref[...]
        first = 1.0 - second
        p_split, p_half = [], {}
        for hh in heads:
            q_in = (q[hh] * second, q[hh] * first)
            k_in = (k[hh] * first, k[hh] * second)
            q1, k1 = [], []
            for d in dirs:
                c = (hh, d)
                d_split = cum[c] - refs[hh][:blk, cols(d, dk)]
                d_half = cum[c] - refs[hh][blk:, cols(d, dk)]
                q1.append((q_in[d] * jnp.exp(jnp.minimum(d_split, GLA_CLAMP_SPLIT))).astype(BF16))
                k1.append((k_in[d] * jnp.exp(jnp.minimum(-d_split, GLA_CLAMP_SPLIT))).astype(BF16))
                q0 = (q[hh] * jnp.exp(jnp.minimum(d_half, GLA_CLAMP_HALF))).astype(BF16)
                k0 = (k[hh] * jnp.exp(jnp.minimum(-d_half, GLA_CLAMP_HALF))).astype(BF16)
                p_half[c] = nt(q0, k0)
                qi_ref[hh, d, rows, :] = (q[hh] * jnp.exp(cum[c])).astype(BF16)
                cum_ref[hh, d, rows, :] = cum[c]
            p_split.append(nt(jnp.concatenate(q1, axis=1), jnp.concatenate(k1, axis=1)))
        for hh in heads:
            att = p_split[hh] * mchunk_ref[...]
            for d in dirs:
                att = att + jnp.where(mhalf_ref[d] > 0.5, p_half[hh, d], 0.0)
            o_ref[rows, cols(hh, dv)] = _dot(att.astype(BF16), v_ref[rows, cols(hh, dv)].astype(BF16))
        return carry

    lax.fori_loop(0, seq // blk, block_pass, 0)

    for hh, d in chains:
        st_ref[hh, d] = s0_ref[d, hh] if has_s0 else jnp.zeros((dk, dv), F32)

    def chunk_step(i, carry):
        for hh, d in chains:
            c = i if d == 0 else n_chunks - 1 - i
            sl = pl.ds(pl.multiple_of(c * chunk, chunk), chunk)
            cum = cum_ref[hh, d, sl, :]
            blast = cum[chunk - 1:chunk] if d == 0 else cum[0:1]
            kh = (k_ref[sl, cols(hh, dk)] * jnp.exp(blast - cum)).astype(BF16)
            vb = v_ref[sl, cols(hh, dv)].astype(BF16)
            st = st_ref[hh, d]
            o_ref[sl, cols(hh, dv)] += _dot(qi_ref[hh, d, sl, :], st.astype(BF16))
            decay = jnp.broadcast_to(jnp.exp(blast), (dk, dk)).T
            st_ref[hh, d] = (jnp.concatenate([decay] * (dv // dk), axis=1) * st
                             + lax.dot_general(kh, vb, _TN, preferred_element_type=F32))
        return carry

    lax.fori_loop(0, n_chunks, chunk_step, 0, unroll=min(n_chunks, 4))

    def finish(jb, carry):
        sl = pl.ds(pl.multiple_of(jb * blk, blk), blk)
        for hh in heads:
            o = o_ref[sl, cols(hh, dv)]
            o = o * lax.rsqrt(jnp.mean(o * o, axis=-1, keepdims=True) + EPS) * og_ref[...]
            y_ref[sl, cols(hh, dv)] = (o * _silu(r_ref[sl, cols(hh, dv)])).astype(BF16)
        return carry

    lax.fori_loop(0, seq // blk, finish, 0)
    if want_state:
        own = sfin_ref if has_states_in else sfin_ref.at[layer]
        for hh, d in chains:
            own[d, hh] = st_ref[hh, d]
        if not has_states_in:
            for other in range(sfin_ref.shape[0]):
                if other != layer:
                    sfin_ref[other] = jnp.zeros(sfin_ref.shape[1:], F32)


def _gla_mixer(proj, lr, w_dec, b_dec, onorm_g, s0, j, n_seq, seq, want_state, states, name):
    t = proj.shape[0]
    h = GLA_HEADS
    key = w_dec.shape[-1]
    dk = key // h
    val = (proj.shape[1] - 2 * key) // 2
    dv = val // h
    has_s0 = s0 is not None
    tables = _gla_tables(min(seq, GLA_BLOCK), dk)
    fixed = lambda a: pl.BlockSpec(a.shape, lambda b, hp: (0,) * a.ndim)
    hps = GLA_HEADS_PER_STEP
    wk, wv = hps * dk, hps * dv
    in_specs = [
        pl.BlockSpec((seq, wk), lambda b, hp: (b, hp)),
        pl.BlockSpec((seq, wk), lambda b, hp: (b, key // wk + hp)),
        pl.BlockSpec((seq, wv), lambda b, hp: (b, (2 * key) // wv + hp)),
        pl.BlockSpec((seq, wv), lambda b, hp: (b, (2 * key + val) // wv + hp)),
        pl.BlockSpec((seq, 2 * GLA_RANK), lambda b, hp: (b, 0)),
        pl.BlockSpec((None, hps, 2 * GLA_RANK, 2 * dk), lambda b, hp: (j, hp, 0, 0)),
        pl.BlockSpec((None, hps, 1, 2 * dk), lambda b, hp: (j, hp, 0, 0)),
        pl.BlockSpec((None, 1, dv), lambda b, hp: (j, 0, 0)),
    ] + [fixed(a) for a in tables]
    n_gla = w_dec.shape[0]
    w5 = w_dec.reshape(n_gla, 2, GLA_RANK, h, dk)
    zero = jnp.zeros_like(w5[:, 0])
    w_blk = jnp.concatenate([jnp.concatenate([w5[:, 0], zero], axis=-1),
                             jnp.concatenate([zero, w5[:, 1]], axis=-1)], axis=1).transpose(0, 2, 1, 3)
    b_cat = b_dec.reshape(n_gla, 2, h, dk).transpose(0, 2, 1, 3).reshape(n_gla, h, 1, 2 * dk)
    args = [proj, proj, proj, proj, lr, w_blk, b_cat, onorm_g.reshape(n_gla, 1, dv), *tables]
    if has_s0:
        in_specs.append(pl.BlockSpec((None, None, 2, hps, dk, dv), lambda b, hp: (b, j, 0, hp, 0, 0)))
        args.append(s0)
    out_specs = [pl.BlockSpec((seq, wv), lambda b, hp: (b, hp))]
    out_shape = [jax.ShapeDtypeStruct((t, val), BF16)]
    aliases = {}
    if want_state:
        out_shape.append(jax.ShapeDtypeStruct((n_seq, n_gla, 2, h, dk, dv), F32))
        if states is None:
            out_specs.append(pl.BlockSpec((None, n_gla, 2, hps, dk, dv), lambda b, hp: (b, 0, 0, hp, 0, 0)))
        else:
            out_specs.append(pl.BlockSpec((None, None, 2, hps, dk, dv), lambda b, hp: (b, j, 0, hp, 0, 0)))
            in_specs.append(pl.BlockSpec(memory_space=pl.ANY))
            args.append(states)
            aliases = {len(args) - 1: 1}
    return pl.pallas_call(
        functools.partial(_gla_body, seq=seq, layer=j, has_s0=has_s0, want_state=want_state,
                          has_states_in=bool(aliases)),
        grid=(n_seq, h // hps),
        in_specs=in_specs,
        out_specs=out_specs,
        out_shape=out_shape,
        input_output_aliases=aliases,
        scratch_shapes=[pltpu.VMEM((seq, wv), F32), pltpu.VMEM((hps, 2, seq, dk), BF16),
                        pltpu.VMEM((hps, 2, seq, dk), F32), pltpu.VMEM((hps, 2, dk, dv), F32)],
        compiler_params=_params("parallel", "parallel"),
        name=name,
    )(*args)


def _dft_tables(n):
    f = np.arange(n, dtype=np.int64)
    ang = (np.outer(f, f) % n).astype(np.float64) * (2.0 * math.pi / n)
    return _trig_bf16(ang)


def _fnet_body(x_ref, z_ref, cc_ref, sc_ref, cl_ref, sl_ref, y_ref, *, seq, n_seq):
    xb = x_ref[...].astype(BF16)
    a = _dot(xb, cc_ref[...]).astype(BF16)
    bm = _dot(xb, sc_ref[...]).astype(BF16)
    inv = 1.0 / math.sqrt(seq * x_ref.shape[-1])
    for i in range(n_seq):
        rows = slice(i * seq, (i + 1) * seq)
        f = _dot(cl_ref[...], a[rows]) - _dot(sl_ref[...], bm[rows])
        y_ref[rows, :] = (f * inv * _silu(z_ref[rows, :])).astype(BF16)


def _fnet_mixer(proj, n_seq_total, seq, seqs_per_step, name):
    t = proj.shape[0]
    width = proj.shape[1] // 2
    gc = width // FN_GROUPS
    cc, sc = _dft_tables(gc)
    cl, sl = _dft_tables(seq)
    rows = seq * seqs_per_step
    const = pl.Buffered(1)
    return pl.pallas_call(
        functools.partial(_fnet_body, seq=seq, n_seq=seqs_per_step),
        grid=(n_seq_total // seqs_per_step, FN_GROUPS),
        in_specs=[
            pl.BlockSpec((rows, gc), lambda i, g: (i, g)),
            pl.BlockSpec((rows, gc), lambda i, g: (i, FN_GROUPS + g)),
            pl.BlockSpec((gc, gc), lambda i, g: (0, 0), pipeline_mode=const),
            pl.BlockSpec((gc, gc), lambda i, g: (0, 0), pipeline_mode=const),
            pl.BlockSpec((seq, seq), lambda i, g: (0, 0), pipeline_mode=const),
            pl.BlockSpec((seq, seq), lambda i, g: (0, 0), pipeline_mode=const),
        ],
        out_specs=pl.BlockSpec((rows, gc), lambda i, g: (i, g)),
        out_shape=jax.ShapeDtypeStruct((t, width), BF16),
        compiler_params=_params("parallel", "parallel"),
        name=name,
    )(proj, proj, cc, sc, cl, sl)


def _signal_tables(n):
    o = 2 * np.arange(n, dtype=np.int64) + 1
    ang = (np.outer(o, o) % (8 * n)).astype(np.float64) * (math.pi / (4 * n))
    return _trig_bf16(ang)


def _filter_tables(n):
    f = 2 * np.arange(n, dtype=np.int64) + 1
    o = np.arange(n, dtype=np.int64)
    ang = (np.outer(f, o) % (4 * n)).astype(np.float64) * (math.pi / (2 * n))
    return _trig_bf16(ang)


def _hyena_positions(seq, bands, pad_to):
    t = np.linspace(0.0, 1.0, seq)[:, None]
    w = 2.0 * math.pi * np.arange(seq)[:, None] / seq
    f = np.linspace(1e-4, bands - 1, bands)[None, :]
    zpos = np.concatenate([t, np.cos(f * w), -np.sin(f * w)], axis=-1)
    zpos = np.pad(zpos, ((0, 0), (0, pad_to - zpos.shape[1])))
    return jnp.asarray(zpos, F32), jnp.asarray(t, F32)


def _hyfilt_body(zpos_ref, w1_ref, b1_ref, w2_ref, b2_ref, w3_ref, b3_ref, fr_ref, w4f_ref,
                 w4b_ref, t_ref, del_ref, ch_ref, sh_ref, hre_ref, him_ref, a3_ref):
    @pl.when(pl.program_id(0) == 0)
    def _():
        fr = fr_ref[...]
        a = jnp.sin(fr * (_dot3(zpos_ref[...], w1_ref[...]) + b1_ref[...]))
        a = jnp.sin(fr * (_dot3(a, w2_ref[...]) + b2_ref[...]))
        a3_ref[...] = jnp.sin(fr * (_dot3(a, w3_ref[...]) + b3_ref[...]))

    a3 = a3_ref[...]
    dec = jnp.exp(-t_ref[...] * del_ref[...])
    fwd = _dot3(a3, w4f_ref[...]) * dec
    bwd = _dot3(a3, w4b_ref[...]) * dec
    first = lax.broadcasted_iota(jnp.int32, bwd.shape, 0) == 0
    bwd = jnp.where(first, 0.0, bwd)
    hre_ref[...] = _dot(ch_ref[...], (fwd + bwd).astype(BF16))
    him_ref[...] = _dot(sh_ref[...], (bwd - fwd).astype(BF16))


def _hyena_filters(seq, j, emb, w1p, b1, w2, b2, w3, b3, w4, freq, ce, name):
    ffn = w2.shape[-1]
    width = w4.shape[-1] // (2 * HY_ORDER)
    emb_pad = w1p.shape[1]
    zpos, tcol = _hyena_positions(seq, (emb - 1) // 2, emb_pad)
    deltas = jnp.asarray(np.abs(np.linspace(HY_MIN_DECAY, HY_MAX_DECAY, width, dtype=np.float32))[None, :])
    ch, sh = _filter_tables(seq)
    nb = width // ce
    const = pl.Buffered(1)
    vec = lambda: pl.BlockSpec((None, 1, ffn), lambda c: (j, 0, 0))
    return pl.pallas_call(
        _hyfilt_body,
        grid=(HY_ORDER * nb,),
        in_specs=[
            pl.BlockSpec((seq, emb_pad), lambda c: (0, 0)),
            pl.BlockSpec((None, emb_pad, ffn), lambda c: (j, 0, 0)),
            vec(),
            pl.BlockSpec((None, ffn, ffn), lambda c: (j, 0, 0)),
            vec(),
            pl.BlockSpec((None, ffn, ffn), lambda c: (j, 0, 0)),
            vec(),
            vec(),
            pl.BlockSpec((None, ffn, ce), lambda c: (j, 0, (c // nb) * 2 * nb + c % nb)),
            pl.BlockSpec((None, ffn, ce), lambda c: (j, 0, (c // nb) * 2 * nb + nb + c % nb)),
            pl.BlockSpec((seq, 1), lambda c: (0, 0)),
            pl.BlockSpec((1, ce), lambda c: (0, c % nb)),
            pl.BlockSpec((seq, seq), lambda c: (0, 0), pipeline_mode=const),
            pl.BlockSpec((seq, seq), lambda c: (0, 0), pipeline_mode=const),
        ],
        out_specs=[pl.BlockSpec((seq, ce), lambda c: (0, c)), pl.BlockSpec((seq, ce), lambda c: (0, c))],
        out_shape=[jax.ShapeDtypeStruct((seq, HY_ORDER * width), F32)] * 2,
        scratch_shapes=[pltpu.VMEM((seq, ffn), F32)],
        compiler_params=_params("arbitrary"),
        name=name,
    )(zpos, w1p, b1, w2, b2, w3, b3, freq, w4, w4, tcol, deltas, ch, sh)


def _hyena_body(x1_ref, x2_ref, v_ref, z_ref, cw1_ref, cw2_ref, cwv_ref, cb1_ref, cb2_ref, cbv_ref,
                hre0_ref, him0_ref, hre1_ref, him1_ref, d_ref, cq_ref, sq_ref, y_ref,
                cur_ref, cb_ref, yr_ref, yi_ref, *, seq, rows):
    n_rb = seq // rows
    inv_len = 1.0 / seq
    rid = lax.broadcasted_iota(jnp.int32, (rows, 1), 0)

    def block(rb):
        return pl.ds(pl.multiple_of(rb * rows, rows), rows)

    def short_conv(u_ref, w_ref, b_ref, rb):
        u = u_ref[block(rb), :]
        before = u_ref[pl.ds(jnp.maximum(rb * rows - 1, 0), 1), :]
        after = u_ref[pl.ds(jnp.minimum((rb + 1) * rows, seq - 1), 1), :]
        before = jnp.where(rb == 0, 0.0, before)
        after = jnp.where(rb == n_rb - 1, 0.0, after)
        prev = jnp.where(rid == 0, before, pltpu.roll(u, 1, 0))
        nxt = jnp.where(rid == rows - 1, after, pltpu.roll(u, rows - 1, 0))
        return prev * w_ref[0:1] + u * w_ref[1:2] + nxt * w_ref[2:3] + b_ref[...]

    def long_conv_gate(x_ref, cw_ref, cbias_ref, hre_ref, him_ref, dn):
        def fwd(rb, carry):
            sl = block(rb)
            ur = _dot(cq_ref[sl, :], cb_ref[...])
            ui = -_dot(sq_ref[sl, :], cb_ref[...])
            hr = hre_ref[sl, :]
            hi = him_ref[sl, :]
            yr_ref[sl, :] = (ur * hr - ui * hi).astype(BF16)
            yi_ref[sl, :] = (ur * hi + ui * hr).astype(BF16)
            return carry

        lax.fori_loop(0, n_rb, fwd, 0)

        def inv(rb, carry):
            sl = block(rb)
            c = (_dot(cq_ref[sl, :], yr_ref[...]) - _dot(sq_ref[sl, :], yi_ref[...])) * inv_len
            new = short_conv(x_ref, cw_ref, cbias_ref, rb) * (c + cur_ref[sl, :] * dn)
            cur_ref[sl, :] = new
            cb_ref[sl, :] = new.astype(BF16)
            return carry

        lax.fori_loop(0, n_rb, inv, 0)

    def load_v(rb, carry):
        v = short_conv(v_ref, cwv_ref, cbv_ref, rb)
        cur_ref[block(rb), :] = v
        cb_ref[block(rb), :] = v.astype(BF16)
        return carry

    lax.fori_loop(0, n_rb, load_v, 0)
    long_conv_gate(x1_ref, cw1_ref, cb1_ref, hre0_ref, him0_ref, d_ref[0:1])
    long_conv_gate(x2_ref, cw2_ref, cb2_ref, hre1_ref, him1_ref, d_ref[1:2])

    def store(rb, carry):
        sl = block(rb)
        y_ref[sl, :] = (cur_ref[sl, :] * _silu(z_ref[sl, :])).astype(BF16)
        return carry

    lax.fori_loop(0, n_rb, store, 0)


def _hyena_mixer(proj, conv_w, conv_b, hre, him, d_skip, j, n_seq, seq, ce, name):
    t = proj.shape[0]
    width = proj.shape[1] // 4
    nb = width // ce
    cq, sq = _signal_tables(seq)
    rows = min(seq, HY_ROW_BLOCK)
    const = pl.Buffered(1)
    seg = lambda s: pl.BlockSpec((seq, ce), lambda e, b: (b, s * nb + e))
    cw = lambda s: pl.BlockSpec((None, conv_w.shape[1], ce), lambda e, b: (j, 0, s * nb + e))
    cb = lambda s: pl.BlockSpec((None, 1, ce), lambda e, b: (j, 0, s * nb + e))
    hs = lambda n: pl.BlockSpec((seq, ce), lambda e, b: (0, n * nb + e), pipeline_mode=const)
    return pl.pallas_call(
        functools.partial(_hyena_body, seq=seq, rows=rows),
        grid=(nb, n_seq),
        in_specs=[seg(0), seg(1), seg(2), seg(3), cw(0), cw(1), cw(2), cb(0), cb(1), cb(2),
                  hs(0), hs(0), hs(1), hs(1),
                  pl.BlockSpec((None, HY_ORDER, ce), lambda e, b: (j, 0, e)),
                  pl.BlockSpec((seq, seq), lambda e, b: (0, 0), pipeline_mode=const),
                  pl.BlockSpec((seq, seq), lambda e, b: (0, 0), pipeline_mode=const)],
        out_specs=pl.BlockSpec((seq, ce), lambda e, b: (b, e)),
        out_shape=jax.ShapeDtypeStruct((t, width), BF16),
        scratch_shapes=[pltpu.VMEM((seq, ce), F32), pltpu.VMEM((seq, ce), BF16),
                        pltpu.VMEM((seq, ce), BF16), pltpu.VMEM((seq, ce), BF16)],
        compiler_params=_params("parallel", "parallel"),
        name=name,
    )(proj, proj, proj, proj, conv_w, conv_w, conv_w,
      conv_b.reshape(conv_b.shape[0], 1, -1), conv_b.reshape(conv_b.shape[0], 1, -1),
      conv_b.reshape(conv_b.shape[0], 1, -1), hre, him, hre, him, d_skip, cq, sq)


def kernel(x_prompt, x_sample, state_gla, c, c_ctx, mod_w, mod_b, norm_g, final_norm_g, gla_w_in, gla_w_dec, gla_b_dec, gla_onorm_g, gla_w_out, fn_w_in, fn_w_out, hy_w_in, hy_conv_w, hy_conv_b, hy_ffn_w1, hy_ffn_b1, hy_ffn_w2, hy_ffn_b2, hy_ffn_w3, hy_ffn_b3, hy_ffn_w4, hy_freq, hy_d, hy_w_out):
    n_p, l_p, d = x_prompt.shape
    n_s, l_s, _ = x_sample.shape
    depth = mod_w.shape[0]
    key = gla_w_dec.shape[-1]
    n_main = gla_w_in.shape[-1] - 2 * GLA_RANK

    cvec = jnp.concatenate([c_ctx[None], c, jnp.zeros((MOD_ROWS - 1 - n_s, d), F32)], axis=0)
    mod4 = _modulation(cvec, mod_w, mod_b).reshape(depth, MOD_ROWS, 1, 3 * d)
    norm_g3 = norm_g.reshape(depth, 1, d)

    gla_w_main = gla_w_in.astype(BF16)
    gla_w_lr = gla_w_in[:, :, n_main:].astype(BF16)
    gla_w_out_b = gla_w_out.astype(BF16)
    fn_w_in_b = fn_w_in.astype(BF16)
    fn_w_out_b = fn_w_out.astype(BF16)
    hy_w_in_b = hy_w_in.astype(BF16)
    hy_w_out_b = hy_w_out.astype(BF16)
    emb = hy_ffn_w1.shape[1]
    emb_pad = -(-emb // 128) * 128
    hy_w1p = jnp.pad(hy_ffn_w1, ((0, 0), (0, emb_pad - emb), (0, 0)))
    ffn = hy_ffn_w2.shape[-1]
    hy_vecs = [a.reshape(a.shape[0], 1, ffn) for a in (hy_ffn_b1, hy_ffn_b2, hy_ffn_b3, hy_freq)]

    tm = 512
    tiles_per_sample = l_s // tm
    streams = {
        "p": dict(x=x_prompt.reshape(n_p * l_p, d), n=n_p, l=l_p, row=lambda i: 0),
        "s": dict(x=x_sample.reshape(n_s * l_s, d), n=n_s, l=l_s, row=lambda i: 1 + i // tiles_per_sample),
    }
    new_states = None
    for i in range(depth):
        kind, j = i % N_MIXERS, i // N_MIXERS
        final_g = final_norm_g if i == depth - 1 else None
        if kind == 2:
            filt = {}
            for tag, st in streams.items():
                ce = min(d, 256 if st["l"] > 256 else d)
                filt[tag] = _hyena_filters(st["l"], j, emb, hy_w1p, hy_vecs[0], hy_ffn_w2, hy_vecs[1], hy_ffn_w3,
                                           hy_vecs[2], hy_ffn_w4, hy_vecs[3], ce, f"hyfilt_{tag}{i}")
        for tag, st in streams.items():
            x, n_seq, seq, row = st["x"], st["n"], st["l"], st["row"]
            if kind == 0:
                proj, lr = _inproj(
                    x, norm_g3, mod4, i, row,
                    [(gla_w_main, (None, d, n_main), (j, 0, 0)), (gla_w_lr, (None, d, 2 * GLA_RANK), (j, 0, 0))],
                    tm, f"inproj_{tag}{i}")
                is_ctx = tag == "p"
                res = _gla_mixer(proj, lr, gla_w_dec, gla_b_dec, gla_onorm_g,
                                 None if is_ctx else state_gla, j, n_seq, seq, is_ctx, new_states,
                                 f"gla_{tag}{i}")
                y = res[0]
                if is_ctx:
                    new_states = res[1]
                w_out = gla_w_out_b
            elif kind == 1:
                (proj,) = _inproj(x, norm_g3, mod4, i, row,
                                  [(fn_w_in_b, (None,) + fn_w_in.shape[1:], (j, 0, 0))], tm, f"inproj_{tag}{i}")
                y = _fnet_mixer(proj, n_seq, seq, 4 if seq <= 256 else 1, f"fnet_{tag}{i}")
                w_out = fn_w_out_b
            else:
                (proj,) = _inproj(x, norm_g3, mod4, i, row,
                                  [(hy_w_in_b, (None,) + hy_w_in.shape[1:], (j, 0, 0))], tm, f"inproj_{tag}{i}")
                ce = min(d, 256 if seq > 256 else d)
                hre, him = filt[tag]
                y = _hyena_mixer(proj, hy_conv_w, hy_conv_b, hre, him, hy_d, j, n_seq, seq, ce, f"hyena_{tag}{i}")
                w_out = hy_w_out_b
            st["x"] = _outproj(y, w_out, j, x, mod4, i, row, final_g, tm, f"outproj_{tag}{i}")
    y_prompt = streams["p"]["x"].reshape(n_p, l_p, d)
    y_sample = streams["s"]["x"].reshape(n_s, l_s, d)
    new_state_gla = new_states.astype(x_prompt.dtype)
    return (y_prompt, y_sample, new_state_gla)
```

```python
import functools
import math

import jax
import jax.numpy as jnp
import numpy as np
from jax import lax
from jax.experimental import pallas as pl
from jax.experimental.pallas import tpu as pltpu

F32 = jnp.float32
BF16 = jnp.bfloat16

EPS = 1e-6
N_MIXERS = 3
GLA_HEADS = 4
GLA_RANK = 16
GLA_GATE_NORM = 16.0
FN_GROUPS = 4
HY_ORDER = 2
HY_TARGET = 1e-2
HY_MIN_DECAY = math.log(HY_TARGET) / 1.5
HY_MAX_DECAY = math.log(HY_TARGET) / 0.3
HY_ROW_BLOCK = 512
HY_SPLIT_MIN_SEQ = 512

V7X_VMEM_BYTES = 64 * 1024 * 1024
VMEM_LIMIT_BYTES = V7X_VMEM_BYTES - 8 * 1024 * 1024
MOD_ROWS = 8

_NT = (((1,), (1,)), ((), ()))
_TN = (((0,), (0,)), ((), ()))


def _params(*sem):
    return pltpu.CompilerParams(dimension_semantics=sem, vmem_limit_bytes=VMEM_LIMIT_BYTES)


def _dot(a, b):
    return jnp.dot(a, b, preferred_element_type=F32)


def _split3(x):
    hi = x.astype(BF16)
    r1 = x - hi.astype(F32)
    mid = r1.astype(BF16)
    lo = (r1 - mid.astype(F32)).astype(BF16)
    return hi, mid, lo


def _dot3(a, b):
    ah = a.astype(BF16)
    al = (a - ah.astype(F32)).astype(BF16)
    bh = b.astype(BF16)
    bl = (b - bh.astype(F32)).astype(BF16)
    return _dot(ah, bh) + (_dot(ah, bl) + _dot(al, bh))


def _dot_exact_lhs(a_bf16, b):
    hi, mid, lo = _split3(b)
    return _dot(a_bf16, hi) + (_dot(a_bf16, mid) + _dot(a_bf16, lo))


def _trig_bf16(ang):
    return (jnp.asarray(np.cos(ang), F32).astype(BF16), jnp.asarray(np.sin(ang), F32).astype(BF16))


def _silu(x):
    return x / (1.0 + jnp.exp(-x))


def _log_sigmoid(x):
    return jnp.minimum(x, 0.0) - jnp.log(1.0 + jnp.exp(-jnp.abs(x)))


def _mod_body(c_ref, w_ref, b_ref, o_ref):
    o_ref[...] = _dot3(_silu(c_ref[...]), w_ref[...]) + b_ref[...]


def _modulation(cvec, mod_w, mod_b):
    depth, d, e = mod_w.shape
    tn = d
    return pl.pallas_call(
        _mod_body,
        grid=(depth, e // tn),
        in_specs=[
            pl.BlockSpec((MOD_ROWS, d), lambda l, j: (0, 0)),
            pl.BlockSpec((None, d, tn), lambda l, j: (l, 0, j)),
            pl.BlockSpec((None, 1, tn), lambda l, j: (l, 0, j)),
        ],
        out_specs=pl.BlockSpec((None, MOD_ROWS, tn), lambda l, j: (l, 0, j)),
        out_shape=jax.ShapeDtypeStruct((depth, MOD_ROWS, e), F32),
        compiler_params=_params("parallel", "parallel"),
        name="modulation",
    )(cvec, mod_w, mod_b.reshape(depth, 1, e))


def _inproj_body(x_ref, g_ref, mod_ref, *refs):
    n_w = len(refs) // 2
    x = x_ref[...]
    d = x.shape[-1]
    h = x * lax.rsqrt(jnp.mean(x * x, axis=-1, keepdims=True) + EPS) * g_ref[...]
    mod = mod_ref[...]
    hb = (h * (1.0 + mod[:, d:2 * d]) + mod[:, :d]).astype(BF16)
    for w_ref, o_ref in zip(refs[:n_w], refs[n_w:]):
        o_ref[...] = _dot(hb, w_ref[...])


def _inproj(x, norm_g3, mod4, layer, mod_row, weights, tm, name):
    t, d = x.shape
    w_specs = [pl.BlockSpec(bs, functools.partial(lambda idx, i: idx, idx)) for _, bs, idx in weights]
    widths = [bs[-1] for _, bs, _ in weights]
    return pl.pallas_call(
        _inproj_body,
        grid=(t // tm,),
        in_specs=[
            pl.BlockSpec((tm, d), lambda i: (i, 0)),
            pl.BlockSpec((None, 1, d), lambda i: (layer, 0, 0)),
            pl.BlockSpec((None, None, 1, mod4.shape[-1]), lambda i: (layer, mod_row(i), 0, 0)),
        ] + w_specs,
        out_specs=[pl.BlockSpec((tm, n), lambda i: (i, 0)) for n in widths],
        out_shape=[jax.ShapeDtypeStruct((t, n), F32) for n in widths],
        compiler_params=_params("parallel"),
        name=name,
    )(x, norm_g3, mod4, *[w for w, _, _ in weights])


def _outproj_body(*refs, n_phase, final):
    y_refs = refs[:n_phase]
    w_ref, x_ref, mod_ref = refs[n_phase:n_phase + 3]
    fg_ref = refs[n_phase + 3] if final else None
    out_ref = refs[-1]
    d = w_ref.shape[-1]
    gate = mod_ref[...][:, 2 * d:]
    for p, y_ref in enumerate(y_refs):
        cols = slice(p * d, (p + 1) * d)
        xn = x_ref[:, cols] + gate * _dot(y_ref[...], w_ref[...])
        if final:
            xn = xn * lax.rsqrt(jnp.mean(xn * xn, axis=-1, keepdims=True) + EPS) * fg_ref[...]
        out_ref[:, cols] = xn


def _outproj(ys, w_out, j, x, mod4, layer, mod_row, final_g, tm, name):
    t, d = x.shape
    n = len(ys)
    final = final_g is not None
    rows = tm // n
    in_specs = [pl.BlockSpec((rows, y.shape[-1]), lambda i: (i, 0)) for y in ys] + [
        pl.BlockSpec((None,) + w_out.shape[1:], lambda i: (j, 0, 0)),
        pl.BlockSpec((rows, n * d), lambda i: (i, 0)),
        pl.BlockSpec((None, None, 1, mod4.shape[-1]), lambda i: (layer, mod_row(i), 0, 0)),
    ]
    args = [*ys, w_out, x.reshape(t // n, n * d), mod4]
    if final:
        in_specs.append(pl.BlockSpec((1, d), lambda i: (0, 0)))
        args.append(final_g.reshape(1, d))
    return pl.pallas_call(
        functools.partial(_outproj_body, n_phase=n, final=final),
        grid=(t // tm,),
        in_specs=in_specs,
        out_specs=pl.BlockSpec((rows, n * d), lambda i: (i, 0)),
        out_shape=jax.ShapeDtypeStruct((t // n, n * d), F32),
        compiler_params=_params("parallel"),
        name=name,
    )(*args).reshape(t, d)


GLA_CHUNK = 64
GLA_BLOCK = 256
GLA_HEADS_PER_STEP = 2
GLA_CLAMP_SPLIT = 8.0
GLA_CLAMP_HALF = 80.0


def _gla_tables(rows, dk):
    c, hc = GLA_CHUNK, GLA_CHUNK // 2
    r = np.arange(rows)[:, None]
    s = np.arange(rows)[None, :]
    same_chunk = (r // c) == (s // c)
    same_half = (r // hc) == (s // hc)
    before = [s <= r, s >= r]
    tri = np.stack([same_chunk & b for b in before])
    m_half = np.stack([same_half & b for b in before])
    sel = np.concatenate([s == (r // c) * c + hc, s == (r // hc) * hc + hc // 2], axis=0)
    second = np.broadcast_to((r // hc) % 2 == 1, (rows, dk))
    as_const = lambda m, dt: jnp.asarray(m.astype(np.float32), dt)
    return (as_const(tri, BF16), as_const(sel, BF16), as_const(same_chunk, F32), as_const(m_half, F32),
            as_const(second, F32))


def _gla_body(q_ref, k_ref, v_ref, r_ref, lr_ref, wdec_ref, bdec_ref, og_ref, tri_ref, sel_ref,
              mchunk_ref, mhalf_ref, second_ref, *rest, seq, layer, has_s0, want_state, has_states_in):
    rest = list(rest)
    s0_ref = rest.pop(0) if has_s0 else None
    if has_states_in:
        rest.pop(0)
    y_ref = rest.pop(0)
    sfin_ref = rest.pop(0) if want_state else None
    o_ref, qi_ref, cum_ref, st_ref = rest
    chunk = GLA_CHUNK
    n_chunks = seq // chunk
    blk = tri_ref.shape[-1]
    n_heads = wdec_ref.shape[0]
    dk = q_ref.shape[-1] // n_heads
    dv = v_ref.shape[-1] // n_heads
    scale = dk ** -0.5
    heads = range(n_heads)
    dirs = range(2)
    chains = [(hh, d) for hh in heads for d in dirs]

    def nt(a, b):
        return lax.dot_general(a, b, _NT, preferred_element_type=F32)

    def cols(hh, width):
        return slice(hh * width, (hh + 1) * width)

    def block_pass(bi, carry):
        rows = pl.ds(pl.multiple_of(bi * blk, blk), blk)
        q = [q_ref[rows, cols(hh, dk)] * scale for hh in heads]
        k = [k_ref[rows, cols(hh, dk)] for hh in heads]
        lr = lr_ref[rows, :]
        lr_hi = lr.astype(BF16)
        lr_lo = (lr - lr_hi.astype(F32)).astype(BF16)
        g = []
        for hh in heads:
            w = wdec_ref[hh]
            w_hi = w.astype(BF16)
            w_lo = (w - w_hi.astype(F32)).astype(BF16)
            logit = _dot(lr_hi, w_hi) + (_dot(lr_hi, w_lo) + _dot(lr_lo, w_hi)) + bdec_ref[hh]
            g.append(_log_sigmoid(logit) * (1.0 / GLA_GATE_NORM))
        cum = {(hh, d): _dot_exact_lhs(tri_ref[d], g[hh][:, cols(d, dk)]) for hh, d in chains}
        refs = [_dot(sel_ref[...], jnp.concatenate([cum[hh, 0], cum[hh, 1]], axis=1).astype(BF16))
                for hh in heads]
        second = second_ref[...]
        first = 1.0 - second
        p_split, p_half = [], {}
        for hh in heads:
            q_in = (q[hh] * second, q[hh] * first)
            k_in = (k[hh] * first, k[hh] * second)
            q1, k1 = [], []
            for d in dirs:
                c = (hh, d)
                d_split = cum[c] - refs[hh][:blk, cols(d, dk)]
                d_half = cum[c] - refs[hh][blk:, cols(d, dk)]
                q1.append((q_in[d] * jnp.exp(jnp.minimum(d_split, GLA_CLAMP_SPLIT))).astype(BF16))
                k1.append((k_in[d] * jnp.exp(jnp.minimum(-d_split, GLA_CLAMP_SPLIT))).astype(BF16))
                q0 = (q[hh] * jnp.exp(jnp.minimum(d_half, GLA_CLAMP_HALF))).astype(BF16)
                k0 = (k[hh] * jnp.exp(jnp.minimum(-d_half, GLA_CLAMP_HALF))).astype(BF16)
                p_half[c] = nt(q0, k0)
                qi_ref[hh, d, rows, :] = (q[hh] * jnp.exp(cum[c])).astype(BF16)
                cum_ref[hh, d, rows, :] = cum[c]
            p_split.append(nt(jnp.concatenate(q1, axis=1), jnp.concatenate(k1, axis=1)))
        for hh in heads:
            att = p_split[hh] * mchunk_ref[...]
            for d in dirs:
                att = att + jnp.where(mhalf_ref[d] > 0.5, p_half[hh, d], 0.0)
            o_ref[rows, cols(hh, dv)] = _dot(att.astype(BF16), v_ref[rows, cols(hh, dv)].astype(BF16))
        return carry

    lax.fori_loop(0, seq // blk, block_pass, 0)

    for hh, d in chains:
        st_ref[hh, d] = s0_ref[d, hh] if has_s0 else jnp.zeros((dk, dv), F32)

    def chunk_step(i, carry):
        for hh, d in chains:
            c = i if d == 0 else n_chunks - 1 - i
            sl = pl.ds(pl.multiple_of(c * chunk, chunk), chunk)
            cum = cum_ref[hh, d, sl, :]
            blast = cum[chunk - 1:chunk] if d == 0 else cum[0:1]
            kh = (k_ref[sl, cols(hh, dk)] * jnp.exp(blast - cum)).astype(BF16)
            vb = v_ref[sl, cols(hh, dv)].astype(BF16)
            st = st_ref[hh, d]
            o_ref[sl, cols(hh, dv)] += _dot(qi_ref[hh, d, sl, :], st.astype(BF16))
            decay = jnp.broadcast_to(jnp.exp(blast), (dk, dk)).T
            st_ref[hh, d] = (jnp.concatenate([decay] * (dv // dk), axis=1) * st
                             + lax.dot_general(kh, vb, _TN, preferred_element_type=F32))
        return carry

    lax.fori_loop(0, n_chunks, chunk_step, 0, unroll=min(n_chunks, 4))

    def finish(jb, carry):
        sl = pl.ds(pl.multiple_of(jb * blk, blk), blk)
        for hh in heads:
            o = o_ref[sl, cols(hh, dv)]
            o = o * lax.rsqrt(jnp.mean(o * o, axis=-1, keepdims=True) + EPS) * og_ref[...]
            y_ref[sl, cols(hh, dv)] = (o * _silu(r_ref[sl, cols(hh, dv)])).astype(BF16)
        return carry

    lax.fori_loop(0, seq // blk, finish, 0)
    if want_state:
        own = sfin_ref if has_states_in else sfin_ref.at[layer]
        for hh, d in chains:
            own[d, hh] = st_ref[hh, d]
        if not has_states_in:
            for other in range(sfin_ref.shape[0]):
                if other != layer:
                    sfin_ref[other] = jnp.zeros(sfin_ref.shape[1:], F32)


def _gla_mixer(proj, lr, w_dec, b_dec, onorm_g, s0, j, n_seq, seq, want_state, states, name):
    t = proj.shape[0]
    h = GLA_HEADS
    key = w_dec.shape[-1]
    dk = key // h
    val = (proj.shape[1] - 2 * key) // 2
    dv = val // h
    has_s0 = s0 is not None
    tables = _gla_tables(min(seq, GLA_BLOCK), dk)
    fixed = lambda a: pl.BlockSpec(a.shape, lambda b, hp: (0,) * a.ndim)
    hps = GLA_HEADS_PER_STEP
    wk, wv = hps * dk, hps * dv
    in_specs = [
        pl.BlockSpec((seq, wk), lambda b, hp: (b, hp)),
        pl.BlockSpec((seq, wk), lambda b, hp: (b, key // wk + hp)),
        pl.BlockSpec((seq, wv), lambda b, hp: (b, (2 * key) // wv + hp)),
        pl.BlockSpec((seq, wv), lambda b, hp: (b, (2 * key + val) // wv + hp)),
        pl.BlockSpec((seq, 2 * GLA_RANK), lambda b, hp: (b, 0)),
        pl.BlockSpec((None, hps, 2 * GLA_RANK, 2 * dk), lambda b, hp: (j, hp, 0, 0)),
        pl.BlockSpec((None, hps, 1, 2 * dk), lambda b, hp: (j, hp, 0, 0)),
        pl.BlockSpec((None, 1, dv), lambda b, hp: (j, 0, 0)),
    ] + [fixed(a) for a in tables]
    n_gla = w_dec.shape[0]
    w5 = w_dec.reshape(n_gla, 2, GLA_RANK, h, dk)
    zero = jnp.zeros_like(w5[:, 0])
    w_blk = jnp.concatenate([jnp.concatenate([w5[:, 0], zero], axis=-1),
                             jnp.concatenate([zero, w5[:, 1]], axis=-1)], axis=1).transpose(0, 2, 1, 3)
    b_cat = b_dec.reshape(n_gla, 2, h, dk).transpose(0, 2, 1, 3).reshape(n_gla, h, 1, 2 * dk)
    args = [proj, proj, proj, proj, lr, w_blk, b_cat, onorm_g.reshape(n_gla, 1, dv), *tables]
    if has_s0:
        in_specs.append(pl.BlockSpec((None, None, 2, hps, dk, dv), lambda b, hp: (b, j, 0, hp, 0, 0)))
        args.append(s0)
    out_specs = [pl.BlockSpec((seq, wv), lambda b, hp: (b, hp))]
    out_shape = [jax.ShapeDtypeStruct((t, val), BF16)]
    aliases = {}
    if want_state:
        out_shape.append(jax.ShapeDtypeStruct((n_seq, n_gla, 2, h, dk, dv), F32))
        if states is None:
            out_specs.append(pl.BlockSpec((None, n_gla, 2, hps, dk, dv), lambda b, hp: (b, 0, 0, hp, 0, 0)))
        else:
            out_specs.append(pl.BlockSpec((None, None, 2, hps, dk, dv), lambda b, hp: (b, j, 0, hp, 0, 0)))
            in_specs.append(pl.BlockSpec(memory_space=pl.ANY))
            args.append(states)
            aliases = {len(args) - 1: 1}
    return pl.pallas_call(
        functools.partial(_gla_body, seq=seq, layer=j, has_s0=has_s0, want_state=want_state,
                          has_states_in=bool(aliases)),
        grid=(n_seq, h // hps),
        in_specs=in_specs,
        out_specs=out_specs,
        out_shape=out_shape,
        input_output_aliases=aliases,
        scratch_shapes=[pltpu.VMEM((seq, wv), F32), pltpu.VMEM((hps, 2, seq, dk), BF16),
                        pltpu.VMEM((hps, 2, seq, dk), F32), pltpu.VMEM((hps, 2, dk, dv), F32)],
        compiler_params=_params("parallel", "parallel"),
        name=name,
    )(*args)


def _dft_tables(n):
    f = np.arange(n, dtype=np.int64)
    ang = (np.outer(f, f) % n).astype(np.float64) * (2.0 * math.pi / n)
    return _trig_bf16(ang)


def _fnet_body(x_ref, z_ref, cc_ref, sc_ref, cl_ref, sl_ref, y_ref, *, seq, n_seq):
    xb = x_ref[...].astype(BF16)
    a = _dot(xb, cc_ref[...]).astype(BF16)
    bm = _dot(xb, sc_ref[...]).astype(BF16)
    inv = 1.0 / math.sqrt(seq * x_ref.shape[-1])
    for i in range(n_seq):
        rows = slice(i * seq, (i + 1) * seq)
        f = _dot(cl_ref[...], a[rows]) - _dot(sl_ref[...], bm[rows])
        y_ref[rows, :] = (f * inv * _silu(z_ref[rows, :])).astype(BF16)


def _fnet_mixer(proj, n_seq_total, seq, seqs_per_step, name):
    t = proj.shape[0]
    width = proj.shape[1] // 2
    gc = width // FN_GROUPS
    cc, sc = _dft_tables(gc)
    cl, sl = _dft_tables(seq)
    rows = seq * seqs_per_step
    const = pl.Buffered(1)
    return pl.pallas_call(
        functools.partial(_fnet_body, seq=seq, n_seq=seqs_per_step),
        grid=(n_seq_total // seqs_per_step, FN_GROUPS),
        in_specs=[
            pl.BlockSpec((rows, gc), lambda i, g: (i, g)),
            pl.BlockSpec((rows, gc), lambda i, g: (i, FN_GROUPS + g)),
            pl.BlockSpec((gc, gc), lambda i, g: (0, 0), pipeline_mode=const),
            pl.BlockSpec((gc, gc), lambda i, g: (0, 0), pipeline_mode=const),
            pl.BlockSpec((seq, seq), lambda i, g: (0, 0), pipeline_mode=const),
            pl.BlockSpec((seq, seq), lambda i, g: (0, 0), pipeline_mode=const),
        ],
        out_specs=pl.BlockSpec((rows, gc), lambda i, g: (i, g)),
        out_shape=jax.ShapeDtypeStruct((t, width), BF16),
        compiler_params=_params("parallel", "parallel"),
        name=name,
    )(proj, proj, cc, sc, cl, sl)


def _signal_tables(n):
    o = 2 * np.arange(n, dtype=np.int64) + 1
    ang = (np.outer(o, o) % (8 * n)).astype(np.float64) * (math.pi / (4 * n))
    return _trig_bf16(ang)


def _filter_tables(n, split):
    f = np.arange(n, dtype=np.int64)
    if split:
        f = np.concatenate([f[:n // 2], f[:n // 2 - 1:-1]])
    f = 2 * f + 1
    o = np.arange(n, dtype=np.int64)
    ang = (np.outer(f, o) % (4 * n)).astype(np.float64) * (math.pi / (2 * n))
    return _trig_bf16(ang)


def _hyena_positions(seq, bands, pad_to):
    t = np.linspace(0.0, 1.0, seq)[:, None]
    w = 2.0 * math.pi * np.arange(seq)[:, None] / seq
    f = np.linspace(1e-4, bands - 1, bands)[None, :]
    zpos = np.concatenate([t, np.cos(f * w), -np.sin(f * w)], axis=-1)
    zpos = np.pad(zpos, ((0, 0), (0, pad_to - zpos.shape[1])))
    return jnp.asarray(zpos, F32), jnp.asarray(t, F32)


def _hyfilt_body(zpos_ref, w1_ref, b1_ref, w2_ref, b2_ref, w3_ref, b3_ref, fr_ref, w4f_ref,
                 w4b_ref, t_ref, del_ref, ch_ref, sh_ref, *rest, split):
    if split:
        c2_ref, s2_ref = rest[:2]
        out_refs, a3_ref = rest[2:-1], rest[-1]
    else:
        out_refs, a3_ref = rest[:-1], rest[-1]

    @pl.when(pl.program_id(0) == 0)
    def _():
        fr = fr_ref[...]
        a = jnp.sin(fr * (_dot3(zpos_ref[...], w1_ref[...]) + b1_ref[...]))
        a = jnp.sin(fr * (_dot3(a, w2_ref[...]) + b2_ref[...]))
        a3_ref[...] = jnp.sin(fr * (_dot3(a, w3_ref[...]) + b3_ref[...]))

    a3 = a3_ref[...]
    dec = jnp.exp(-t_ref[...] * del_ref[...])
    fwd = _dot3(a3, w4f_ref[...]) * dec
    bwd = _dot3(a3, w4b_ref[...]) * dec
    first = lax.broadcasted_iota(jnp.int32, bwd.shape, 0) == 0
    bwd = jnp.where(first, 0.0, bwd)
    hre = _dot(ch_ref[...], (fwd + bwd).astype(BF16))
    him = _dot(sh_ref[...], (bwd - fwd).astype(BF16))
    if not split:
        out_refs[0][...] = hre
        out_refs[1][...] = him
        return
    half = hre.shape[0] // 2
    plus_r, plus_i = hre[:half] + hre[half:], him[:half] - him[half:]
    minus_r, minus_i = hre[:half] - hre[half:], him[:half] + him[half:]
    c2, s2 = c2_ref[...], s2_ref[...]
    for ref, val in zip(out_refs, (plus_r, plus_i,
                                   c2 * minus_r + s2 * minus_i, c2 * minus_i - s2 * minus_r,
                                   c2 * minus_r - s2 * minus_i, c2 * minus_i + s2 * minus_r)):
        ref[...] = val


def _hyena_filters(seq, j, emb, w1p, b1, w2, b2, w3, b3, w4, freq, ce, split, name):
    ffn = w2.shape[-1]
    width = w4.shape[-1] // (2 * HY_ORDER)
    emb_pad = w1p.shape[1]
    zpos, tcol = _hyena_positions(seq, (emb - 1) // 2, emb_pad)
    deltas = jnp.asarray(np.abs(np.linspace(HY_MIN_DECAY, HY_MAX_DECAY, width, dtype=np.float32))[None, :])
    ch, sh = _filter_tables(seq, split)
    nb = width // ce
    const = pl.Buffered(1)
    vec = lambda: pl.BlockSpec((None, 1, ffn), lambda c: (j, 0, 0))
    out_rows, n_out = (seq // 2, 6) if split else (seq, 2)
    twiddles, twiddle_specs = [], []
    if split:
        ang = np.broadcast_to((math.pi * (2 * np.arange(out_rows) + 1) / (2 * seq))[:, None], (out_rows, ce))
        twiddles = [jnp.asarray(np.cos(ang), F32), jnp.asarray(np.sin(ang), F32)]
        twiddle_specs = [pl.BlockSpec((out_rows, ce), lambda c: (0, 0), pipeline_mode=const)] * 2
    return pl.pallas_call(
        functools.partial(_hyfilt_body, split=split),
        grid=(HY_ORDER * nb,),
        in_specs=[
            pl.BlockSpec((seq, emb_pad), lambda c: (0, 0)),
            pl.BlockSpec((None, emb_pad, ffn), lambda c: (j, 0, 0)),
            vec(),
            pl.BlockSpec((None, ffn, ffn), lambda c: (j, 0, 0)),
            vec(),
            pl.BlockSpec((None, ffn, ffn), lambda c: (j, 0, 0)),
            vec(),
            vec(),
            pl.BlockSpec((None, ffn, ce), lambda c: (j, 0, (c // nb) * 2 * nb + c % nb)),
            pl.BlockSpec((None, ffn, ce), lambda c: (j, 0, (c // nb) * 2 * nb + nb + c % nb)),
            pl.BlockSpec((seq, 1), lambda c: (0, 0)),
            pl.BlockSpec((1, ce), lambda c: (0, c % nb)),
            pl.BlockSpec((seq, seq), lambda c: (0, 0), pipeline_mode=const),
            pl.BlockSpec((seq, seq), lambda c: (0, 0), pipeline_mode=const),
        ] + twiddle_specs,
        out_specs=[pl.BlockSpec((out_rows, ce), lambda c: (0, c))] * n_out,
        out_shape=[jax.ShapeDtypeStruct((out_rows, HY_ORDER * width), F32)] * n_out,
        scratch_shapes=[pltpu.VMEM((seq, ffn), F32)],
        compiler_params=_params("arbitrary"),
        name=name,
    )(zpos, w1p, b1, w2, b2, w3, b3, freq, w4, w4, tcol, deltas, ch, sh, *twiddles)


def _hyena_body(x1_ref, x2_ref, v_ref, z_ref, cw1_ref, cw2_ref, cwv_ref, cb1_ref, cb2_ref, cbv_ref,
                hre0_ref, him0_ref, hre1_ref, him1_ref, d_ref, cq_ref, sq_ref, y_ref,
                cur_ref, cb_ref, yr_ref, yi_ref, *, seq, rows):
    n_rb = seq // rows
    inv_len = 1.0 / seq
    rid = lax.broadcasted_iota(jnp.int32, (rows, 1), 0)

    def block(rb):
        return pl.ds(pl.multiple_of(rb * rows, rows), rows)

    def short_conv(u_ref, w_ref, b_ref, rb):
        u = u_ref[block(rb), :]
        before = u_ref[pl.ds(jnp.maximum(rb * rows - 1, 0), 1), :]
        after = u_ref[pl.ds(jnp.minimum((rb + 1) * rows, seq - 1), 1), :]
        before = jnp.where(rb == 0, 0.0, before)
        after = jnp.where(rb == n_rb - 1, 0.0, after)
        prev = jnp.where(rid == 0, before, pltpu.roll(u, 1, 0))
        nxt = jnp.where(rid == rows - 1, after, pltpu.roll(u, rows - 1, 0))
        return prev * w_ref[0:1] + u * w_ref[1:2] + nxt * w_ref[2:3] + b_ref[...]

    def long_conv_gate(x_ref, cw_ref, cbias_ref, hre_ref, him_ref, dn):
        def fwd(rb, carry):
            sl = block(rb)
            ur = _dot(cq_ref[sl, :], cb_ref[...])
            ui = -_dot(sq_ref[sl, :], cb_ref[...])
            hr = hre_ref[sl, :]
            hi = him_ref[sl, :]
            yr_ref[sl, :] = (ur * hr - ui * hi).astype(BF16)
            yi_ref[sl, :] = (ur * hi + ui * hr).astype(BF16)
            return carry

        lax.fori_loop(0, n_rb, fwd, 0)

        def inv(rb, carry):
            sl = block(rb)
            c = (_dot(cq_ref[sl, :], yr_ref[...]) - _dot(sq_ref[sl, :], yi_ref[...])) * inv_len
            new = short_conv(x_ref, cw_ref, cbias_ref, rb) * (c + cur_ref[sl, :] * dn)
            cur_ref[sl, :] = new
            cb_ref[sl, :] = new.astype(BF16)
            return carry

        lax.fori_loop(0, n_rb, inv, 0)

    def load_v(rb, carry):
        v = short_conv(v_ref, cwv_ref, cbv_ref, rb)
        cur_ref[block(rb), :] = v
        cb_ref[block(rb), :] = v.astype(BF16)
        return carry

    lax.fori_loop(0, n_rb, load_v, 0)
    long_conv_gate(x1_ref, cw1_ref, cb1_ref, hre0_ref, him0_ref, d_ref[0:1])
    long_conv_gate(x2_ref, cw2_ref, cb2_ref, hre1_ref, him1_ref, d_ref[1:2])

    def store(rb, carry):
        sl = block(rb)
        y_ref[sl, :] = (cur_ref[sl, :] * _silu(z_ref[sl, :])).astype(BF16)
        return carry

    lax.fori_loop(0, n_rb, store, 0)


def _hyena_split_body(x1_refs, x2_refs, v_refs, z_refs, cw1_ref, cw2_ref, cwv_ref, cb1_ref, cb2_ref, cbv_ref,
                      k_refs, d_ref, cq_ref, sq_ref, y_refs, cur_ref, sig_ref, z_buf, *, seq, rows):
    half = seq // 2
    n_rb = half // rows
    inv_len = 1.0 / seq
    rid = lax.broadcasted_iota(jnp.int32, (rows, 1), 0)

    def block(rb, shift=0):
        return slice(rb * rows + shift, (rb + 1) * rows + shift)

    def short_conv(u_refs, w_ref, b_ref, rb):
        e_ref, o_ref = u_refs
        even, odd = e_ref[block(rb), :], o_ref[block(rb), :]
        odd_before = o_ref[block(rb, -1), :] if rb > 0 else jnp.where(rid == 0, 0.0, pltpu.roll(odd, 1, 0))
        even_after = (e_ref[block(rb, 1), :] if rb < n_rb - 1
                      else jnp.where(rid == rows - 1, 0.0, pltpu.roll(even, rows - 1, 0)))
        w0, w1, w2, b = w_ref[0:1], w_ref[1:2], w_ref[2:3], b_ref[...]
        return (odd_before * w0 + even * w1 + odd * w2 + b, even * w0 + odd * w1 + even_after * w2 + b)

    def long_conv_gate(x_ref, cw_ref, cbias_ref, k, dn):
        k1r, k1i, k2r, k2i, k3r, k3i = k
        for fb in range(n_rb):
            sl = block(fb)
            c, s = cq_ref[sl, :], sq_ref[sl, :]
            er, ei = _dot(c, sig_ref[0]), -_dot(s, sig_ref[0])
            orr, oi = _dot(c, sig_ref[1]), -_dot(s, sig_ref[1])
            a1r, a1i, a2r, a2i, a3r, a3i = (r[sl, :] for r in (k1r, k1i, k2r, k2i, k3r, k3i))
            z_buf[0, sl, :] = (er * a1r - ei * a1i + orr * a2r - oi * a2i).astype(BF16)
            z_buf[1, sl, :] = (er * a1i + ei * a1r + orr * a2i + oi * a2r).astype(BF16)
            z_buf[2, sl, :] = (er * a3r - ei * a3i + orr * a1r - oi * a1i).astype(BF16)
            z_buf[3, sl, :] = (er * a3i + ei * a3r + orr * a1i + oi * a1r).astype(BF16)
        for tb in range(n_rb):
            sl = block(tb)
            c, s = cq_ref[sl, :], sq_ref[sl, :]
            conv = ((_dot(c, z_buf[0]) - _dot(s, z_buf[1])) * inv_len,
                    (_dot(c, z_buf[2]) - _dot(s, z_buf[3])) * inv_len)
            gate = short_conv(x_ref, cw_ref, cbias_ref, tb)
            for p in range(2):
                new = gate[p] * (conv[p] + cur_ref[p, sl, :] * dn)
                cur_ref[p, sl, :] = new
                sig_ref[p, sl, :] = new.astype(BF16)

    for rb in range(n_rb):
        for p, val in enumerate(short_conv(v_refs, cwv_ref, cbv_ref, rb)):
            cur_ref[p, block(rb), :] = val
            sig_ref[p, block(rb), :] = val.astype(BF16)
    long_conv_gate(x1_refs, cw1_ref, cb1_ref, k_refs[0], d_ref[0:1])
    long_conv_gate(x2_refs, cw2_ref, cb2_ref, k_refs[1], d_ref[1:2])
    for rb in range(n_rb):
        for p in range(2):
            sl = block(rb)
            y_refs[p][sl, :] = (cur_ref[p, sl, :] * _silu(z_refs[p][sl, :])).astype(BF16)


def _hyena_split_entry(*refs, seq, rows):
    it = iter(refs)
    take = lambda n: tuple(next(it) for _ in range(n))
    x1, x2, v, z = take(2), take(2), take(2), take(2)
    conv_refs = take(6)
    k = tuple(take(6) for _ in range(HY_ORDER))
    d, cq, sq = take(3)
    y = take(2)
    _hyena_split_body(x1, x2, v, z, *conv_refs, k, d, cq, sq, y, *it, seq=seq, rows=rows)


def _hyena_mixer(proj, conv_w, conv_b, filt, d_skip, j, n_seq, seq, ce, split, name):
    t = proj.shape[0]
    width = proj.shape[1] // 4
    nb = width // ce
    n_dft = seq // 2 if split else seq
    cq, sq = _signal_tables(n_dft)
    rows = min(n_dft, HY_ROW_BLOCK)
    const = pl.Buffered(1)
    seg = lambda s: pl.BlockSpec((seq, ce), lambda e, b: (b, s * nb + e))
    cw = lambda s: pl.BlockSpec((None, conv_w.shape[1], ce), lambda e, b: (j, 0, s * nb + e))
    cb = lambda s: pl.BlockSpec((None, 1, ce), lambda e, b: (j, 0, s * nb + e))
    hs = lambda n: pl.BlockSpec((n_dft, ce), lambda e, b: (0, n * nb + e), pipeline_mode=const)
    conv_b3 = conv_b.reshape(conv_b.shape[0], 1, -1)
    if split:
        proj = proj.reshape(t // 2, 2 * proj.shape[1])
        body = functools.partial(_hyena_split_entry, seq=seq, rows=rows)
        phase = lambda s, p: pl.BlockSpec((n_dft, ce), lambda e, b: (b, p * 4 * nb + s * nb + e))
        x_specs = [phase(s, p) for s in range(4) for p in range(2)]
        out_specs = [pl.BlockSpec((n_dft, ce), lambda e, b: (b, e))] * 2
        out_shape = [jax.ShapeDtypeStruct((t // 2, width), BF16)] * 2
        scratch = [pltpu.VMEM((2, n_dft, ce), F32), pltpu.VMEM((2, n_dft, ce), BF16),
                   pltpu.VMEM((4, n_dft, ce), BF16)]
    else:
        body = functools.partial(_hyena_body, seq=seq, rows=rows)
        x_specs = [seg(0), seg(1), seg(2), seg(3)]
        out_specs = pl.BlockSpec((seq, ce), lambda e, b: (b, e))
        out_shape = jax.ShapeDtypeStruct((t, width), BF16)
        scratch = [pltpu.VMEM((seq, ce), F32), pltpu.VMEM((seq, ce), BF16),
                   pltpu.VMEM((seq, ce), BF16), pltpu.VMEM((seq, ce), BF16)]
    return pl.pallas_call(
        body,
        grid=(nb, n_seq),
        in_specs=x_specs + [cw(0), cw(1), cw(2), cb(0), cb(1), cb(2)]
        + [hs(n) for n in range(HY_ORDER) for _ in filt]
        + [pl.BlockSpec((None, HY_ORDER, ce), lambda e, b: (j, 0, e)),
           pl.BlockSpec((n_dft, n_dft), lambda e, b: (0, 0), pipeline_mode=const),
           pl.BlockSpec((n_dft, n_dft), lambda e, b: (0, 0), pipeline_mode=const)],
        out_specs=out_specs,
        out_shape=out_shape,
        scratch_shapes=scratch,
        compiler_params=_params("parallel", "parallel"),
        name=name,
    )(*([proj] * len(x_specs)), conv_w, conv_w, conv_w, conv_b3, conv_b3, conv_b3,
      *(list(filt) * HY_ORDER), d_skip, cq, sq)


def kernel(x_prompt, x_sample, state_gla, c, c_ctx, mod_w, mod_b, norm_g, final_norm_g, gla_w_in, gla_w_dec, gla_b_dec, gla_onorm_g, gla_w_out, fn_w_in, fn_w_out, hy_w_in, hy_conv_w, hy_conv_b, hy_ffn_w1, hy_ffn_b1, hy_ffn_w2, hy_ffn_b2, hy_ffn_w3, hy_ffn_b3, hy_ffn_w4, hy_freq, hy_d, hy_w_out):
    n_p, l_p, d = x_prompt.shape
    n_s, l_s, _ = x_sample.shape
    depth = mod_w.shape[0]
    key = gla_w_dec.shape[-1]
    n_main = gla_w_in.shape[-1] - 2 * GLA_RANK

    cvec = jnp.concatenate([c_ctx[None], c, jnp.zeros((MOD_ROWS - 1 - n_s, d), F32)], axis=0)
    mod4 = _modulation(cvec, mod_w, mod_b).reshape(depth, MOD_ROWS, 1, 3 * d)
    norm_g3 = norm_g.reshape(depth, 1, d)

    gla_w_main = gla_w_in.astype(BF16)
    gla_w_lr = gla_w_in[:, :, n_main:].astype(BF16)
    gla_w_out_b = gla_w_out.astype(BF16)
    fn_w_in_b = fn_w_in.astype(BF16)
    fn_w_out_b = fn_w_out.astype(BF16)
    hy_w_in_b = hy_w_in.astype(BF16)
    hy_w_out_b = hy_w_out.astype(BF16)
    emb = hy_ffn_w1.shape[1]
    emb_pad = -(-emb // 128) * 128
    hy_w1p = jnp.pad(hy_ffn_w1, ((0, 0), (0, emb_pad - emb), (0, 0)))
    ffn = hy_ffn_w2.shape[-1]
    hy_vecs = [a.reshape(a.shape[0], 1, ffn) for a in (hy_ffn_b1, hy_ffn_b2, hy_ffn_b3, hy_freq)]

    tm = 512
    tiles_per_sample = l_s // tm
    streams = {
        "p": dict(x=x_prompt.reshape(n_p * l_p, d), n=n_p, l=l_p, row=lambda i: 0),
        "s": dict(x=x_sample.reshape(n_s * l_s, d), n=n_s, l=l_s, row=lambda i: 1 + i // tiles_per_sample),
    }
    new_states = None
    for i in range(depth):
        kind, j = i % N_MIXERS, i // N_MIXERS
        final_g = final_norm_g if i == depth - 1 else None
        if kind == 2:
            filt = {}
            for tag, st in streams.items():
                ce = min(d, 256 if st["l"] > 256 else d)
                filt[tag] = _hyena_filters(st["l"], j, emb, hy_w1p, hy_vecs[0], hy_ffn_w2, hy_vecs[1], hy_ffn_w3,
                                           hy_vecs[2], hy_ffn_w4, hy_vecs[3], ce, st["l"] >= HY_SPLIT_MIN_SEQ,
                                           f"hyfilt_{tag}{i}")
        for tag, st in streams.items():
            x, n_seq, seq, row = st["x"], st["n"], st["l"], st["row"]
            if kind == 0:
                proj, lr = _inproj(
                    x, norm_g3, mod4, i, row,
                    [(gla_w_main, (None, d, n_main), (j, 0, 0)), (gla_w_lr, (None, d, 2 * GLA_RANK), (j, 0, 0))],
                    tm, f"inproj_{tag}{i}")
                is_ctx = tag == "p"
                res = _gla_mixer(proj, lr, gla_w_dec, gla_b_dec, gla_onorm_g,
                                 None if is_ctx else state_gla, j, n_seq, seq, is_ctx, new_states,
                                 f"gla_{tag}{i}")
                y = res[0]
                if is_ctx:
                    new_states = res[1]
                w_out = gla_w_out_b
            elif kind == 1:
                (proj,) = _inproj(x, norm_g3, mod4, i, row,
                                  [(fn_w_in_b, (None,) + fn_w_in.shape[1:], (j, 0, 0))], tm, f"inproj_{tag}{i}")
                y = _fnet_mixer(proj, n_seq, seq, 4 if seq <= 256 else 1, f"fnet_{tag}{i}")
                w_out = fn_w_out_b
            else:
                (proj,) = _inproj(x, norm_g3, mod4, i, row,
                                  [(hy_w_in_b, (None,) + hy_w_in.shape[1:], (j, 0, 0))], tm, f"inproj_{tag}{i}")
                ce = min(d, 256 if seq > 256 else d)
                y = _hyena_mixer(proj, hy_conv_w, hy_conv_b, filt[tag], hy_d, j, n_seq, seq, ce,
                                 seq >= HY_SPLIT_MIN_SEQ, f"hyena_{tag}{i}")
                w_out = hy_w_out_b
            ys = y if isinstance(y, (list, tuple)) else [y]
            st["x"] = _outproj(ys, w_out, j, x, mod4, i, row, final_g, tm, f"outproj_{tag}{i}")
    y_prompt = streams["p"]["x"].reshape(n_p, l_p, d)
    y_sample = streams["s"]["x"].reshape(n_s, l_s, d)
    new_state_gla = new_states.astype(x_prompt.dtype)
    return (y_prompt, y_sample, new_state_gla)
```

```python
import functools
import math

import jax
import jax.numpy as jnp
import numpy as np
from jax import lax
from jax.experimental import pallas as pl
from jax.experimental.pallas import tpu as pltpu

F32 = jnp.float32
BF16 = jnp.bfloat16

EPS = 1e-6
N_MIXERS = 3
GLA_HEADS = 4
GLA_RANK = 16
GLA_GATE_NORM = 16.0
FN_GROUPS = 4
HY_ORDER = 2
HY_TARGET = 1e-2
HY_MIN_DECAY = math.log(HY_TARGET) / 1.5
HY_MAX_DECAY = math.log(HY_TARGET) / 0.3
HY_ROW_BLOCK = 512
HY_SPLIT_MIN_SEQ = 512

V7X_VMEM_BYTES = 64 * 1024 * 1024
VMEM_LIMIT_BYTES = V7X_VMEM_BYTES - 8 * 1024 * 1024
MOD_ROWS = 8

_NT = (((1,), (1,)), ((), ()))
_TN = (((0,), (0,)), ((), ()))


def _params(*sem):
    return pltpu.CompilerParams(dimension_semantics=sem, vmem_limit_bytes=VMEM_LIMIT_BYTES)


def _dot(a, b):
    return jnp.dot(a, b, preferred_element_type=F32)


def _split3(x):
    hi = x.astype(BF16)
    r1 = x - hi.astype(F32)
    mid = r1.astype(BF16)
    lo = (r1 - mid.astype(F32)).astype(BF16)
    return hi, mid, lo


def _dot3(a, b):
    ah = a.astype(BF16)
    al = (a - ah.astype(F32)).astype(BF16)
    bh = b.astype(BF16)
    bl = (b - bh.astype(F32)).astype(BF16)
    return _dot(ah, bh) + (_dot(ah, bl) + _dot(al, bh))


def _dot_exact_lhs(a_bf16, b):
    hi, mid, lo = _split3(b)
    return _dot(a_bf16, hi) + (_dot(a_bf16, mid) + _dot(a_bf16, lo))


def _trig_bf16(ang):
    return (jnp.asarray(np.cos(ang), F32).astype(BF16), jnp.asarray(np.sin(ang), F32).astype(BF16))


def _silu(x):
    return x / (1.0 + jnp.exp(-x))


def _log_sigmoid(x):
    return jnp.minimum(x, 0.0) - jnp.log(1.0 + jnp.exp(-jnp.abs(x)))


def _mod_body(c_ref, w_ref, b_ref, o_ref):
    o_ref[...] = _dot3(_silu(c_ref[...]), w_ref[...]) + b_ref[...]


def _modulation(cvec, mod_w, mod_b):
    depth, d, e = mod_w.shape
    tn = d
    return pl.pallas_call(
        _mod_body,
        grid=(depth, e // tn),
        in_specs=[
            pl.BlockSpec((MOD_ROWS, d), lambda l, j: (0, 0)),
            pl.BlockSpec((None, d, tn), lambda l, j: (l, 0, j)),
            pl.BlockSpec((None, 1, tn), lambda l, j: (l, 0, j)),
        ],
        out_specs=pl.BlockSpec((None, MOD_ROWS, tn), lambda l, j: (l, 0, j)),
        out_shape=jax.ShapeDtypeStruct((depth, MOD_ROWS, e), F32),
        compiler_params=_params("parallel", "parallel"),
        name="modulation",
    )(cvec, mod_w, mod_b.reshape(depth, 1, e))


def _phase_permutation(rows):
    order = np.concatenate([np.arange(0, rows, 2), np.arange(1, rows, 2)])
    return jnp.asarray(np.eye(rows, dtype=np.float32)[order], BF16)


def _inproj_body(x_ref, g_ref, mod_ref, *refs, split):
    perm_ref = refs[0] if split else None
    refs = refs[1:] if split else refs
    n_w = len(refs) // 2
    x = x_ref[...]
    d = x.shape[-1]
    h = x * lax.rsqrt(jnp.mean(x * x, axis=-1, keepdims=True) + EPS) * g_ref[...]
    mod = mod_ref[...]
    hb = (h * (1.0 + mod[:, d:2 * d]) + mod[:, :d]).astype(BF16)
    if split:
        hb = _dot(perm_ref[...], hb).astype(BF16)
    for w_ref, o_ref in zip(refs[:n_w], refs[n_w:]):
        res = _dot(hb, w_ref[...])
        if split:
            half, n = res.shape[0] // 2, res.shape[1]
            o_ref[:, :n] = res[:half]
            o_ref[:, n:] = res[half:]
        else:
            o_ref[...] = res


def _inproj(x, norm_g3, mod4, layer, mod_row, weights, tm, split, name):
    t, d = x.shape
    w_specs = [pl.BlockSpec(bs, functools.partial(lambda idx, i: idx, idx)) for _, bs, idx in weights]
    widths = [bs[-1] for _, bs, _ in weights]
    p = 2 if split else 1
    extra_specs = [pl.BlockSpec((tm, tm), lambda i: (0, 0))] if split else []
    extra = [_phase_permutation(tm)] if split else []
    return pl.pallas_call(
        functools.partial(_inproj_body, split=split),
        grid=(t // tm,),
        in_specs=[
            pl.BlockSpec((tm, d), lambda i: (i, 0)),
            pl.BlockSpec((None, 1, d), lambda i: (layer, 0, 0)),
            pl.BlockSpec((None, None, 1, mod4.shape[-1]), lambda i: (layer, mod_row(i), 0, 0)),
        ] + extra_specs + w_specs,
        out_specs=[pl.BlockSpec((tm // p, p * n), lambda i: (i, 0)) for n in widths],
        out_shape=[jax.ShapeDtypeStruct((t // p, p * n), F32) for n in widths],
        compiler_params=_params("parallel"),
        name=name,
    )(x, norm_g3, mod4, *extra, *[w for w, _, _ in weights])


def _outproj_body(*refs, n_y, final):
    y_refs = refs[:n_y]
    refs = refs[n_y:]
    if n_y == 2:
        y = _dot(refs[0][...], jnp.concatenate([r[...] for r in y_refs], axis=0)).astype(BF16)
        refs = refs[1:]
    else:
        y = y_refs[0][...]
    w_ref, x_ref, mod_ref = refs[:3]
    out_ref = refs[-1]
    d = w_ref.shape[-1]
    xn = x_ref[...] + mod_ref[...][:, 2 * d:] * _dot(y, w_ref[...])
    if final:
        xn = xn * lax.rsqrt(jnp.mean(xn * xn, axis=-1, keepdims=True) + EPS) * refs[3][...]
    out_ref[...] = xn


def _outproj(ys, w_out, j, x, mod4, layer, mod_row, final_g, tm, name):
    t, d = x.shape
    n = len(ys)
    final = final_g is not None
    in_specs = [pl.BlockSpec((tm // n, y.shape[-1]), lambda i: (i, 0)) for y in ys]
    args = list(ys)
    if n == 2:
        in_specs.append(pl.BlockSpec((tm, tm), lambda i: (0, 0)))
        args.append(_phase_permutation(tm).T)
    in_specs += [
        pl.BlockSpec((None,) + w_out.shape[1:], lambda i: (j, 0, 0)),
        pl.BlockSpec((tm, d), lambda i: (i, 0)),
        pl.BlockSpec((None, None, 1, mod4.shape[-1]), lambda i: (layer, mod_row(i), 0, 0)),
    ]
    args += [w_out, x, mod4]
    if final:
        in_specs.append(pl.BlockSpec((1, d), lambda i: (0, 0)))
        args.append(final_g.reshape(1, d))
    return pl.pallas_call(
        functools.partial(_outproj_body, n_y=n, final=final),
        grid=(t // tm,),
        in_specs=in_specs,
        out_specs=pl.BlockSpec((tm, d), lambda i: (i, 0)),
        out_shape=jax.ShapeDtypeStruct((t, d), F32),
        compiler_params=_params("parallel"),
        name=name,
    )(*args)


GLA_CHUNK = 64
GLA_BLOCK = 256
GLA_HEADS_PER_STEP = 2
GLA_CLAMP_SPLIT = 8.0
GLA_CLAMP_HALF = 80.0


def _gla_tables(rows, dk):
    c, hc = GLA_CHUNK, GLA_CHUNK // 2
    r = np.arange(rows)[:, None]
    s = np.arange(rows)[None, :]
    same_chunk = (r // c) == (s // c)
    same_half = (r // hc) == (s // hc)
    before = [s <= r, s >= r]
    tri = np.stack([same_chunk & b for b in before])
    m_half = np.stack([same_half & b for b in before])
    sel = np.concatenate([s == (r // c) * c + hc, s == (r // hc) * hc + hc // 2], axis=0)
    second = np.broadcast_to((r // hc) % 2 == 1, (rows, dk))
    as_const = lambda m, dt: jnp.asarray(m.astype(np.float32), dt)
    return (as_const(tri, BF16), as_const(sel, BF16), as_const(same_chunk, F32), as_const(m_half, F32),
            as_const(second, F32))


def _gla_body(q_ref, k_ref, v_ref, r_ref, lr_ref, wdec_ref, bdec_ref, og_ref, tri_ref, sel_ref,
              mchunk_ref, mhalf_ref, second_ref, *rest, seq, layer, has_s0, want_state, has_states_in):
    rest = list(rest)
    s0_ref = rest.pop(0) if has_s0 else None
    if has_states_in:
        rest.pop(0)
    y_ref = rest.pop(0)
    sfin_ref = rest.pop(0) if want_state else None
    o_ref, qi_ref, cum_ref, st_ref = rest
    chunk = GLA_CHUNK
    n_chunks = seq // chunk
    blk = tri_ref.shape[-1]
    n_heads = wdec_ref.shape[0]
    dk = q_ref.shape[-1] // n_heads
    dv = v_ref.shape[-1] // n_heads
    scale = dk ** -0.5
    heads = range(n_heads)
    dirs = range(2)
    chains = [(hh, d) for hh in heads for d in dirs]

    def nt(a, b):
        return lax.dot_general(a, b, _NT, preferred_element_type=F32)

    def cols(hh, width):
        return slice(hh * width, (hh + 1) * width)

    def block_pass(bi, carry):
        rows = pl.ds(pl.multiple_of(bi * blk, blk), blk)
        q = [q_ref[rows, cols(hh, dk)] * scale for hh in heads]
        k = [k_ref[rows, cols(hh, dk)] for hh in heads]
        lr = lr_ref[rows, :]
        lr_hi = lr.astype(BF16)
        lr_lo = (lr - lr_hi.astype(F32)).astype(BF16)
        g = []
        for hh in heads:
            w = wdec_ref[hh]
            w_hi = w.astype(BF16)
            w_lo = (w - w_hi.astype(F32)).astype(BF16)
            logit = _dot(lr_hi, w_hi) + (_dot(lr_hi, w_lo) + _dot(lr_lo, w_hi)) + bdec_ref[hh]
            g.append(_log_sigmoid(logit) * (1.0 / GLA_GATE_NORM))
        cum = {(hh, d): _dot_exact_lhs(tri_ref[d], g[hh][:, cols(d, dk)]) for hh, d in chains}
        refs = [_dot(sel_ref[...], jnp.concatenate([cum[hh, 0], cum[hh, 1]], axis=1).astype(BF16))
                for hh in heads]
        second = second_ref[...]
        first = 1.0 - second
        p_split, p_half = [], {}
        for hh in heads:
            q_in = (q[hh] * second, q[hh] * first)
            k_in = (k[hh] * first, k[hh] * second)
            q1, k1 = [], []
            for d in dirs:
                c = (hh, d)
                d_split = cum[c] - refs[hh][:blk, cols(d, dk)]
                d_half = cum[c] - refs[hh][blk:, cols(d, dk)]
                q1.append((q_in[d] * jnp.exp(jnp.minimum(d_split, GLA_CLAMP_SPLIT))).astype(BF16))
                k1.append((k_in[d] * jnp.exp(jnp.minimum(-d_split, GLA_CLAMP_SPLIT))).astype(BF16))
                q0 = (q[hh] * jnp.exp(jnp.minimum(d_half, GLA_CLAMP_HALF))).astype(BF16)
                k0 = (k[hh] * jnp.exp(jnp.minimum(-d_half, GLA_CLAMP_HALF))).astype(BF16)
                p_half[c] = nt(q0, k0)
                qi_ref[hh, d, rows, :] = (q[hh] * jnp.exp(cum[c])).astype(BF16)
                cum_ref[hh, d, rows, :] = cum[c]
            p_split.append(nt(jnp.concatenate(q1, axis=1), jnp.concatenate(k1, axis=1)))
        for hh in heads:
            att = p_split[hh] * mchunk_ref[...]
            for d in dirs:
                att = att + jnp.where(mhalf_ref[d] > 0.5, p_half[hh, d], 0.0)
            o_ref[rows, cols(hh, dv)] = _dot(att.astype(BF16), v_ref[rows, cols(hh, dv)].astype(BF16))
        return carry

    lax.fori_loop(0, seq // blk, block_pass, 0)

    for hh, d in chains:
        st_ref[hh, d] = s0_ref[d, hh] if has_s0 else jnp.zeros((dk, dv), F32)

    def chunk_step(i, carry):
        for hh, d in chains:
            c = i if d == 0 else n_chunks - 1 - i
            sl = pl.ds(pl.multiple_of(c * chunk, chunk), chunk)
            cum = cum_ref[hh, d, sl, :]
            blast = cum[chunk - 1:chunk] if d == 0 else cum[0:1]
            kh = (k_ref[sl, cols(hh, dk)] * jnp.exp(blast - cum)).astype(BF16)
            vb = v_ref[sl, cols(hh, dv)].astype(BF16)
            st = st_ref[hh, d]
            o_ref[sl, cols(hh, dv)] += _dot(qi_ref[hh, d, sl, :], st.astype(BF16))
            decay = jnp.broadcast_to(jnp.exp(blast), (dk, dk)).T
            st_ref[hh, d] = (jnp.concatenate([decay] * (dv // dk), axis=1) * st
                             + lax.dot_general(kh, vb, _TN, preferred_element_type=F32))
        return carry

    lax.fori_loop(0, n_chunks, chunk_step, 0, unroll=min(n_chunks, 4))

    def finish(jb, carry):
        sl = pl.ds(pl.multiple_of(jb * blk, blk), blk)
        for hh in heads:
            o = o_ref[sl, cols(hh, dv)]
            o = o * lax.rsqrt(jnp.mean(o * o, axis=-1, keepdims=True) + EPS) * og_ref[...]
            y_ref[sl, cols(hh, dv)] = (o * _silu(r_ref[sl, cols(hh, dv)])).astype(BF16)
        return carry

    lax.fori_loop(0, seq // blk, finish, 0)
    if want_state:
        own = sfin_ref if has_states_in else sfin_ref.at[layer]
        for hh, d in chains:
            own[d, hh] = st_ref[hh, d]
        if not has_states_in:
            for other in range(sfin_ref.shape[0]):
                if other != layer:
                    sfin_ref[other] = jnp.zeros(sfin_ref.shape[1:], F32)


def _gla_mixer(proj, lr, w_dec, b_dec, onorm_g, s0, j, n_seq, seq, want_state, states, name):
    t = proj.shape[0]
    h = GLA_HEADS
    key = w_dec.shape[-1]
    dk = key // h
    val = (proj.shape[1] - 2 * key) // 2
    dv = val // h
    has_s0 = s0 is not None
    tables = _gla_tables(min(seq, GLA_BLOCK), dk)
    fixed = lambda a: pl.BlockSpec(a.shape, lambda b, hp: (0,) * a.ndim)
    hps = GLA_HEADS_PER_STEP
    wk, wv = hps * dk, hps * dv
    in_specs = [
        pl.BlockSpec((seq, wk), lambda b, hp: (b, hp)),
        pl.BlockSpec((seq, wk), lambda b, hp: (b, key // wk + hp)),
        pl.BlockSpec((seq, wv), lambda b, hp: (b, (2 * key) // wv + hp)),
        pl.BlockSpec((seq, wv), lambda b, hp: (b, (2 * key + val) // wv + hp)),
        pl.BlockSpec((seq, 2 * GLA_RANK), lambda b, hp: (b, 0)),
        pl.BlockSpec((None, hps, 2 * GLA_RANK, 2 * dk), lambda b, hp: (j, hp, 0, 0)),
        pl.BlockSpec((None, hps, 1, 2 * dk), lambda b, hp: (j, hp, 0, 0)),
        pl.BlockSpec((None, 1, dv), lambda b, hp: (j, 0, 0)),
    ] + [fixed(a) for a in tables]
    n_gla = w_dec.shape[0]
    w5 = w_dec.reshape(n_gla, 2, GLA_RANK, h, dk)
    zero = jnp.zeros_like(w5[:, 0])
    w_blk = jnp.concatenate([jnp.concatenate([w5[:, 0], zero], axis=-1),
                             jnp.concatenate([zero, w5[:, 1]], axis=-1)], axis=1).transpose(0, 2, 1, 3)
    b_cat = b_dec.reshape(n_gla, 2, h, dk).transpose(0, 2, 1, 3).reshape(n_gla, h, 1, 2 * dk)
    args = [proj, proj, proj, proj, lr, w_blk, b_cat, onorm_g.reshape(n_gla, 1, dv), *tables]
    if has_s0:
        in_specs.append(pl.BlockSpec((None, None, 2, hps, dk, dv), lambda b, hp: (b, j, 0, hp, 0, 0)))
        args.append(s0)
    out_specs = [pl.BlockSpec((seq, wv), lambda b, hp: (b, hp))]
    out_shape = [jax.ShapeDtypeStruct((t, val), BF16)]
    aliases = {}
    if want_state:
        out_shape.append(jax.ShapeDtypeStruct((n_seq, n_gla, 2, h, dk, dv), F32))
        if states is None:
            out_specs.append(pl.BlockSpec((None, n_gla, 2, hps, dk, dv), lambda b, hp: (b, 0, 0, hp, 0, 0)))
        else:
            out_specs.append(pl.BlockSpec((None, None, 2, hps, dk, dv), lambda b, hp: (b, j, 0, hp, 0, 0)))
            in_specs.append(pl.BlockSpec(memory_space=pl.ANY))
            args.append(states)
            aliases = {len(args) - 1: 1}
    return pl.pallas_call(
        functools.partial(_gla_body, seq=seq, layer=j, has_s0=has_s0, want_state=want_state,
                          has_states_in=bool(aliases)),
        grid=(n_seq, h // hps),
        in_specs=in_specs,
        out_specs=out_specs,
        out_shape=out_shape,
        input_output_aliases=aliases,
        scratch_shapes=[pltpu.VMEM((seq, wv), F32), pltpu.VMEM((hps, 2, seq, dk), BF16),
                        pltpu.VMEM((hps, 2, seq, dk), F32), pltpu.VMEM((hps, 2, dk, dv), F32)],
        compiler_params=_params("parallel", "parallel"),
        name=name,
    )(*args)


def _dft_tables(n):
    f = np.arange(n, dtype=np.int64)
    ang = (np.outer(f, f) % n).astype(np.float64) * (2.0 * math.pi / n)
    return _trig_bf16(ang)


def _fnet_body(x_ref, z_ref, cc_ref, sc_ref, cl_ref, sl_ref, y_ref, *, seq, n_seq):
    xb = x_ref[...].astype(BF16)
    a = _dot(xb, cc_ref[...]).astype(BF16)
    bm = _dot(xb, sc_ref[...]).astype(BF16)
    inv = 1.0 / math.sqrt(seq * x_ref.shape[-1])
    for i in range(n_seq):
        rows = slice(i * seq, (i + 1) * seq)
        f = _dot(cl_ref[...], a[rows]) - _dot(sl_ref[...], bm[rows])
        y_ref[rows, :] = (f * inv * _silu(z_ref[rows, :])).astype(BF16)


def _fnet_mixer(proj, n_seq_total, seq, seqs_per_step, name):
    t = proj.shape[0]
    width = proj.shape[1] // 2
    gc = width // FN_GROUPS
    cc, sc = _dft_tables(gc)
    cl, sl = _dft_tables(seq)
    rows = seq * seqs_per_step
    const = pl.Buffered(1)
    return pl.pallas_call(
        functools.partial(_fnet_body, seq=seq, n_seq=seqs_per_step),
        grid=(n_seq_total // seqs_per_step, FN_GROUPS),
        in_specs=[
            pl.BlockSpec((rows, gc), lambda i, g: (i, g)),
            pl.BlockSpec((rows, gc), lambda i, g: (i, FN_GROUPS + g)),
            pl.BlockSpec((gc, gc), lambda i, g: (0, 0), pipeline_mode=const),
            pl.BlockSpec((gc, gc), lambda i, g: (0, 0), pipeline_mode=const),
            pl.BlockSpec((seq, seq), lambda i, g: (0, 0), pipeline_mode=const),
            pl.BlockSpec((seq, seq), lambda i, g: (0, 0), pipeline_mode=const),
        ],
        out_specs=pl.BlockSpec((rows, gc), lambda i, g: (i, g)),
        out_shape=jax.ShapeDtypeStruct((t, width), BF16),
        compiler_params=_params("parallel", "parallel"),
        name=name,
    )(proj, proj, cc, sc, cl, sl)


def _signal_tables(n):
    o = 2 * np.arange(n, dtype=np.int64) + 1
    ang = (np.outer(o, o) % (8 * n)).astype(np.float64) * (math.pi / (4 * n))
    return _trig_bf16(ang)


def _filter_tables(n, split):
    f = np.arange(n, dtype=np.int64)
    if split:
        f = np.concatenate([f[:n // 2], f[:n // 2 - 1:-1]])
    f = 2 * f + 1
    o = np.arange(n, dtype=np.int64)
    ang = (np.outer(f, o) % (4 * n)).astype(np.float64) * (math.pi / (2 * n))
    return _trig_bf16(ang)


def _hyena_positions(seq, bands, pad_to):
    t = np.linspace(0.0, 1.0, seq)[:, None]
    w = 2.0 * math.pi * np.arange(seq)[:, None] / seq
    f = np.linspace(1e-4, bands - 1, bands)[None, :]
    zpos = np.concatenate([t, np.cos(f * w), -np.sin(f * w)], axis=-1)
    zpos = np.pad(zpos, ((0, 0), (0, pad_to - zpos.shape[1])))
    return jnp.asarray(zpos, F32), jnp.asarray(t, F32)


def _hyfilt_body(zpos_ref, w1_ref, b1_ref, w2_ref, b2_ref, w3_ref, b3_ref, fr_ref, w4f_ref,
                 w4b_ref, t_ref, del_ref, ch_ref, sh_ref, *rest, split):
    if split:
        c2_ref, s2_ref = rest[:2]
        out_refs, a3_ref = rest[2:-1], rest[-1]
    else:
        out_refs, a3_ref = rest[:-1], rest[-1]

    @pl.when(pl.program_id(0) == 0)
    def _():
        fr = fr_ref[...]
        a = jnp.sin(fr * (_dot3(zpos_ref[...], w1_ref[...]) + b1_ref[...]))
        a = jnp.sin(fr * (_dot3(a, w2_ref[...]) + b2_ref[...]))
        a3_ref[...] = jnp.sin(fr * (_dot3(a, w3_ref[...]) + b3_ref[...]))

    a3 = a3_ref[...]
    dec = jnp.exp(-t_ref[...] * del_ref[...])
    fwd = _dot3(a3, w4f_ref[...]) * dec
    bwd = _dot3(a3, w4b_ref[...]) * dec
    first = lax.broadcasted_iota(jnp.int32, bwd.shape, 0) == 0
    bwd = jnp.where(first, 0.0, bwd)
    hre = _dot(ch_ref[...], (fwd + bwd).astype(BF16))
    him = _dot(sh_ref[...], (bwd - fwd).astype(BF16))
    if not split:
        out_refs[0][...] = hre
        out_refs[1][...] = him
        return
    half = hre.shape[0] // 2
    plus_r, plus_i = hre[:half] + hre[half:], him[:half] - him[half:]
    minus_r, minus_i = hre[:half] - hre[half:], him[:half] + him[half:]
    c2, s2 = c2_ref[...], s2_ref[...]
    for ref, val in zip(out_refs, (plus_r, plus_i,
                                   c2 * minus_r + s2 * minus_i, c2 * minus_i - s2 * minus_r,
                                   c2 * minus_r - s2 * minus_i, c2 * minus_i + s2 * minus_r)):
        ref[...] = val


def _hyena_filters(seq, j, emb, w1p, b1, w2, b2, w3, b3, w4, freq, ce, split, name):
    ffn = w2.shape[-1]
    width = w4.shape[-1] // (2 * HY_ORDER)
    emb_pad = w1p.shape[1]
    zpos, tcol = _hyena_positions(seq, (emb - 1) // 2, emb_pad)
    deltas = jnp.asarray(np.abs(np.linspace(HY_MIN_DECAY, HY_MAX_DECAY, width, dtype=np.float32))[None, :])
    ch, sh = _filter_tables(seq, split)
    nb = width // ce
    const = pl.Buffered(1)
    vec = lambda: pl.BlockSpec((None, 1, ffn), lambda c: (j, 0, 0))
    out_rows, n_out = (seq // 2, 6) if split else (seq, 2)
    twiddles, twiddle_specs = [], []
    if split:
        ang = np.broadcast_to((math.pi * (2 * np.arange(out_rows) + 1) / (2 * seq))[:, None], (out_rows, ce))
        twiddles = [jnp.asarray(np.cos(ang), F32), jnp.asarray(np.sin(ang), F32)]
        twiddle_specs = [pl.BlockSpec((out_rows, ce), lambda c: (0, 0), pipeline_mode=const)] * 2
    return pl.pallas_call(
        functools.partial(_hyfilt_body, split=split),
        grid=(HY_ORDER * nb,),
        in_specs=[
            pl.BlockSpec((seq, emb_pad), lambda c: (0, 0)),
            pl.BlockSpec((None, emb_pad, ffn), lambda c: (j, 0, 0)),
            vec(),
            pl.BlockSpec((None, ffn, ffn), lambda c: (j, 0, 0)),
            vec(),
            pl.BlockSpec((None, ffn, ffn), lambda c: (j, 0, 0)),
            vec(),
            vec(),
            pl.BlockSpec((None, ffn, ce), lambda c: (j, 0, (c // nb) * 2 * nb + c % nb)),
            pl.BlockSpec((None, ffn, ce), lambda c: (j, 0, (c // nb) * 2 * nb + nb + c % nb)),
            pl.BlockSpec((seq, 1), lambda c: (0, 0)),
            pl.BlockSpec((1, ce), lambda c: (0, c % nb)),
            pl.BlockSpec((seq, seq), lambda c: (0, 0), pipeline_mode=const),
            pl.BlockSpec((seq, seq), lambda c: (0, 0), pipeline_mode=const),
        ] + twiddle_specs,
        out_specs=[pl.BlockSpec((out_rows, ce), lambda c: (0, c))] * n_out,
        out_shape=[jax.ShapeDtypeStruct((out_rows, HY_ORDER * width), F32)] * n_out,
        scratch_shapes=[pltpu.VMEM((seq, ffn), F32)],
        compiler_params=_params("arbitrary"),
        name=name,
    )(zpos, w1p, b1, w2, b2, w3, b3, freq, w4, w4, tcol, deltas, ch, sh, *twiddles)


def _hyena_body(x1_ref, x2_ref, v_ref, z_ref, cw1_ref, cw2_ref, cwv_ref, cb1_ref, cb2_ref, cbv_ref,
                hre0_ref, him0_ref, hre1_ref, him1_ref, d_ref, cq_ref, sq_ref, y_ref,
                cur_ref, cb_ref, yr_ref, yi_ref, *, seq, rows):
    n_rb = seq // rows
    inv_len = 1.0 / seq
    rid = lax.broadcasted_iota(jnp.int32, (rows, 1), 0)

    def block(rb):
        return pl.ds(pl.multiple_of(rb * rows, rows), rows)

    def short_conv(u_ref, w_ref, b_ref, rb):
        u = u_ref[block(rb), :]
        before = u_ref[pl.ds(jnp.maximum(rb * rows - 1, 0), 1), :]
        after = u_ref[pl.ds(jnp.minimum((rb + 1) * rows, seq - 1), 1), :]
        before = jnp.where(rb == 0, 0.0, before)
        after = jnp.where(rb == n_rb - 1, 0.0, after)
        prev = jnp.where(rid == 0, before, pltpu.roll(u, 1, 0))
        nxt = jnp.where(rid == rows - 1, after, pltpu.roll(u, rows - 1, 0))
        return prev * w_ref[0:1] + u * w_ref[1:2] + nxt * w_ref[2:3] + b_ref[...]

    def long_conv_gate(x_ref, cw_ref, cbias_ref, hre_ref, him_ref, dn):
        def fwd(rb, carry):
            sl = block(rb)
            ur = _dot(cq_ref[sl, :], cb_ref[...])
            ui = -_dot(sq_ref[sl, :], cb_ref[...])
            hr = hre_ref[sl, :]
            hi = him_ref[sl, :]
            yr_ref[sl, :] = (ur * hr - ui * hi).astype(BF16)
            yi_ref[sl, :] = (ur * hi + ui * hr).astype(BF16)
            return carry

        lax.fori_loop(0, n_rb, fwd, 0)

        def inv(rb, carry):
            sl = block(rb)
            c = (_dot(cq_ref[sl, :], yr_ref[...]) - _dot(sq_ref[sl, :], yi_ref[...])) * inv_len
            new = short_conv(x_ref, cw_ref, cbias_ref, rb) * (c + cur_ref[sl, :] * dn)
            cur_ref[sl, :] = new
            cb_ref[sl, :] = new.astype(BF16)
            return carry

        lax.fori_loop(0, n_rb, inv, 0)

    def load_v(rb, carry):
        v = short_conv(v_ref, cwv_ref, cbv_ref, rb)
        cur_ref[block(rb), :] = v
        cb_ref[block(rb), :] = v.astype(BF16)
        return carry

    lax.fori_loop(0, n_rb, load_v, 0)
    long_conv_gate(x1_ref, cw1_ref, cb1_ref, hre0_ref, him0_ref, d_ref[0:1])
    long_conv_gate(x2_ref, cw2_ref, cb2_ref, hre1_ref, him1_ref, d_ref[1:2])

    def store(rb, carry):
        sl = block(rb)
        y_ref[sl, :] = (cur_ref[sl, :] * _silu(z_ref[sl, :])).astype(BF16)
        return carry

    lax.fori_loop(0, n_rb, store, 0)


def _hyena_split_body(x1_refs, x2_refs, v_refs, z_refs, cw1_ref, cw2_ref, cwv_ref, cb1_ref, cb2_ref, cbv_ref,
                      k_refs, d_ref, cq_ref, sq_ref, y_refs, cur_ref, sig_ref, z_buf, *, seq, rows):
    half = seq // 2
    n_rb = half // rows
    inv_len = 1.0 / seq
    rid = lax.broadcasted_iota(jnp.int32, (rows, 1), 0)

    def block(rb, shift=0):
        return slice(rb * rows + shift, (rb + 1) * rows + shift)

    def short_conv(u_refs, w_ref, b_ref, rb):
        e_ref, o_ref = u_refs
        even, odd = e_ref[block(rb), :], o_ref[block(rb), :]
        odd_before = o_ref[block(rb, -1), :] if rb > 0 else jnp.where(rid == 0, 0.0, pltpu.roll(odd, 1, 0))
        even_after = (e_ref[block(rb, 1), :] if rb < n_rb - 1
                      else jnp.where(rid == rows - 1, 0.0, pltpu.roll(even, rows - 1, 0)))
        w0, w1, w2, b = w_ref[0:1], w_ref[1:2], w_ref[2:3], b_ref[...]
        return (odd_before * w0 + even * w1 + odd * w2 + b, even * w0 + odd * w1 + even_after * w2 + b)

    def long_conv_gate(x_ref, cw_ref, cbias_ref, k, dn):
        k1r, k1i, k2r, k2i, k3r, k3i = k
        for fb in range(n_rb):
            sl = block(fb)
            c, s = cq_ref[sl, :], sq_ref[sl, :]
            er, ei = _dot(c, sig_ref[0]), -_dot(s, sig_ref[0])
            orr, oi = _dot(c, sig_ref[1]), -_dot(s, sig_ref[1])
            a1r, a1i, a2r, a2i, a3r, a3i = (r[sl, :] for r in (k1r, k1i, k2r, k2i, k3r, k3i))
            z_buf[0, sl, :] = (er * a1r - ei * a1i + orr * a2r - oi * a2i).astype(BF16)
            z_buf[1, sl, :] = (er * a1i + ei * a1r + orr * a2i + oi * a2r).astype(BF16)
            z_buf[2, sl, :] = (er * a3r - ei * a3i + orr * a1r - oi * a1i).astype(BF16)
            z_buf[3, sl, :] = (er * a3i + ei * a3r + orr * a1i + oi * a1r).astype(BF16)
        for tb in range(n_rb):
            sl = block(tb)
            c, s = cq_ref[sl, :], sq_ref[sl, :]
            conv = ((_dot(c, z_buf[0]) - _dot(s, z_buf[1])) * inv_len,
                    (_dot(c, z_buf[2]) - _dot(s, z_buf[3])) * inv_len)
            gate = short_conv(x_ref, cw_ref, cbias_ref, tb)
            for p in range(2):
                new = gate[p] * (conv[p] + cur_ref[p, sl, :] * dn)
                cur_ref[p, sl, :] = new
                sig_ref[p, sl, :] = new.astype(BF16)

    for rb in range(n_rb):
        for p, val in enumerate(short_conv(v_refs, cwv_ref, cbv_ref, rb)):
            cur_ref[p, block(rb), :] = val
            sig_ref[p, block(rb), :] = val.astype(BF16)
    long_conv_gate(x1_refs, cw1_ref, cb1_ref, k_refs[0], d_ref[0:1])
    long_conv_gate(x2_refs, cw2_ref, cb2_ref, k_refs[1], d_ref[1:2])
    for rb in range(n_rb):
        for p in range(2):
            sl = block(rb)
            y_refs[p][sl, :] = (cur_ref[p, sl, :] * _silu(z_refs[p][sl, :])).astype(BF16)


def _hyena_split_entry(*refs, seq, rows):
    it = iter(refs)
    take = lambda n: tuple(next(it) for _ in range(n))
    x1, x2, v, z = take(2), take(2), take(2), take(2)
    conv_refs = take(6)
    k = tuple(take(6) for _ in range(HY_ORDER))
    d, cq, sq = take(3)
    y = take(2)
    _hyena_split_body(x1, x2, v, z, *conv_refs, k, d, cq, sq, y, *it, seq=seq, rows=rows)


def _hyena_mixer(proj, conv_w, conv_b, filt, d_skip, j, n_seq, seq, ce, split, name):
    phases = 2 if split else 1
    t = proj.shape[0] * phases
    width = proj.shape[1] // (4 * phases)
    nb = width // ce
    n_dft = seq // phases
    cq, sq = _signal_tables(n_dft)
    rows = min(n_dft, HY_ROW_BLOCK)
    const = pl.Buffered(1)
    seg = lambda s: pl.BlockSpec((seq, ce), lambda e, b: (b, s * nb + e))
    cw = lambda s: pl.BlockSpec((None, conv_w.shape[1], ce), lambda e, b: (j, 0, s * nb + e))
    cb = lambda s: pl.BlockSpec((None, 1, ce), lambda e, b: (j, 0, s * nb + e))
    hs = lambda n: pl.BlockSpec((n_dft, ce), lambda e, b: (0, n * nb + e), pipeline_mode=const)
    conv_b3 = conv_b.reshape(conv_b.shape[0], 1, -1)
    if split:
        body = functools.partial(_hyena_split_entry, seq=seq, rows=rows)
        phase = lambda s, p: pl.BlockSpec((n_dft, ce), lambda e, b: (b, p * 4 * nb + s * nb + e))
        x_specs = [phase(s, p) for s in range(4) for p in range(2)]
        out_specs = [pl.BlockSpec((n_dft, ce), lambda e, b: (b, e))] * 2
        out_shape = [jax.ShapeDtypeStruct((t // 2, width), BF16)] * 2
        scratch = [pltpu.VMEM((2, n_dft, ce), F32), pltpu.VMEM((2, n_dft, ce), BF16),
                   pltpu.VMEM((4, n_dft, ce), BF16)]
    else:
        body = functools.partial(_hyena_body, seq=seq, rows=rows)
        x_specs = [seg(0), seg(1), seg(2), seg(3)]
        out_specs = pl.BlockSpec((seq, ce), lambda e, b: (b, e))
        out_shape = jax.ShapeDtypeStruct((t, width), BF16)
        scratch = [pltpu.VMEM((seq, ce), F32), pltpu.VMEM((seq, ce), BF16),
                   pltpu.VMEM((seq, ce), BF16), pltpu.VMEM((seq, ce), BF16)]
    return pl.pallas_call(
        body,
        grid=(nb, n_seq),
        in_specs=x_specs + [cw(0), cw(1), cw(2), cb(0), cb(1), cb(2)]
        + [hs(n) for n in range(HY_ORDER) for _ in filt]
        + [pl.BlockSpec((None, HY_ORDER, ce), lambda e, b: (j, 0, e)),
           pl.BlockSpec((n_dft, n_dft), lambda e, b: (0, 0), pipeline_mode=const),
           pl.BlockSpec((n_dft, n_dft), lambda e, b: (0, 0), pipeline_mode=const)],
        out_specs=out_specs,
        out_shape=out_shape,
        scratch_shapes=scratch,
        compiler_params=_params("parallel", "parallel"),
        name=name,
    )(*([proj] * len(x_specs)), conv_w, conv_w, conv_w, conv_b3, conv_b3, conv_b3,
      *(list(filt) * HY_ORDER), d_skip, cq, sq)


def kernel(x_prompt, x_sample, state_gla, c, c_ctx, mod_w, mod_b, norm_g, final_norm_g, gla_w_in, gla_w_dec, gla_b_dec, gla_onorm_g, gla_w_out, fn_w_in, fn_w_out, hy_w_in, hy_conv_w, hy_conv_b, hy_ffn_w1, hy_ffn_b1, hy_ffn_w2, hy_ffn_b2, hy_ffn_w3, hy_ffn_b3, hy_ffn_w4, hy_freq, hy_d, hy_w_out):
    n_p, l_p, d = x_prompt.shape
    n_s, l_s, _ = x_sample.shape
    depth = mod_w.shape[0]
    key = gla_w_dec.shape[-1]
    n_main = gla_w_in.shape[-1] - 2 * GLA_RANK

    cvec = jnp.concatenate([c_ctx[None], c, jnp.zeros((MOD_ROWS - 1 - n_s, d), F32)], axis=0)
    mod4 = _modulation(cvec, mod_w, mod_b).reshape(depth, MOD_ROWS, 1, 3 * d)
    norm_g3 = norm_g.reshape(depth, 1, d)

    gla_w_main = gla_w_in.astype(BF16)
    gla_w_lr = gla_w_in[:, :, n_main:].astype(BF16)
    gla_w_out_b = gla_w_out.astype(BF16)
    fn_w_in_b = fn_w_in.astype(BF16)
    fn_w_out_b = fn_w_out.astype(BF16)
    hy_w_in_b = hy_w_in.astype(BF16)
    hy_w_out_b = hy_w_out.astype(BF16)
    emb = hy_ffn_w1.shape[1]
    emb_pad = -(-emb // 128) * 128
    hy_w1p = jnp.pad(hy_ffn_w1, ((0, 0), (0, emb_pad - emb), (0, 0)))
    ffn = hy_ffn_w2.shape[-1]
    hy_vecs = [a.reshape(a.shape[0], 1, ffn) for a in (hy_ffn_b1, hy_ffn_b2, hy_ffn_b3, hy_freq)]

    tm = 512
    tiles_per_sample = l_s // tm
    streams = {
        "p": dict(x=x_prompt.reshape(n_p * l_p, d), n=n_p, l=l_p, row=lambda i: 0),
        "s": dict(x=x_sample.reshape(n_s * l_s, d), n=n_s, l=l_s, row=lambda i: 1 + i // tiles_per_sample),
    }
    new_states = None
    for i in range(depth):
        kind, j = i % N_MIXERS, i // N_MIXERS
        final_g = final_norm_g if i == depth - 1 else None
        if kind == 2:
            filt = {}
            for tag, st in streams.items():
                ce = min(d, 256 if st["l"] > 256 else d)
                filt[tag] = _hyena_filters(st["l"], j, emb, hy_w1p, hy_vecs[0], hy_ffn_w2, hy_vecs[1], hy_ffn_w3,
                                           hy_vecs[2], hy_ffn_w4, hy_vecs[3], ce, st["l"] >= HY_SPLIT_MIN_SEQ,
                                           f"hyfilt_{tag}{i}")
        for tag, st in streams.items():
            x, n_seq, seq, row = st["x"], st["n"], st["l"], st["row"]
            if kind == 0:
                proj, lr = _inproj(
                    x, norm_g3, mod4, i, row,
                    [(gla_w_main, (None, d, n_main), (j, 0, 0)), (gla_w_lr, (None, d, 2 * GLA_RANK), (j, 0, 0))],
                    tm, False, f"inproj_{tag}{i}")
                is_ctx = tag == "p"
                res = _gla_mixer(proj, lr, gla_w_dec, gla_b_dec, gla_onorm_g,
                                 None if is_ctx else state_gla, j, n_seq, seq, is_ctx, new_states,
                                 f"gla_{tag}{i}")
                y = res[0]
                if is_ctx:
                    new_states = res[1]
                w_out = gla_w_out_b
            elif kind == 1:
                (proj,) = _inproj(x, norm_g3, mod4, i, row,
                                  [(fn_w_in_b, (None,) + fn_w_in.shape[1:], (j, 0, 0))], tm, False,
                                  f"inproj_{tag}{i}")
                y = _fnet_mixer(proj, n_seq, seq, 4 if seq <= 256 else 1, f"fnet_{tag}{i}")
                w_out = fn_w_out_b
            else:
                split = seq >= HY_SPLIT_MIN_SEQ
                (proj,) = _inproj(x, norm_g3, mod4, i, row,
                                  [(hy_w_in_b, (None,) + hy_w_in.shape[1:], (j, 0, 0))], tm, split,
                                  f"inproj_{tag}{i}")
                ce = min(d, 256 if seq > 256 else d)
                y = _hyena_mixer(proj, hy_conv_w, hy_conv_b, filt[tag], hy_d, j, n_seq, seq, ce, split,
                                 f"hyena_{tag}{i}")
                w_out = hy_w_out_b
            ys = y if isinstance(y, (list, tuple)) else [y]
            st["x"] = _outproj(ys, w_out, j, x, mod4, i, row, final_g, tm, f"outproj_{tag}{i}")
    y_prompt = streams["p"]["x"].reshape(n_p, l_p, d)
    y_sample = streams["s"]["x"].reshape(n_s, l_s, d)
    new_state_gla = new_states.astype(x_prompt.dtype)
    return (y_prompt, y_sample, new_state_gla)
```

```python
import functools
import math

import jax
import jax.numpy as jnp
import numpy as np
from jax import lax
from jax.experimental import pallas as pl
from jax.experimental.pallas import tpu as pltpu

F32 = jnp.float32
BF16 = jnp.bfloat16

EPS = 1e-6
N_MIXERS = 3
GLA_HEADS = 4
GLA_RANK = 16
GLA_GATE_NORM = 16.0
FN_GROUPS = 4
HY_ORDER = 2
HY_TARGET = 1e-2
HY_MIN_DECAY = math.log(HY_TARGET) / 1.5
HY_MAX_DECAY = math.log(HY_TARGET) / 0.3
HY_ROW_BLOCK = 512
HY_SPLIT_MIN_SEQ = 512

V7X_VMEM_BYTES = 64 * 1024 * 1024
VMEM_LIMIT_BYTES = V7X_VMEM_BYTES - 8 * 1024 * 1024
MOD_ROWS = 8

_NT = (((1,), (1,)), ((), ()))
_TN = (((0,), (0,)), ((), ()))


def _params(*sem):
    return pltpu.CompilerParams(dimension_semantics=sem, vmem_limit_bytes=VMEM_LIMIT_BYTES)


def _dot(a, b):
    return jnp.dot(a, b, preferred_element_type=F32)


def _split3(x):
    hi = x.astype(BF16)
    r1 = x - hi.astype(F32)
    mid = r1.astype(BF16)
    lo = (r1 - mid.astype(F32)).astype(BF16)
    return hi, mid, lo


def _dot3(a, b):
    ah = a.astype(BF16)
    al = (a - ah.astype(F32)).astype(BF16)
    bh = b.astype(BF16)
    bl = (b - bh.astype(F32)).astype(BF16)
    return _dot(jnp.concatenate([ah, ah, al], axis=1), jnp.concatenate([bh, bl, bh], axis=0))


def _dot_exact_lhs(a_bf16, b):
    return _dot(jnp.concatenate([a_bf16] * 3, axis=1), jnp.concatenate(_split3(b), axis=0))


def _trig_bf16(ang):
    return (jnp.asarray(np.cos(ang), F32).astype(BF16), jnp.asarray(np.sin(ang), F32).astype(BF16))


def _silu(x):
    return x / (1.0 + jnp.exp(-x))


def _log_sigmoid(x):
    return jnp.minimum(x, 0.0) - jnp.log(1.0 + jnp.exp(-jnp.abs(x)))


def _mod_body(c_ref, w_ref, b_ref, o_ref):
    o_ref[...] = _dot3(_silu(c_ref[...]), w_ref[...]) + b_ref[...]


def _modulation(cvec, mod_w, mod_b):
    depth, d, e = mod_w.shape
    tn = d
    return pl.pallas_call(
        _mod_body,
        grid=(depth, e // tn),
        in_specs=[
            pl.BlockSpec((MOD_ROWS, d), lambda l, j: (0, 0)),
            pl.BlockSpec((None, d, tn), lambda l, j: (l, 0, j)),
            pl.BlockSpec((None, 1, tn), lambda l, j: (l, 0, j)),
        ],
        out_specs=pl.BlockSpec((None, MOD_ROWS, tn), lambda l, j: (l, 0, j)),
        out_shape=jax.ShapeDtypeStruct((depth, MOD_ROWS, e), F32),
        compiler_params=_params("parallel", "parallel"),
        name="modulation",
    )(cvec, mod_w, mod_b.reshape(depth, 1, e))


def _phase_permutation(rows):
    order = np.concatenate([np.arange(0, rows, 2), np.arange(1, rows, 2)])
    return jnp.asarray(np.eye(rows, dtype=np.float32)[order], BF16)


def _inproj_body(x_ref, g_ref, mod_ref, *refs, split):
    perm_ref = refs[0] if split else None
    refs = refs[1:] if split else refs
    n_w = len(refs) // 2
    x = x_ref[...]
    d = x.shape[-1]
    h = x * lax.rsqrt(jnp.mean(x * x, axis=-1, keepdims=True) + EPS) * g_ref[...]
    mod = mod_ref[...]
    hb = (h * (1.0 + mod[:, d:2 * d]) + mod[:, :d]).astype(BF16)
    if split:
        hb = _dot(perm_ref[...], hb).astype(BF16)
    for w_ref, o_ref in zip(refs[:n_w], refs[n_w:]):
        res = _dot(hb, w_ref[...])
        if split:
            half, n = res.shape[0] // 2, res.shape[1]
            o_ref[:, :n] = res[:half]
            o_ref[:, n:] = res[half:]
        else:
            o_ref[...] = res


def _inproj(x, norm_g3, mod4, layer, mod_row, weights, tm, split, name):
    t, d = x.shape
    w_specs = [pl.BlockSpec(bs, functools.partial(lambda idx, i: idx, idx)) for _, bs, idx in weights]
    widths = [bs[-1] for _, bs, _ in weights]
    p = 2 if split else 1
    extra_specs = [pl.BlockSpec((tm, tm), lambda i: (0, 0))] if split else []
    extra = [_phase_permutation(tm)] if split else []
    return pl.pallas_call(
        functools.partial(_inproj_body, split=split),
        grid=(t // tm,),
        in_specs=[
            pl.BlockSpec((tm, d), lambda i: (i, 0)),
            pl.BlockSpec((None, 1, d), lambda i: (layer, 0, 0)),
            pl.BlockSpec((None, None, 1, mod4.shape[-1]), lambda i: (layer, mod_row(i), 0, 0)),
        ] + extra_specs + w_specs,
        out_specs=[pl.BlockSpec((tm // p, p * n), lambda i: (i, 0)) for n in widths],
        out_shape=[jax.ShapeDtypeStruct((t // p, p * n), F32) for n in widths],
        compiler_params=_params("parallel"),
        name=name,
    )(x, norm_g3, mod4, *extra, *[w for w, _, _ in weights])


def _outproj_body(*refs, n_y, final):
    y_refs = refs[:n_y]
    refs = refs[n_y:]
    if n_y == 2:
        y = _dot(refs[0][...], jnp.concatenate([r[...] for r in y_refs], axis=0)).astype(BF16)
        refs = refs[1:]
    else:
        y = y_refs[0][...]
    w_ref, x_ref, mod_ref = refs[:3]
    out_ref = refs[-1]
    d = w_ref.shape[-1]
    xn = x_ref[...] + mod_ref[...][:, 2 * d:] * _dot(y, w_ref[...])
    if final:
        xn = xn * lax.rsqrt(jnp.mean(xn * xn, axis=-1, keepdims=True) + EPS) * refs[3][...]
    out_ref[...] = xn


def _outproj(ys, w_out, j, x, mod4, layer, mod_row, final_g, tm, name):
    t, d = x.shape
    n = len(ys)
    final = final_g is not None
    in_specs = [pl.BlockSpec((tm // n, y.shape[-1]), lambda i: (i, 0)) for y in ys]
    args = list(ys)
    if n == 2:
        in_specs.append(pl.BlockSpec((tm, tm), lambda i: (0, 0)))
        args.append(_phase_permutation(tm).T)
    in_specs += [
        pl.BlockSpec((None,) + w_out.shape[1:], lambda i: (j, 0, 0)),
        pl.BlockSpec((tm, d), lambda i: (i, 0)),
        pl.BlockSpec((None, None, 1, mod4.shape[-1]), lambda i: (layer, mod_row(i), 0, 0)),
    ]
    args += [w_out, x, mod4]
    if final:
        in_specs.append(pl.BlockSpec((1, d), lambda i: (0, 0)))
        args.append(final_g.reshape(1, d))
    return pl.pallas_call(
        functools.partial(_outproj_body, n_y=n, final=final),
        grid=(t // tm,),
        in_specs=in_specs,
        out_specs=pl.BlockSpec((tm, d), lambda i: (i, 0)),
        out_shape=jax.ShapeDtypeStruct((t, d), F32),
        compiler_params=_params("parallel"),
        name=name,
    )(*args)


GLA_CHUNK = 64
GLA_BLOCK = 256
GLA_HEADS_PER_STEP = 2
LOG2_E = math.log2(math.e)
GLA_CLAMP_SPLIT = 8.0 * LOG2_E
GLA_CLAMP_HALF = 80.0 * LOG2_E


def _gla_tables(rows, dk):
    c, hc = GLA_CHUNK, GLA_CHUNK // 2
    r = np.arange(rows)[:, None]
    s = np.arange(rows)[None, :]
    same_chunk = (r // c) == (s // c)
    same_half = (r // hc) == (s // hc)
    before = [s <= r, s >= r]
    tri = np.stack([same_chunk & b for b in before])
    m_half = np.stack([same_half & b for b in before])
    sel = np.concatenate([s == (r // c) * c + hc, s == (r // hc) * hc + hc // 2], axis=0)
    second = np.broadcast_to((r // hc) % 2 == 1, (rows, dk))
    as_const = lambda m, dt: jnp.asarray(m.astype(np.float32), dt)
    return (as_const(tri, BF16), as_const(sel, BF16), as_const(same_chunk, F32), as_const(m_half, F32),
            as_const(second, F32))


def _gla_body(q_ref, k_ref, v_ref, r_ref, lr_ref, wdec_ref, bdec_ref, og_ref, tri_ref, sel_ref,
              mchunk_ref, mhalf_ref, second_ref, *rest, seq, layer, has_s0, want_state, has_states_in):
    rest = list(rest)
    s0_ref = rest.pop(0) if has_s0 else None
    if has_states_in:
        rest.pop(0)
    y_ref = rest.pop(0)
    sfin_ref = rest.pop(0) if want_state else None
    o_ref, qi_ref, cum_ref, st_ref = rest
    chunk = GLA_CHUNK
    n_chunks = seq // chunk
    blk = tri_ref.shape[-1]
    n_heads = wdec_ref.shape[0]
    dk = q_ref.shape[-1] // n_heads
    dv = v_ref.shape[-1] // n_heads
    scale = dk ** -0.5
    heads = range(n_heads)
    dirs = range(2)
    chains = [(hh, d) for hh in heads for d in dirs]

    def nt(a, b):
        return lax.dot_general(a, b, _NT, preferred_element_type=F32)

    def cols(hh, width):
        return slice(hh * width, (hh + 1) * width)

    def block_pass(bi, carry):
        rows = pl.ds(pl.multiple_of(bi * blk, blk), blk)
        q = [q_ref[rows, cols(hh, dk)] * scale for hh in heads]
        k = [k_ref[rows, cols(hh, dk)] for hh in heads]
        lr = lr_ref[rows, :]
        g = [_log_sigmoid(_dot3(lr, wdec_ref[hh]) + bdec_ref[hh]) * (LOG2_E / GLA_GATE_NORM) for hh in heads]
        cum = {(hh, d): _dot_exact_lhs(tri_ref[d], g[hh][:, cols(d, dk)]) for hh, d in chains}
        refs = [_dot(sel_ref[...], jnp.concatenate([cum[hh, 0], cum[hh, 1]], axis=1).astype(BF16))
                for hh in heads]
        second = second_ref[...]
        first = 1.0 - second
        p_split, p_half = [], {}
        for hh in heads:
            q_in = (q[hh] * second, q[hh] * first)
            k_in = (k[hh] * first, k[hh] * second)
            q1, k1 = [], []
            for d in dirs:
                c = (hh, d)
                d_split = cum[c] - refs[hh][:blk, cols(d, dk)]
                d_half = cum[c] - refs[hh][blk:, cols(d, dk)]
                q1.append((q_in[d] * jnp.exp2(jnp.minimum(d_split, GLA_CLAMP_SPLIT))).astype(BF16))
                k1.append((k_in[d] * jnp.exp2(jnp.minimum(-d_split, GLA_CLAMP_SPLIT))).astype(BF16))
                q0 = (q[hh] * jnp.exp2(jnp.minimum(d_half, GLA_CLAMP_HALF))).astype(BF16)
                k0 = (k[hh] * jnp.exp2(jnp.minimum(-d_half, GLA_CLAMP_HALF))).astype(BF16)
                p_half[c] = nt(q0, k0)
                qi_ref[hh, d, rows, :] = (q[hh] * jnp.exp2(cum[c])).astype(BF16)
                cum_ref[hh, d, rows, :] = cum[c]
            p_split.append(nt(jnp.concatenate(q1, axis=1), jnp.concatenate(k1, axis=1)))
        for hh in heads:
            att = p_split[hh] * mchunk_ref[...]
            for d in dirs:
                att = att + jnp.where(mhalf_ref[d] > 0.5, p_half[hh, d], 0.0)
            o_ref[rows, cols(hh, dv)] = _dot(att.astype(BF16), v_ref[rows, cols(hh, dv)].astype(BF16))
        return carry

    lax.fori_loop(0, seq // blk, block_pass, 0)

    for hh, d in chains:
        st_ref[hh, d] = s0_ref[d, hh] if has_s0 else jnp.zeros((dk, dv), F32)

    def chunk_step(i, carry):
        for hh, d in chains:
            c = i if d == 0 else n_chunks - 1 - i
            sl = pl.ds(pl.multiple_of(c * chunk, chunk), chunk)
            cum = cum_ref[hh, d, sl, :]
            blast = cum[chunk - 1:chunk] if d == 0 else cum[0:1]
            kh = (k_ref[sl, cols(hh, dk)] * jnp.exp2(blast - cum)).astype(BF16)
            vb = v_ref[sl, cols(hh, dv)].astype(BF16)
            st = st_ref[hh, d]
            o_ref[sl, cols(hh, dv)] += _dot(qi_ref[hh, d, sl, :], st.astype(BF16))
            decay = jnp.broadcast_to(jnp.exp2(blast), (dk, dk)).T
            st_ref[hh, d] = (jnp.concatenate([decay] * (dv // dk), axis=1) * st
                             + lax.dot_general(kh, vb, _TN, preferred_element_type=F32))
        return carry

    lax.fori_loop(0, n_chunks, chunk_step, 0, unroll=min(n_chunks, 4))

    def finish(jb, carry):
        sl = pl.ds(pl.multiple_of(jb * blk, blk), blk)
        for hh in heads:
            o = o_ref[sl, cols(hh, dv)]
            o = o * lax.rsqrt(jnp.mean(o * o, axis=-1, keepdims=True) + EPS) * og_ref[...]
            y_ref[sl, cols(hh, dv)] = (o * _silu(r_ref[sl, cols(hh, dv)])).astype(BF16)
        return carry

    lax.fori_loop(0, seq // blk, finish, 0)
    if want_state:
        own = sfin_ref if has_states_in else sfin_ref.at[layer]
        for hh, d in chains:
            own[d, hh] = st_ref[hh, d]
        if not has_states_in:
            for other in range(sfin_ref.shape[0]):
                if other != layer:
                    sfin_ref[other] = jnp.zeros(sfin_ref.shape[1:], F32)


def _gla_mixer(proj, lr, w_dec, b_dec, onorm_g, s0, j, n_seq, seq, want_state, states, name):
    t = proj.shape[0]
    h = GLA_HEADS
    key = w_dec.shape[-1]
    dk = key // h
    val = (proj.shape[1] - 2 * key) // 2
    dv = val // h
    has_s0 = s0 is not None
    tables = _gla_tables(min(seq, GLA_BLOCK), dk)
    fixed = lambda a: pl.BlockSpec(a.shape, lambda b, hp: (0,) * a.ndim)
    hps = h if seq <= GLA_BLOCK else GLA_HEADS_PER_STEP
    wk, wv = hps * dk, hps * dv
    in_specs = [
        pl.BlockSpec((seq, wk), lambda b, hp: (b, hp)),
        pl.BlockSpec((seq, wk), lambda b, hp: (b, key // wk + hp)),
        pl.BlockSpec((seq, wv), lambda b, hp: (b, (2 * key) // wv + hp)),
        pl.BlockSpec((seq, wv), lambda b, hp: (b, (2 * key + val) // wv + hp)),
        pl.BlockSpec((seq, 2 * GLA_RANK), lambda b, hp: (b, 0)),
        pl.BlockSpec((None, hps, 2 * GLA_RANK, 2 * dk), lambda b, hp: (j, hp, 0, 0)),
        pl.BlockSpec((None, hps, 1, 2 * dk), lambda b, hp: (j, hp, 0, 0)),
        pl.BlockSpec((None, 1, dv), lambda b, hp: (j, 0, 0)),
    ] + [fixed(a) for a in tables]
    n_gla = w_dec.shape[0]
    w5 = w_dec.reshape(n_gla, 2, GLA_RANK, h, dk)
    zero = jnp.zeros_like(w5[:, 0])
    w_blk = jnp.concatenate([jnp.concatenate([w5[:, 0], zero], axis=-1),
                             jnp.concatenate([zero, w5[:, 1]], axis=-1)], axis=1).transpose(0, 2, 1, 3)
    b_cat = b_dec.reshape(n_gla, 2, h, dk).transpose(0, 2, 1, 3).reshape(n_gla, h, 1, 2 * dk)
    args = [proj, proj, proj, proj, lr, w_blk, b_cat, onorm_g.reshape(n_gla, 1, dv), *tables]
    if has_s0:
        in_specs.append(pl.BlockSpec((None, None, 2, hps, dk, dv), lambda b, hp: (b, j, 0, hp, 0, 0)))
        args.append(s0)
    out_specs = [pl.BlockSpec((seq, wv), lambda b, hp: (b, hp))]
    out_shape = [jax.ShapeDtypeStruct((t, val), BF16)]
    aliases = {}
    if want_state:
        out_shape.append(jax.ShapeDtypeStruct((n_seq, n_gla, 2, h, dk, dv), F32))
        if states is None:
            out_specs.append(pl.BlockSpec((None, n_gla, 2, hps, dk, dv), lambda b, hp: (b, 0, 0, hp, 0, 0)))
        else:
            out_specs.append(pl.BlockSpec((None, None, 2, hps, dk, dv), lambda b, hp: (b, j, 0, hp, 0, 0)))
            in_specs.append(pl.BlockSpec(memory_space=pl.ANY))
            args.append(states)
            aliases = {len(args) - 1: 1}
    return pl.pallas_call(
        functools.partial(_gla_body, seq=seq, layer=j, has_s0=has_s0, want_state=want_state,
                          has_states_in=bool(aliases)),
        grid=(n_seq, h // hps),
        in_specs=in_specs,
        out_specs=out_specs,
        out_shape=out_shape,
        input_output_aliases=aliases,
        scratch_shapes=[pltpu.VMEM((seq, wv), F32), pltpu.VMEM((hps, 2, seq, dk), BF16),
                        pltpu.VMEM((hps, 2, seq, dk), F32), pltpu.VMEM((hps, 2, dk, dv), F32)],
        compiler_params=_params("parallel", "parallel"),
        name=name,
    )(*args)


def _dft_tables(n):
    f = np.arange(n, dtype=np.int64)
    ang = (np.outer(f, f) % n).astype(np.float64) * (2.0 * math.pi / n)
    return _trig_bf16(ang)


def _fnet_body(x_ref, z_ref, cc_ref, sc_ref, cl_ref, sl_ref, y_ref, *, seq, n_seq):
    xb = x_ref[...].astype(BF16)
    a = _dot(xb, cc_ref[...]).astype(BF16)
    bm = _dot(xb, sc_ref[...]).astype(BF16)
    inv = 1.0 / math.sqrt(seq * x_ref.shape[-1])
    for i in range(n_seq):
        rows = slice(i * seq, (i + 1) * seq)
        f = _dot(cl_ref[...], a[rows]) - _dot(sl_ref[...], bm[rows])
        y_ref[rows, :] = (f * inv * _silu(z_ref[rows, :])).astype(BF16)


def _fnet_mixer(proj, n_seq_total, seq, seqs_per_step, name):
    t = proj.shape[0]
    width = proj.shape[1] // 2
    gc = width // FN_GROUPS
    cc, sc = _dft_tables(gc)
    cl, sl = _dft_tables(seq)
    rows = seq * seqs_per_step
    const = pl.Buffered(1)
    return pl.pallas_call(
        functools.partial(_fnet_body, seq=seq, n_seq=seqs_per_step),
        grid=(n_seq_total // seqs_per_step, FN_GROUPS),
        in_specs=[
            pl.BlockSpec((rows, gc), lambda i, g: (i, g)),
            pl.BlockSpec((rows, gc), lambda i, g: (i, FN_GROUPS + g)),
            pl.BlockSpec((gc, gc), lambda i, g: (0, 0), pipeline_mode=const),
            pl.BlockSpec((gc, gc), lambda i, g: (0, 0), pipeline_mode=const),
            pl.BlockSpec((seq, seq), lambda i, g: (0, 0), pipeline_mode=const),
            pl.BlockSpec((seq, seq), lambda i, g: (0, 0), pipeline_mode=const),
        ],
        out_specs=pl.BlockSpec((rows, gc), lambda i, g: (i, g)),
        out_shape=jax.ShapeDtypeStruct((t, width), BF16),
        compiler_params=_params("parallel", "parallel"),
        name=name,
    )(proj, proj, cc, sc, cl, sl)


def _signal_tables(n):
    o = 2 * np.arange(n, dtype=np.int64) + 1
    ang = (np.outer(o, o) % (8 * n)).astype(np.float64) * (math.pi / (4 * n))
    return _trig_bf16(ang)


def _filter_tables(n, split):
    f = np.arange(n, dtype=np.int64)
    if split:
        f = np.concatenate([f[:n // 2], f[:n // 2 - 1:-1]])
    f = 2 * f + 1
    o = np.arange(n, dtype=np.int64)
    ang = (np.outer(f, o) % (4 * n)).astype(np.float64) * (math.pi / (2 * n))
    return _trig_bf16(ang)


def _hyena_positions(seq, bands, pad_to):
    t = np.linspace(0.0, 1.0, seq)[:, None]
    w = 2.0 * math.pi * np.arange(seq)[:, None] / seq
    f = np.linspace(1e-4, bands - 1, bands)[None, :]
    zpos = np.concatenate([t, np.cos(f * w), -np.sin(f * w)], axis=-1)
    zpos = np.pad(zpos, ((0, 0), (0, pad_to - zpos.shape[1])))
    return jnp.asarray(zpos, F32), jnp.asarray(t, F32)


def _hyfilt_body(zpos_ref, w1_ref, b1_ref, w2_ref, b2_ref, w3_ref, b3_ref, fr_ref, w4f_ref,
                 w4b_ref, t_ref, del_ref, ch_ref, sh_ref, *rest, split):
    if split:
        c2_ref, s2_ref = rest[:2]
        out_refs, a3_ref = rest[2:-1], rest[-1]
    else:
        out_refs, a3_ref = rest[:-1], rest[-1]

    @pl.when(pl.program_id(0) == 0)
    def _():
        fr = fr_ref[...]
        a = jnp.sin(fr * (_dot3(zpos_ref[...], w1_ref[...]) + b1_ref[...]))
        a = jnp.sin(fr * (_dot3(a, w2_ref[...]) + b2_ref[...]))
        a3_ref[...] = jnp.sin(fr * (_dot3(a, w3_ref[...]) + b3_ref[...]))

    a3 = a3_ref[...]
    dec = jnp.exp(-t_ref[...] * del_ref[...])
    fwd = _dot3(a3, w4f_ref[...]) * dec
    bwd = _dot3(a3, w4b_ref[...]) * dec
    first = lax.broadcasted_iota(jnp.int32, bwd.shape, 0) == 0
    bwd = jnp.where(first, 0.0, bwd)
    hre = _dot(ch_ref[...], (fwd + bwd).astype(BF16))
    him = _dot(sh_ref[...], (bwd - fwd).astype(BF16))
    if not split:
        out_refs[0][...] = hre
        out_refs[1][...] = him
        return
    half = hre.shape[0] // 2
    plus_r, plus_i = hre[:half] + hre[half:], him[:half] - him[half:]
    minus_r, minus_i = hre[:half] - hre[half:], him[:half] + him[half:]
    c2, s2 = c2_ref[...], s2_ref[...]
    for ref, val in zip(out_refs, (plus_r, plus_i,
                                   c2 * minus_r + s2 * minus_i, c2 * minus_i - s2 * minus_r,
                                   c2 * minus_r - s2 * minus_i, c2 * minus_i + s2 * minus_r)):
        ref[...] = val


def _hyena_filters(seq, j, emb, w1p, b1, w2, b2, w3, b3, w4, freq, ce, split, name):
    ffn = w2.shape[-1]
    width = w4.shape[-1] // (2 * HY_ORDER)
    emb_pad = w1p.shape[1]
    zpos, tcol = _hyena_positions(seq, (emb - 1) // 2, emb_pad)
    deltas = jnp.asarray(np.abs(np.linspace(HY_MIN_DECAY, HY_MAX_DECAY, width, dtype=np.float32))[None, :])
    ch, sh = _filter_tables(seq, split)
    nb = width // ce
    const = pl.Buffered(1)
    vec = lambda: pl.BlockSpec((None, 1, ffn), lambda c: (j, 0, 0))
    out_rows, n_out = (seq // 2, 6) if split else (seq, 2)
    twiddles, twiddle_specs = [], []
    if split:
        ang = np.broadcast_to((math.pi * (2 * np.arange(out_rows) + 1) / (2 * seq))[:, None], (out_rows, ce))
        twiddles = [jnp.asarray(np.cos(ang), F32), jnp.asarray(np.sin(ang), F32)]
        twiddle_specs = [pl.BlockSpec((out_rows, ce), lambda c: (0, 0), pipeline_mode=const)] * 2
    return pl.pallas_call(
        functools.partial(_hyfilt_body, split=split),
        grid=(HY_ORDER * nb,),
        in_specs=[
            pl.BlockSpec((seq, emb_pad), lambda c: (0, 0)),
            pl.BlockSpec((None, emb_pad, ffn), lambda c: (j, 0, 0)),
            vec(),
            pl.BlockSpec((None, ffn, ffn), lambda c: (j, 0, 0)),
            vec(),
            pl.BlockSpec((None, ffn, ffn), lambda c: (j, 0, 0)),
            vec(),
            vec(),
            pl.BlockSpec((None, ffn, ce), lambda c: (j, 0, (c // nb) * 2 * nb + c % nb)),
            pl.BlockSpec((None, ffn, ce), lambda c: (j, 0, (c // nb) * 2 * nb + nb + c % nb)),
            pl.BlockSpec((seq, 1), lambda c: (0, 0)),
            pl.BlockSpec((1, ce), lambda c: (0, c % nb)),
            pl.BlockSpec((seq, seq), lambda c: (0, 0), pipeline_mode=const),
            pl.BlockSpec((seq, seq), lambda c: (0, 0), pipeline_mode=const),
        ] + twiddle_specs,
        out_specs=[pl.BlockSpec((out_rows, ce), lambda c: (0, c))] * n_out,
        out_shape=[jax.ShapeDtypeStruct((out_rows, HY_ORDER * width), F32)] * n_out,
        scratch_shapes=[pltpu.VMEM((seq, ffn), F32)],
        compiler_params=_params("arbitrary"),
        name=name,
    )(zpos, w1p, b1, w2, b2, w3, b3, freq, w4, w4, tcol, deltas, ch, sh, *twiddles)


def _hyena_body(x1_ref, x2_ref, v_ref, z_ref, cw1_ref, cw2_ref, cwv_ref, cb1_ref, cb2_ref, cbv_ref,
                hre0_ref, him0_ref, hre1_ref, him1_ref, d_ref, cq_ref, sq_ref, y_ref,
                cur_ref, cb_ref, yr_ref, yi_ref, *, seq, rows):
    n_rb = seq // rows
    inv_len = 1.0 / seq
    rid = lax.broadcasted_iota(jnp.int32, (rows, 1), 0)

    def block(rb):
        return pl.ds(pl.multiple_of(rb * rows, rows), rows)

    def short_conv(u_ref, w_ref, b_ref, rb):
        u = u_ref[block(rb), :]
        before = u_ref[pl.ds(jnp.maximum(rb * rows - 1, 0), 1), :]
        after = u_ref[pl.ds(jnp.minimum((rb + 1) * rows, seq - 1), 1), :]
        before = jnp.where(rb == 0, 0.0, before)
        after = jnp.where(rb == n_rb - 1, 0.0, after)
        prev = jnp.where(rid == 0, before, pltpu.roll(u, 1, 0))
        nxt = jnp.where(rid == rows - 1, after, pltpu.roll(u, rows - 1, 0))
        return prev * w_ref[0:1] + u * w_ref[1:2] + nxt * w_ref[2:3] + b_ref[...]

    def long_conv_gate(x_ref, cw_ref, cbias_ref, hre_ref, him_ref, dn):
        def fwd(rb, carry):
            sl = block(rb)
            ur = _dot(cq_ref[sl, :], cb_ref[...])
            ui = -_dot(sq_ref[sl, :], cb_ref[...])
            hr = hre_ref[sl, :]
            hi = him_ref[sl, :]
            yr_ref[sl, :] = (ur * hr - ui * hi).astype(BF16)
            yi_ref[sl, :] = (ur * hi + ui * hr).astype(BF16)
            return carry

        lax.fori_loop(0, n_rb, fwd, 0)

        def inv(rb, carry):
            sl = block(rb)
            c = (_dot(cq_ref[sl, :], yr_ref[...]) - _dot(sq_ref[sl, :], yi_ref[...])) * inv_len
            new = short_conv(x_ref, cw_ref, cbias_ref, rb) * (c + cur_ref[sl, :] * dn)
            cur_ref[sl, :] = new
            cb_ref[sl, :] = new.astype(BF16)
            return carry

        lax.fori_loop(0, n_rb, inv, 0)

    def load_v(rb, carry):
        v = short_conv(v_ref, cwv_ref, cbv_ref, rb)
        cur_ref[block(rb), :] = v
        cb_ref[block(rb), :] = v.astype(BF16)
        return carry

    lax.fori_loop(0, n_rb, load_v, 0)
    long_conv_gate(x1_ref, cw1_ref, cb1_ref, hre0_ref, him0_ref, d_ref[0:1])
    long_conv_gate(x2_ref, cw2_ref, cb2_ref, hre1_ref, him1_ref, d_ref[1:2])

    def store(rb, carry):
        sl = block(rb)
        y_ref[sl, :] = (cur_ref[sl, :] * _silu(z_ref[sl, :])).astype(BF16)
        return carry

    lax.fori_loop(0, n_rb, store, 0)


def _hyena_split_body(x1_refs, x2_refs, v_refs, z_refs, cw1_ref, cw2_ref, cwv_ref, cb1_ref, cb2_ref, cbv_ref,
                      k_refs, d_ref, cq_ref, sq_ref, y_refs, cur_ref, sig_ref, z_buf, *, seq, rows):
    half = seq // 2
    n_rb = half // rows
    inv_len = 1.0 / seq
    rid = lax.broadcasted_iota(jnp.int32, (rows, 1), 0)

    def block(rb, shift=0):
        return slice(rb * rows + shift, (rb + 1) * rows + shift)

    def short_conv(u_refs, w_ref, b_ref, rb):
        e_ref, o_ref = u_refs
        even, odd = e_ref[block(rb), :], o_ref[block(rb), :]
        odd_before = o_ref[block(rb, -1), :] if rb > 0 else jnp.where(rid == 0, 0.0, pltpu.roll(odd, 1, 0))
        even_after = (e_ref[block(rb, 1), :] if rb < n_rb - 1
                      else jnp.where(rid == rows - 1, 0.0, pltpu.roll(even, rows - 1, 0)))
        w0, w1, w2, b = w_ref[0:1], w_ref[1:2], w_ref[2:3], b_ref[...]
        return (odd_before * w0 + even * w1 + odd * w2 + b, even * w0 + odd * w1 + even_after * w2 + b)

    def long_conv_gate(x_ref, cw_ref, cbias_ref, k, dn):
        k1r, k1i, k2r, k2i, k3r, k3i = k
        for fb in range(n_rb):
            sl = block(fb)
            c, s = cq_ref[sl, :], sq_ref[sl, :]
            er, ei = _dot(c, sig_ref[0]), -_dot(s, sig_ref[0])
            orr, oi = _dot(c, sig_ref[1]), -_dot(s, sig_ref[1])
            a1r, a1i, a2r, a2i, a3r, a3i = (r[sl, :] for r in (k1r, k1i, k2r, k2i, k3r, k3i))
            z_buf[0, sl, :] = (er * a1r - ei * a1i + orr * a2r - oi * a2i).astype(BF16)
            z_buf[1, sl, :] = (er * a1i + ei * a1r + orr * a2i + oi * a2r).astype(BF16)
            z_buf[2, sl, :] = (er * a3r - ei * a3i + orr * a1r - oi * a1i).astype(BF16)
            z_buf[3, sl, :] = (er * a3i + ei * a3r + orr * a1i + oi * a1r).astype(BF16)
        for tb in range(n_rb):
            sl = block(tb)
            c, s = cq_ref[sl, :], sq_ref[sl, :]
            conv = ((_dot(c, z_buf[0]) - _dot(s, z_buf[1])) * inv_len,
                    (_dot(c, z_buf[2]) - _dot(s, z_buf[3])) * inv_len)
            gate = short_conv(x_ref, cw_ref, cbias_ref, tb)
            for p in range(2):
                new = gate[p] * (conv[p] + cur_ref[p, sl, :] * dn)
                cur_ref[p, sl, :] = new
                sig_ref[p, sl, :] = new.astype(BF16)

    for rb in range(n_rb):
        for p, val in enumerate(short_conv(v_refs, cwv_ref, cbv_ref, rb)):
            cur_ref[p, block(rb), :] = val
            sig_ref[p, block(rb), :] = val.astype(BF16)
    long_conv_gate(x1_refs, cw1_ref, cb1_ref, k_refs[0], d_ref[0:1])
    long_conv_gate(x2_refs, cw2_ref, cb2_ref, k_refs[1], d_ref[1:2])
    for rb in range(n_rb):
        for p in range(2):
            sl = block(rb)
            y_refs[p][sl, :] = (cur_ref[p, sl, :] * _silu(z_refs[p][sl, :])).astype(BF16)


def _hyena_split_entry(*refs, seq, rows):
    it = iter(refs)
    take = lambda n: tuple(next(it) for _ in range(n))
    x1, x2, v, z = take(2), take(2), take(2), take(2)
    conv_refs = take(6)
    k = tuple(take(6) for _ in range(HY_ORDER))
    d, cq, sq = take(3)
    y = take(2)
    _hyena_split_body(x1, x2, v, z, *conv_refs, k, d, cq, sq, y, *it, seq=seq, rows=rows)


def _hyena_mixer(proj, conv_w, conv_b, filt, d_skip, j, n_seq, seq, ce, split, name):
    phases = 2 if split else 1
    t = proj.shape[0] * phases
    width = proj.shape[1] // (4 * phases)
    nb = width // ce
    n_dft = seq // phases
    cq, sq = _signal_tables(n_dft)
    rows = min(n_dft, HY_ROW_BLOCK)
    const = pl.Buffered(1)
    seg = lambda s: pl.BlockSpec((seq, ce), lambda e, b: (b, s * nb + e))
    cw = lambda s: pl.BlockSpec((None, conv_w.shape[1], ce), lambda e, b: (j, 0, s * nb + e))
    cb = lambda s: pl.BlockSpec((None, 1, ce), lambda e, b: (j, 0, s * nb + e))
    hs = lambda n: pl.BlockSpec((n_dft, ce), lambda e, b: (0, n * nb + e), pipeline_mode=const)
    conv_b3 = conv_b.reshape(conv_b.shape[0], 1, -1)
    if split:
        body = functools.partial(_hyena_split_entry, seq=seq, rows=rows)
        phase = lambda s, p: pl.BlockSpec((n_dft, ce), lambda e, b: (b, p * 4 * nb + s * nb + e))
        x_specs = [phase(s, p) for s in range(4) for p in range(2)]
        out_specs = [pl.BlockSpec((n_dft, ce), lambda e, b: (b, e))] * 2
        out_shape = [jax.ShapeDtypeStruct((t // 2, width), BF16)] * 2
        scratch = [pltpu.VMEM((2, n_dft, ce), F32), pltpu.VMEM((2, n_dft, ce), BF16),
                   pltpu.VMEM((4, n_dft, ce), BF16)]
    else:
        body = functools.partial(_hyena_body, seq=seq, rows=rows)
        x_specs = [seg(0), seg(1), seg(2), seg(3)]
        out_specs = pl.BlockSpec((seq, ce), lambda e, b: (b, e))
        out_shape = jax.ShapeDtypeStruct((t, width), BF16)
        scratch = [pltpu.VMEM((seq, ce), F32), pltpu.VMEM((seq, ce), BF16),
                   pltpu.VMEM((seq, ce), BF16), pltpu.VMEM((seq, ce), BF16)]
    return pl.pallas_call(
        body,
        grid=(nb, n_seq),
        in_specs=x_specs + [cw(0), cw(1), cw(2), cb(0), cb(1), cb(2)]
        + [hs(n) for n in range(HY_ORDER) for _ in filt]
        + [pl.BlockSpec((None, HY_ORDER, ce), lambda e, b: (j, 0, e)),
           pl.BlockSpec((n_dft, n_dft), lambda e, b: (0, 0), pipeline_mode=const),
           pl.BlockSpec((n_dft, n_dft), lambda e, b: (0, 0), pipeline_mode=const)],
        out_specs=out_specs,
        out_shape=out_shape,
        scratch_shapes=scratch,
        compiler_params=_params("parallel", "parallel"),
        name=name,
    )(*([proj] * len(x_specs)), conv_w, conv_w, conv_w, conv_b3, conv_b3, conv_b3,
      *(list(filt) * HY_ORDER), d_skip, cq, sq)


def kernel(x_prompt, x_sample, state_gla, c, c_ctx, mod_w, mod_b, norm_g, final_norm_g, gla_w_in, gla_w_dec, gla_b_dec, gla_onorm_g, gla_w_out, fn_w_in, fn_w_out, hy_w_in, hy_conv_w, hy_conv_b, hy_ffn_w1, hy_ffn_b1, hy_ffn_w2, hy_ffn_b2, hy_ffn_w3, hy_ffn_b3, hy_ffn_w4, hy_freq, hy_d, hy_w_out):
    n_p, l_p, d = x_prompt.shape
    n_s, l_s, _ = x_sample.shape
    depth = mod_w.shape[0]
    key = gla_w_dec.shape[-1]
    n_main = gla_w_in.shape[-1] - 2 * GLA_RANK

    cvec = jnp.concatenate([c_ctx[None], c, jnp.zeros((MOD_ROWS - 1 - n_s, d), F32)], axis=0)
    mod4 = _modulation(cvec, mod_w, mod_b).reshape(depth, MOD_ROWS, 1, 3 * d)
    norm_g3 = norm_g.reshape(depth, 1, d)

    gla_w_main = gla_w_in.astype(BF16)
    gla_w_lr = gla_w_in[:, :, n_main:].astype(BF16)
    gla_w_out_b = gla_w_out.astype(BF16)
    fn_w_in_b = fn_w_in.astype(BF16)
    fn_w_out_b = fn_w_out.astype(BF16)
    hy_w_in_b = hy_w_in.astype(BF16)
    hy_w_out_b = hy_w_out.astype(BF16)
    emb = hy_ffn_w1.shape[1]
    emb_pad = -(-emb // 128) * 128
    hy_w1p = jnp.pad(hy_ffn_w1, ((0, 0), (0, emb_pad - emb), (0, 0)))
    ffn = hy_ffn_w2.shape[-1]
    hy_vecs = [a.reshape(a.shape[0], 1, ffn) for a in (hy_ffn_b1, hy_ffn_b2, hy_ffn_b3, hy_freq)]

    tm = 512
    tiles_per_sample = l_s // tm
    streams = {
        "p": dict(x=x_prompt.reshape(n_p * l_p, d), n=n_p, l=l_p, row=lambda i: 0),
        "s": dict(x=x_sample.reshape(n_s * l_s, d), n=n_s, l=l_s, row=lambda i: 1 + i // tiles_per_sample),
    }
    new_states = None
    for i in range(depth):
        kind, j = i % N_MIXERS, i // N_MIXERS
        final_g = final_norm_g if i == depth - 1 else None
        if kind == 2:
            filt = {}
            for tag, st in streams.items():
                ce = min(d, 256 if st["l"] > 256 else d)
                filt[tag] = _hyena_filters(st["l"], j, emb, hy_w1p, hy_vecs[0], hy_ffn_w2, hy_vecs[1], hy_ffn_w3,
                                           hy_vecs[2], hy_ffn_w4, hy_vecs[3], ce, st["l"] >= HY_SPLIT_MIN_SEQ,
                                           f"hyfilt_{tag}{i}")
        for tag, st in streams.items():
            x, n_seq, seq, row = st["x"], st["n"], st["l"], st["row"]
            if kind == 0:
                proj, lr = _inproj(
                    x, norm_g3, mod4, i, row,
                    [(gla_w_main, (None, d, n_main), (j, 0, 0)), (gla_w_lr, (None, d, 2 * GLA_RANK), (j, 0, 0))],
                    tm, False, f"inproj_{tag}{i}")
                is_ctx = tag == "p"
                res = _gla_mixer(proj, lr, gla_w_dec, gla_b_dec, gla_onorm_g,
                                 None if is_ctx else state_gla, j, n_seq, seq, is_ctx, new_states,
                                 f"gla_{tag}{i}")
                y = res[0]
                if is_ctx:
                    new_states = res[1]
                w_out = gla_w_out_b
            elif kind == 1:
                (proj,) = _inproj(x, norm_g3, mod4, i, row,
                                  [(fn_w_in_b, (None,) + fn_w_in.shape[1:], (j, 0, 0))], tm, False,
                                  f"inproj_{tag}{i}")
                y = _fnet_mixer(proj, n_seq, seq, 4 if seq <= 256 else 1, f"fnet_{tag}{i}")
                w_out = fn_w_out_b
            else:
                split = seq >= HY_SPLIT_MIN_SEQ
                (proj,) = _inproj(x, norm_g3, mod4, i, row,
                                  [(hy_w_in_b, (None,) + hy_w_in.shape[1:], (j, 0, 0))], tm, split,
                                  f"inproj_{tag}{i}")
                ce = min(d, 256 if seq > 256 else d)
                y = _hyena_mixer(proj, hy_conv_w, hy_conv_b, filt[tag], hy_d, j, n_seq, seq, ce, split,
                                 f"hyena_{tag}{i}")
                w_out = hy_w_out_b
            ys = y if isinstance(y, (list, tuple)) else [y]
            st["x"] = _outproj(ys, w_out, j, x, mod4, i, row, final_g, tm, f"outproj_{tag}{i}")
    y_prompt = streams["p"]["x"].reshape(n_p, l_p, d)
    y_sample = streams["s"]["x"].reshape(n_s, l_s, d)
    new_state_gla = new_states.astype(x_prompt.dtype)
    return (y_prompt, y_sample, new_state_gla)
```

```python
import functools
import math

import jax
import jax.numpy as jnp
import numpy as np
from jax import lax
from jax.experimental import pallas as pl
from jax.experimental.pallas import tpu as pltpu

F32 = jnp.float32
BF16 = jnp.bfloat16

EPS = 1e-6
N_MIXERS = 3
GLA_HEADS = 4
GLA_RANK = 16
GLA_GATE_NORM = 16.0
FN_GROUPS = 4
HY_ORDER = 2
HY_TARGET = 1e-2
HY_MIN_DECAY = math.log(HY_TARGET) / 1.5
HY_MAX_DECAY = math.log(HY_TARGET) / 0.3
HY_ROW_BLOCK = 512
HY_SPLIT_MIN_SEQ = 512

V7X_VMEM_BYTES = 64 * 1024 * 1024
VMEM_LIMIT_BYTES = V7X_VMEM_BYTES - 8 * 1024 * 1024
MOD_ROWS = 8

_NT = (((1,), (1,)), ((), ()))
_TN = (((0,), (0,)), ((), ()))


def _params(*sem):
    return pltpu.CompilerParams(dimension_semantics=sem, vmem_limit_bytes=VMEM_LIMIT_BYTES)


def _dot(a, b):
    return jnp.dot(a, b, preferred_element_type=F32)


def _split3(x):
    hi = x.astype(BF16)
    r1 = x - hi.astype(F32)
    mid = r1.astype(BF16)
    lo = (r1 - mid.astype(F32)).astype(BF16)
    return hi, mid, lo


def _dot3(a, b):
    ah = a.astype(BF16)
    al = (a - ah.astype(F32)).astype(BF16)
    bh = b.astype(BF16)
    bl = (b - bh.astype(F32)).astype(BF16)
    return _dot(jnp.concatenate([ah, ah, al], axis=1), jnp.concatenate([bh, bl, bh], axis=0))


def _dot_exact_lhs(a_bf16, b):
    return _dot(jnp.concatenate([a_bf16] * 3, axis=1), jnp.concatenate(_split3(b), axis=0))


def _trig_bf16(ang):
    return (jnp.asarray(np.cos(ang), F32).astype(BF16), jnp.asarray(np.sin(ang), F32).astype(BF16))


def _silu(x):
    return x / (1.0 + jnp.exp(-x))


def _log_sigmoid(x):
    return jnp.minimum(x, 0.0) - jnp.log(1.0 + jnp.exp(-jnp.abs(x)))


def _mod_body(c_ref, w_ref, b_ref, o_ref):
    o_ref[...] = _dot3(_silu(c_ref[...]), w_ref[...]) + b_ref[...]


def _modulation(cvec, mod_w, mod_b):
    depth, d, e = mod_w.shape
    tn = d
    return pl.pallas_call(
        _mod_body,
        grid=(depth, e // tn),
        in_specs=[
            pl.BlockSpec((MOD_ROWS, d), lambda l, j: (0, 0)),
            pl.BlockSpec((None, d, tn), lambda l, j: (l, 0, j)),
            pl.BlockSpec((None, 1, tn), lambda l, j: (l, 0, j)),
        ],
        out_specs=pl.BlockSpec((None, MOD_ROWS, tn), lambda l, j: (l, 0, j)),
        out_shape=jax.ShapeDtypeStruct((depth, MOD_ROWS, e), F32),
        compiler_params=_params("parallel", "parallel"),
        name="modulation",
    )(cvec, mod_w, mod_b.reshape(depth, 1, e))


def _phase_permutation(rows):
    order = np.concatenate([np.arange(0, rows, 2), np.arange(1, rows, 2)])
    return jnp.asarray(np.eye(rows, dtype=np.float32)[order], BF16)


def _proj_body(*refs, n_y, has_res, final, has_next, split, n_w):
    it = iter(refs)
    take = lambda n: [next(it) for _ in range(n)]
    if has_res:
        y_refs = take(n_y)
        unperm_ref = next(it) if n_y == 2 else None
        w_out_ref, gate_ref = take(2)
        fg_ref = next(it) if final else None
    x_ref = next(it)
    if has_next:
        g_ref, mod_ref = take(2)
        perm_ref = next(it) if split else None
        w_refs = take(n_w)
    x_out_ref = next(it) if has_res else None
    o_refs = take(n_w) if has_next else []

    x = x_ref[...]
    d = x.shape[-1]
    if has_res:
        if n_y == 2:
            y = _dot(unperm_ref[...], jnp.concatenate([r[...] for r in y_refs], axis=0)).astype(BF16)
        else:
            y = y_refs[0][...]
        x = x + gate_ref[...][:, 2 * d:] * _dot(y, w_out_ref[...])
        if final:
            x_out_ref[...] = x * lax.rsqrt(jnp.mean(x * x, axis=-1, keepdims=True) + EPS) * fg_ref[...]
        else:
            x_out_ref[...] = x
    if has_next:
        h = x * lax.rsqrt(jnp.mean(x * x, axis=-1, keepdims=True) + EPS) * g_ref[...]
        mod = mod_ref[...]
        hb = (h * (1.0 + mod[:, d:2 * d]) + mod[:, :d]).astype(BF16)
        if split:
            hb = _dot(perm_ref[...], hb).astype(BF16)
        for w_ref, o_ref in zip(w_refs, o_refs):
            res = _dot(hb, w_ref[...])
            if split:
                half, n = res.shape[0] // 2, res.shape[1]
                o_ref[:, :n] = res[:half]
                o_ref[:, n:] = res[half:]
            else:
                o_ref[...] = res


def _proj(x, mod4, mod_row, tm, name, res=None, nxt=None, final_g=None):
    t, d = x.shape
    has_res, has_next, final = res is not None, nxt is not None, final_g is not None
    mod_spec = lambda layer: pl.BlockSpec((None, None, 1, mod4.shape[-1]), lambda i: (layer, mod_row(i), 0, 0))
    in_specs, args, out_specs, out_shape = [], [], [], []
    n_y, split, n_w = 0, False, 0
    if has_res:
        ys, w_out, j, layer = res
        n_y = len(ys)
        in_specs += [pl.BlockSpec((tm // n_y, y.shape[-1]), lambda i: (i, 0)) for y in ys]
        args += list(ys)
        if n_y == 2:
            in_specs.append(pl.BlockSpec((tm, tm), lambda i: (0, 0)))
            args.append(_phase_permutation(tm).T)
        in_specs += [pl.BlockSpec((None,) + w_out.shape[1:], lambda i: (j, 0, 0), pipeline_mode=pl.Buffered(1)),
                     mod_spec(layer)]
        args += [w_out, mod4]
        if final:
            in_specs.append(pl.BlockSpec((1, d), lambda i: (0, 0)))
            args.append(final_g.reshape(1, d))
        out_specs.append(pl.BlockSpec((tm, d), lambda i: (i, 0)))
        out_shape.append(jax.ShapeDtypeStruct((t, d), F32))
    in_specs.append(pl.BlockSpec((tm, d), lambda i: (i, 0)))
    args.append(x)
    if has_next:
        norm_g3, next_layer, weights, split = nxt
        n_w = len(weights)
        in_specs += [pl.BlockSpec((None, 1, d), lambda i: (next_layer, 0, 0)), mod_spec(next_layer)]
        args += [norm_g3, mod4]
        if split:
            in_specs.append(pl.BlockSpec((tm, tm), lambda i: (0, 0)))
            args.append(_phase_permutation(tm))
        in_specs += [pl.BlockSpec(bs, functools.partial(lambda idx, i: idx, idx), pipeline_mode=pl.Buffered(1))
                     for _, bs, idx in weights]
        args += [w for w, _, _ in weights]
        p = 2 if split else 1
        widths = [bs[-1] for _, bs, _ in weights]
        out_specs += [pl.BlockSpec((tm // p, p * n), lambda i: (i, 0)) for n in widths]
        out_shape += [jax.ShapeDtypeStruct((t // p, p * n), F32) for n in widths]
    return pl.pallas_call(
        functools.partial(_proj_body, n_y=n_y, has_res=has_res, final=final, has_next=has_next,
                          split=split, n_w=n_w),
        grid=(t // tm,),
        in_specs=in_specs,
        out_specs=out_specs,
        out_shape=out_shape,
        compiler_params=_params("parallel"),
        name=name,
    )(*args)


GLA_CHUNK = 64
GLA_BLOCK = 256
GLA_HEADS_PER_STEP = 2
LOG2_E = math.log2(math.e)
GLA_CLAMP_SPLIT = 8.0 * LOG2_E
GLA_CLAMP_HALF = 80.0 * LOG2_E


def _gla_tables(rows, dk):
    c, hc = GLA_CHUNK, GLA_CHUNK // 2
    r = np.arange(rows)[:, None]
    s = np.arange(rows)[None, :]
    same_chunk = (r // c) == (s // c)
    same_half = (r // hc) == (s // hc)
    before = [s <= r, s >= r]
    tri = np.stack([same_chunk & b for b in before])
    m_half = np.stack([same_half & b for b in before])
    sel = np.concatenate([s == (r // c) * c + hc, s == (r // hc) * hc + hc // 2], axis=0)
    second = np.broadcast_to((r // hc) % 2 == 1, (rows, dk))
    as_const = lambda m, dt: jnp.asarray(m.astype(np.float32), dt)
    return (as_const(tri, BF16), as_const(sel, BF16), as_const(same_chunk, F32), as_const(m_half, F32),
            as_const(second, F32))


def _gla_body(q_ref, k_ref, v_ref, r_ref, lr_ref, wdec_ref, bdec_ref, og_ref, tri_ref, sel_ref,
              mchunk_ref, mhalf_ref, second_ref, *rest, seq, layer, has_s0, want_state, has_states_in):
    rest = list(rest)
    s0_ref = rest.pop(0) if has_s0 else None
    if has_states_in:
        rest.pop(0)
    y_ref = rest.pop(0)
    sfin_ref = rest.pop(0) if want_state else None
    o_ref, qi_ref, cum_ref, st_ref = rest
    chunk = GLA_CHUNK
    n_chunks = seq // chunk
    blk = tri_ref.shape[-1]
    n_heads = wdec_ref.shape[0]
    dk = q_ref.shape[-1] // n_heads
    dv = v_ref.shape[-1] // n_heads
    scale = dk ** -0.5
    heads = range(n_heads)
    dirs = range(2)
    chains = [(hh, d) for hh in heads for d in dirs]

    def nt(a, b):
        return lax.dot_general(a, b, _NT, preferred_element_type=F32)

    def cols(hh, width):
        return slice(hh * width, (hh + 1) * width)

    def block_pass(bi, carry):
        rows = pl.ds(pl.multiple_of(bi * blk, blk), blk)
        q = [q_ref[rows, cols(hh, dk)] * scale for hh in heads]
        k = [k_ref[rows, cols(hh, dk)] for hh in heads]
        lr = lr_ref[rows, :]
        g = [_log_sigmoid(_dot3(lr, wdec_ref[hh]) + bdec_ref[hh]) * (LOG2_E / GLA_GATE_NORM) for hh in heads]
        cum = {(hh, d): _dot_exact_lhs(tri_ref[d], g[hh][:, cols(d, dk)]) for hh, d in chains}
        refs = [_dot(sel_ref[...], jnp.concatenate([cum[hh, 0], cum[hh, 1]], axis=1).astype(BF16))
                for hh in heads]
        second = second_ref[...]
        first = 1.0 - second
        p_split, p_half = [], {}
        for hh in heads:
            q_in = (q[hh] * second, q[hh] * first)
            k_in = (k[hh] * first, k[hh] * second)
            q1, k1 = [], []
            for d in dirs:
                c = (hh, d)
                d_split = cum[c] - refs[hh][:blk, cols(d, dk)]
                d_half = cum[c] - refs[hh][blk:, cols(d, dk)]
                q1.append((q_in[d] * jnp.exp2(jnp.minimum(d_split, GLA_CLAMP_SPLIT))).astype(BF16))
                k1.append((k_in[d] * jnp.exp2(jnp.minimum(-d_split, GLA_CLAMP_SPLIT))).astype(BF16))
                q0 = (q[hh] * jnp.exp2(jnp.minimum(d_half, GLA_CLAMP_HALF))).astype(BF16)
                k0 = (k[hh] * jnp.exp2(jnp.minimum(-d_half, GLA_CLAMP_HALF))).astype(BF16)
                p_half[c] = nt(q0, k0)
                qi_ref[hh, d, rows, :] = (q[hh] * jnp.exp2(cum[c])).astype(BF16)
                cum_ref[hh, d, rows, :] = cum[c]
            p_split.append(nt(jnp.concatenate(q1, axis=1), jnp.concatenate(k1, axis=1)))
        for hh in heads:
            att = p_split[hh] * mchunk_ref[...]
            for d in dirs:
                att = att + jnp.where(mhalf_ref[d] > 0.5, p_half[hh, d], 0.0)
            o_ref[rows, cols(hh, dv)] = _dot(att.astype(BF16), v_ref[rows, cols(hh, dv)].astype(BF16))
        return carry

    lax.fori_loop(0, seq // blk, block_pass, 0)

    for hh, d in chains:
        st_ref[hh, d] = s0_ref[d, hh] if has_s0 else jnp.zeros((dk, dv), F32)

    def chunk_step(i, carry):
        for hh, d in chains:
            c = i if d == 0 else n_chunks - 1 - i
            sl = pl.ds(pl.multiple_of(c * chunk, chunk), chunk)
            cum = cum_ref[hh, d, sl, :]
            blast = cum[chunk - 1:chunk] if d == 0 else cum[0:1]
            kh = (k_ref[sl, cols(hh, dk)] * jnp.exp2(blast - cum)).astype(BF16)
            vb = v_ref[sl, cols(hh, dv)].astype(BF16)
            st = st_ref[hh, d]
            o_ref[sl, cols(hh, dv)] += _dot(qi_ref[hh, d, sl, :], st.astype(BF16))
            decay = jnp.broadcast_to(jnp.exp2(blast), (dk, dk)).T
            st_ref[hh, d] = (jnp.concatenate([decay] * (dv // dk), axis=1) * st
                             + lax.dot_general(kh, vb, _TN, preferred_element_type=F32))
        return carry

    lax.fori_loop(0, n_chunks, chunk_step, 0, unroll=min(n_chunks, 4))

    def finish(jb, carry):
        sl = pl.ds(pl.multiple_of(jb * blk, blk), blk)
        for hh in heads:
            o = o_ref[sl, cols(hh, dv)]
            o = o * lax.rsqrt(jnp.mean(o * o, axis=-1, keepdims=True) + EPS) * og_ref[...]
            y_ref[sl, cols(hh, dv)] = (o * _silu(r_ref[sl, cols(hh, dv)])).astype(BF16)
        return carry

    lax.fori_loop(0, seq // blk, finish, 0)
    if want_state:
        own = sfin_ref if has_states_in else sfin_ref.at[layer]
        for hh, d in chains:
            own[d, hh] = st_ref[hh, d]
        if not has_states_in:
            for other in range(sfin_ref.shape[0]):
                if other != layer:
                    sfin_ref[other] = jnp.zeros(sfin_ref.shape[1:], F32)


def _gla_mixer(proj, lr, w_dec, b_dec, onorm_g, s0, j, n_seq, seq, want_state, states, name):
    t = proj.shape[0]
    h = GLA_HEADS
    key = w_dec.shape[-1]
    dk = key // h
    val = (proj.shape[1] - 2 * key) // 2
    dv = val // h
    has_s0 = s0 is not None
    tables = _gla_tables(min(seq, GLA_BLOCK), dk)
    fixed = lambda a: pl.BlockSpec(a.shape, lambda b, hp: (0,) * a.ndim)
    hps = h if seq <= GLA_BLOCK else GLA_HEADS_PER_STEP
    wk, wv = hps * dk, hps * dv
    in_specs = [
        pl.BlockSpec((seq, wk), lambda b, hp: (b, hp)),
        pl.BlockSpec((seq, wk), lambda b, hp: (b, key // wk + hp)),
        pl.BlockSpec((seq, wv), lambda b, hp: (b, (2 * key) // wv + hp)),
        pl.BlockSpec((seq, wv), lambda b, hp: (b, (2 * key + val) // wv + hp)),
        pl.BlockSpec((seq, 2 * GLA_RANK), lambda b, hp: (b, 0)),
        pl.BlockSpec((None, hps, 2 * GLA_RANK, 2 * dk), lambda b, hp: (j, hp, 0, 0)),
        pl.BlockSpec((None, hps, 1, 2 * dk), lambda b, hp: (j, hp, 0, 0)),
        pl.BlockSpec((None, 1, dv), lambda b, hp: (j, 0, 0)),
    ] + [fixed(a) for a in tables]
    n_gla = w_dec.shape[0]
    w5 = w_dec.reshape(n_gla, 2, GLA_RANK, h, dk)
    zero = jnp.zeros_like(w5[:, 0])
    w_blk = jnp.concatenate([jnp.concatenate([w5[:, 0], zero], axis=-1),
                             jnp.concatenate([zero, w5[:, 1]], axis=-1)], axis=1).transpose(0, 2, 1, 3)
    b_cat = b_dec.reshape(n_gla, 2, h, dk).transpose(0, 2, 1, 3).reshape(n_gla, h, 1, 2 * dk)
    args = [proj, proj, proj, proj, lr, w_blk, b_cat, onorm_g.reshape(n_gla, 1, dv), *tables]
    if has_s0:
        in_specs.append(pl.BlockSpec((None, None, 2, hps, dk, dv), lambda b, hp: (b, j, 0, hp, 0, 0)))
        args.append(s0)
    out_specs = [pl.BlockSpec((seq, wv), lambda b, hp: (b, hp))]
    out_shape = [jax.ShapeDtypeStruct((t, val), BF16)]
    aliases = {}
    if want_state:
        out_shape.append(jax.ShapeDtypeStruct((n_seq, n_gla, 2, h, dk, dv), F32))
        if states is None:
            out_specs.append(pl.BlockSpec((None, n_gla, 2, hps, dk, dv), lambda b, hp: (b, 0, 0, hp, 0, 0)))
        else:
            out_specs.append(pl.BlockSpec((None, None, 2, hps, dk, dv), lambda b, hp: (b, j, 0, hp, 0, 0)))
            in_specs.append(pl.BlockSpec(memory_space=pl.ANY))
            args.append(states)
            aliases = {len(args) - 1: 1}
    return pl.pallas_call(
        functools.partial(_gla_body, seq=seq, layer=j, has_s0=has_s0, want_state=want_state,
                          has_states_in=bool(aliases)),
        grid=(n_seq, h // hps),
        in_specs=in_specs,
        out_specs=out_specs,
        out_shape=out_shape,
        input_output_aliases=aliases,
        scratch_shapes=[pltpu.VMEM((seq, wv), F32), pltpu.VMEM((hps, 2, seq, dk), BF16),
                        pltpu.VMEM((hps, 2, seq, dk), F32), pltpu.VMEM((hps, 2, dk, dv), F32)],
        compiler_params=_params("parallel", "parallel"),
        name=name,
    )(*args)


def _dft_tables(n):
    f = np.arange(n, dtype=np.int64)
    ang = (np.outer(f, f) % n).astype(np.float64) * (2.0 * math.pi / n)
    return _trig_bf16(ang)


def _fnet_body(x_ref, z_ref, cc_ref, sc_ref, cl_ref, sl_ref, y_ref, *, seq, n_seq):
    xb = x_ref[...].astype(BF16)
    a = _dot(xb, cc_ref[...]).astype(BF16)
    bm = _dot(xb, sc_ref[...]).astype(BF16)
    inv = 1.0 / math.sqrt(seq * x_ref.shape[-1])
    for i in range(n_seq):
        rows = slice(i * seq, (i + 1) * seq)
        f = _dot(cl_ref[...], a[rows]) - _dot(sl_ref[...], bm[rows])
        y_ref[rows, :] = (f * inv * _silu(z_ref[rows, :])).astype(BF16)


def _fnet_mixer(proj, n_seq_total, seq, seqs_per_step, name):
    t = proj.shape[0]
    width = proj.shape[1] // 2
    gc = width // FN_GROUPS
    cc, sc = _dft_tables(gc)
    cl, sl = _dft_tables(seq)
    rows = seq * seqs_per_step
    const = pl.Buffered(1)
    return pl.pallas_call(
        functools.partial(_fnet_body, seq=seq, n_seq=seqs_per_step),
        grid=(n_seq_total // seqs_per_step, FN_GROUPS),
        in_specs=[
            pl.BlockSpec((rows, gc), lambda i, g: (i, g)),
            pl.BlockSpec((rows, gc), lambda i, g: (i, FN_GROUPS + g)),
            pl.BlockSpec((gc, gc), lambda i, g: (0, 0), pipeline_mode=const),
            pl.BlockSpec((gc, gc), lambda i, g: (0, 0), pipeline_mode=const),
            pl.BlockSpec((seq, seq), lambda i, g: (0, 0), pipeline_mode=const),
            pl.BlockSpec((seq, seq), lambda i, g: (0, 0), pipeline_mode=const),
        ],
        out_specs=pl.BlockSpec((rows, gc), lambda i, g: (i, g)),
        out_shape=jax.ShapeDtypeStruct((t, width), BF16),
        compiler_params=_params("parallel", "parallel"),
        name=name,
    )(proj, proj, cc, sc, cl, sl)


def _signal_tables(n):
    o = 2 * np.arange(n, dtype=np.int64) + 1
    ang = (np.outer(o, o) % (8 * n)).astype(np.float64) * (math.pi / (4 * n))
    return _trig_bf16(ang)


def _filter_tables(n, split):
    f = np.arange(n, dtype=np.int64)
    if split:
        f = np.concatenate([f[:n // 2], f[:n // 2 - 1:-1]])
    f = 2 * f + 1
    o = np.arange(n, dtype=np.int64)
    ang = (np.outer(f, o) % (4 * n)).astype(np.float64) * (math.pi / (2 * n))
    return _trig_bf16(ang)


def _hyena_positions(seq, bands, pad_to):
    t = np.linspace(0.0, 1.0, seq)[:, None]
    w = 2.0 * math.pi * np.arange(seq)[:, None] / seq
    f = np.linspace(1e-4, bands - 1, bands)[None, :]
    zpos = np.concatenate([t, np.cos(f * w), -np.sin(f * w)], axis=-1)
    zpos = np.pad(zpos, ((0, 0), (0, pad_to - zpos.shape[1])))
    return jnp.asarray(zpos, F32), jnp.asarray(t, F32)


def _hyfilt_body(zpos_ref, w1_ref, b1_ref, w2_ref, b2_ref, w3_ref, b3_ref, fr_ref, w4f_ref,
                 w4b_ref, t_ref, del_ref, ch_ref, sh_ref, *rest, split):
    if split:
        c2_ref, s2_ref = rest[:2]
        out_refs, a3_ref = rest[2:-1], rest[-1]
    else:
        out_refs, a3_ref = rest[:-1], rest[-1]

    @pl.when(pl.program_id(0) == 0)
    def _():
        fr = fr_ref[...]
        a = jnp.sin(fr * (_dot3(zpos_ref[...], w1_ref[...]) + b1_ref[...]))
        a = jnp.sin(fr * (_dot3(a, w2_ref[...]) + b2_ref[...]))
        a3_ref[...] = jnp.sin(fr * (_dot3(a, w3_ref[...]) + b3_ref[...]))

    a3 = a3_ref[...]
    dec = jnp.exp(-t_ref[...] * del_ref[...])
    fwd = _dot3(a3, w4f_ref[...]) * dec
    bwd = _dot3(a3, w4b_ref[...]) * dec
    first = lax.broadcasted_iota(jnp.int32, bwd.shape, 0) == 0
    bwd = jnp.where(first, 0.0, bwd)
    hre = _dot(ch_ref[...], (fwd + bwd).astype(BF16))
    him = _dot(sh_ref[...], (bwd - fwd).astype(BF16))
    if not split:
        out_refs[0][...] = hre
        out_refs[1][...] = him
        return
    half = hre.shape[0] // 2
    plus_r, plus_i = hre[:half] + hre[half:], him[:half] - him[half:]
    minus_r, minus_i = hre[:half] - hre[half:], him[:half] + him[half:]
    c2, s2 = c2_ref[...], s2_ref[...]
    for ref, val in zip(out_refs, (plus_r, plus_i,
                                   c2 * minus_r + s2 * minus_i, c2 * minus_i - s2 * minus_r,
                                   c2 * minus_r - s2 * minus_i, c2 * minus_i + s2 * minus_r)):
        ref[...] = val


def _hyena_filters(seq, j, emb, w1p, b1, w2, b2, w3, b3, w4, freq, ce, split, name):
    ffn = w2.shape[-1]
    width = w4.shape[-1] // (2 * HY_ORDER)
    emb_pad = w1p.shape[1]
    zpos, tcol = _hyena_positions(seq, (emb - 1) // 2, emb_pad)
    deltas = jnp.asarray(np.abs(np.linspace(HY_MIN_DECAY, HY_MAX_DECAY, width, dtype=np.float32))[None, :])
    ch, sh = _filter_tables(seq, split)
    nb = width // ce
    const = pl.Buffered(1)
    vec = lambda: pl.BlockSpec((None, 1, ffn), lambda c: (j, 0, 0))
    out_rows, n_out = (seq // 2, 6) if split else (seq, 2)
    twiddles, twiddle_specs = [], []
    if split:
        ang = np.broadcast_to((math.pi * (2 * np.arange(out_rows) + 1) / (2 * seq))[:, None], (out_rows, ce))
        twiddles = [jnp.asarray(np.cos(ang), F32), jnp.asarray(np.sin(ang), F32)]
        twiddle_specs = [pl.BlockSpec((out_rows, ce), lambda c: (0, 0), pipeline_mode=const)] * 2
    return pl.pallas_call(
        functools.partial(_hyfilt_body, split=split),
        grid=(HY_ORDER * nb,),
        in_specs=[
            pl.BlockSpec((seq, emb_pad), lambda c: (0, 0)),
            pl.BlockSpec((None, emb_pad, ffn), lambda c: (j, 0, 0)),
            vec(),
            pl.BlockSpec((None, ffn, ffn), lambda c: (j, 0, 0)),
            vec(),
            pl.BlockSpec((None, ffn, ffn), lambda c: (j, 0, 0)),
            vec(),
            vec(),
            pl.BlockSpec((None, ffn, ce), lambda c: (j, 0, (c // nb) * 2 * nb + c % nb)),
            pl.BlockSpec((None, ffn, ce), lambda c: (j, 0, (c // nb) * 2 * nb + nb + c % nb)),
            pl.BlockSpec((seq, 1), lambda c: (0, 0)),
            pl.BlockSpec((1, ce), lambda c: (0, c % nb)),
            pl.BlockSpec((seq, seq), lambda c: (0, 0), pipeline_mode=const),
            pl.BlockSpec((seq, seq), lambda c: (0, 0), pipeline_mode=const),
        ] + twiddle_specs,
        out_specs=[pl.BlockSpec((out_rows, ce), lambda c: (0, c))] * n_out,
        out_shape=[jax.ShapeDtypeStruct((out_rows, HY_ORDER * width), F32)] * n_out,
        scratch_shapes=[pltpu.VMEM((seq, ffn), F32)],
        compiler_params=_params("arbitrary"),
        name=name,
    )(zpos, w1p, b1, w2, b2, w3, b3, freq, w4, w4, tcol, deltas, ch, sh, *twiddles)


def _hyena_body(x1_ref, x2_ref, v_ref, z_ref, cw1_ref, cw2_ref, cwv_ref, cb1_ref, cb2_ref, cbv_ref,
                hre0_ref, him0_ref, hre1_ref, him1_ref, d_ref, cq_ref, sq_ref, y_ref,
                cur_ref, cb_ref, yr_ref, yi_ref, *, seq, rows):
    n_rb = seq // rows
    inv_len = 1.0 / seq
    rid = lax.broadcasted_iota(jnp.int32, (rows, 1), 0)

    def block(rb):
        return pl.ds(pl.multiple_of(rb * rows, rows), rows)

    def short_conv(u_ref, w_ref, b_ref, rb):
        u = u_ref[block(rb), :]
        before = u_ref[pl.ds(jnp.maximum(rb * rows - 1, 0), 1), :]
        after = u_ref[pl.ds(jnp.minimum((rb + 1) * rows, seq - 1), 1), :]
        before = jnp.where(rb == 0, 0.0, before)
        after = jnp.where(rb == n_rb - 1, 0.0, after)
        prev = jnp.where(rid == 0, before, pltpu.roll(u, 1, 0))
        nxt = jnp.where(rid == rows - 1, after, pltpu.roll(u, rows - 1, 0))
        return prev * w_ref[0:1] + u * w_ref[1:2] + nxt * w_ref[2:3] + b_ref[...]

    def long_conv_gate(x_ref, cw_ref, cbias_ref, hre_ref, him_ref, dn):
        def fwd(rb, carry):
            sl = block(rb)
            ur = _dot(cq_ref[sl, :], cb_ref[...])
            ui = -_dot(sq_ref[sl, :], cb_ref[...])
            hr = hre_ref[sl, :]
            hi = him_ref[sl, :]
            yr_ref[sl, :] = (ur * hr - ui * hi).astype(BF16)
            yi_ref[sl, :] = (ur * hi + ui * hr).astype(BF16)
            return carry

        lax.fori_loop(0, n_rb, fwd, 0)

        def inv(rb, carry):
            sl = block(rb)
            c = (_dot(cq_ref[sl, :], yr_ref[...]) - _dot(sq_ref[sl, :], yi_ref[...])) * inv_len
            new = short_conv(x_ref, cw_ref, cbias_ref, rb) * (c + cur_ref[sl, :] * dn)
            cur_ref[sl, :] = new
            cb_ref[sl, :] = new.astype(BF16)
            return carry

        lax.fori_loop(0, n_rb, inv, 0)

    def load_v(rb, carry):
        v = short_conv(v_ref, cwv_ref, cbv_ref, rb)
        cur_ref[block(rb), :] = v
        cb_ref[block(rb), :] = v.astype(BF16)
        return carry

    lax.fori_loop(0, n_rb, load_v, 0)
    long_conv_gate(x1_ref, cw1_ref, cb1_ref, hre0_ref, him0_ref, d_ref[0:1])
    long_conv_gate(x2_ref, cw2_ref, cb2_ref, hre1_ref, him1_ref, d_ref[1:2])

    def store(rb, carry):
        sl = block(rb)
        y_ref[sl, :] = (cur_ref[sl, :] * _silu(z_ref[sl, :])).astype(BF16)
        return carry

    lax.fori_loop(0, n_rb, store, 0)


def _hyena_split_body(x1_refs, x2_refs, v_refs, z_refs, cw1_ref, cw2_ref, cwv_ref, cb1_ref, cb2_ref, cbv_ref,
                      k_refs, d_ref, cq_ref, sq_ref, y_refs, cur_ref, sig_ref, z_buf, *, seq, rows):
    half = seq // 2
    n_rb = half // rows
    inv_len = 1.0 / seq
    rid = lax.broadcasted_iota(jnp.int32, (rows, 1), 0)

    def block(rb, shift=0):
        return slice(rb * rows + shift, (rb + 1) * rows + shift)

    def short_conv(u_refs, w_ref, b_ref, rb):
        e_ref, o_ref = u_refs
        even, odd = e_ref[block(rb), :], o_ref[block(rb), :]
        odd_before = o_ref[block(rb, -1), :] if rb > 0 else jnp.where(rid == 0, 0.0, pltpu.roll(odd, 1, 0))
        even_after = (e_ref[block(rb, 1), :] if rb < n_rb - 1
                      else jnp.where(rid == rows - 1, 0.0, pltpu.roll(even, rows - 1, 0)))
        w0, w1, w2, b = w_ref[0:1], w_ref[1:2], w_ref[2:3], b_ref[...]
        return (odd_before * w0 + even * w1 + odd * w2 + b, even * w0 + odd * w1 + even_after * w2 + b)

    def long_conv_gate(x_ref, cw_ref, cbias_ref, k, dn):
        k1r, k1i, k2r, k2i, k3r, k3i = k
        for fb in range(n_rb):
            sl = block(fb)
            c, s = cq_ref[sl, :], sq_ref[sl, :]
            er, ei = _dot(c, sig_ref[0]), -_dot(s, sig_ref[0])
            orr, oi = _dot(c, sig_ref[1]), -_dot(s, sig_ref[1])
            a1r, a1i, a2r, a2i, a3r, a3i = (r[sl, :] for r in (k1r, k1i, k2r, k2i, k3r, k3i))
            z_buf[0, sl, :] = (er * a1r - ei * a1i + orr * a2r - oi * a2i).astype(BF16)
            z_buf[1, sl, :] = (er * a1i + ei * a1r + orr * a2i + oi * a2r).astype(BF16)
            z_buf[2, sl, :] = (er * a3r - ei * a3i + orr * a1r - oi * a1i).astype(BF16)
            z_buf[3, sl, :] = (er * a3i + ei * a3r + orr * a1i + oi * a1r).astype(BF16)
        for tb in range(n_rb):
            sl = block(tb)
            c, s = cq_ref[sl, :], sq_ref[sl, :]
            conv = ((_dot(c, z_buf[0]) - _dot(s, z_buf[1])) * inv_len,
                    (_dot(c, z_buf[2]) - _dot(s, z_buf[3])) * inv_len)
            gate = short_conv(x_ref, cw_ref, cbias_ref, tb)
            for p in range(2):
                new = gate[p] * (conv[p] + cur_ref[p, sl, :] * dn)
                cur_ref[p, sl, :] = new
                sig_ref[p, sl, :] = new.astype(BF16)

    for rb in range(n_rb):
        for p, val in enumerate(short_conv(v_refs, cwv_ref, cbv_ref, rb)):
            cur_ref[p, block(rb), :] = val
            sig_ref[p, block(rb), :] = val.astype(BF16)
    long_conv_gate(x1_refs, cw1_ref, cb1_ref, k_refs[0], d_ref[0:1])
    long_conv_gate(x2_refs, cw2_ref, cb2_ref, k_refs[1], d_ref[1:2])
    for rb in range(n_rb):
        for p in range(2):
            sl = block(rb)
            y_refs[p][sl, :] = (cur_ref[p, sl, :] * _silu(z_refs[p][sl, :])).astype(BF16)


def _hyena_split_entry(*refs, seq, rows):
    it = iter(refs)
    take = lambda n: tuple(next(it) for _ in range(n))
    x1, x2, v, z = take(2), take(2), take(2), take(2)
    conv_refs = take(6)
    k = tuple(take(6) for _ in range(HY_ORDER))
    d, cq, sq = take(3)
    y = take(2)
    _hyena_split_body(x1, x2, v, z, *conv_refs, k, d, cq, sq, y, *it, seq=seq, rows=rows)


def _hyena_mixer(proj, conv_w, conv_b, filt, d_skip, j, n_seq, seq, ce, split, name):
    phases = 2 if split else 1
    t = proj.shape[0] * phases
    width = proj.shape[1] // (4 * phases)
    nb = width // ce
    n_dft = seq // phases
    cq, sq = _signal_tables(n_dft)
    rows = min(n_dft, HY_ROW_BLOCK)
    const = pl.Buffered(1)
    seg = lambda s: pl.BlockSpec((seq, ce), lambda e, b: (b, s * nb + e))
    cw = lambda s: pl.BlockSpec((None, conv_w.shape[1], ce), lambda e, b: (j, 0, s * nb + e))
    cb = lambda s: pl.BlockSpec((None, 1, ce), lambda e, b: (j, 0, s * nb + e))
    hs = lambda n: pl.BlockSpec((n_dft, ce), lambda e, b: (0, n * nb + e), pipeline_mode=const)
    conv_b3 = conv_b.reshape(conv_b.shape[0], 1, -1)
    if split:
        body = functools.partial(_hyena_split_entry, seq=seq, rows=rows)
        phase = lambda s, p: pl.BlockSpec((n_dft, ce), lambda e, b: (b, p * 4 * nb + s * nb + e))
        x_specs = [phase(s, p) for s in range(4) for p in range(2)]
        out_specs = [pl.BlockSpec((n_dft, ce), lambda e, b: (b, e))] * 2
        out_shape = [jax.ShapeDtypeStruct((t // 2, width), BF16)] * 2
        scratch = [pltpu.VMEM((2, n_dft, ce), F32), pltpu.VMEM((2, n_dft, ce), BF16),
                   pltpu.VMEM((4, n_dft, ce), BF16)]
    else:
        body = functools.partial(_hyena_body, seq=seq, rows=rows)
        x_specs = [seg(0), seg(1), seg(2), seg(3)]
        out_specs = pl.BlockSpec((seq, ce), lambda e, b: (b, e))
        out_shape = jax.ShapeDtypeStruct((t, width), BF16)
        scratch = [pltpu.VMEM((seq, ce), F32), pltpu.VMEM((seq, ce), BF16),
                   pltpu.VMEM((seq, ce), BF16), pltpu.VMEM((seq, ce), BF16)]
    return pl.pallas_call(
        body,
        grid=(nb, n_seq),
        in_specs=x_specs + [cw(0), cw(1), cw(2), cb(0), cb(1), cb(2)]
        + [hs(n) for n in range(HY_ORDER) for _ in filt]
        + [pl.BlockSpec((None, HY_ORDER, ce), lambda e, b: (j, 0, e)),
           pl.BlockSpec((n_dft, n_dft), lambda e, b: (0, 0), pipeline_mode=const),
           pl.BlockSpec((n_dft, n_dft), lambda e, b: (0, 0), pipeline_mode=const)],
        out_specs=out_specs,
        out_shape=out_shape,
        scratch_shapes=scratch,
        compiler_params=_params("parallel", "parallel"),
        name=name,
    )(*([proj] * len(x_specs)), conv_w, conv_w, conv_w, conv_b3, conv_b3, conv_b3,
      *(list(filt) * HY_ORDER), d_skip, cq, sq)


def kernel(x_prompt, x_sample, state_gla, c, c_ctx, mod_w, mod_b, norm_g, final_norm_g, gla_w_in, gla_w_dec, gla_b_dec, gla_onorm_g, gla_w_out, fn_w_in, fn_w_out, hy_w_in, hy_conv_w, hy_conv_b, hy_ffn_w1, hy_ffn_b1, hy_ffn_w2, hy_ffn_b2, hy_ffn_w3, hy_ffn_b3, hy_ffn_w4, hy_freq, hy_d, hy_w_out):
    n_p, l_p, d = x_prompt.shape
    n_s, l_s, _ = x_sample.shape
    depth = mod_w.shape[0]
    key = gla_w_dec.shape[-1]
    n_main = gla_w_in.shape[-1] - 2 * GLA_RANK

    cvec = jnp.concatenate([c_ctx[None], c, jnp.zeros((MOD_ROWS - 1 - n_s, d), F32)], axis=0)
    mod4 = _modulation(cvec, mod_w, mod_b).reshape(depth, MOD_ROWS, 1, 3 * d)
    norm_g3 = norm_g.reshape(depth, 1, d)

    gla_w_main = gla_w_in.astype(BF16)
    gla_w_lr = gla_w_in[:, :, n_main:].astype(BF16)
    gla_w_out_b = gla_w_out.astype(BF16)
    fn_w_in_b = fn_w_in.astype(BF16)
    fn_w_out_b = fn_w_out.astype(BF16)
    hy_w_in_b = hy_w_in.astype(BF16)
    hy_w_out_b = hy_w_out.astype(BF16)
    emb = hy_ffn_w1.shape[1]
    emb_pad = -(-emb // 128) * 128
    hy_w1p = jnp.pad(hy_ffn_w1, ((0, 0), (0, emb_pad - emb), (0, 0)))
    ffn = hy_ffn_w2.shape[-1]
    hy_vecs = [a.reshape(a.shape[0], 1, ffn) for a in (hy_ffn_b1, hy_ffn_b2, hy_ffn_b3, hy_freq)]

    tm = 512
    tiles_per_sample = l_s // tm
    streams = {
        "p": dict(x=x_prompt.reshape(n_p * l_p, d), n=n_p, l=l_p, row=lambda i: 0),
        "s": dict(x=x_sample.reshape(n_s * l_s, d), n=n_s, l=l_s, row=lambda i: 1 + i // tiles_per_sample),
    }
    def in_weights(i, seq):
        kind, j = i % N_MIXERS, i // N_MIXERS
        if kind == 0:
            return [(gla_w_main, (None, d, n_main), (j, 0, 0)), (gla_w_lr, (None, d, 2 * GLA_RANK), (j, 0, 0))], False
        if kind == 1:
            return [(fn_w_in_b, (None,) + fn_w_in.shape[1:], (j, 0, 0))], False
        return [(hy_w_in_b, (None,) + hy_w_in.shape[1:], (j, 0, 0))], seq >= HY_SPLIT_MIN_SEQ

    for tag, st in streams.items():
        weights, split = in_weights(0, st["l"])
        st["proj"] = _proj(st["x"], mod4, st["row"], tm, f"proj_{tag}0", nxt=(norm_g3, 0, weights, split))
    new_states = None
    for i in range(depth):
        kind, j = i % N_MIXERS, i // N_MIXERS
        if kind == 2:
            filt = {}
            for tag, st in streams.items():
                ce = min(d, 256 if st["l"] > 256 else d)
                filt[tag] = _hyena_filters(st["l"], j, emb, hy_w1p, hy_vecs[0], hy_ffn_w2, hy_vecs[1], hy_ffn_w3,
                                           hy_vecs[2], hy_ffn_w4, hy_vecs[3], ce, st["l"] >= HY_SPLIT_MIN_SEQ,
                                           f"hyfilt_{tag}{i}")
        for tag, st in streams.items():
            n_seq, seq = st["n"], st["l"]
            if kind == 0:
                proj, lr = st["proj"]
                is_ctx = tag == "p"
                res = _gla_mixer(proj, lr, gla_w_dec, gla_b_dec, gla_onorm_g,
                                 None if is_ctx else state_gla, j, n_seq, seq, is_ctx, new_states,
                                 f"gla_{tag}{i}")
                ys = [res[0]]
                if is_ctx:
                    new_states = res[1]
                w_out = gla_w_out_b
            elif kind == 1:
                ys = [_fnet_mixer(st["proj"][0], n_seq, seq, 4 if seq <= 256 else 1, f"fnet_{tag}{i}")]
                w_out = fn_w_out_b
            else:
                split = seq >= HY_SPLIT_MIN_SEQ
                ce = min(d, 256 if seq > 256 else d)
                y = _hyena_mixer(st["proj"][0], hy_conv_w, hy_conv_b, filt[tag], hy_d, j, n_seq, seq, ce, split,
                                 f"hyena_{tag}{i}")
                ys = list(y) if split else [y]
                w_out = hy_w_out_b
            if i + 1 < depth:
                weights, split = in_weights(i + 1, seq)
                st["x"], *st["proj"] = _proj(st["x"], mod4, st["row"], tm, f"proj_{tag}{i + 1}",
                                             res=(ys, w_out, j, i), nxt=(norm_g3, i + 1, weights, split))
            else:
                (st["x"],) = _proj(st["x"], mod4, st["row"], tm, f"proj_{tag}{i + 1}",
                                   res=(ys, w_out, j, i), final_g=final_norm_g)
    y_prompt = streams["p"]["x"].reshape(n_p, l_p, d)
    y_sample = streams["s"]["x"].reshape(n_s, l_s, d)
    new_state_gla = new_states.astype(x_prompt.dtype)
    return (y_prompt, y_sample, new_state_gla)
```

```python
import functools
import math

import jax
import jax.numpy as jnp
import numpy as np
from jax import lax
from jax.experimental import pallas as pl
from jax.experimental.pallas import tpu as pltpu

F32 = jnp.float32
BF16 = jnp.bfloat16

EPS = 1e-6
N_MIXERS = 3
GLA_HEADS = 4
GLA_RANK = 16
GLA_GATE_NORM = 16.0
FN_GROUPS = 4
HY_ORDER = 2
HY_TARGET = 1e-2
HY_MIN_DECAY = math.log(HY_TARGET) / 1.5
HY_MAX_DECAY = math.log(HY_TARGET) / 0.3
HY_ROW_BLOCK = 512
HY_SPLIT_MIN_SEQ = 512

V7X_VMEM_BYTES = 64 * 1024 * 1024
VMEM_LIMIT_BYTES = V7X_VMEM_BYTES - 8 * 1024 * 1024
MOD_ROWS = 8

_NT = (((1,), (1,)), ((), ()))
_TN = (((0,), (0,)), ((), ()))


def _params(*sem):
    return pltpu.CompilerParams(dimension_semantics=sem, vmem_limit_bytes=VMEM_LIMIT_BYTES)


def _dot(a, b):
    return jnp.dot(a, b, preferred_element_type=F32)


def _split3(x):
    hi = x.astype(BF16)
    r1 = x - hi.astype(F32)
    mid = r1.astype(BF16)
    lo = (r1 - mid.astype(F32)).astype(BF16)
    return hi, mid, lo


def _dot3(a, b):
    ah = a.astype(BF16)
    al = (a - ah.astype(F32)).astype(BF16)
    bh = b.astype(BF16)
    bl = (b - bh.astype(F32)).astype(BF16)
    return _dot(jnp.concatenate([ah, ah, al], axis=1), jnp.concatenate([bh, bl, bh], axis=0))


def _dot_exact_lhs(a_bf16, b):
    return _dot(jnp.concatenate([a_bf16] * 3, axis=1), jnp.concatenate(_split3(b), axis=0))


def _trig_bf16(ang):
    return (jnp.asarray(np.cos(ang), F32).astype(BF16), jnp.asarray(np.sin(ang), F32).astype(BF16))


def _silu(x):
    return x / (1.0 + jnp.exp(-x))


def _log_sigmoid(x):
    return jnp.minimum(x, 0.0) - jnp.log(1.0 + jnp.exp(-jnp.abs(x)))


def _mod_body(c_ref, w_ref, b_ref, o_ref):
    o_ref[...] = _dot3(_silu(c_ref[...]), w_ref[...]) + b_ref[...]


def _modulation(cvec, mod_w, mod_b):
    depth, d, e = mod_w.shape
    tn = d
    return pl.pallas_call(
        _mod_body,
        grid=(depth, e // tn),
        in_specs=[
            pl.BlockSpec((MOD_ROWS, d), lambda l, j: (0, 0)),
            pl.BlockSpec((None, d, tn), lambda l, j: (l, 0, j)),
            pl.BlockSpec((None, 1, tn), lambda l, j: (l, 0, j)),
        ],
        out_specs=pl.BlockSpec((None, MOD_ROWS, tn), lambda l, j: (l, 0, j)),
        out_shape=jax.ShapeDtypeStruct((depth, MOD_ROWS, e), F32),
        compiler_params=_params("parallel", "parallel"),
        name="modulation",
    )(cvec, mod_w, mod_b.reshape(depth, 1, e))


def _phase_permutation(rows):
    order = np.concatenate([np.arange(0, rows, 2), np.arange(1, rows, 2)])
    return jnp.asarray(np.eye(rows, dtype=np.float32)[order], BF16)


def _proj_body(*refs, n_y, has_res, final, has_next, split, n_w):
    it = iter(refs)
    take = lambda n: [next(it) for _ in range(n)]
    if has_res:
        y_refs = take(n_y)
        unperm_ref = next(it) if n_y == 2 else None
        w_out_ref, gate_ref = take(2)
        fg_ref = next(it) if final else None
    x_ref = next(it)
    if has_next:
        g_ref, mod_ref = take(2)
        perm_ref = next(it) if split else None
        w_refs = take(n_w)
    x_out_ref = next(it) if has_res else None
    o_refs = take(n_w) if has_next else []

    x = x_ref[...]
    d = x.shape[-1]
    if has_res:
        if n_y == 2:
            y = _dot(unperm_ref[...], jnp.concatenate([r[...] for r in y_refs], axis=0)).astype(BF16)
        else:
            y = y_refs[0][...]
        x = x + gate_ref[...][:, 2 * d:] * _dot(y, w_out_ref[...])
        if final:
            x_out_ref[...] = x * lax.rsqrt(jnp.mean(x * x, axis=-1, keepdims=True) + EPS) * fg_ref[...]
        else:
            x_out_ref[...] = x
    if has_next:
        h = x * lax.rsqrt(jnp.mean(x * x, axis=-1, keepdims=True) + EPS) * g_ref[...]
        mod = mod_ref[...]
        hb = (h * (1.0 + mod[:, d:2 * d]) + mod[:, :d]).astype(BF16)
        if split:
            hb = _dot(perm_ref[...], hb).astype(BF16)
        for w_ref, o_ref in zip(w_refs, o_refs):
            res = _dot(hb, w_ref[...])
            if split:
                half, n = res.shape[0] // 2, res.shape[1]
                o_ref[:, :n] = res[:half]
                o_ref[:, n:] = res[half:]
            else:
                o_ref[...] = res


def _proj(x, mod4, mod_row, tm, name, res=None, nxt=None, final_g=None):
    t, d = x.shape
    has_res, has_next, final = res is not None, nxt is not None, final_g is not None
    mod_spec = lambda layer: pl.BlockSpec((None, None, 1, mod4.shape[-1]), lambda i: (layer, mod_row(i), 0, 0))
    in_specs, args, out_specs, out_shape = [], [], [], []
    n_y, split, n_w = 0, False, 0
    if has_res:
        ys, w_out, j, layer = res
        n_y = len(ys)
        in_specs += [pl.BlockSpec((tm // n_y, y.shape[-1]), lambda i: (i, 0)) for y in ys]
        args += list(ys)
        if n_y == 2:
            in_specs.append(pl.BlockSpec((tm, tm), lambda i: (0, 0)))
            args.append(_phase_permutation(tm).T)
        in_specs += [pl.BlockSpec((None,) + w_out.shape[1:], lambda i: (j, 0, 0), pipeline_mode=pl.Buffered(1)),
                     mod_spec(layer)]
        args += [w_out, mod4]
        if final:
            in_specs.append(pl.BlockSpec((1, d), lambda i: (0, 0)))
            args.append(final_g.reshape(1, d))
        out_specs.append(pl.BlockSpec((tm, d), lambda i: (i, 0)))
        out_shape.append(jax.ShapeDtypeStruct((t, d), F32))
    in_specs.append(pl.BlockSpec((tm, d), lambda i: (i, 0)))
    args.append(x)
    if has_next:
        norm_g3, next_layer, weights, split = nxt
        n_w = len(weights)
        in_specs += [pl.BlockSpec((None, 1, d), lambda i: (next_layer, 0, 0)), mod_spec(next_layer)]
        args += [norm_g3, mod4]
        if split:
            in_specs.append(pl.BlockSpec((tm, tm), lambda i: (0, 0)))
            args.append(_phase_permutation(tm))
        in_specs += [pl.BlockSpec(bs, functools.partial(lambda idx, i: idx, idx), pipeline_mode=pl.Buffered(1))
                     for _, bs, idx in weights]
        args += [w for w, _, _ in weights]
        p = 2 if split else 1
        widths = [bs[-1] for _, bs, _ in weights]
        out_specs += [pl.BlockSpec((tm // p, p * n), lambda i: (i, 0)) for n in widths]
        out_shape += [jax.ShapeDtypeStruct((t // p, p * n), F32) for n in widths]
    return pl.pallas_call(
        functools.partial(_proj_body, n_y=n_y, has_res=has_res, final=final, has_next=has_next,
                          split=split, n_w=n_w),
        grid=(t // tm,),
        in_specs=in_specs,
        out_specs=out_specs,
        out_shape=out_shape,
        compiler_params=_params("parallel"),
        name=name,
    )(*args)


GLA_CHUNK = 64
GLA_BLOCK = 256
GLA_HEADS_PER_STEP = 2
LOG2_E = math.log2(math.e)
GLA_CLAMP_SPLIT = 8.0 * LOG2_E
GLA_CLAMP_HALF = 80.0 * LOG2_E


def _gla_tables(rows, dk):
    c, hc = GLA_CHUNK, GLA_CHUNK // 2
    r = np.arange(rows)[:, None]
    s = np.arange(rows)[None, :]
    same_chunk = (r // c) == (s // c)
    same_half = (r // hc) == (s // hc)
    before = [s <= r, s >= r]
    tri = np.stack([same_chunk & b for b in before])
    m_half = np.stack([same_half & b for b in before])
    sel = np.concatenate([s == (r // c) * c + hc, s == (r // hc) * hc + hc // 2], axis=0)
    second = np.broadcast_to((r // hc) % 2 == 1, (rows, dk))
    as_const = lambda m, dt: jnp.asarray(m.astype(np.float32), dt)
    return (as_const(tri, BF16), as_const(sel, BF16), as_const(same_chunk, F32), as_const(m_half, F32),
            as_const(second, F32))


def _gla_body(q_ref, k_ref, v_ref, r_ref, lr_ref, wdec_ref, bdec_ref, og_ref, tri_ref, sel_ref,
              mchunk_ref, mhalf_ref, second_ref, *rest, seq, layer, has_s0, want_state, has_states_in):
    rest = list(rest)
    s0_ref = rest.pop(0) if has_s0 else None
    if has_states_in:
        rest.pop(0)
    y_ref = rest.pop(0)
    sfin_ref = rest.pop(0) if want_state else None
    o_ref, qi_ref, cum_ref, st_ref = rest
    chunk = GLA_CHUNK
    n_chunks = seq // chunk
    blk = tri_ref.shape[-1]
    n_heads = wdec_ref.shape[0]
    dk = q_ref.shape[-1] // n_heads
    dv = v_ref.shape[-1] // n_heads
    scale = dk ** -0.5
    heads = range(n_heads)
    dirs = range(2)
    chains = [(hh, d) for hh in heads for d in dirs]

    def nt(a, b):
        return lax.dot_general(a, b, _NT, preferred_element_type=F32)

    def cols(hh, width):
        return slice(hh * width, (hh + 1) * width)

    def block_pass(bi, carry):
        rows = pl.ds(pl.multiple_of(bi * blk, blk), blk)
        q = [q_ref[rows, cols(hh, dk)] * scale for hh in heads]
        k = [k_ref[rows, cols(hh, dk)] for hh in heads]
        lr = lr_ref[rows, :]
        g = [_log_sigmoid(_dot3(lr, wdec_ref[hh]) + bdec_ref[hh]) * (LOG2_E / GLA_GATE_NORM) for hh in heads]
        cum = {(hh, d): _dot_exact_lhs(tri_ref[d], g[hh][:, cols(d, dk)]) for hh, d in chains}
        refs = [_dot(sel_ref[...], jnp.concatenate([cum[hh, 0], cum[hh, 1]], axis=1).astype(BF16))
                for hh in heads]
        second = second_ref[...]
        first = 1.0 - second
        p_split, p_half = [], {}
        for hh in heads:
            q_in = (q[hh] * second, q[hh] * first)
            k_in = (k[hh] * first, k[hh] * second)
            q1, k1 = [], []
            for d in dirs:
                c = (hh, d)
                d_split = cum[c] - refs[hh][:blk, cols(d, dk)]
                d_half = cum[c] - refs[hh][blk:, cols(d, dk)]
                q1.append((q_in[d] * jnp.exp2(jnp.minimum(d_split, GLA_CLAMP_SPLIT))).astype(BF16))
                k1.append((k_in[d] * jnp.exp2(jnp.minimum(-d_split, GLA_CLAMP_SPLIT))).astype(BF16))
                q0 = (q[hh] * jnp.exp2(jnp.minimum(d_half, GLA_CLAMP_HALF))).astype(BF16)
                k0 = (k[hh] * jnp.exp2(jnp.minimum(-d_half, GLA_CLAMP_HALF))).astype(BF16)
                p_half[c] = nt(q0, k0)
                qi_ref[hh, d, rows, :] = (q[hh] * jnp.exp2(cum[c])).astype(BF16)
                cum_ref[hh, d, rows, :] = cum[c]
            p_split.append(nt(jnp.concatenate(q1, axis=1), jnp.concatenate(k1, axis=1)))
        for hh in heads:
            att = p_split[hh] * mchunk_ref[...]
            for d in dirs:
                att = att + jnp.where(mhalf_ref[d] > 0.5, p_half[hh, d], 0.0)
            o_ref[rows, cols(hh, dv)] = _dot(att.astype(BF16), v_ref[rows, cols(hh, dv)].astype(BF16))
        return carry

    lax.fori_loop(0, seq // blk, block_pass, 0)

    for hh, d in chains:
        st_ref[hh, d] = s0_ref[d, hh] if has_s0 else jnp.zeros((dk, dv), F32)

    def chunk_step(i, carry):
        for hh, d in chains:
            c = i if d == 0 else n_chunks - 1 - i
            sl = pl.ds(pl.multiple_of(c * chunk, chunk), chunk)
            cum = cum_ref[hh, d, sl, :]
            blast = cum[chunk - 1:chunk] if d == 0 else cum[0:1]
            kh = (k_ref[sl, cols(hh, dk)] * jnp.exp2(blast - cum)).astype(BF16)
            vb = v_ref[sl, cols(hh, dv)].astype(BF16)
            st = st_ref[hh, d]
            o_ref[sl, cols(hh, dv)] += _dot(qi_ref[hh, d, sl, :], st.astype(BF16))
            decay = jnp.broadcast_to(jnp.exp2(blast), (dk, dk)).T
            st_ref[hh, d] = (jnp.concatenate([decay] * (dv // dk), axis=1) * st
                             + lax.dot_general(kh, vb, _TN, preferred_element_type=F32))
        return carry

    lax.fori_loop(0, n_chunks, chunk_step, 0, unroll=min(n_chunks, 4))

    def finish(jb, carry):
        sl = pl.ds(pl.multiple_of(jb * blk, blk), blk)
        for hh in heads:
            o = o_ref[sl, cols(hh, dv)]
            o = o * lax.rsqrt(jnp.mean(o * o, axis=-1, keepdims=True) + EPS) * og_ref[...]
            y_ref[sl, cols(hh, dv)] = (o * _silu(r_ref[sl, cols(hh, dv)])).astype(BF16)
        return carry

    lax.fori_loop(0, seq // blk, finish, 0)
    if want_state:
        own = sfin_ref if has_states_in else sfin_ref.at[layer]
        for hh, d in chains:
            own[d, hh] = st_ref[hh, d]
        if not has_states_in:
            for other in range(sfin_ref.shape[0]):
                if other != layer:
                    sfin_ref[other] = jnp.zeros(sfin_ref.shape[1:], F32)


def _gla_mixer(proj, lr, w_dec, b_dec, onorm_g, s0, j, n_seq, seq, want_state, states, name):
    t = proj.shape[0]
    h = GLA_HEADS
    key = w_dec.shape[-1]
    dk = key // h
    val = (proj.shape[1] - 2 * key) // 2
    dv = val // h
    has_s0 = s0 is not None
    tables = _gla_tables(min(seq, GLA_BLOCK), dk)
    fixed = lambda a: pl.BlockSpec(a.shape, lambda b, hp: (0,) * a.ndim)
    hps = h if seq <= GLA_BLOCK else GLA_HEADS_PER_STEP
    wk, wv = hps * dk, hps * dv
    in_specs = [
        pl.BlockSpec((seq, wk), lambda b, hp: (b, hp)),
        pl.BlockSpec((seq, wk), lambda b, hp: (b, key // wk + hp)),
        pl.BlockSpec((seq, wv), lambda b, hp: (b, (2 * key) // wv + hp)),
        pl.BlockSpec((seq, wv), lambda b, hp: (b, (2 * key + val) // wv + hp)),
        pl.BlockSpec((seq, 2 * GLA_RANK), lambda b, hp: (b, 0)),
        pl.BlockSpec((None, hps, 2 * GLA_RANK, 2 * dk), lambda b, hp: (j, hp, 0, 0)),
        pl.BlockSpec((None, hps, 1, 2 * dk), lambda b, hp: (j, hp, 0, 0)),
        pl.BlockSpec((None, 1, dv), lambda b, hp: (j, 0, 0)),
    ] + [fixed(a) for a in tables]
    n_gla = w_dec.shape[0]
    w5 = w_dec.reshape(n_gla, 2, GLA_RANK, h, dk)
    zero = jnp.zeros_like(w5[:, 0])
    w_blk = jnp.concatenate([jnp.concatenate([w5[:, 0], zero], axis=-1),
                             jnp.concatenate([zero, w5[:, 1]], axis=-1)], axis=1).transpose(0, 2, 1, 3)
    b_cat = b_dec.reshape(n_gla, 2, h, dk).transpose(0, 2, 1, 3).reshape(n_gla, h, 1, 2 * dk)
    args = [proj, proj, proj, proj, lr, w_blk, b_cat, onorm_g.reshape(n_gla, 1, dv), *tables]
    if has_s0:
        in_specs.append(pl.BlockSpec((None, None, 2, hps, dk, dv), lambda b, hp: (b, j, 0, hp, 0, 0)))
        args.append(s0)
    out_specs = [pl.BlockSpec((seq, wv), lambda b, hp: (b, hp))]
    out_shape = [jax.ShapeDtypeStruct((t, val), BF16)]
    aliases = {}
    if want_state:
        out_shape.append(jax.ShapeDtypeStruct((n_seq, n_gla, 2, h, dk, dv), F32))
        if states is None:
            out_specs.append(pl.BlockSpec((None, n_gla, 2, hps, dk, dv), lambda b, hp: (b, 0, 0, hp, 0, 0)))
        else:
            out_specs.append(pl.BlockSpec((None, None, 2, hps, dk, dv), lambda b, hp: (b, j, 0, hp, 0, 0)))
            in_specs.append(pl.BlockSpec(memory_space=pl.ANY))
            args.append(states)
            aliases = {len(args) - 1: 1}
    return pl.pallas_call(
        functools.partial(_gla_body, seq=seq, layer=j, has_s0=has_s0, want_state=want_state,
                          has_states_in=bool(aliases)),
        grid=(n_seq, h // hps),
        in_specs=in_specs,
        out_specs=out_specs,
        out_shape=out_shape,
        input_output_aliases=aliases,
        scratch_shapes=[pltpu.VMEM((seq, wv), F32), pltpu.VMEM((hps, 2, seq, dk), BF16),
                        pltpu.VMEM((hps, 2, seq, dk), F32), pltpu.VMEM((hps, 2, dk, dv), F32)],
        compiler_params=_params("parallel", "parallel"),
        name=name,
    )(*args)


def _dft_tables(n):
    f = np.arange(n, dtype=np.int64)
    ang = (np.outer(f, f) % n).astype(np.float64) * (2.0 * math.pi / n)
    return _trig_bf16(ang)


def _fnet_body(x_ref, z_ref, cc_ref, sc_ref, cl_ref, sl_ref, y_ref, *, seq, n_seq):
    xb = x_ref[...].astype(BF16)
    a = _dot(xb, cc_ref[...]).astype(BF16)
    bm = _dot(xb, sc_ref[...]).astype(BF16)
    inv = 1.0 / math.sqrt(seq * x_ref.shape[-1])
    for i in range(n_seq):
        rows = slice(i * seq, (i + 1) * seq)
        f = _dot(cl_ref[...], a[rows]) - _dot(sl_ref[...], bm[rows])
        y_ref[rows, :] = (f * inv * _silu(z_ref[rows, :])).astype(BF16)


def _fnet_mixer(proj, n_seq_total, seq, seqs_per_step, name):
    t = proj.shape[0]
    width = proj.shape[1] // 2
    gc = width // FN_GROUPS
    cc, sc = _dft_tables(gc)
    cl, sl = _dft_tables(seq)
    rows = seq * seqs_per_step
    const = pl.Buffered(1)
    return pl.pallas_call(
        functools.partial(_fnet_body, seq=seq, n_seq=seqs_per_step),
        grid=(n_seq_total // seqs_per_step, FN_GROUPS),
        in_specs=[
            pl.BlockSpec((rows, gc), lambda i, g: (i, g)),
            pl.BlockSpec((rows, gc), lambda i, g: (i, FN_GROUPS + g)),
            pl.BlockSpec((gc, gc), lambda i, g: (0, 0), pipeline_mode=const),
            pl.BlockSpec((gc, gc), lambda i, g: (0, 0), pipeline_mode=const),
            pl.BlockSpec((seq, seq), lambda i, g: (0, 0), pipeline_mode=const),
            pl.BlockSpec((seq, seq), lambda i, g: (0, 0), pipeline_mode=const),
        ],
        out_specs=pl.BlockSpec((rows, gc), lambda i, g: (i, g)),
        out_shape=jax.ShapeDtypeStruct((t, width), BF16),
        compiler_params=_params("parallel", "parallel"),
        name=name,
    )(proj, proj, cc, sc, cl, sl)


def _signal_tables(n):
    o = 2 * np.arange(n, dtype=np.int64) + 1
    ang = (np.outer(o, o) % (8 * n)).astype(np.float64) * (math.pi / (4 * n))
    cos, sin = _trig_bf16(ang)
    return cos, -sin


def _filter_tables(n, split):
    f = np.arange(n, dtype=np.int64)
    if split:
        f = np.concatenate([f[:n // 2], f[:n // 2 - 1:-1]])
    f = 2 * f + 1
    o = np.arange(n, dtype=np.int64)
    ang = (np.outer(f, o) % (4 * n)).astype(np.float64) * (math.pi / (2 * n))
    return _trig_bf16(ang)


def _hyena_positions(seq, bands, pad_to):
    t = np.linspace(0.0, 1.0, seq)[:, None]
    w = 2.0 * math.pi * np.arange(seq)[:, None] / seq
    f = np.linspace(1e-4, bands - 1, bands)[None, :]
    zpos = np.concatenate([t, np.cos(f * w), -np.sin(f * w)], axis=-1)
    zpos = np.pad(zpos, ((0, 0), (0, pad_to - zpos.shape[1])))
    return jnp.asarray(zpos, F32), jnp.asarray(t, F32)


def _hyfilt_body(zpos_ref, w1_ref, b1_ref, w2_ref, b2_ref, w3_ref, b3_ref, fr_ref, w4f_ref,
                 w4b_ref, t_ref, del_ref, ch_ref, sh_ref, *rest, split):
    if split:
        c2_ref, s2_ref = rest[:2]
        out_refs, a3_ref = rest[2:-1], rest[-1]
    else:
        out_refs, a3_ref = rest[:-1], rest[-1]

    @pl.when(pl.program_id(0) == 0)
    def _():
        fr = fr_ref[...]
        a = jnp.sin(fr * (_dot3(zpos_ref[...], w1_ref[...]) + b1_ref[...]))
        a = jnp.sin(fr * (_dot3(a, w2_ref[...]) + b2_ref[...]))
        a3_ref[...] = jnp.sin(fr * (_dot3(a, w3_ref[...]) + b3_ref[...]))

    a3 = a3_ref[...]
    dec = jnp.exp(-t_ref[...] * del_ref[...]) * (1.0 / t_ref.shape[0])
    fwd = _dot3(a3, w4f_ref[...]) * dec
    bwd = _dot3(a3, w4b_ref[...]) * dec
    first = lax.broadcasted_iota(jnp.int32, bwd.shape, 0) == 0
    bwd = jnp.where(first, 0.0, bwd)
    hre = _dot(ch_ref[...], (fwd + bwd).astype(BF16))
    him = _dot(sh_ref[...], (bwd - fwd).astype(BF16))
    if not split:
        out_refs[0][...] = hre
        out_refs[1][...] = him
        return
    half = hre.shape[0] // 2
    plus_r, plus_i = hre[:half] + hre[half:], him[:half] - him[half:]
    minus_r, minus_i = hre[:half] - hre[half:], him[:half] + him[half:]
    c2, s2 = c2_ref[...], s2_ref[...]
    for ref, val in zip(out_refs, (plus_r, plus_i,
                                   c2 * minus_r + s2 * minus_i, c2 * minus_i - s2 * minus_r,
                                   c2 * minus_r - s2 * minus_i, c2 * minus_i + s2 * minus_r)):
        ref[...] = val


def _hyena_filters(seq, j, emb, w1p, b1, w2, b2, w3, b3, w4, freq, ce, split, name):
    ffn = w2.shape[-1]
    width = w4.shape[-1] // (2 * HY_ORDER)
    emb_pad = w1p.shape[1]
    zpos, tcol = _hyena_positions(seq, (emb - 1) // 2, emb_pad)
    deltas = jnp.asarray(np.abs(np.linspace(HY_MIN_DECAY, HY_MAX_DECAY, width, dtype=np.float32))[None, :])
    ch, sh = _filter_tables(seq, split)
    nb = width // ce
    const = pl.Buffered(1)
    vec = lambda: pl.BlockSpec((None, 1, ffn), lambda c: (j, 0, 0))
    out_rows, n_out = (seq // 2, 6) if split else (seq, 2)
    twiddles, twiddle_specs = [], []
    if split:
        ang = np.broadcast_to((math.pi * (2 * np.arange(out_rows) + 1) / (2 * seq))[:, None], (out_rows, ce))
        twiddles = [jnp.asarray(np.cos(ang), F32), jnp.asarray(np.sin(ang), F32)]
        twiddle_specs = [pl.BlockSpec((out_rows, ce), lambda c: (0, 0), pipeline_mode=const)] * 2
    return pl.pallas_call(
        functools.partial(_hyfilt_body, split=split),
        grid=(HY_ORDER * nb,),
        in_specs=[
            pl.BlockSpec((seq, emb_pad), lambda c: (0, 0)),
            pl.BlockSpec((None, emb_pad, ffn), lambda c: (j, 0, 0)),
            vec(),
            pl.BlockSpec((None, ffn, ffn), lambda c: (j, 0, 0)),
            vec(),
            pl.BlockSpec((None, ffn, ffn), lambda c: (j, 0, 0)),
            vec(),
            vec(),
            pl.BlockSpec((None, ffn, ce), lambda c: (j, 0, (c // nb) * 2 * nb + c % nb)),
            pl.BlockSpec((None, ffn, ce), lambda c: (j, 0, (c // nb) * 2 * nb + nb + c % nb)),
            pl.BlockSpec((seq, 1), lambda c: (0, 0)),
            pl.BlockSpec((1, ce), lambda c: (0, c % nb)),
            pl.BlockSpec((seq, seq), lambda c: (0, 0), pipeline_mode=const),
            pl.BlockSpec((seq, seq), lambda c: (0, 0), pipeline_mode=const),
        ] + twiddle_specs,
        out_specs=[pl.BlockSpec((out_rows, ce), lambda c: (0, c))] * n_out,
        out_shape=[jax.ShapeDtypeStruct((out_rows, HY_ORDER * width), F32)] * n_out,
        scratch_shapes=[pltpu.VMEM((seq, ffn), F32)],
        compiler_params=_params("arbitrary"),
        name=name,
    )(zpos, w1p, b1, w2, b2, w3, b3, freq, w4, w4, tcol, deltas, ch, sh, *twiddles)


SUBLANES = 8


def _shift_down(u, first):
    rolled = pltpu.roll(u, 1, 0)
    row = lax.broadcasted_iota(jnp.int32, (SUBLANES, 1), 0)
    return jnp.concatenate([jnp.where(row == 0, first, rolled[:SUBLANES]), rolled[SUBLANES:]], axis=0)


def _shift_up(u, last):
    n = u.shape[0]
    rolled = pltpu.roll(u, n - 1, 0)
    row = lax.broadcasted_iota(jnp.int32, (SUBLANES, 1), 0)
    tail = jnp.where(row == SUBLANES - 1, last, rolled[n - SUBLANES:])
    return jnp.concatenate([rolled[:n - SUBLANES], tail], axis=0)


def _hyena_body(x1_ref, x2_ref, v_ref, z_ref, cw1_ref, cw2_ref, cwv_ref, cb1_ref, cb2_ref, cbv_ref,
                hre0_ref, him0_ref, hre1_ref, him1_ref, d_ref, cq_ref, sq_ref, y_ref,
                cur_ref, cb_ref, yr_ref, yi_ref, *, seq, rows):
    n_rb = seq // rows

    def block(rb):
        return pl.ds(pl.multiple_of(rb * rows, rows), rows)

    def short_conv(u_ref, w_ref, b_ref, rb):
        u = u_ref[block(rb), :]
        before = u_ref[pl.ds(jnp.maximum(rb * rows - 1, 0), 1), :]
        after = u_ref[pl.ds(jnp.minimum((rb + 1) * rows, seq - 1), 1), :]
        before = jnp.where(rb == 0, 0.0, before)
        after = jnp.where(rb == n_rb - 1, 0.0, after)
        prev, nxt = _shift_down(u, before), _shift_up(u, after)
        return prev * w_ref[0:1] + u * w_ref[1:2] + nxt * w_ref[2:3] + b_ref[...]

    def long_conv_gate(x_ref, cw_ref, cbias_ref, hre_ref, him_ref, dn):
        def fwd(rb, carry):
            sl = block(rb)
            ur = _dot(cq_ref[sl, :], cb_ref[...])
            ui = _dot(sq_ref[sl, :], cb_ref[...])
            hr = hre_ref[sl, :]
            hi = him_ref[sl, :]
            yr_ref[sl, :] = (ur * hr - ui * hi).astype(BF16)
            yi_ref[sl, :] = (ur * hi + ui * hr).astype(BF16)
            return carry

        lax.fori_loop(0, n_rb, fwd, 0)

        def inv(rb, carry):
            sl = block(rb)
            c = _dot(cq_ref[sl, :], yr_ref[...]) + _dot(sq_ref[sl, :], yi_ref[...])
            new = short_conv(x_ref, cw_ref, cbias_ref, rb) * (c + cur_ref[sl, :] * dn)
            cur_ref[sl, :] = new
            cb_ref[sl, :] = new.astype(BF16)
            return carry

        lax.fori_loop(0, n_rb, inv, 0)

    def load_v(rb, carry):
        v = short_conv(v_ref, cwv_ref, cbv_ref, rb)
        cur_ref[block(rb), :] = v
        cb_ref[block(rb), :] = v.astype(BF16)
        return carry

    lax.fori_loop(0, n_rb, load_v, 0)
    long_conv_gate(x1_ref, cw1_ref, cb1_ref, hre0_ref, him0_ref, d_ref[0:1])
    long_conv_gate(x2_ref, cw2_ref, cb2_ref, hre1_ref, him1_ref, d_ref[1:2])

    def store(rb, carry):
        sl = block(rb)
        y_ref[sl, :] = (cur_ref[sl, :] * _silu(z_ref[sl, :])).astype(BF16)
        return carry

    lax.fori_loop(0, n_rb, store, 0)


def _hyena_split_body(x1_refs, x2_refs, v_refs, z_refs, cw1_ref, cw2_ref, cwv_ref, cb1_ref, cb2_ref, cbv_ref,
                      k_refs, d_ref, cq_ref, sq_ref, y_refs, cur_ref, sig_ref, z_buf, *, seq, rows):
    half = seq // 2
    n_rb = half // rows

    def block(rb, shift=0):
        return slice(rb * rows + shift, (rb + 1) * rows + shift)

    def short_conv(u_refs, w_ref, b_ref, rb):
        e_ref, o_ref = u_refs
        even, odd = e_ref[block(rb), :], o_ref[block(rb), :]
        odd_before = o_ref[block(rb, -1), :] if rb > 0 else _shift_down(odd, 0.0)
        even_after = e_ref[block(rb, 1), :] if rb < n_rb - 1 else _shift_up(even, 0.0)
        w0, w1, w2, b = w_ref[0:1], w_ref[1:2], w_ref[2:3], b_ref[...]
        return (odd_before * w0 + even * w1 + odd * w2 + b, even * w0 + odd * w1 + even_after * w2 + b)

    def long_conv_gate(x_ref, cw_ref, cbias_ref, k, dn):
        k1r, k1i, k2r, k2i, k3r, k3i = k
        for fb in range(n_rb):
            sl = block(fb)
            c, s = cq_ref[sl, :], sq_ref[sl, :]
            er, ei = _dot(c, sig_ref[0]), _dot(s, sig_ref[0])
            orr, oi = _dot(c, sig_ref[1]), _dot(s, sig_ref[1])
            a1r, a1i, a2r, a2i, a3r, a3i = (r[sl, :] for r in (k1r, k1i, k2r, k2i, k3r, k3i))
            z_buf[0, sl, :] = (er * a1r - ei * a1i + orr * a2r - oi * a2i).astype(BF16)
            z_buf[1, sl, :] = (er * a1i + ei * a1r + orr * a2i + oi * a2r).astype(BF16)
            z_buf[2, sl, :] = (er * a3r - ei * a3i + orr * a1r - oi * a1i).astype(BF16)
            z_buf[3, sl, :] = (er * a3i + ei * a3r + orr * a1i + oi * a1r).astype(BF16)
        for tb in range(n_rb):
            sl = block(tb)
            c, s = cq_ref[sl, :], sq_ref[sl, :]
            conv = (_dot(c, z_buf[0]) + _dot(s, z_buf[1]), _dot(c, z_buf[2]) + _dot(s, z_buf[3]))
            gate = short_conv(x_ref, cw_ref, cbias_ref, tb)
            for p in range(2):
                new = gate[p] * (conv[p] + cur_ref[p, sl, :] * dn)
                cur_ref[p, sl, :] = new
                sig_ref[p, sl, :] = new.astype(BF16)

    for rb in range(n_rb):
        for p, val in enumerate(short_conv(v_refs, cwv_ref, cbv_ref, rb)):
            cur_ref[p, block(rb), :] = val
            sig_ref[p, block(rb), :] = val.astype(BF16)
    long_conv_gate(x1_refs, cw1_ref, cb1_ref, k_refs[0], d_ref[0:1])
    long_conv_gate(x2_refs, cw2_ref, cb2_ref, k_refs[1], d_ref[1:2])
    for rb in range(n_rb):
        for p in range(2):
            sl = block(rb)
            y_refs[p][sl, :] = (cur_ref[p, sl, :] * _silu(z_refs[p][sl, :])).astype(BF16)


def _hyena_split_entry(*refs, seq, rows):
    it = iter(refs)
    take = lambda n: tuple(next(it) for _ in range(n))
    x1, x2, v, z = take(2), take(2), take(2), take(2)
    conv_refs = take(6)
    k = tuple(take(6) for _ in range(HY_ORDER))
    d, cq, sq = take(3)
    y = take(2)
    _hyena_split_body(x1, x2, v, z, *conv_refs, k, d, cq, sq, y, *it, seq=seq, rows=rows)


def _hyena_mixer(proj, conv_w, conv_b, filt, d_skip, j, n_seq, seq, ce, split, name):
    phases = 2 if split else 1
    t = proj.shape[0] * phases
    width = proj.shape[1] // (4 * phases)
    nb = width // ce
    n_dft = seq // phases
    cq, sq = _signal_tables(n_dft)
    rows = min(n_dft, HY_ROW_BLOCK)
    const = pl.Buffered(1)
    seg = lambda s: pl.BlockSpec((seq, ce), lambda e, b: (b, s * nb + e))
    cw = lambda s: pl.BlockSpec((None, conv_w.shape[1], ce), lambda e, b: (j, 0, s * nb + e))
    cb = lambda s: pl.BlockSpec((None, 1, ce), lambda e, b: (j, 0, s * nb + e))
    hs = lambda n: pl.BlockSpec((n_dft, ce), lambda e, b: (0, n * nb + e),
                                pipeline_mode=const if nb == 1 else pl.Buffered(2))
    conv_b3 = conv_b.reshape(conv_b.shape[0], 1, -1)
    if split:
        body = functools.partial(_hyena_split_entry, seq=seq, rows=rows)
        phase = lambda s, p: pl.BlockSpec((n_dft, ce), lambda e, b: (b, p * 4 * nb + s * nb + e))
        x_specs = [phase(s, p) for s in range(4) for p in range(2)]
        out_specs = [pl.BlockSpec((n_dft, ce), lambda e, b: (b, e))] * 2
        out_shape = [jax.ShapeDtypeStruct((t // 2, width), BF16)] * 2
        scratch = [pltpu.VMEM((2, n_dft, ce), F32), pltpu.VMEM((2, n_dft, ce), BF16),
                   pltpu.VMEM((4, n_dft, ce), BF16)]
    else:
        body = functools.partial(_hyena_body, seq=seq, rows=rows)
        x_specs = [seg(0), seg(1), seg(2), seg(3)]
        out_specs = pl.BlockSpec((seq, ce), lambda e, b: (b, e))
        out_shape = jax.ShapeDtypeStruct((t, width), BF16)
        scratch = [pltpu.VMEM((seq, ce), F32), pltpu.VMEM((seq, ce), BF16),
                   pltpu.VMEM((seq, ce), BF16), pltpu.VMEM((seq, ce), BF16)]
    return pl.pallas_call(
        body,
        grid=(nb, n_seq),
        in_specs=x_specs + [cw(0), cw(1), cw(2), cb(0), cb(1), cb(2)]
        + [hs(n) for n in range(HY_ORDER) for _ in filt]
        + [pl.BlockSpec((None, HY_ORDER, ce), lambda e, b: (j, 0, e)),
           pl.BlockSpec((n_dft, n_dft), lambda e, b: (0, 0), pipeline_mode=const),
           pl.BlockSpec((n_dft, n_dft), lambda e, b: (0, 0), pipeline_mode=const)],
        out_specs=out_specs,
        out_shape=out_shape,
        scratch_shapes=scratch,
        compiler_params=_params("parallel", "parallel"),
        name=name,
    )(*([proj] * len(x_specs)), conv_w, conv_w, conv_w, conv_b3, conv_b3, conv_b3,
      *(list(filt) * HY_ORDER), d_skip, cq, sq)


def kernel(x_prompt, x_sample, state_gla, c, c_ctx, mod_w, mod_b, norm_g, final_norm_g, gla_w_in, gla_w_dec, gla_b_dec, gla_onorm_g, gla_w_out, fn_w_in, fn_w_out, hy_w_in, hy_conv_w, hy_conv_b, hy_ffn_w1, hy_ffn_b1, hy_ffn_w2, hy_ffn_b2, hy_ffn_w3, hy_ffn_b3, hy_ffn_w4, hy_freq, hy_d, hy_w_out):
    n_p, l_p, d = x_prompt.shape
    n_s, l_s, _ = x_sample.shape
    depth = mod_w.shape[0]
    key = gla_w_dec.shape[-1]
    n_main = gla_w_in.shape[-1] - 2 * GLA_RANK

    cvec = jnp.concatenate([c_ctx[None], c, jnp.zeros((MOD_ROWS - 1 - n_s, d), F32)], axis=0)
    mod4 = _modulation(cvec, mod_w, mod_b).reshape(depth, MOD_ROWS, 1, 3 * d)
    norm_g3 = norm_g.reshape(depth, 1, d)

    gla_w_main = gla_w_in.astype(BF16)
    gla_w_lr = gla_w_in[:, :, n_main:].astype(BF16)
    gla_w_out_b = gla_w_out.astype(BF16)
    fn_w_in_b = fn_w_in.astype(BF16)
    fn_w_out_b = fn_w_out.astype(BF16)
    hy_w_in_b = hy_w_in.astype(BF16)
    hy_w_out_b = hy_w_out.astype(BF16)
    emb = hy_ffn_w1.shape[1]
    emb_pad = -(-emb // 128) * 128
    hy_w1p = jnp.pad(hy_ffn_w1, ((0, 0), (0, emb_pad - emb), (0, 0)))
    ffn = hy_ffn_w2.shape[-1]
    hy_vecs = [a.reshape(a.shape[0], 1, ffn) for a in (hy_ffn_b1, hy_ffn_b2, hy_ffn_b3, hy_freq)]

    tm = 512
    tiles_per_sample = l_s // tm
    streams = {
        "p": dict(x=x_prompt.reshape(n_p * l_p, d), n=n_p, l=l_p, row=lambda i: 0),
        "s": dict(x=x_sample.reshape(n_s * l_s, d), n=n_s, l=l_s, row=lambda i: 1 + i // tiles_per_sample),
    }
    def in_weights(i, seq):
        kind, j = i % N_MIXERS, i // N_MIXERS
        if kind == 0:
            return [(gla_w_main, (None, d, n_main), (j, 0, 0)), (gla_w_lr, (None, d, 2 * GLA_RANK), (j, 0, 0))], False
        if kind == 1:
            return [(fn_w_in_b, (None,) + fn_w_in.shape[1:], (j, 0, 0))], False
        return [(hy_w_in_b, (None,) + hy_w_in.shape[1:], (j, 0, 0))], seq >= HY_SPLIT_MIN_SEQ

    for tag, st in streams.items():
        weights, split = in_weights(0, st["l"])
        st["proj"] = _proj(st["x"], mod4, st["row"], tm, f"proj_{tag}0", nxt=(norm_g3, 0, weights, split))
    new_states = None
    for i in range(depth):
        kind, j = i % N_MIXERS, i // N_MIXERS
        if kind == 2:
            filt = {}
            for tag, st in streams.items():
                ce = min(d, 256 if st["l"] > 256 else d)
                filt[tag] = _hyena_filters(st["l"], j, emb, hy_w1p, hy_vecs[0], hy_ffn_w2, hy_vecs[1], hy_ffn_w3,
                                           hy_vecs[2], hy_ffn_w4, hy_vecs[3], ce, st["l"] >= HY_SPLIT_MIN_SEQ,
                                           f"hyfilt_{tag}{i}")
        for tag, st in streams.items():
            n_seq, seq = st["n"], st["l"]
            if kind == 0:
                proj, lr = st["proj"]
                is_ctx = tag == "p"
                res = _gla_mixer(proj, lr, gla_w_dec, gla_b_dec, gla_onorm_g,
                                 None if is_ctx else state_gla, j, n_seq, seq, is_ctx, new_states,
                                 f"gla_{tag}{i}")
                ys = [res[0]]
                if is_ctx:
                    new_states = res[1]
                w_out = gla_w_out_b
            elif kind == 1:
                ys = [_fnet_mixer(st["proj"][0], n_seq, seq, 4 if seq <= 256 else 1, f"fnet_{tag}{i}")]
                w_out = fn_w_out_b
            else:
                split = seq >= HY_SPLIT_MIN_SEQ
                ce = min(d, 256 if seq > 256 else d)
                y = _hyena_mixer(st["proj"][0], hy_conv_w, hy_conv_b, filt[tag], hy_d, j, n_seq, seq, ce, split,
                                 f"hyena_{tag}{i}")
                ys = list(y) if split else [y]
                w_out = hy_w_out_b
            if i + 1 < depth:
                weights, split = in_weights(i + 1, seq)
                st["x"], *st["proj"] = _proj(st["x"], mod4, st["row"], tm, f"proj_{tag}{i + 1}",
                                             res=(ys, w_out, j, i), nxt=(norm_g3, i + 1, weights, split))
            else:
                (st["x"],) = _proj(st["x"], mod4, st["row"], tm, f"proj_{tag}{i + 1}",
                                   res=(ys, w_out, j, i), final_g=final_norm_g)
    y_prompt = streams["p"]["x"].reshape(n_p, l_p, d)
    y_sample = streams["s"]["x"].reshape(n_s, l_s, d)
    new_state_gla = new_states.astype(x_prompt.dtype)
    return (y_prompt, y_sample, new_state_gla)
```

```python
import functools
import math

import jax
import jax.numpy as jnp
import numpy as np
from jax import lax
from jax.experimental import pallas as pl
from jax.experimental.pallas import tpu as pltpu

F32 = jnp.float32
BF16 = jnp.bfloat16

EPS = 1e-6
N_MIXERS = 3
GLA_HEADS = 4
GLA_RANK = 16
GLA_GATE_NORM = 16.0
FN_GROUPS = 4
HY_ORDER = 2
HY_TARGET = 1e-2
HY_MIN_DECAY = math.log(HY_TARGET) / 1.5
HY_MAX_DECAY = math.log(HY_TARGET) / 0.3
HY_ROW_BLOCK = 512
HY_SPLIT_MIN_SEQ = 512

V7X_VMEM_BYTES = 64 * 1024 * 1024
VMEM_LIMIT_BYTES = V7X_VMEM_BYTES - 8 * 1024 * 1024
MOD_ROWS = 8

_NT = (((1,), (1,)), ((), ()))
_TN = (((0,), (0,)), ((), ()))


def _params(*sem):
    return pltpu.CompilerParams(dimension_semantics=sem, vmem_limit_bytes=VMEM_LIMIT_BYTES)


def _dot(a, b):
    return jnp.dot(a, b, preferred_element_type=F32)


def _split3(x):
    hi = x.astype(BF16)
    r1 = x - hi.astype(F32)
    mid = r1.astype(BF16)
    lo = (r1 - mid.astype(F32)).astype(BF16)
    return hi, mid, lo


def _dot3(a, b):
    return _dot(_dot3_lhs(a), _dot3_rhs(b))


def _dot3_lhs(a):
    ah = a.astype(BF16)
    al = (a - ah.astype(F32)).astype(BF16)
    return jnp.concatenate([ah, ah, al], axis=1)


def _dot3_rhs(b):
    bh = b.astype(BF16)
    bl = (b - bh.astype(F32)).astype(BF16)
    return jnp.concatenate([bh, bl, bh], axis=0)


def _dot_exact_lhs(a_bf16, b):
    return _dot(jnp.concatenate([a_bf16] * 3, axis=1), jnp.concatenate(_split3(b), axis=0))


def _trig_bf16(ang):
    return (jnp.asarray(np.cos(ang), F32).astype(BF16), jnp.asarray(np.sin(ang), F32).astype(BF16))


def _silu(x):
    return x / (1.0 + jnp.exp(-x))


def _log_sigmoid(x):
    return jnp.minimum(x, 0.0) - jnp.log(1.0 + jnp.exp(-jnp.abs(x)))


def _mod_body(c_ref, w_ref, b_ref, o_ref):
    o_ref[...] = _dot3(_silu(c_ref[...]), w_ref[...]) + b_ref[...]


def _modulation(cvec, mod_w, mod_b):
    depth, d, e = mod_w.shape
    tn = d
    return pl.pallas_call(
        _mod_body,
        grid=(depth, e // tn),
        in_specs=[
            pl.BlockSpec((MOD_ROWS, d), lambda l, j: (0, 0)),
            pl.BlockSpec((None, d, tn), lambda l, j: (l, 0, j)),
            pl.BlockSpec((None, 1, tn), lambda l, j: (l, 0, j)),
        ],
        out_specs=pl.BlockSpec((None, MOD_ROWS, tn), lambda l, j: (l, 0, j)),
        out_shape=jax.ShapeDtypeStruct((depth, MOD_ROWS, e), F32),
        compiler_params=_params("parallel", "parallel"),
        name="modulation",
    )(cvec, mod_w, mod_b.reshape(depth, 1, e))


def _phase_permutation(rows):
    order = np.concatenate([np.arange(0, rows, 2), np.arange(1, rows, 2)])
    return jnp.asarray(np.eye(rows, dtype=np.float32)[order], BF16)


def _proj_body(*refs, n_y, has_res, final, has_next, split, n_w):
    it = iter(refs)
    take = lambda n: [next(it) for _ in range(n)]
    if has_res:
        y_refs = take(n_y)
        unperm_ref = next(it) if n_y == 2 else None
        w_out_ref, gate_ref = take(2)
        fg_ref = next(it) if final else None
    x_ref = next(it)
    if has_next:
        g_ref, mod_ref = take(2)
        perm_ref = next(it) if split else None
        w_refs = take(n_w)
    x_out_ref = next(it) if has_res else None
    o_refs = take(n_w) if has_next else []

    x = x_ref[...]
    d = x.shape[-1]
    if has_res:
        if n_y == 2:
            y = _dot(unperm_ref[...], jnp.concatenate([r[...] for r in y_refs], axis=0)).astype(BF16)
        else:
            y = y_refs[0][...]
        x = x + gate_ref[...][:, 2 * d:] * _dot(y, w_out_ref[...])
        if final:
            x_out_ref[...] = x * lax.rsqrt(jnp.mean(x * x, axis=-1, keepdims=True) + EPS) * fg_ref[...]
        else:
            x_out_ref[...] = x
    if has_next:
        h = x * lax.rsqrt(jnp.mean(x * x, axis=-1, keepdims=True) + EPS) * g_ref[...]
        mod = mod_ref[...]
        hb = (h * (1.0 + mod[:, d:2 * d]) + mod[:, :d]).astype(BF16)
        if split:
            hb = _dot(perm_ref[...], hb).astype(BF16)
        for w_ref, o_ref in zip(w_refs, o_refs):
            res = _dot(hb, w_ref[...])
            if split:
                half, n = res.shape[0] // 2, res.shape[1]
                o_ref[:, :n] = res[:half]
                o_ref[:, n:] = res[half:]
            else:
                o_ref[...] = res


def _proj(x, mod4, mod_row, tm, name, res=None, nxt=None, final_g=None):
    t, d = x.shape
    has_res, has_next, final = res is not None, nxt is not None, final_g is not None
    mod_spec = lambda layer: pl.BlockSpec((None, None, 1, mod4.shape[-1]), lambda i: (layer, mod_row(i), 0, 0))
    in_specs, args, out_specs, out_shape = [], [], [], []
    n_y, split, n_w = 0, False, 0
    if has_res:
        ys, w_out, j, layer = res
        n_y = len(ys)
        in_specs += [pl.BlockSpec((tm // n_y, y.shape[-1]), lambda i: (i, 0)) for y in ys]
        args += list(ys)
        if n_y == 2:
            in_specs.append(pl.BlockSpec((tm, tm), lambda i: (0, 0)))
            args.append(_phase_permutation(tm).T)
        in_specs += [pl.BlockSpec((None,) + w_out.shape[1:], lambda i: (j, 0, 0), pipeline_mode=pl.Buffered(1)),
                     mod_spec(layer)]
        args += [w_out, mod4]
        if final:
            in_specs.append(pl.BlockSpec((1, d), lambda i: (0, 0)))
            args.append(final_g.reshape(1, d))
        out_specs.append(pl.BlockSpec((tm, d), lambda i: (i, 0)))
        out_shape.append(jax.ShapeDtypeStruct((t, d), F32))
    in_specs.append(pl.BlockSpec((tm, d), lambda i: (i, 0)))
    args.append(x)
    if has_next:
        norm_g3, next_layer, weights, split = nxt
        n_w = len(weights)
        in_specs += [pl.BlockSpec((None, 1, d), lambda i: (next_layer, 0, 0)), mod_spec(next_layer)]
        args += [norm_g3, mod4]
        if split:
            in_specs.append(pl.BlockSpec((tm, tm), lambda i: (0, 0)))
            args.append(_phase_permutation(tm))
        in_specs += [pl.BlockSpec(bs, functools.partial(lambda idx, i: idx, idx), pipeline_mode=pl.Buffered(1))
                     for _, bs, idx in weights]
        args += [w for w, _, _ in weights]
        p = 2 if split else 1
        widths = [bs[-1] for _, bs, _ in weights]
        out_specs += [pl.BlockSpec((tm // p, p * n), lambda i: (i, 0)) for n in widths]
        out_shape += [jax.ShapeDtypeStruct((t // p, p * n), F32) for n in widths]
    return pl.pallas_call(
        functools.partial(_proj_body, n_y=n_y, has_res=has_res, final=final, has_next=has_next,
                          split=split, n_w=n_w),
        grid=(t // tm,),
        in_specs=in_specs,
        out_specs=out_specs,
        out_shape=out_shape,
        compiler_params=_params("parallel"),
        name=name,
    )(*args)


GLA_CHUNK = 64
GLA_BLOCK = 256
GLA_HEADS_PER_STEP = 2
LOG2_E = math.log2(math.e)
GLA_CLAMP_SPLIT = 8.0 * LOG2_E
GLA_CLAMP_HALF = 80.0 * LOG2_E


def _gla_tables(rows, dk):
    c, hc = GLA_CHUNK, GLA_CHUNK // 2
    r = np.arange(rows)[:, None]
    s = np.arange(rows)[None, :]
    same_chunk = (r // c) == (s // c)
    same_half = (r // hc) == (s // hc)
    before = [s <= r, s >= r]
    tri = np.stack([same_chunk & b for b in before])
    m_half = np.stack([same_half & b for b in before])
    sel = np.concatenate([s == (r // c) * c + hc, s == (r // hc) * hc + hc // 2], axis=0)
    second = np.broadcast_to((r // hc) % 2 == 1, (rows, dk))
    as_const = lambda m, dt: jnp.asarray(m.astype(np.float32), dt)
    return (as_const(tri, BF16), as_const(sel, BF16), as_const(same_chunk, F32), as_const(m_half, F32),
            as_const(second, F32))


def _gla_body(q_ref, k_ref, v_ref, r_ref, lr_ref, wdec_ref, bdec_ref, og_ref, tri_ref, sel_ref,
              mchunk_ref, mhalf_ref, second_ref, *rest, seq, layer, has_s0, want_state, has_states_in):
    rest = list(rest)
    s0_ref = rest.pop(0) if has_s0 else None
    if has_states_in:
        rest.pop(0)
    y_ref = rest.pop(0)
    sfin_ref = rest.pop(0) if want_state else None
    o_ref, qi_ref, cum_ref, st_ref = rest
    chunk = GLA_CHUNK
    n_chunks = seq // chunk
    blk = tri_ref.shape[-1]
    n_heads = wdec_ref.shape[0]
    dk = q_ref.shape[-1] // n_heads
    dv = v_ref.shape[-1] // n_heads
    scale = dk ** -0.5
    heads = range(n_heads)
    dirs = range(2)
    chains = [(hh, d) for hh in heads for d in dirs]

    def nt(a, b):
        return lax.dot_general(a, b, _NT, preferred_element_type=F32)

    def cols(hh, width):
        return slice(hh * width, (hh + 1) * width)

    def block_pass(bi, carry):
        rows = pl.ds(pl.multiple_of(bi * blk, blk), blk)
        q = [q_ref[rows, cols(hh, dk)] * scale for hh in heads]
        k = [k_ref[rows, cols(hh, dk)] for hh in heads]
        lr = lr_ref[rows, :]
        g = [_log_sigmoid(_dot3(lr, wdec_ref[hh]) + bdec_ref[hh]) * (LOG2_E / GLA_GATE_NORM) for hh in heads]
        cum = {(hh, d): _dot_exact_lhs(tri_ref[d], g[hh][:, cols(d, dk)]) for hh, d in chains}
        refs = [_dot(sel_ref[...], jnp.concatenate([cum[hh, 0], cum[hh, 1]], axis=1).astype(BF16))
                for hh in heads]
        second = second_ref[...]
        first = 1.0 - second
        p_split, p_half = [], {}
        for hh in heads:
            q_in = (q[hh] * second, q[hh] * first)
            k_in = (k[hh] * first, k[hh] * second)
            q1, k1 = [], []
            for d in dirs:
                c = (hh, d)
                d_split = cum[c] - refs[hh][:blk, cols(d, dk)]
                d_half = cum[c] - refs[hh][blk:, cols(d, dk)]
                q1.append((q_in[d] * jnp.exp2(jnp.minimum(d_split, GLA_CLAMP_SPLIT))).astype(BF16))
                k1.append((k_in[d] * jnp.exp2(jnp.minimum(-d_split, GLA_CLAMP_SPLIT))).astype(BF16))
                q0 = (q[hh] * jnp.exp2(jnp.minimum(d_half, GLA_CLAMP_HALF))).astype(BF16)
                k0 = (k[hh] * jnp.exp2(jnp.minimum(-d_half, GLA_CLAMP_HALF))).astype(BF16)
                p_half[c] = nt(q0, k0)
                qi_ref[hh, d, rows, :] = (q[hh] * jnp.exp2(cum[c])).astype(BF16)
                cum_ref[hh, d, rows, :] = cum[c]
            p_split.append(nt(jnp.concatenate(q1, axis=1), jnp.concatenate(k1, axis=1)))
        for hh in heads:
            att = p_split[hh] * mchunk_ref[...]
            for d in dirs:
                att = att + jnp.where(mhalf_ref[d] > 0.5, p_half[hh, d], 0.0)
            o_ref[rows, cols(hh, dv)] = _dot(att.astype(BF16), v_ref[rows, cols(hh, dv)].astype(BF16))
        return carry

    lax.fori_loop(0, seq // blk, block_pass, 0)

    for hh, d in chains:
        st_ref[hh, d] = s0_ref[d, hh] if has_s0 else jnp.zeros((dk, dv), F32)

    def chunk_step(i, carry):
        for hh, d in chains:
            c = i if d == 0 else n_chunks - 1 - i
            sl = pl.ds(pl.multiple_of(c * chunk, chunk), chunk)
            cum = cum_ref[hh, d, sl, :]
            blast = cum[chunk - 1:chunk] if d == 0 else cum[0:1]
            kh = (k_ref[sl, cols(hh, dk)] * jnp.exp2(blast - cum)).astype(BF16)
            vb = v_ref[sl, cols(hh, dv)].astype(BF16)
            st = st_ref[hh, d]
            o_ref[sl, cols(hh, dv)] += _dot(qi_ref[hh, d, sl, :], st.astype(BF16))
            decay = jnp.broadcast_to(jnp.exp2(blast), (dk, dk)).T
            st_ref[hh, d] = (jnp.concatenate([decay] * (dv // dk), axis=1) * st
                             + lax.dot_general(kh, vb, _TN, preferred_element_type=F32))
        return carry

    lax.fori_loop(0, n_chunks, chunk_step, 0, unroll=min(n_chunks, 4))

    def finish(jb, carry):
        sl = pl.ds(pl.multiple_of(jb * blk, blk), blk)
        for hh in heads:
            o = o_ref[sl, cols(hh, dv)]
            o = o * lax.rsqrt(jnp.mean(o * o, axis=-1, keepdims=True) + EPS) * og_ref[...]
            y_ref[sl, cols(hh, dv)] = (o * _silu(r_ref[sl, cols(hh, dv)])).astype(BF16)
        return carry

    lax.fori_loop(0, seq // blk, finish, 0)
    if want_state:
        own = sfin_ref if has_states_in else sfin_ref.at[layer]
        for hh, d in chains:
            own[d, hh] = st_ref[hh, d]
        if not has_states_in:
            for other in range(sfin_ref.shape[0]):
                if other != layer:
                    sfin_ref[other] = jnp.zeros(sfin_ref.shape[1:], F32)


def _gla_mixer(proj, lr, w_dec, b_dec, onorm_g, s0, j, n_seq, seq, want_state, states, name):
    t = proj.shape[0]
    h = GLA_HEADS
    key = w_dec.shape[-1]
    dk = key // h
    val = (proj.shape[1] - 2 * key) // 2
    dv = val // h
    has_s0 = s0 is not None
    tables = _gla_tables(min(seq, GLA_BLOCK), dk)
    fixed = lambda a: pl.BlockSpec(a.shape, lambda b, hp: (0,) * a.ndim)
    hps = h if seq <= GLA_BLOCK else GLA_HEADS_PER_STEP
    wk, wv = hps * dk, hps * dv
    in_specs = [
        pl.BlockSpec((seq, wk), lambda b, hp: (b, hp)),
        pl.BlockSpec((seq, wk), lambda b, hp: (b, key // wk + hp)),
        pl.BlockSpec((seq, wv), lambda b, hp: (b, (2 * key) // wv + hp)),
        pl.BlockSpec((seq, wv), lambda b, hp: (b, (2 * key + val) // wv + hp)),
        pl.BlockSpec((seq, 2 * GLA_RANK), lambda b, hp: (b, 0)),
        pl.BlockSpec((None, hps, 2 * GLA_RANK, 2 * dk), lambda b, hp: (j, hp, 0, 0)),
        pl.BlockSpec((None, hps, 1, 2 * dk), lambda b, hp: (j, hp, 0, 0)),
        pl.BlockSpec((None, 1, dv), lambda b, hp: (j, 0, 0)),
    ] + [fixed(a) for a in tables]
    n_gla = w_dec.shape[0]
    w5 = w_dec.reshape(n_gla, 2, GLA_RANK, h, dk)
    zero = jnp.zeros_like(w5[:, 0])
    w_blk = jnp.concatenate([jnp.concatenate([w5[:, 0], zero], axis=-1),
                             jnp.concatenate([zero, w5[:, 1]], axis=-1)], axis=1).transpose(0, 2, 1, 3)
    b_cat = b_dec.reshape(n_gla, 2, h, dk).transpose(0, 2, 1, 3).reshape(n_gla, h, 1, 2 * dk)
    args = [proj, proj, proj, proj, lr, w_blk, b_cat, onorm_g.reshape(n_gla, 1, dv), *tables]
    if has_s0:
        in_specs.append(pl.BlockSpec((None, None, 2, hps, dk, dv), lambda b, hp: (b, j, 0, hp, 0, 0)))
        args.append(s0)
    out_specs = [pl.BlockSpec((seq, wv), lambda b, hp: (b, hp))]
    out_shape = [jax.ShapeDtypeStruct((t, val), BF16)]
    aliases = {}
    if want_state:
        out_shape.append(jax.ShapeDtypeStruct((n_seq, n_gla, 2, h, dk, dv), F32))
        if states is None:
            out_specs.append(pl.BlockSpec((None, n_gla, 2, hps, dk, dv), lambda b, hp: (b, 0, 0, hp, 0, 0)))
        else:
            out_specs.append(pl.BlockSpec((None, None, 2, hps, dk, dv), lambda b, hp: (b, j, 0, hp, 0, 0)))
            in_specs.append(pl.BlockSpec(memory_space=pl.ANY))
            args.append(states)
            aliases = {len(args) - 1: 1}
    return pl.pallas_call(
        functools.partial(_gla_body, seq=seq, layer=j, has_s0=has_s0, want_state=want_state,
                          has_states_in=bool(aliases)),
        grid=(n_seq, h // hps),
        in_specs=in_specs,
        out_specs=out_specs,
        out_shape=out_shape,
        input_output_aliases=aliases,
        scratch_shapes=[pltpu.VMEM((seq, wv), F32), pltpu.VMEM((hps, 2, seq, dk), BF16),
                        pltpu.VMEM((hps, 2, seq, dk), F32), pltpu.VMEM((hps, 2, dk, dv), F32)],
        compiler_params=_params("parallel", "parallel"),
        name=name,
    )(*args)


def _dft_tables(n):
    f = np.arange(n, dtype=np.int64)
    ang = (np.outer(f, f) % n).astype(np.float64) * (2.0 * math.pi / n)
    return _trig_bf16(ang)


def _fnet_body(x_ref, z_ref, cc_ref, sc_ref, cl_ref, sl_ref, y_ref, *, seq, n_seq, gc):
    inv = 1.0 / math.sqrt(seq * gc)
    for g in range(x_ref.shape[-1] // gc):
        cols = slice(g * gc, (g + 1) * gc)
        xb = x_ref[:, cols].astype(BF16)
        a = _dot(xb, cc_ref[...]).astype(BF16)
        bm = _dot(xb, sc_ref[...]).astype(BF16)
        for i in range(n_seq):
            rows = slice(i * seq, (i + 1) * seq)
            f = _dot(cl_ref[...], a[rows]) - _dot(sl_ref[...], bm[rows])
            y_ref[rows, cols] = (f * inv * _silu(z_ref[rows, cols])).astype(BF16)


def _fnet_mixer(proj, n_seq_total, seq, seqs_per_step, groups_per_step, name):
    t = proj.shape[0]
    width = proj.shape[1] // 2
    gc = width // FN_GROUPS
    cc, sc = _dft_tables(gc)
    cl, sl = _dft_tables(seq)
    rows = seq * seqs_per_step
    wide = gc * groups_per_step
    n_wide = width // wide
    const = pl.Buffered(1)
    return pl.pallas_call(
        functools.partial(_fnet_body, seq=seq, n_seq=seqs_per_step, gc=gc),
        grid=(n_seq_total // seqs_per_step, n_wide),
        in_specs=[
            pl.BlockSpec((rows, wide), lambda i, g: (i, g)),
            pl.BlockSpec((rows, wide), lambda i, g: (i, n_wide + g)),
            pl.BlockSpec((gc, gc), lambda i, g: (0, 0), pipeline_mode=const),
            pl.BlockSpec((gc, gc), lambda i, g: (0, 0), pipeline_mode=const),
            pl.BlockSpec((seq, seq), lambda i, g: (0, 0), pipeline_mode=const),
            pl.BlockSpec((seq, seq), lambda i, g: (0, 0), pipeline_mode=const),
        ],
        out_specs=pl.BlockSpec((rows, wide), lambda i, g: (i, g)),
        out_shape=jax.ShapeDtypeStruct((t, width), BF16),
        compiler_params=_params("parallel", "parallel"),
        name=name,
    )(proj, proj, cc, sc, cl, sl)


def _signal_tables(n):
    o = 2 * np.arange(n, dtype=np.int64) + 1
    ang = (np.outer(o, o) % (8 * n)).astype(np.float64) * (math.pi / (4 * n))
    cos, sin = _trig_bf16(ang)
    return cos, -sin


def _filter_tables(n, split):
    f = np.arange(n, dtype=np.int64)
    if split:
        f = np.concatenate([f[:n // 2], f[:n // 2 - 1:-1]])
    f = 2 * f + 1
    o = np.arange(n, dtype=np.int64)
    ang = (np.outer(f, o) % (4 * n)).astype(np.float64) * (math.pi / (2 * n))
    return _trig_bf16(ang)


def _hyena_positions(seq, bands, pad_to):
    t = np.linspace(0.0, 1.0, seq)[:, None]
    w = 2.0 * math.pi * np.arange(seq)[:, None] / seq
    f = np.linspace(1e-4, bands - 1, bands)[None, :]
    zpos = np.concatenate([t, np.cos(f * w), -np.sin(f * w)], axis=-1)
    zpos = np.pad(zpos, ((0, 0), (0, pad_to - zpos.shape[1])))
    return jnp.asarray(zpos, F32), jnp.asarray(t, F32)


def _hyfilt_body(zpos_ref, w1_ref, b1_ref, w2_ref, b2_ref, w3_ref, b3_ref, fr_ref, w4f_ref,
                 w4b_ref, t_ref, del_ref, ch_ref, sh_ref, *rest, split):
    if split:
        c2_ref, s2_ref = rest[:2]
        out_refs, a3_ref = rest[2:-1], rest[-1]
    else:
        out_refs, a3_ref = rest[:-1], rest[-1]

    @pl.when(pl.program_id(0) == 0)
    def _():
        fr = fr_ref[...]
        a = jnp.sin(fr * (_dot3(zpos_ref[...], w1_ref[...]) + b1_ref[...]))
        a = jnp.sin(fr * (_dot3(a, w2_ref[...]) + b2_ref[...]))
        a3_ref[...] = _dot3_lhs(jnp.sin(fr * (_dot3(a, w3_ref[...]) + b3_ref[...])))

    a3 = a3_ref[...]
    dec = jnp.exp(-t_ref[...] * del_ref[...]) * (1.0 / t_ref.shape[0])
    fwd = _dot(a3, _dot3_rhs(w4f_ref[...])) * dec
    bwd = _dot(a3, _dot3_rhs(w4b_ref[...])) * dec
    first = lax.broadcasted_iota(jnp.int32, bwd.shape, 0) == 0
    bwd = jnp.where(first, 0.0, bwd)
    hre = _dot(ch_ref[...], (fwd + bwd).astype(BF16))
    him = _dot(sh_ref[...], (bwd - fwd).astype(BF16))
    if not split:
        out_refs[0][...] = hre
        out_refs[1][...] = him
        return
    half = hre.shape[0] // 2
    plus_r, plus_i = hre[:half] + hre[half:], him[:half] - him[half:]
    minus_r, minus_i = hre[:half] - hre[half:], him[:half] + him[half:]
    c2, s2 = c2_ref[...], s2_ref[...]
    for ref, val in zip(out_refs, (plus_r, plus_i,
                                   c2 * minus_r + s2 * minus_i, c2 * minus_i - s2 * minus_r,
                                   c2 * minus_r - s2 * minus_i, c2 * minus_i + s2 * minus_r)):
        ref[...] = val


def _hyena_filters(seq, j, emb, w1p, b1, w2, b2, w3, b3, w4, freq, ce, split, name):
    ffn = w2.shape[-1]
    width = w4.shape[-1] // (2 * HY_ORDER)
    emb_pad = w1p.shape[1]
    zpos, tcol = _hyena_positions(seq, (emb - 1) // 2, emb_pad)
    deltas = jnp.asarray(np.abs(np.linspace(HY_MIN_DECAY, HY_MAX_DECAY, width, dtype=np.float32))[None, :])
    ch, sh = _filter_tables(seq, split)
    nb = width // ce
    const = pl.Buffered(1)
    vec = lambda: pl.BlockSpec((None, 1, ffn), lambda c: (j, 0, 0))
    out_rows, n_out = (seq // 2, 6) if split else (seq, 2)
    twiddles, twiddle_specs = [], []
    if split:
        ang = np.broadcast_to((math.pi * (2 * np.arange(out_rows) + 1) / (2 * seq))[:, None], (out_rows, ce))
        twiddles = [jnp.asarray(np.cos(ang), F32), jnp.asarray(np.sin(ang), F32)]
        twiddle_specs = [pl.BlockSpec((out_rows, ce), lambda c: (0, 0), pipeline_mode=const)] * 2
    return pl.pallas_call(
        functools.partial(_hyfilt_body, split=split),
        grid=(HY_ORDER * nb,),
        in_specs=[
            pl.BlockSpec((seq, emb_pad), lambda c: (0, 0)),
            pl.BlockSpec((None, emb_pad, ffn), lambda c: (j, 0, 0)),
            vec(),
            pl.BlockSpec((None, ffn, ffn), lambda c: (j, 0, 0)),
            vec(),
            pl.BlockSpec((None, ffn, ffn), lambda c: (j, 0, 0)),
            vec(),
            vec(),
            pl.BlockSpec((None, ffn, ce), lambda c: (j, 0, (c // nb) * 2 * nb + c % nb)),
            pl.BlockSpec((None, ffn, ce), lambda c: (j, 0, (c // nb) * 2 * nb + nb + c % nb)),
            pl.BlockSpec((seq, 1), lambda c: (0, 0)),
            pl.BlockSpec((1, ce), lambda c: (0, c % nb)),
            pl.BlockSpec((seq, seq), lambda c: (0, 0), pipeline_mode=const),
            pl.BlockSpec((seq, seq), lambda c: (0, 0), pipeline_mode=const),
        ] + twiddle_specs,
        out_specs=[pl.BlockSpec((out_rows, ce), lambda c: (0, c))] * n_out,
        out_shape=[jax.ShapeDtypeStruct((out_rows, HY_ORDER * width), F32)] * n_out,
        scratch_shapes=[pltpu.VMEM((seq, 3 * ffn), BF16)],
        compiler_params=_params("arbitrary"),
        name=name,
    )(zpos, w1p, b1, w2, b2, w3, b3, freq, w4, w4, tcol, deltas, ch, sh, *twiddles)


SUBLANES = 8


def _shift_down(u, first):
    rolled = pltpu.roll(u, 1, 0)
    row = lax.broadcasted_iota(jnp.int32, (SUBLANES, 1), 0)
    return jnp.concatenate([jnp.where(row == 0, first, rolled[:SUBLANES]), rolled[SUBLANES:]], axis=0)


def _shift_up(u, last):
    n = u.shape[0]
    rolled = pltpu.roll(u, n - 1, 0)
    row = lax.broadcasted_iota(jnp.int32, (SUBLANES, 1), 0)
    tail = jnp.where(row == SUBLANES - 1, last, rolled[n - SUBLANES:])
    return jnp.concatenate([rolled[:n - SUBLANES], tail], axis=0)


def _hyena_body(x1_ref, x2_ref, v_ref, z_ref, cw1_ref, cw2_ref, cwv_ref, cb1_ref, cb2_ref, cbv_ref,
                hre0_ref, him0_ref, hre1_ref, him1_ref, d_ref, cq_ref, sq_ref, y_ref,
                cur_ref, cb_ref, yr_ref, yi_ref, *, seq, rows):
    n_rb = seq // rows

    def block(rb):
        return pl.ds(pl.multiple_of(rb * rows, rows), rows)

    def short_conv(u_ref, w_ref, b_ref, rb):
        u = u_ref[block(rb), :]
        before = u_ref[pl.ds(jnp.maximum(rb * rows - 1, 0), 1), :]
        after = u_ref[pl.ds(jnp.minimum((rb + 1) * rows, seq - 1), 1), :]
        before = jnp.where(rb == 0, 0.0, before)
        after = jnp.where(rb == n_rb - 1, 0.0, after)
        prev, nxt = _shift_down(u, before), _shift_up(u, after)
        return prev * w_ref[0:1] + u * w_ref[1:2] + nxt * w_ref[2:3] + b_ref[...]

    def long_conv_gate(x_ref, cw_ref, cbias_ref, hre_ref, him_ref, dn):
        def fwd(rb, carry):
            sl = block(rb)
            ur = _dot(cq_ref[sl, :], cb_ref[...])
            ui = _dot(sq_ref[sl, :], cb_ref[...])
            hr = hre_ref[sl, :]
            hi = him_ref[sl, :]
            yr_ref[sl, :] = (ur * hr - ui * hi).astype(BF16)
            yi_ref[sl, :] = (ur * hi + ui * hr).astype(BF16)
            return carry

        lax.fori_loop(0, n_rb, fwd, 0)

        def inv(rb, carry):
            sl = block(rb)
            c = _dot(cq_ref[sl, :], yr_ref[...]) + _dot(sq_ref[sl, :], yi_ref[...])
            new = short_conv(x_ref, cw_ref, cbias_ref, rb) * (c + cur_ref[sl, :] * dn)
            cur_ref[sl, :] = new
            cb_ref[sl, :] = new.astype(BF16)
            return carry

        lax.fori_loop(0, n_rb, inv, 0)

    def load_v(rb, carry):
        v = short_conv(v_ref, cwv_ref, cbv_ref, rb)
        cur_ref[block(rb), :] = v
        cb_ref[block(rb), :] = v.astype(BF16)
        return carry

    lax.fori_loop(0, n_rb, load_v, 0)
    long_conv_gate(x1_ref, cw1_ref, cb1_ref, hre0_ref, him0_ref, d_ref[0:1])
    long_conv_gate(x2_ref, cw2_ref, cb2_ref, hre1_ref, him1_ref, d_ref[1:2])

    def store(rb, carry):
        sl = block(rb)
        y_ref[sl, :] = (cur_ref[sl, :] * _silu(z_ref[sl, :])).astype(BF16)
        return carry

    lax.fori_loop(0, n_rb, store, 0)


def _hyena_split_body(x1_refs, x2_refs, v_refs, z_refs, cw1_ref, cw2_ref, cwv_ref, cb1_ref, cb2_ref, cbv_ref,
                      k_refs, d_ref, cq_ref, sq_ref, y_refs, cur_ref, sig_ref, z_buf, *, seq, rows):
    half = seq // 2
    n_rb = half // rows

    def block(rb, shift=0):
        return slice(rb * rows + shift, (rb + 1) * rows + shift)

    def short_conv(u_refs, w_ref, b_ref, rb):
        e_ref, o_ref = u_refs
        even, odd = e_ref[block(rb), :], o_ref[block(rb), :]
        odd_before = o_ref[block(rb, -1), :] if rb > 0 else _shift_down(odd, 0.0)
        even_after = e_ref[block(rb, 1), :] if rb < n_rb - 1 else _shift_up(even, 0.0)
        w0, w1, w2, b = w_ref[0:1], w_ref[1:2], w_ref[2:3], b_ref[...]
        return (odd_before * w0 + even * w1 + odd * w2 + b, even * w0 + odd * w1 + even_after * w2 + b)

    def long_conv_gate(x_ref, cw_ref, cbias_ref, k, dn):
        k1r, k1i, k2r, k2i, k3r, k3i = k
        for fb in range(n_rb):
            sl = block(fb)
            c, s = cq_ref[sl, :], sq_ref[sl, :]
            er, ei = _dot(c, sig_ref[0]), _dot(s, sig_ref[0])
            orr, oi = _dot(c, sig_ref[1]), _dot(s, sig_ref[1])
            a1r, a1i, a2r, a2i, a3r, a3i = (r[sl, :] for r in (k1r, k1i, k2r, k2i, k3r, k3i))
            z_buf[0, sl, :] = (er * a1r - ei * a1i + orr * a2r - oi * a2i).astype(BF16)
            z_buf[1, sl, :] = (er * a1i + ei * a1r + orr * a2i + oi * a2r).astype(BF16)
            z_buf[2, sl, :] = (er * a3r - ei * a3i + orr * a1r - oi * a1i).astype(BF16)
            z_buf[3, sl, :] = (er * a3i + ei * a3r + orr * a1i + oi * a1r).astype(BF16)
        for tb in range(n_rb):
            sl = block(tb)
            c, s = cq_ref[sl, :], sq_ref[sl, :]
            conv = (_dot(c, z_buf[0]) + _dot(s, z_buf[1]), _dot(c, z_buf[2]) + _dot(s, z_buf[3]))
            gate = short_conv(x_ref, cw_ref, cbias_ref, tb)
            for p in range(2):
                new = gate[p] * (conv[p] + cur_ref[p, sl, :] * dn)
                cur_ref[p, sl, :] = new
                sig_ref[p, sl, :] = new.astype(BF16)

    for rb in range(n_rb):
        for p, val in enumerate(short_conv(v_refs, cwv_ref, cbv_ref, rb)):
            cur_ref[p, block(rb), :] = val
            sig_ref[p, block(rb), :] = val.astype(BF16)
    long_conv_gate(x1_refs, cw1_ref, cb1_ref, k_refs[0], d_ref[0:1])
    long_conv_gate(x2_refs, cw2_ref, cb2_ref, k_refs[1], d_ref[1:2])
    for rb in range(n_rb):
        for p in range(2):
            sl = block(rb)
            y_refs[p][sl, :] = (cur_ref[p, sl, :] * _silu(z_refs[p][sl, :])).astype(BF16)


def _hyena_split_entry(*refs, seq, rows):
    it = iter(refs)
    take = lambda n: tuple(next(it) for _ in range(n))
    x1, x2, v, z = take(2), take(2), take(2), take(2)
    conv_refs = take(6)
    k = tuple(take(6) for _ in range(HY_ORDER))
    d, cq, sq = take(3)
    y = take(2)
    _hyena_split_body(x1, x2, v, z, *conv_refs, k, d, cq, sq, y, *it, seq=seq, rows=rows)


def _hyena_mixer(proj, conv_w, conv_b, filt, d_skip, j, n_seq, seq, ce, split, name):
    phases = 2 if split else 1
    t = proj.shape[0] * phases
    width = proj.shape[1] // (4 * phases)
    nb = width // ce
    n_dft = seq // phases
    cq, sq = _signal_tables(n_dft)
    rows = min(n_dft, HY_ROW_BLOCK)
    const = pl.Buffered(1)
    seg = lambda s: pl.BlockSpec((seq, ce), lambda e, b: (b, s * nb + e))
    cw = lambda s: pl.BlockSpec((None, conv_w.shape[1], ce), lambda e, b: (j, 0, s * nb + e))
    cb = lambda s: pl.BlockSpec((None, 1, ce), lambda e, b: (j, 0, s * nb + e))
    hs = lambda n: pl.BlockSpec((n_dft, ce), lambda e, b: (0, n * nb + e),
                                pipeline_mode=const if nb == 1 else pl.Buffered(2))
    conv_b3 = conv_b.reshape(conv_b.shape[0], 1, -1)
    if split:
        body = functools.partial(_hyena_split_entry, seq=seq, rows=rows)
        phase = lambda s, p: pl.BlockSpec((n_dft, ce), lambda e, b: (b, p * 4 * nb + s * nb + e))
        x_specs = [phase(s, p) for s in range(4) for p in range(2)]
        out_specs = [pl.BlockSpec((n_dft, ce), lambda e, b: (b, e))] * 2
        out_shape = [jax.ShapeDtypeStruct((t // 2, width), BF16)] * 2
        scratch = [pltpu.VMEM((2, n_dft, ce), F32), pltpu.VMEM((2, n_dft, ce), BF16),
                   pltpu.VMEM((4, n_dft, ce), BF16)]
    else:
        body = functools.partial(_hyena_body, seq=seq, rows=rows)
        x_specs = [seg(0), seg(1), seg(2), seg(3)]
        out_specs = pl.BlockSpec((seq, ce), lambda e, b: (b, e))
        out_shape = jax.ShapeDtypeStruct((t, width), BF16)
        scratch = [pltpu.VMEM((seq, ce), F32), pltpu.VMEM((seq, ce), BF16),
                   pltpu.VMEM((seq, ce), BF16), pltpu.VMEM((seq, ce), BF16)]
    return pl.pallas_call(
        body,
        grid=(nb, n_seq),
        in_specs=x_specs + [cw(0), cw(1), cw(2), cb(0), cb(1), cb(2)]
        + [hs(n) for n in range(HY_ORDER) for _ in filt]
        + [pl.BlockSpec((None, HY_ORDER, ce), lambda e, b: (j, 0, e)),
           pl.BlockSpec((n_dft, n_dft), lambda e, b: (0, 0), pipeline_mode=const),
           pl.BlockSpec((n_dft, n_dft), lambda e, b: (0, 0), pipeline_mode=const)],
        out_specs=out_specs,
        out_shape=out_shape,
        scratch_shapes=scratch,
        compiler_params=_params("parallel", "parallel"),
        name=name,
    )(*([proj] * len(x_specs)), conv_w, conv_w, conv_w, conv_b3, conv_b3, conv_b3,
      *(list(filt) * HY_ORDER), d_skip, cq, sq)


def kernel(x_prompt, x_sample, state_gla, c, c_ctx, mod_w, mod_b, norm_g, final_norm_g, gla_w_in, gla_w_dec, gla_b_dec, gla_onorm_g, gla_w_out, fn_w_in, fn_w_out, hy_w_in, hy_conv_w, hy_conv_b, hy_ffn_w1, hy_ffn_b1, hy_ffn_w2, hy_ffn_b2, hy_ffn_w3, hy_ffn_b3, hy_ffn_w4, hy_freq, hy_d, hy_w_out):
    n_p, l_p, d = x_prompt.shape
    n_s, l_s, _ = x_sample.shape
    depth = mod_w.shape[0]
    key = gla_w_dec.shape[-1]
    n_main = gla_w_in.shape[-1] - 2 * GLA_RANK

    cvec = jnp.concatenate([c_ctx[None], c, jnp.zeros((MOD_ROWS - 1 - n_s, d), F32)], axis=0)
    mod4 = _modulation(cvec, mod_w, mod_b).reshape(depth, MOD_ROWS, 1, 3 * d)
    norm_g3 = norm_g.reshape(depth, 1, d)

    gla_w_main = gla_w_in.astype(BF16)
    gla_w_lr = gla_w_in[:, :, n_main:].astype(BF16)
    gla_w_out_b = gla_w_out.astype(BF16)
    fn_w_in_b = fn_w_in.astype(BF16)
    fn_w_out_b = fn_w_out.astype(BF16)
    hy_w_in_b = hy_w_in.astype(BF16)
    hy_w_out_b = hy_w_out.astype(BF16)
    emb = hy_ffn_w1.shape[1]
    emb_pad = -(-emb // 128) * 128
    hy_w1p = jnp.pad(hy_ffn_w1, ((0, 0), (0, emb_pad - emb), (0, 0)))
    ffn = hy_ffn_w2.shape[-1]
    hy_vecs = [a.reshape(a.shape[0], 1, ffn) for a in (hy_ffn_b1, hy_ffn_b2, hy_ffn_b3, hy_freq)]

    tm = 512
    tiles_per_sample = l_s // tm
    streams = {
        "p": dict(x=x_prompt.reshape(n_p * l_p, d), n=n_p, l=l_p, row=lambda i: 0),
        "s": dict(x=x_sample.reshape(n_s * l_s, d), n=n_s, l=l_s, row=lambda i: 1 + i // tiles_per_sample),
    }
    def in_weights(i, seq):
        kind, j = i % N_MIXERS, i // N_MIXERS
        if kind == 0:
            return [(gla_w_main, (None, d, n_main), (j, 0, 0)), (gla_w_lr, (None, d, 2 * GLA_RANK), (j, 0, 0))], False
        if kind == 1:
            return [(fn_w_in_b, (None,) + fn_w_in.shape[1:], (j, 0, 0))], False
        return [(hy_w_in_b, (None,) + hy_w_in.shape[1:], (j, 0, 0))], seq >= HY_SPLIT_MIN_SEQ

    for tag, st in streams.items():
        weights, split = in_weights(0, st["l"])
        st["proj"] = _proj(st["x"], mod4, st["row"], tm, f"proj_{tag}0", nxt=(norm_g3, 0, weights, split))
    new_states = None
    for i in range(depth):
        kind, j = i % N_MIXERS, i // N_MIXERS
        if kind == 2:
            filt = {}
            for tag, st in streams.items():
                ce = min(d, 256 if st["l"] > 256 else d)
                filt[tag] = _hyena_filters(st["l"], j, emb, hy_w1p, hy_vecs[0], hy_ffn_w2, hy_vecs[1], hy_ffn_w3,
                                           hy_vecs[2], hy_ffn_w4, hy_vecs[3], ce, st["l"] >= HY_SPLIT_MIN_SEQ,
                                           f"hyfilt_{tag}{i}")
        for tag, st in streams.items():
            n_seq, seq = st["n"], st["l"]
            if kind == 0:
                proj, lr = st["proj"]
                is_ctx = tag == "p"
                res = _gla_mixer(proj, lr, gla_w_dec, gla_b_dec, gla_onorm_g,
                                 None if is_ctx else state_gla, j, n_seq, seq, is_ctx, new_states,
                                 f"gla_{tag}{i}")
                ys = [res[0]]
                if is_ctx:
                    new_states = res[1]
                w_out = gla_w_out_b
            elif kind == 1:
                short = seq <= 256
                ys = [_fnet_mixer(st["proj"][0], n_seq, seq, 4 if short else 1, FN_GROUPS if short else 2,
                                  f"fnet_{tag}{i}")]
                w_out = fn_w_out_b
            else:
                split = seq >= HY_SPLIT_MIN_SEQ
                ce = min(d, 256 if seq > 256 else d)
                y = _hyena_mixer(st["proj"][0], hy_conv_w, hy_conv_b, filt[tag], hy_d, j, n_seq, seq, ce, split,
                                 f"hyena_{tag}{i}")
                ys = list(y) if split else [y]
                w_out = hy_w_out_b
            if i + 1 < depth:
                weights, split = in_weights(i + 1, seq)
                st["x"], *st["proj"] = _proj(st["x"], mod4, st["row"], tm, f"proj_{tag}{i + 1}",
                                             res=(ys, w_out, j, i), nxt=(norm_g3, i + 1, weights, split))
            else:
                (st["x"],) = _proj(st["x"], mod4, st["row"], tm, f"proj_{tag}{i + 1}",
                                   res=(ys, w_out, j, i), final_g=final_norm_g)
    y_prompt = streams["p"]["x"].reshape(n_p, l_p, d)
    y_sample = streams["s"]["x"].reshape(n_s, l_s, d)
    new_state_gla = new_states.astype(x_prompt.dtype)
    return (y_prompt, y_sample, new_state_gla)
```

```python
import functools
import math

import jax
import jax.numpy as jnp
import numpy as np
from jax import lax
from jax.experimental import pallas as pl
from jax.experimental.pallas import tpu as pltpu

F32 = jnp.float32
BF16 = jnp.bfloat16

EPS = 1e-6
N_MIXERS = 3
GLA_HEADS = 4
GLA_RANK = 16
GLA_GATE_NORM = 16.0
FN_GROUPS = 4
HY_ORDER = 2
HY_TARGET = 1e-2
HY_MIN_DECAY = math.log(HY_TARGET) / 1.5
HY_MAX_DECAY = math.log(HY_TARGET) / 0.3
HY_ROW_BLOCK = 512
HY_SPLIT_MIN_SEQ = 512

V7X_VMEM_BYTES = 64 * 1024 * 1024
VMEM_LIMIT_BYTES = V7X_VMEM_BYTES - 8 * 1024 * 1024
MOD_ROWS = 8

_NT = (((1,), (1,)), ((), ()))
_TN = (((0,), (0,)), ((), ()))


def _params(*sem):
    return pltpu.CompilerParams(dimension_semantics=sem, vmem_limit_bytes=VMEM_LIMIT_BYTES)


def _dot(a, b):
    return jnp.dot(a, b, preferred_element_type=F32)


def _split3(x):
    hi = x.astype(BF16)
    r1 = x - hi.astype(F32)
    mid = r1.astype(BF16)
    lo = (r1 - mid.astype(F32)).astype(BF16)
    return hi, mid, lo


def _dot3(a, b):
    return _dot(_dot3_lhs(a), _dot3_rhs(b))


def _dot3_lhs(a):
    ah = a.astype(BF16)
    al = (a - ah.astype(F32)).astype(BF16)
    return jnp.concatenate([ah, ah, al], axis=1)


def _dot3_rhs(b):
    bh = b.astype(BF16)
    bl = (b - bh.astype(F32)).astype(BF16)
    return jnp.concatenate([bh, bl, bh], axis=0)


def _dot_exact_lhs(a_bf16, b):
    return _dot(jnp.concatenate([a_bf16] * 3, axis=1), jnp.concatenate(_split3(b), axis=0))


def _trig_bf16(ang):
    return (jnp.asarray(np.cos(ang), F32).astype(BF16), jnp.asarray(np.sin(ang), F32).astype(BF16))


def _silu(x):
    return x / (1.0 + jnp.exp(-x))


def _log_sigmoid(x):
    return jnp.minimum(x, 0.0) - jnp.log(1.0 + jnp.exp(-jnp.abs(x)))


def _mod_body(c_ref, w_ref, b_ref, o_ref):
    o_ref[...] = _dot3(_silu(c_ref[...]), w_ref[...]) + b_ref[...]


def _modulation(cvec, mod_w, mod_b):
    depth, d, e = mod_w.shape
    tn = d
    return pl.pallas_call(
        _mod_body,
        grid=(depth, e // tn),
        in_specs=[
            pl.BlockSpec((MOD_ROWS, d), lambda l, j: (0, 0)),
            pl.BlockSpec((None, d, tn), lambda l, j: (l, 0, j)),
            pl.BlockSpec((None, 1, tn), lambda l, j: (l, 0, j)),
        ],
        out_specs=pl.BlockSpec((None, MOD_ROWS, tn), lambda l, j: (l, 0, j)),
        out_shape=jax.ShapeDtypeStruct((depth, MOD_ROWS, e), F32),
        compiler_params=_params("parallel", "parallel"),
        name="modulation",
    )(cvec, mod_w, mod_b.reshape(depth, 1, e))


def _phase_permutation(rows):
    order = np.concatenate([np.arange(0, rows, 2), np.arange(1, rows, 2)])
    return jnp.asarray(np.eye(rows, dtype=np.float32)[order], BF16)


def _proj_body(*refs, n_y, has_res, final, has_next, split, n_w):
    it = iter(refs)
    take = lambda n: [next(it) for _ in range(n)]
    if has_res:
        y_refs = take(n_y)
        unperm_ref = next(it) if n_y == 2 else None
        w_out_ref, gate_ref = take(2)
        fg_ref = next(it) if final else None
    x_ref = next(it)
    if has_next:
        g_ref, mod_ref = take(2)
        perm_ref = next(it) if split else None
        w_refs = take(n_w)
    x_out_ref = next(it) if has_res else None
    o_refs = take(n_w) if has_next else []

    x = x_ref[...]
    d = x.shape[-1]
    if has_res:
        if n_y == 2:
            y = _dot(unperm_ref[...], jnp.concatenate([r[...] for r in y_refs], axis=0)).astype(BF16)
        else:
            y = y_refs[0][...]
        x = x + gate_ref[...][:, 2 * d:] * _dot(y, w_out_ref[...])
        if final:
            x_out_ref[...] = x * lax.rsqrt(jnp.mean(x * x, axis=-1, keepdims=True) + EPS) * fg_ref[...]
        else:
            x_out_ref[...] = x
    if has_next:
        h = x * lax.rsqrt(jnp.mean(x * x, axis=-1, keepdims=True) + EPS) * g_ref[...]
        mod = mod_ref[...]
        hb = (h * (1.0 + mod[:, d:2 * d]) + mod[:, :d]).astype(BF16)
        if split:
            hb = _dot(perm_ref[...], hb).astype(BF16)
        for w_ref, o_ref in zip(w_refs, o_refs):
            res = _dot(hb, w_ref[...])
            if split:
                half, n = res.shape[0] // 2, res.shape[1]
                o_ref[:, :n] = res[:half]
                o_ref[:, n:] = res[half:]
            else:
                o_ref[...] = res


def _proj(x, mod4, mod_row, tm, name, res=None, nxt=None, final_g=None):
    t, d = x.shape
    has_res, has_next, final = res is not None, nxt is not None, final_g is not None
    mod_spec = lambda layer: pl.BlockSpec((None, None, 1, mod4.shape[-1]), lambda i: (layer, mod_row(i), 0, 0))
    in_specs, args, out_specs, out_shape = [], [], [], []
    n_y, split, n_w = 0, False, 0
    if has_res:
        ys, w_out, j, layer = res
        n_y = len(ys)
        in_specs += [pl.BlockSpec((tm // n_y, y.shape[-1]), lambda i: (i, 0)) for y in ys]
        args += list(ys)
        if n_y == 2:
            in_specs.append(pl.BlockSpec((tm, tm), lambda i: (0, 0)))
            args.append(_phase_permutation(tm).T)
        in_specs += [pl.BlockSpec((None,) + w_out.shape[1:], lambda i: (j, 0, 0), pipeline_mode=pl.Buffered(1)),
                     mod_spec(layer)]
        args += [w_out, mod4]
        if final:
            in_specs.append(pl.BlockSpec((1, d), lambda i: (0, 0)))
            args.append(final_g.reshape(1, d))
        out_specs.append(pl.BlockSpec((tm, d), lambda i: (i, 0)))
        out_shape.append(jax.ShapeDtypeStruct((t, d), F32))
    in_specs.append(pl.BlockSpec((tm, d), lambda i: (i, 0)))
    args.append(x)
    if has_next:
        norm_g3, next_layer, weights, split = nxt
        n_w = len(weights)
        in_specs += [pl.BlockSpec((None, 1, d), lambda i: (next_layer, 0, 0)), mod_spec(next_layer)]
        args += [norm_g3, mod4]
        if split:
            in_specs.append(pl.BlockSpec((tm, tm), lambda i: (0, 0)))
            args.append(_phase_permutation(tm))
        in_specs += [pl.BlockSpec(bs, functools.partial(lambda idx, i: idx, idx), pipeline_mode=pl.Buffered(1))
                     for _, bs, idx in weights]
        args += [w for w, _, _ in weights]
        p = 2 if split else 1
        widths = [bs[-1] for _, bs, _ in weights]
        out_specs += [pl.BlockSpec((tm // p, p * n), lambda i: (i, 0)) for n in widths]
        out_shape += [jax.ShapeDtypeStruct((t // p, p * n), F32) for n in widths]
    return pl.pallas_call(
        functools.partial(_proj_body, n_y=n_y, has_res=has_res, final=final, has_next=has_next,
                          split=split, n_w=n_w),
        grid=(t // tm,),
        in_specs=in_specs,
        out_specs=out_specs,
        out_shape=out_shape,
        compiler_params=_params("parallel"),
        name=name,
    )(*args)


GLA_CHUNK = 64
GLA_BLOCK = 256
GLA_HEADS_PER_STEP = 2
LOG2_E = math.log2(math.e)
GLA_CLAMP_SPLIT = 8.0 * LOG2_E
GLA_CLAMP_HALF = 80.0 * LOG2_E


def _gla_tables(rows, dk):
    c, hc = GLA_CHUNK, GLA_CHUNK // 2
    r = np.arange(rows)[:, None]
    s = np.arange(rows)[None, :]
    same_chunk = (r // c) == (s // c)
    same_half = (r // hc) == (s // hc)
    tri = np.stack([same_chunk & (s <= r), same_chunk & (s >= r)])
    m_half = np.stack([same_half & (s < r), same_half & (s > r)])
    sel = np.concatenate([s == (r // c) * c + hc, s == (r // hc) * hc + hc // 2], axis=0)
    second = np.broadcast_to((r // hc) % 2 == 1, (rows, dk))
    as_const = lambda m, dt: jnp.asarray(m.astype(np.float32), dt)
    return (as_const(tri, BF16), as_const(sel, BF16), as_const(same_chunk, F32), as_const(m_half, F32),
            as_const(second, F32), as_const(2.0 * (s == r), F32))


def _gla_body(q_ref, k_ref, v_ref, r_ref, lr_ref, wdec_ref, bdec_ref, og_ref, tri_ref, sel_ref,
              mchunk_ref, mhalf_ref, second_ref, diag2_ref, *rest, seq, layer, has_s0, want_state,
              has_states_in):
    rest = list(rest)
    s0_ref = rest.pop(0) if has_s0 else None
    if has_states_in:
        rest.pop(0)
    y_ref = rest.pop(0)
    sfin_ref = rest.pop(0) if want_state else None
    o_ref, qi_ref, cum_ref, st_ref = rest
    chunk = GLA_CHUNK
    n_chunks = seq // chunk
    blk = tri_ref.shape[-1]
    n_heads = wdec_ref.shape[0]
    dk = q_ref.shape[-1] // n_heads
    dv = v_ref.shape[-1] // n_heads
    scale = dk ** -0.5
    heads = range(n_heads)
    dirs = range(2)
    chains = [(hh, d) for hh in heads for d in dirs]

    def nt(a, b):
        return lax.dot_general(a, b, _NT, preferred_element_type=F32)

    def cols(hh, width):
        return slice(hh * width, (hh + 1) * width)

    def block_pass(bi, carry):
        rows = pl.ds(pl.multiple_of(bi * blk, blk), blk)
        q = [q_ref[rows, cols(hh, dk)] * scale for hh in heads]
        k = [k_ref[rows, cols(hh, dk)] for hh in heads]
        lr = lr_ref[rows, :]
        g = [_log_sigmoid(_dot3(lr, wdec_ref[hh]) + bdec_ref[hh]) * (LOG2_E / GLA_GATE_NORM) for hh in heads]
        cum = {(hh, d): _dot_exact_lhs(tri_ref[d], g[hh][:, cols(d, dk)]) for hh, d in chains}
        refs = [_dot(sel_ref[...], jnp.concatenate([cum[hh, 0], cum[hh, 1]], axis=1).astype(BF16))
                for hh in heads]
        second = second_ref[...]
        first = 1.0 - second
        p_split, p_half = [], {}
        for hh in heads:
            q_in = (q[hh] * second, q[hh] * first)
            k_in = (k[hh] * first, k[hh] * second)
            q1, k1 = [], []
            for d in dirs:
                c = (hh, d)
                d_split = cum[c] - refs[hh][:blk, cols(d, dk)]
                d_half = cum[c] - refs[hh][blk:, cols(d, dk)]
                q1.append((q_in[d] * jnp.exp2(jnp.minimum(d_split, GLA_CLAMP_SPLIT))).astype(BF16))
                k1.append((k_in[d] * jnp.exp2(jnp.minimum(-d_split, GLA_CLAMP_SPLIT))).astype(BF16))
                q0 = (q[hh] * jnp.exp2(jnp.minimum(d_half, GLA_CLAMP_HALF))).astype(BF16)
                k0 = (k[hh] * jnp.exp2(jnp.minimum(-d_half, GLA_CLAMP_HALF))).astype(BF16)
                p_half[c] = nt(q0, k0)
                qi_ref[hh, d, rows, :] = (q[hh] * jnp.exp2(cum[c])).astype(BF16)
                cum_ref[hh, d, rows, :] = cum[c]
            p_split.append(nt(jnp.concatenate(q1, axis=1), jnp.concatenate(k1, axis=1)))
        for hh in heads:
            att = p_split[hh] * mchunk_ref[...] + diag2_ref[...] * jnp.sum(q[hh] * k[hh], axis=-1, keepdims=True)
            for d in dirs:
                att = att + jnp.where(mhalf_ref[d] > 0.5, p_half[hh, d], 0.0)
            o_ref[rows, cols(hh, dv)] = _dot(att.astype(BF16), v_ref[rows, cols(hh, dv)].astype(BF16))
        return carry

    lax.fori_loop(0, seq // blk, block_pass, 0)

    for hh, d in chains:
        st_ref[hh, d] = s0_ref[d, hh] if has_s0 else jnp.zeros((dk, dv), F32)

    def chunk_step(i, carry):
        for hh, d in chains:
            c = i if d == 0 else n_chunks - 1 - i
            sl = pl.ds(pl.multiple_of(c * chunk, chunk), chunk)
            cum = cum_ref[hh, d, sl, :]
            blast = cum[chunk - 1:chunk] if d == 0 else cum[0:1]
            kh = (k_ref[sl, cols(hh, dk)] * jnp.exp2(blast - cum)).astype(BF16)
            vb = v_ref[sl, cols(hh, dv)].astype(BF16)
            st = st_ref[hh, d]
            o_ref[sl, cols(hh, dv)] += _dot(qi_ref[hh, d, sl, :], st.astype(BF16))
            decay = jnp.broadcast_to(jnp.exp2(blast), (dk, dk)).T
            st_ref[hh, d] = (jnp.concatenate([decay] * (dv // dk), axis=1) * st
                             + lax.dot_general(kh, vb, _TN, preferred_element_type=F32))
        return carry

    lax.fori_loop(0, n_chunks, chunk_step, 0, unroll=min(n_chunks, 4))

    def finish(jb, carry):
        sl = pl.ds(pl.multiple_of(jb * blk, blk), blk)
        for hh in heads:
            o = o_ref[sl, cols(hh, dv)]
            o = o * lax.rsqrt(jnp.mean(o * o, axis=-1, keepdims=True) + EPS) * og_ref[...]
            y_ref[sl, cols(hh, dv)] = (o * _silu(r_ref[sl, cols(hh, dv)])).astype(BF16)
        return carry

    lax.fori_loop(0, seq // blk, finish, 0)
    if want_state:
        own = sfin_ref if has_states_in else sfin_ref.at[layer]
        for hh, d in chains:
            own[d, hh] = st_ref[hh, d]
        if not has_states_in:
            for other in range(sfin_ref.shape[0]):
                if other != layer:
                    sfin_ref[other] = jnp.zeros(sfin_ref.shape[1:], F32)


def _gla_mixer(proj, lr, w_dec, b_dec, onorm_g, s0, j, n_seq, seq, want_state, states, name):
    t = proj.shape[0]
    h = GLA_HEADS
    key = w_dec.shape[-1]
    dk = key // h
    val = (proj.shape[1] - 2 * key) // 2
    dv = val // h
    has_s0 = s0 is not None
    tables = _gla_tables(min(seq, GLA_BLOCK), dk)
    fixed = lambda a: pl.BlockSpec(a.shape, lambda b, hp: (0,) * a.ndim)
    hps = h if seq <= GLA_BLOCK else GLA_HEADS_PER_STEP
    wk, wv = hps * dk, hps * dv
    in_specs = [
        pl.BlockSpec((seq, wk), lambda b, hp: (b, hp)),
        pl.BlockSpec((seq, wk), lambda b, hp: (b, key // wk + hp)),
        pl.BlockSpec((seq, wv), lambda b, hp: (b, (2 * key) // wv + hp)),
        pl.BlockSpec((seq, wv), lambda b, hp: (b, (2 * key + val) // wv + hp)),
        pl.BlockSpec((seq, 2 * GLA_RANK), lambda b, hp: (b, 0)),
        pl.BlockSpec((None, hps, 2 * GLA_RANK, 2 * dk), lambda b, hp: (j, hp, 0, 0)),
        pl.BlockSpec((None, hps, 1, 2 * dk), lambda b, hp: (j, hp, 0, 0)),
        pl.BlockSpec((None, 1, dv), lambda b, hp: (j, 0, 0)),
    ] + [fixed(a) for a in tables]
    n_gla = w_dec.shape[0]
    w5 = w_dec.reshape(n_gla, 2, GLA_RANK, h, dk)
    zero = jnp.zeros_like(w5[:, 0])
    w_blk = jnp.concatenate([jnp.concatenate([w5[:, 0], zero], axis=-1),
                             jnp.concatenate([zero, w5[:, 1]], axis=-1)], axis=1).transpose(0, 2, 1, 3)
    b_cat = b_dec.reshape(n_gla, 2, h, dk).transpose(0, 2, 1, 3).reshape(n_gla, h, 1, 2 * dk)
    args = [proj, proj, proj, proj, lr, w_blk, b_cat, onorm_g.reshape(n_gla, 1, dv), *tables]
    if has_s0:
        in_specs.append(pl.BlockSpec((None, None, 2, hps, dk, dv), lambda b, hp: (b, j, 0, hp, 0, 0)))
        args.append(s0)
    out_specs = [pl.BlockSpec((seq, wv), lambda b, hp: (b, hp))]
    out_shape = [jax.ShapeDtypeStruct((t, val), BF16)]
    aliases = {}
    if want_state:
        out_shape.append(jax.ShapeDtypeStruct((n_seq, n_gla, 2, h, dk, dv), F32))
        if states is None:
            out_specs.append(pl.BlockSpec((None, n_gla, 2, hps, dk, dv), lambda b, hp: (b, 0, 0, hp, 0, 0)))
        else:
            out_specs.append(pl.BlockSpec((None, None, 2, hps, dk, dv), lambda b, hp: (b, j, 0, hp, 0, 0)))
            in_specs.append(pl.BlockSpec(memory_space=pl.ANY))
            args.append(states)
            aliases = {len(args) - 1: 1}
    return pl.pallas_call(
        functools.partial(_gla_body, seq=seq, layer=j, has_s0=has_s0, want_state=want_state,
                          has_states_in=bool(aliases)),
        grid=(n_seq, h // hps),
        in_specs=in_specs,
        out_specs=out_specs,
        out_shape=out_shape,
        input_output_aliases=aliases,
        scratch_shapes=[pltpu.VMEM((seq, wv), F32), pltpu.VMEM((hps, 2, seq, dk), BF16),
                        pltpu.VMEM((hps, 2, seq, dk), F32), pltpu.VMEM((hps, 2, dk, dv), F32)],
        compiler_params=_params("parallel", "parallel"),
        name=name,
    )(*args)


def _dft_tables(n):
    f = np.arange(n, dtype=np.int64)
    ang = (np.outer(f, f) % n).astype(np.float64) * (2.0 * math.pi / n)
    return _trig_bf16(ang)


def _fnet_body(x_ref, z_ref, cc_ref, sc_ref, cl_ref, sl_ref, y_ref, *, seq, n_seq, gc):
    inv = 1.0 / math.sqrt(seq * gc)
    for g in range(x_ref.shape[-1] // gc):
        cols = slice(g * gc, (g + 1) * gc)
        xb = x_ref[:, cols].astype(BF16)
        a = _dot(xb, cc_ref[...]).astype(BF16)
        bm = _dot(xb, sc_ref[...]).astype(BF16)
        for i in range(n_seq):
            rows = slice(i * seq, (i + 1) * seq)
            f = _dot(cl_ref[...], a[rows]) - _dot(sl_ref[...], bm[rows])
            y_ref[rows, cols] = (f * inv * _silu(z_ref[rows, cols])).astype(BF16)


def _fnet_mixer(proj, n_seq_total, seq, seqs_per_step, groups_per_step, name):
    t = proj.shape[0]
    width = proj.shape[1] // 2
    gc = width // FN_GROUPS
    cc, sc = _dft_tables(gc)
    cl, sl = _dft_tables(seq)
    rows = seq * seqs_per_step
    wide = gc * groups_per_step
    n_wide = width // wide
    const = pl.Buffered(1)
    return pl.pallas_call(
        functools.partial(_fnet_body, seq=seq, n_seq=seqs_per_step, gc=gc),
        grid=(n_seq_total // seqs_per_step, n_wide),
        in_specs=[
            pl.BlockSpec((rows, wide), lambda i, g: (i, g)),
            pl.BlockSpec((rows, wide), lambda i, g: (i, n_wide + g)),
            pl.BlockSpec((gc, gc), lambda i, g: (0, 0), pipeline_mode=const),
            pl.BlockSpec((gc, gc), lambda i, g: (0, 0), pipeline_mode=const),
            pl.BlockSpec((seq, seq), lambda i, g: (0, 0), pipeline_mode=const),
            pl.BlockSpec((seq, seq), lambda i, g: (0, 0), pipeline_mode=const),
        ],
        out_specs=pl.BlockSpec((rows, wide), lambda i, g: (i, g)),
        out_shape=jax.ShapeDtypeStruct((t, width), BF16),
        compiler_params=_params("parallel", "parallel"),
        name=name,
    )(proj, proj, cc, sc, cl, sl)


def _signal_tables(n):
    o = 2 * np.arange(n, dtype=np.int64) + 1
    ang = (np.outer(o, o) % (8 * n)).astype(np.float64) * (math.pi / (4 * n))
    cos, sin = _trig_bf16(ang)
    return cos, -sin


def _filter_tables(n, split):
    f = np.arange(n, dtype=np.int64)
    if split:
        f = np.concatenate([f[:n // 2], f[:n // 2 - 1:-1]])
    f = 2 * f + 1
    o = np.arange(n, dtype=np.int64)
    ang = (np.outer(f, o) % (4 * n)).astype(np.float64) * (math.pi / (2 * n))
    return _trig_bf16(ang)


def _hyena_positions(seq, bands, pad_to):
    t = np.linspace(0.0, 1.0, seq)[:, None]
    w = 2.0 * math.pi * np.arange(seq)[:, None] / seq
    f = np.linspace(1e-4, bands - 1, bands)[None, :]
    zpos = np.concatenate([t, np.cos(f * w), -np.sin(f * w)], axis=-1)
    zpos = np.pad(zpos, ((0, 0), (0, pad_to - zpos.shape[1])))
    return jnp.asarray(zpos, F32), jnp.asarray(t, F32)


def _hyfilt_body(zpos_ref, w1_ref, b1_ref, w2_ref, b2_ref, w3_ref, b3_ref, fr_ref, w4f_ref,
                 w4b_ref, t_ref, del_ref, ch_ref, sh_ref, *rest, split):
    if split:
        c2_ref, s2_ref = rest[:2]
        out_refs, a3_ref = rest[2:-1], rest[-1]
    else:
        out_refs, a3_ref = rest[:-1], rest[-1]

    @pl.when(pl.program_id(0) == 0)
    def _():
        fr = fr_ref[...]
        a = jnp.sin(fr * (_dot3(zpos_ref[...], w1_ref[...]) + b1_ref[...]))
        a = jnp.sin(fr * (_dot3(a, w2_ref[...]) + b2_ref[...]))
        a3_ref[...] = _dot3_lhs(jnp.sin(fr * (_dot3(a, w3_ref[...]) + b3_ref[...])))

    a3 = a3_ref[...]
    dec = jnp.exp(-t_ref[...] * del_ref[...]) * (1.0 / t_ref.shape[0])
    fwd = _dot(a3, _dot3_rhs(w4f_ref[...])) * dec
    bwd = _dot(a3, _dot3_rhs(w4b_ref[...])) * dec
    first = lax.broadcasted_iota(jnp.int32, bwd.shape, 0) == 0
    bwd = jnp.where(first, 0.0, bwd)
    hre = _dot(ch_ref[...], (fwd + bwd).astype(BF16))
    him = _dot(sh_ref[...], (bwd - fwd).astype(BF16))
    if not split:
        out_refs[0][...] = hre
        out_refs[1][...] = him
        return
    half = hre.shape[0] // 2
    plus_r, plus_i = hre[:half] + hre[half:], him[:half] - him[half:]
    minus_r, minus_i = hre[:half] - hre[half:], him[:half] + him[half:]
    c2, s2 = c2_ref[...], s2_ref[...]
    for ref, val in zip(out_refs, (plus_r, plus_i,
                                   c2 * minus_r + s2 * minus_i, c2 * minus_i - s2 * minus_r,
                                   c2 * minus_r - s2 * minus_i, c2 * minus_i + s2 * minus_r)):
        ref[...] = val


def _hyena_filters(seq, j, emb, w1p, b1, w2, b2, w3, b3, w4, freq, ce, split, name):
    ffn = w2.shape[-1]
    width = w4.shape[-1] // (2 * HY_ORDER)
    emb_pad = w1p.shape[1]
    zpos, tcol = _hyena_positions(seq, (emb - 1) // 2, emb_pad)
    deltas = jnp.asarray(np.abs(np.linspace(HY_MIN_DECAY, HY_MAX_DECAY, width, dtype=np.float32))[None, :])
    ch, sh = _filter_tables(seq, split)
    nb = width // ce
    const = pl.Buffered(1)
    vec = lambda: pl.BlockSpec((None, 1, ffn), lambda c: (j, 0, 0))
    out_rows, n_out = (seq // 2, 6) if split else (seq, 2)
    twiddles, twiddle_specs = [], []
    if split:
        ang = np.broadcast_to((math.pi * (2 * np.arange(out_rows) + 1) / (2 * seq))[:, None], (out_rows, ce))
        twiddles = [jnp.asarray(np.cos(ang), F32), jnp.asarray(np.sin(ang), F32)]
        twiddle_specs = [pl.BlockSpec((out_rows, ce), lambda c: (0, 0), pipeline_mode=const)] * 2
    return pl.pallas_call(
        functools.partial(_hyfilt_body, split=split),
        grid=(HY_ORDER * nb,),
        in_specs=[
            pl.BlockSpec((seq, emb_pad), lambda c: (0, 0)),
            pl.BlockSpec((None, emb_pad, ffn), lambda c: (j, 0, 0)),
            vec(),
            pl.BlockSpec((None, ffn, ffn), lambda c: (j, 0, 0)),
            vec(),
            pl.BlockSpec((None, ffn, ffn), lambda c: (j, 0, 0)),
            vec(),
            vec(),
            pl.BlockSpec((None, ffn, ce), lambda c: (j, 0, (c // nb) * 2 * nb + c % nb)),
            pl.BlockSpec((None, ffn, ce), lambda c: (j, 0, (c // nb) * 2 * nb + nb + c % nb)),
            pl.BlockSpec((seq, 1), lambda c: (0, 0)),
            pl.BlockSpec((1, ce), lambda c: (0, c % nb)),
            pl.BlockSpec((seq, seq), lambda c: (0, 0), pipeline_mode=const),
            pl.BlockSpec((seq, seq), lambda c: (0, 0), pipeline_mode=const),
        ] + twiddle_specs,
        out_specs=[pl.BlockSpec((out_rows, ce), lambda c: (0, c))] * n_out,
        out_shape=[jax.ShapeDtypeStruct((out_rows, HY_ORDER * width), F32)] * n_out,
        scratch_shapes=[pltpu.VMEM((seq, 3 * ffn), BF16)],
        compiler_params=_params("arbitrary"),
        name=name,
    )(zpos, w1p, b1, w2, b2, w3, b3, freq, w4, w4, tcol, deltas, ch, sh, *twiddles)


SUBLANES = 8


def _shift_down(u, first):
    rolled = pltpu.roll(u, 1, 0)
    row = lax.broadcasted_iota(jnp.int32, (SUBLANES, 1), 0)
    return jnp.concatenate([jnp.where(row == 0, first, rolled[:SUBLANES]), rolled[SUBLANES:]], axis=0)


def _shift_up(u, last):
    n = u.shape[0]
    rolled = pltpu.roll(u, n - 1, 0)
    row = lax.broadcasted_iota(jnp.int32, (SUBLANES, 1), 0)
    tail = jnp.where(row == SUBLANES - 1, last, rolled[n - SUBLANES:])
    return jnp.concatenate([rolled[:n - SUBLANES], tail], axis=0)


def _hyena_body(x1_ref, x2_ref, v_ref, z_ref, cw1_ref, cw2_ref, cwv_ref, cb1_ref, cb2_ref, cbv_ref,
                hre0_ref, him0_ref, hre1_ref, him1_ref, d_ref, cq_ref, sq_ref, y_ref,
                cur_ref, cb_ref, yr_ref, yi_ref, *, seq, rows):
    n_rb = seq // rows

    def block(rb):
        return pl.ds(pl.multiple_of(rb * rows, rows), rows)

    def short_conv(u_ref, w_ref, b_ref, rb):
        u = u_ref[block(rb), :]
        before = u_ref[pl.ds(jnp.maximum(rb * rows - 1, 0), 1), :]
        after = u_ref[pl.ds(jnp.minimum((rb + 1) * rows, seq - 1), 1), :]
        before = jnp.where(rb == 0, 0.0, before)
        after = jnp.where(rb == n_rb - 1, 0.0, after)
        prev, nxt = _shift_down(u, before), _shift_up(u, after)
        return prev * w_ref[0:1] + u * w_ref[1:2] + nxt * w_ref[2:3] + b_ref[...]

    def long_conv_gate(x_ref, cw_ref, cbias_ref, hre_ref, him_ref, dn):
        def fwd(rb, carry):
            sl = block(rb)
            ur = _dot(cq_ref[sl, :], cb_ref[...])
            ui = _dot(sq_ref[sl, :], cb_ref[...])
            hr = hre_ref[sl, :]
            hi = him_ref[sl, :]
            yr_ref[sl, :] = (ur * hr - ui * hi).astype(BF16)
            yi_ref[sl, :] = (ur * hi + ui * hr).astype(BF16)
            return carry

        lax.fori_loop(0, n_rb, fwd, 0)

        def inv(rb, carry):
            sl = block(rb)
            c = _dot(cq_ref[sl, :], yr_ref[...]) + _dot(sq_ref[sl, :], yi_ref[...])
            new = short_conv(x_ref, cw_ref, cbias_ref, rb) * (c + cur_ref[sl, :] * dn)
            cur_ref[sl, :] = new
            cb_ref[sl, :] = new.astype(BF16)
            return carry

        lax.fori_loop(0, n_rb, inv, 0)

    def load_v(rb, carry):
        v = short_conv(v_ref, cwv_ref, cbv_ref, rb)
        cur_ref[block(rb), :] = v
        cb_ref[block(rb), :] = v.astype(BF16)
        return carry

    lax.fori_loop(0, n_rb, load_v, 0)
    long_conv_gate(x1_ref, cw1_ref, cb1_ref, hre0_ref, him0_ref, d_ref[0:1])
    long_conv_gate(x2_ref, cw2_ref, cb2_ref, hre1_ref, him1_ref, d_ref[1:2])

    def store(rb, carry):
        sl = block(rb)
        y_ref[sl, :] = (cur_ref[sl, :] * _silu(z_ref[sl, :])).astype(BF16)
        return carry

    lax.fori_loop(0, n_rb, store, 0)


def _hyena_split_body(x1_refs, x2_refs, v_refs, z_refs, cw1_ref, cw2_ref, cwv_ref, cb1_ref, cb2_ref, cbv_ref,
                      k_refs, d_ref, cq_ref, sq_ref, y_refs, cur_ref, sig_ref, z_buf, *, seq, rows):
    half = seq // 2
    n_rb = half // rows

    def block(rb, shift=0):
        return slice(rb * rows + shift, (rb + 1) * rows + shift)

    def short_conv(u_refs, w_ref, b_ref, rb):
        e_ref, o_ref = u_refs
        even, odd = e_ref[block(rb), :], o_ref[block(rb), :]
        odd_before = o_ref[block(rb, -1), :] if rb > 0 else _shift_down(odd, 0.0)
        even_after = e_ref[block(rb, 1), :] if rb < n_rb - 1 else _shift_up(even, 0.0)
        w0, w1, w2, b = w_ref[0:1], w_ref[1:2], w_ref[2:3], b_ref[...]
        return (odd_before * w0 + even * w1 + odd * w2 + b, even * w0 + odd * w1 + even_after * w2 + b)

    def long_conv_gate(x_ref, cw_ref, cbias_ref, k, dn):
        k1r, k1i, k2r, k2i, k3r, k3i = k
        for fb in range(n_rb):
            sl = block(fb)
            c, s = cq_ref[sl, :], sq_ref[sl, :]
            er, ei = _dot(c, sig_ref[0]), _dot(s, sig_ref[0])
            orr, oi = _dot(c, sig_ref[1]), _dot(s, sig_ref[1])
            a1r, a1i, a2r, a2i, a3r, a3i = (r[sl, :] for r in (k1r, k1i, k2r, k2i, k3r, k3i))
            z_buf[0, sl, :] = (er * a1r - ei * a1i + orr * a2r - oi * a2i).astype(BF16)
            z_buf[1, sl, :] = (er * a1i + ei * a1r + orr * a2i + oi * a2r).astype(BF16)
            z_buf[2, sl, :] = (er * a3r - ei * a3i + orr * a1r - oi * a1i).astype(BF16)
            z_buf[3, sl, :] = (er * a3i + ei * a3r + orr * a1i + oi * a1r).astype(BF16)
        for tb in range(n_rb):
            sl = block(tb)
            c, s = cq_ref[sl, :], sq_ref[sl, :]
            conv = (_dot(c, z_buf[0]) + _dot(s, z_buf[1]), _dot(c, z_buf[2]) + _dot(s, z_buf[3]))
            gate = short_conv(x_ref, cw_ref, cbias_ref, tb)
            for p in range(2):
                new = gate[p] * (conv[p] + cur_ref[p, sl, :] * dn)
                cur_ref[p, sl, :] = new
                sig_ref[p, sl, :] = new.astype(BF16)

    for rb in range(n_rb):
        for p, val in enumerate(short_conv(v_refs, cwv_ref, cbv_ref, rb)):
            cur_ref[p, block(rb), :] = val
            sig_ref[p, block(rb), :] = val.astype(BF16)
    long_conv_gate(x1_refs, cw1_ref, cb1_ref, k_refs[0], d_ref[0:1])
    long_conv_gate(x2_refs, cw2_ref, cb2_ref, k_refs[1], d_ref[1:2])
    for rb in range(n_rb):
        for p in range(2):
            sl = block(rb)
            y_refs[p][sl, :] = (cur_ref[p, sl, :] * _silu(z_refs[p][sl, :])).astype(BF16)


def _hyena_split_entry(*refs, seq, rows):
    it = iter(refs)
    take = lambda n: tuple(next(it) for _ in range(n))
    x1, x2, v, z = take(2), take(2), take(2), take(2)
    conv_refs = take(6)
    k = tuple(take(6) for _ in range(HY_ORDER))
    d, cq, sq = take(3)
    y = take(2)
    _hyena_split_body(x1, x2, v, z, *conv_refs, k, d, cq, sq, y, *it, seq=seq, rows=rows)


def _hyena_mixer(proj, conv_w, conv_b, filt, d_skip, j, n_seq, seq, ce, split, name):
    phases = 2 if split else 1
    t = proj.shape[0] * phases
    width = proj.shape[1] // (4 * phases)
    nb = width // ce
    n_dft = seq // phases
    cq, sq = _signal_tables(n_dft)
    rows = min(n_dft, HY_ROW_BLOCK)
    const = pl.Buffered(1)
    seg = lambda s: pl.BlockSpec((seq, ce), lambda e, b: (b, s * nb + e))
    cw = lambda s: pl.BlockSpec((None, conv_w.shape[1], ce), lambda e, b: (j, 0, s * nb + e))
    cb = lambda s: pl.BlockSpec((None, 1, ce), lambda e, b: (j, 0, s * nb + e))
    hs = lambda n: pl.BlockSpec((n_dft, ce), lambda e, b: (0, n * nb + e),
                                pipeline_mode=const if nb == 1 else pl.Buffered(2))
    conv_b3 = conv_b.reshape(conv_b.shape[0], 1, -1)
    if split:
        body = functools.partial(_hyena_split_entry, seq=seq, rows=rows)
        phase = lambda s, p: pl.BlockSpec((n_dft, ce), lambda e, b: (b, p * 4 * nb + s * nb + e))
        x_specs = [phase(s, p) for s in range(4) for p in range(2)]
        out_specs = [pl.BlockSpec((n_dft, ce), lambda e, b: (b, e))] * 2
        out_shape = [jax.ShapeDtypeStruct((t // 2, width), BF16)] * 2
        scratch = [pltpu.VMEM((2, n_dft, ce), F32), pltpu.VMEM((2, n_dft, ce), BF16),
                   pltpu.VMEM((4, n_dft, ce), BF16)]
    else:
        body = functools.partial(_hyena_body, seq=seq, rows=rows)
        x_specs = [seg(0), seg(1), seg(2), seg(3)]
        out_specs = pl.BlockSpec((seq, ce), lambda e, b: (b, e))
        out_shape = jax.ShapeDtypeStruct((t, width), BF16)
        scratch = [pltpu.VMEM((seq, ce), F32), pltpu.VMEM((seq, ce), BF16),
                   pltpu.VMEM((seq, ce), BF16), pltpu.VMEM((seq, ce), BF16)]
    return pl.pallas_call(
        body,
        grid=(nb, n_seq),
        in_specs=x_specs + [cw(0), cw(1), cw(2), cb(0), cb(1), cb(2)]
        + [hs(n) for n in range(HY_ORDER) for _ in filt]
        + [pl.BlockSpec((None, HY_ORDER, ce), lambda e, b: (j, 0, e)),
           pl.BlockSpec((n_dft, n_dft), lambda e, b: (0, 0), pipeline_mode=const),
           pl.BlockSpec((n_dft, n_dft), lambda e, b: (0, 0), pipeline_mode=const)],
        out_specs=out_specs,
        out_shape=out_shape,
        scratch_shapes=scratch,
        compiler_params=_params("parallel", "parallel"),
        name=name,
    )(*([proj] * len(x_specs)), conv_w, conv_w, conv_w, conv_b3, conv_b3, conv_b3,
      *(list(filt) * HY_ORDER), d_skip, cq, sq)


def kernel(x_prompt, x_sample, state_gla, c, c_ctx, mod_w, mod_b, norm_g, final_norm_g, gla_w_in, gla_w_dec, gla_b_dec, gla_onorm_g, gla_w_out, fn_w_in, fn_w_out, hy_w_in, hy_conv_w, hy_conv_b, hy_ffn_w1, hy_ffn_b1, hy_ffn_w2, hy_ffn_b2, hy_ffn_w3, hy_ffn_b3, hy_ffn_w4, hy_freq, hy_d, hy_w_out):
    n_p, l_p, d = x_prompt.shape
    n_s, l_s, _ = x_sample.shape
    depth = mod_w.shape[0]
    key = gla_w_dec.shape[-1]
    n_main = gla_w_in.shape[-1] - 2 * GLA_RANK

    cvec = jnp.concatenate([c_ctx[None], c, jnp.zeros((MOD_ROWS - 1 - n_s, d), F32)], axis=0)
    mod4 = _modulation(cvec, mod_w, mod_b).reshape(depth, MOD_ROWS, 1, 3 * d)
    norm_g3 = norm_g.reshape(depth, 1, d)

    gla_w_main = gla_w_in[:, :, :n_main].astype(BF16)
    gla_w_lr = gla_w_in[:, :, n_main:].astype(BF16)
    gla_w_out_b = gla_w_out.astype(BF16)
    fn_w_in_b = fn_w_in.astype(BF16)
    fn_w_out_b = fn_w_out.astype(BF16)
    hy_w_in_b = hy_w_in.astype(BF16)
    hy_w_out_b = hy_w_out.astype(BF16)
    emb = hy_ffn_w1.shape[1]
    emb_pad = -(-emb // 128) * 128
    hy_w1p = jnp.pad(hy_ffn_w1, ((0, 0), (0, emb_pad - emb), (0, 0)))
    ffn = hy_ffn_w2.shape[-1]
    hy_vecs = [a.reshape(a.shape[0], 1, ffn) for a in (hy_ffn_b1, hy_ffn_b2, hy_ffn_b3, hy_freq)]

    tm = 512
    tiles_per_sample = l_s // tm
    streams = {
        "p": dict(x=x_prompt.reshape(n_p * l_p, d), n=n_p, l=l_p, row=lambda i: 0),
        "s": dict(x=x_sample.reshape(n_s * l_s, d), n=n_s, l=l_s, row=lambda i: 1 + i // tiles_per_sample),
    }
    def in_weights(i, seq):
        kind, j = i % N_MIXERS, i // N_MIXERS
        if kind == 0:
            return [(gla_w_main, (None, d, n_main), (j, 0, 0)), (gla_w_lr, (None, d, 2 * GLA_RANK), (j, 0, 0))], False
        if kind == 1:
            return [(fn_w_in_b, (None,) + fn_w_in.shape[1:], (j, 0, 0))], False
        return [(hy_w_in_b, (None,) + hy_w_in.shape[1:], (j, 0, 0))], seq >= HY_SPLIT_MIN_SEQ

    for tag, st in streams.items():
        weights, split = in_weights(0, st["l"])
        st["proj"] = _proj(st["x"], mod4, st["row"], tm, f"proj_{tag}0", nxt=(norm_g3, 0, weights, split))
    new_states = None
    for i in range(depth):
        kind, j = i % N_MIXERS, i // N_MIXERS
        if kind == 2:
            filt = {}
            for tag, st in streams.items():
                ce = min(d, 256 if st["l"] > 256 else d)
                filt[tag] = _hyena_filters(st["l"], j, emb, hy_w1p, hy_vecs[0], hy_ffn_w2, hy_vecs[1], hy_ffn_w3,
                                           hy_vecs[2], hy_ffn_w4, hy_vecs[3], ce, st["l"] >= HY_SPLIT_MIN_SEQ,
                                           f"hyfilt_{tag}{i}")
        for tag, st in streams.items():
            n_seq, seq = st["n"], st["l"]
            if kind == 0:
                proj, lr = st["proj"]
                is_ctx = tag == "p"
                res = _gla_mixer(proj, lr, gla_w_dec, gla_b_dec, gla_onorm_g,
                                 None if is_ctx else state_gla, j, n_seq, seq, is_ctx, new_states,
                                 f"gla_{tag}{i}")
                ys = [res[0]]
                if is_ctx:
                    new_states = res[1]
                w_out = gla_w_out_b
            elif kind == 1:
                short = seq <= 256
                ys = [_fnet_mixer(st["proj"][0], n_seq, seq, 4 if short else 1, FN_GROUPS if short else 2,
                                  f"fnet_{tag}{i}")]
                w_out = fn_w_out_b
            else:
                split = seq >= HY_SPLIT_MIN_SEQ
                ce = min(d, 256 if seq > 256 else d)
                y = _hyena_mixer(st["proj"][0], hy_conv_w, hy_conv_b, filt[tag], hy_d, j, n_seq, seq, ce, split,
                                 f"hyena_{tag}{i}")
                ys = list(y) if split else [y]
                w_out = hy_w_out_b
            if i + 1 < depth:
                weights, split = in_weights(i + 1, seq)
                st["x"], *st["proj"] = _proj(st["x"], mod4, st["row"], tm, f"proj_{tag}{i + 1}",
                                             res=(ys, w_out, j, i), nxt=(norm_g3, i + 1, weights, split))
            else:
                (st["x"],) = _proj(st["x"], mod4, st["row"], tm, f"proj_{tag}{i + 1}",
                                   res=(ys, w_out, j, i), final_g=final_norm_g)
    y_prompt = streams["p"]["x"].reshape(n_p, l_p, d)
    y_sample = streams["s"]["x"].reshape(n_s, l_s, d)
    new_state_gla = new_states.astype(x_prompt.dtype)
    return (y_prompt, y_sample, new_state_gla)
```

```python
import functools
import math

import jax
import jax.numpy as jnp
import numpy as np
from jax import lax
from jax.experimental import pallas as pl
from jax.experimental.pallas import tpu as pltpu

F32 = jnp.float32
BF16 = jnp.bfloat16

EPS = 1e-6
N_MIXERS = 3
GLA_HEADS = 4
GLA_RANK = 16
GLA_GATE_NORM = 16.0
FN_GROUPS = 4
HY_ORDER = 2
HY_TARGET = 1e-2
HY_MIN_DECAY = math.log(HY_TARGET) / 1.5
HY_MAX_DECAY = math.log(HY_TARGET) / 0.3
HY_ROW_BLOCK = 512
HY_SPLIT_MIN_SEQ = 512

V7X_VMEM_BYTES = 64 * 1024 * 1024
VMEM_LIMIT_BYTES = V7X_VMEM_BYTES - 8 * 1024 * 1024
MOD_ROWS = 8

_NT = (((1,), (1,)), ((), ()))
_TN = (((0,), (0,)), ((), ()))


def _params(*sem):
    return pltpu.CompilerParams(dimension_semantics=sem, vmem_limit_bytes=VMEM_LIMIT_BYTES)


def _dot(a, b):
    return jnp.dot(a, b, preferred_element_type=F32)


def _split3(x):
    hi = x.astype(BF16)
    r1 = x - hi.astype(F32)
    mid = r1.astype(BF16)
    lo = (r1 - mid.astype(F32)).astype(BF16)
    return hi, mid, lo


def _dot3(a, b):
    return _dot(_dot3_lhs(a), _dot3_rhs(b))


def _dot3_lhs(a):
    ah = a.astype(BF16)
    al = (a - ah.astype(F32)).astype(BF16)
    return jnp.concatenate([ah, ah, al], axis=1)


def _dot3_rhs(b):
    bh = b.astype(BF16)
    bl = (b - bh.astype(F32)).astype(BF16)
    return jnp.concatenate([bh, bl, bh], axis=0)


def _dot_exact_lhs(a_bf16, b):
    return _dot(jnp.concatenate([a_bf16] * 3, axis=1), jnp.concatenate(_split3(b), axis=0))


def _trig_bf16(ang):
    return (jnp.asarray(np.cos(ang), F32).astype(BF16), jnp.asarray(np.sin(ang), F32).astype(BF16))


def _silu(x):
    return x / (1.0 + jnp.exp(-x))


def _log_sigmoid(x):
    return jnp.minimum(x, 0.0) - jnp.log(1.0 + jnp.exp(-jnp.abs(x)))


def _mod_body(c_ref, w_ref, b_ref, o_ref):
    o_ref[...] = _dot3(_silu(c_ref[...]), w_ref[...]) + b_ref[...]


def _modulation(cvec, mod_w, mod_b):
    depth, d, e = mod_w.shape
    tn = d
    return pl.pallas_call(
        _mod_body,
        grid=(depth, e // tn),
        in_specs=[
            pl.BlockSpec((MOD_ROWS, d), lambda l, j: (0, 0)),
            pl.BlockSpec((None, d, tn), lambda l, j: (l, 0, j)),
            pl.BlockSpec((None, 1, tn), lambda l, j: (l, 0, j)),
        ],
        out_specs=pl.BlockSpec((None, MOD_ROWS, tn), lambda l, j: (l, 0, j)),
        out_shape=jax.ShapeDtypeStruct((depth, MOD_ROWS, e), F32),
        compiler_params=_params("parallel", "parallel"),
        name="modulation",
    )(cvec, mod_w, mod_b.reshape(depth, 1, e))


def _phase_permutation(rows):
    order = np.concatenate([np.arange(0, rows, 2), np.arange(1, rows, 2)])
    return jnp.asarray(np.eye(rows, dtype=np.float32)[order], BF16)


def _proj_body(*refs, n_y, has_res, final, has_next, split, n_w):
    it = iter(refs)
    take = lambda n: [next(it) for _ in range(n)]
    if has_res:
        y_refs = take(n_y)
        unperm_ref = next(it) if n_y == 2 else None
        w_out_ref, gate_ref = take(2)
        fg_ref = next(it) if final else None
    x_ref = next(it)
    if has_next:
        g_ref, mod_ref = take(2)
        perm_ref = next(it) if split else None
        w_refs = take(n_w)
    x_out_ref = next(it) if has_res else None
    o_refs = take(n_w) if has_next else []

    x = x_ref[...]
    d = x.shape[-1]
    if has_res:
        if n_y == 2:
            y = _dot(unperm_ref[...], jnp.concatenate([r[...] for r in y_refs], axis=0)).astype(BF16)
        else:
            y = y_refs[0][...]
        x = x + gate_ref[...][:, 2 * d:] * _dot(y, w_out_ref[...])
        if final:
            x_out_ref[...] = x * lax.rsqrt(jnp.mean(x * x, axis=-1, keepdims=True) + EPS) * fg_ref[...]
        else:
            x_out_ref[...] = x
    if has_next:
        h = x * lax.rsqrt(jnp.mean(x * x, axis=-1, keepdims=True) + EPS) * g_ref[...]
        mod = mod_ref[...]
        hb = (h * (1.0 + mod[:, d:2 * d]) + mod[:, :d]).astype(BF16)
        if split:
            hb = _dot(perm_ref[...], hb).astype(BF16)
        for w_ref, o_ref in zip(w_refs, o_refs):
            res = _dot(hb, w_ref[...])
            if split:
                half, n = res.shape[0] // 2, res.shape[1]
                o_ref[:, :n] = res[:half]
                o_ref[:, n:] = res[half:]
            else:
                o_ref[...] = res


def _proj(x, mod4, mod_row, tm, name, res=None, nxt=None, final_g=None):
    t, d = x.shape
    has_res, has_next, final = res is not None, nxt is not None, final_g is not None
    mod_spec = lambda layer: pl.BlockSpec((None, None, 1, mod4.shape[-1]), lambda i: (layer, mod_row(i), 0, 0))
    in_specs, args, out_specs, out_shape = [], [], [], []
    n_y, split, n_w = 0, False, 0
    if has_res:
        ys, w_out, j, layer = res
        n_y = len(ys)
        in_specs += [pl.BlockSpec((tm // n_y, y.shape[-1]), lambda i: (i, 0)) for y in ys]
        args += list(ys)
        if n_y == 2:
            in_specs.append(pl.BlockSpec((tm, tm), lambda i: (0, 0)))
            args.append(_phase_permutation(tm).T)
        in_specs += [pl.BlockSpec((None,) + w_out.shape[1:], lambda i: (j, 0, 0), pipeline_mode=pl.Buffered(1)),
                     mod_spec(layer)]
        args += [w_out, mod4]
        if final:
            in_specs.append(pl.BlockSpec((1, d), lambda i: (0, 0)))
            args.append(final_g.reshape(1, d))
        out_specs.append(pl.BlockSpec((tm, d), lambda i: (i, 0)))
        out_shape.append(jax.ShapeDtypeStruct((t, d), F32))
    in_specs.append(pl.BlockSpec((tm, d), lambda i: (i, 0)))
    args.append(x)
    if has_next:
        norm_g3, next_layer, weights, split = nxt
        n_w = len(weights)
        in_specs += [pl.BlockSpec((None, 1, d), lambda i: (next_layer, 0, 0)), mod_spec(next_layer)]
        args += [norm_g3, mod4]
        if split:
            in_specs.append(pl.BlockSpec((tm, tm), lambda i: (0, 0)))
            args.append(_phase_permutation(tm))
        in_specs += [pl.BlockSpec(bs, functools.partial(lambda idx, i: idx, idx), pipeline_mode=pl.Buffered(1))
                     for _, bs, idx in weights]
        args += [w for w, _, _ in weights]
        p = 2 if split else 1
        widths = [bs[-1] for _, bs, _ in weights]
        out_specs += [pl.BlockSpec((tm // p, p * n), lambda i: (i, 0)) for n in widths]
        out_shape += [jax.ShapeDtypeStruct((t // p, p * n), F32) for n in widths]
    return pl.pallas_call(
        functools.partial(_proj_body, n_y=n_y, has_res=has_res, final=final, has_next=has_next,
                          split=split, n_w=n_w),
        grid=(t // tm,),
        in_specs=in_specs,
        out_specs=out_specs,
        out_shape=out_shape,
        compiler_params=_params("parallel"),
        name=name,
    )(*args)


GLA_CHUNK = 64
GLA_BLOCK = 256
GLA_HEADS_PER_STEP = 2
LOG2_E = math.log2(math.e)
GLA_CLAMP_SPLIT = 8.0 * LOG2_E
GLA_CLAMP_HALF = 80.0 * LOG2_E


def _gla_tables(rows, dk):
    c, hc = GLA_CHUNK, GLA_CHUNK // 2
    r = np.arange(rows)[:, None]
    s = np.arange(rows)[None, :]
    same_chunk = (r // c) == (s // c)
    same_half = (r // hc) == (s // hc)
    tri = np.stack([same_chunk & (s <= r), same_chunk & (s >= r)])
    m_half = np.stack([same_half & (s < r), same_half & (s > r)])
    sel = np.concatenate([s == (r // c) * c + hc, s == (r // hc) * hc + hc // 2], axis=0)
    second = np.broadcast_to((r // hc) % 2 == 1, (rows, dk))
    as_const = lambda m, dt: jnp.asarray(m.astype(np.float32), dt)
    return (as_const(tri, BF16), as_const(sel, BF16), as_const(same_chunk, F32), as_const(m_half, F32),
            as_const(second, F32), as_const(2.0 * (s == r), F32))


def _gla_body(q_ref, k_ref, v_ref, r_ref, lr_ref, wdec_ref, bdec_ref, og_ref, tri_ref, sel_ref,
              mchunk_ref, mhalf_ref, second_ref, diag2_ref, *rest, seq, layer, has_s0, want_state,
              has_states_in):
    rest = list(rest)
    s0_ref = rest.pop(0) if has_s0 else None
    if has_states_in:
        rest.pop(0)
    y_ref = rest.pop(0)
    sfin_ref = rest.pop(0) if want_state else None
    o_ref, qi_ref, cum_ref, st_ref = rest
    chunk = GLA_CHUNK
    n_chunks = seq // chunk
    blk = tri_ref.shape[-1]
    n_heads = wdec_ref.shape[0]
    dk = q_ref.shape[-1] // n_heads
    dv = v_ref.shape[-1] // n_heads
    scale = dk ** -0.5
    heads = range(n_heads)
    dirs = range(2)
    chains = [(hh, d) for hh in heads for d in dirs]

    def nt(a, b):
        return lax.dot_general(a, b, _NT, preferred_element_type=F32)

    def cols(hh, width):
        return slice(hh * width, (hh + 1) * width)

    def block_pass(bi, carry):
        rows = pl.ds(pl.multiple_of(bi * blk, blk), blk)
        q = [q_ref[rows, cols(hh, dk)] * scale for hh in heads]
        k = [k_ref[rows, cols(hh, dk)] for hh in heads]
        lr = lr_ref[rows, :]
        g = [_log_sigmoid(_dot3(lr, wdec_ref[hh]) + bdec_ref[hh]) * (LOG2_E / GLA_GATE_NORM) for hh in heads]
        cum = {(hh, d): _dot_exact_lhs(tri_ref[d], g[hh][:, cols(d, dk)]) for hh, d in chains}
        refs = [_dot(sel_ref[...], jnp.concatenate([cum[hh, 0], cum[hh, 1]], axis=1).astype(BF16))
                for hh in heads]
        second = second_ref[...]
        first = 1.0 - second
        p_split, p_half = [], {}
        for hh in heads:
            q_in = (q[hh] * second, q[hh] * first)
            k_in = (k[hh] * first, k[hh] * second)
            q1, k1 = [], []
            for d in dirs:
                c = (hh, d)
                d_split = cum[c] - refs[hh][:blk, cols(d, dk)]
                d_half = cum[c] - refs[hh][blk:, cols(d, dk)]
                q1.append((q_in[d] * jnp.exp2(jnp.minimum(d_split, GLA_CLAMP_SPLIT))).astype(BF16))
                k1.append((k_in[d] * jnp.exp2(jnp.minimum(-d_split, GLA_CLAMP_SPLIT))).astype(BF16))
                q0 = (q[hh] * jnp.exp2(jnp.minimum(d_half, GLA_CLAMP_HALF))).astype(BF16)
                k0 = (k[hh] * jnp.exp2(jnp.minimum(-d_half, GLA_CLAMP_HALF))).astype(BF16)
                p_half[c] = nt(q0, k0)
                qi_ref[hh, d, rows, :] = (q[hh] * jnp.exp2(cum[c])).astype(BF16)
                cum_ref[hh, d, rows, :] = cum[c]
            p_split.append(nt(jnp.concatenate(q1, axis=1), jnp.concatenate(k1, axis=1)))
        for hh in heads:
            att = p_split[hh] * mchunk_ref[...] + diag2_ref[...] * jnp.sum(q[hh] * k[hh], axis=-1, keepdims=True)
            for d in dirs:
                att = att + jnp.where(mhalf_ref[d] > 0.5, p_half[hh, d], 0.0)
            o_ref[rows, cols(hh, dv)] = _dot(att.astype(BF16), v_ref[rows, cols(hh, dv)].astype(BF16))
        return carry

    lax.fori_loop(0, seq // blk, block_pass, 0)

    for hh, d in chains:
        st_ref[hh, d] = s0_ref[d, hh] if has_s0 else jnp.zeros((dk, dv), F32)

    def chunk_step(i, carry):
        for hh, d in chains:
            c = i if d == 0 else n_chunks - 1 - i
            sl = pl.ds(pl.multiple_of(c * chunk, chunk), chunk)
            cum = cum_ref[hh, d, sl, :]
            blast = cum[chunk - 1:chunk] if d == 0 else cum[0:1]
            kh = (k_ref[sl, cols(hh, dk)] * jnp.exp2(blast - cum)).astype(BF16)
            vb = v_ref[sl, cols(hh, dv)].astype(BF16)
            st = st_ref[hh, d]
            o_ref[sl, cols(hh, dv)] += _dot(qi_ref[hh, d, sl, :], st.astype(BF16))
            decay = jnp.broadcast_to(jnp.exp2(blast), (dk, dk)).T
            st_ref[hh, d] = (jnp.concatenate([decay] * (dv // dk), axis=1) * st
                             + lax.dot_general(kh, vb, _TN, preferred_element_type=F32))
        return carry

    lax.fori_loop(0, n_chunks, chunk_step, 0, unroll=min(n_chunks, 8))

    def finish(jb, carry):
        sl = pl.ds(pl.multiple_of(jb * blk, blk), blk)
        for hh in heads:
            o = o_ref[sl, cols(hh, dv)]
            o = o * lax.rsqrt(jnp.mean(o * o, axis=-1, keepdims=True) + EPS) * og_ref[...]
            y_ref[sl, cols(hh, dv)] = (o * _silu(r_ref[sl, cols(hh, dv)])).astype(BF16)
        return carry

    lax.fori_loop(0, seq // blk, finish, 0)
    if want_state:
        own = sfin_ref if has_states_in else sfin_ref.at[layer]
        for hh, d in chains:
            own[d, hh] = st_ref[hh, d]
        if not has_states_in:
            for other in range(sfin_ref.shape[0]):
                if other != layer:
                    sfin_ref[other] = jnp.zeros(sfin_ref.shape[1:], F32)


def _gla_mixer(proj, lr, w_dec, b_dec, onorm_g, s0, j, n_seq, seq, want_state, states, name):
    t = proj.shape[0]
    h = GLA_HEADS
    key = w_dec.shape[-1]
    dk = key // h
    val = (proj.shape[1] - 2 * key) // 2
    dv = val // h
    has_s0 = s0 is not None
    tables = _gla_tables(min(seq, GLA_BLOCK), dk)
    fixed = lambda a: pl.BlockSpec(a.shape, lambda b, hp: (0,) * a.ndim)
    hps = h if seq <= GLA_BLOCK else GLA_HEADS_PER_STEP
    wk, wv = hps * dk, hps * dv
    in_specs = [
        pl.BlockSpec((seq, wk), lambda b, hp: (b, hp)),
        pl.BlockSpec((seq, wk), lambda b, hp: (b, key // wk + hp)),
        pl.BlockSpec((seq, wv), lambda b, hp: (b, (2 * key) // wv + hp)),
        pl.BlockSpec((seq, wv), lambda b, hp: (b, (2 * key + val) // wv + hp)),
        pl.BlockSpec((seq, 2 * GLA_RANK), lambda b, hp: (b, 0)),
        pl.BlockSpec((None, hps, 2 * GLA_RANK, 2 * dk), lambda b, hp: (j, hp, 0, 0)),
        pl.BlockSpec((None, hps, 1, 2 * dk), lambda b, hp: (j, hp, 0, 0)),
        pl.BlockSpec((None, 1, dv), lambda b, hp: (j, 0, 0)),
    ] + [fixed(a) for a in tables]
    n_gla = w_dec.shape[0]
    w5 = w_dec.reshape(n_gla, 2, GLA_RANK, h, dk)
    zero = jnp.zeros_like(w5[:, 0])
    w_blk = jnp.concatenate([jnp.concatenate([w5[:, 0], zero], axis=-1),
                             jnp.concatenate([zero, w5[:, 1]], axis=-1)], axis=1).transpose(0, 2, 1, 3)
    b_cat = b_dec.reshape(n_gla, 2, h, dk).transpose(0, 2, 1, 3).reshape(n_gla, h, 1, 2 * dk)
    args = [proj, proj, proj, proj, lr, w_blk, b_cat, onorm_g.reshape(n_gla, 1, dv), *tables]
    if has_s0:
        in_specs.append(pl.BlockSpec((None, None, 2, hps, dk, dv), lambda b, hp: (b, j, 0, hp, 0, 0)))
        args.append(s0)
    out_specs = [pl.BlockSpec((seq, wv), lambda b, hp: (b, hp))]
    out_shape = [jax.ShapeDtypeStruct((t, val), BF16)]
    aliases = {}
    if want_state:
        out_shape.append(jax.ShapeDtypeStruct((n_seq, n_gla, 2, h, dk, dv), F32))
        if states is None:
            out_specs.append(pl.BlockSpec((None, n_gla, 2, hps, dk, dv), lambda b, hp: (b, 0, 0, hp, 0, 0)))
        else:
            out_specs.append(pl.BlockSpec((None, None, 2, hps, dk, dv), lambda b, hp: (b, j, 0, hp, 0, 0)))
            in_specs.append(pl.BlockSpec(memory_space=pl.ANY))
            args.append(states)
            aliases = {len(args) - 1: 1}
    return pl.pallas_call(
        functools.partial(_gla_body, seq=seq, layer=j, has_s0=has_s0, want_state=want_state,
                          has_states_in=bool(aliases)),
        grid=(n_seq, h // hps),
        in_specs=in_specs,
        out_specs=out_specs,
        out_shape=out_shape,
        input_output_aliases=aliases,
        scratch_shapes=[pltpu.VMEM((seq, wv), F32), pltpu.VMEM((hps, 2, seq, dk), BF16),
                        pltpu.VMEM((hps, 2, seq, dk), F32), pltpu.VMEM((hps, 2, dk, dv), F32)],
        compiler_params=_params("parallel", "parallel"),
        name=name,
    )(*args)


def _dft_tables(n):
    f = np.arange(n, dtype=np.int64)
    ang = (np.outer(f, f) % n).astype(np.float64) * (2.0 * math.pi / n)
    return _trig_bf16(ang)


def _fnet_body(x_ref, z_ref, cc_ref, sc_ref, cl_ref, sl_ref, y_ref, *, seq, n_seq, gc):
    inv = 1.0 / math.sqrt(seq * gc)
    for g in range(x_ref.shape[-1] // gc):
        cols = slice(g * gc, (g + 1) * gc)
        xb = x_ref[:, cols].astype(BF16)
        a = _dot(xb, cc_ref[...]).astype(BF16)
        bm = _dot(xb, sc_ref[...]).astype(BF16)
        for i in range(n_seq):
            rows = slice(i * seq, (i + 1) * seq)
            f = _dot(cl_ref[...], a[rows]) - _dot(sl_ref[...], bm[rows])
            y_ref[rows, cols] = (f * inv * _silu(z_ref[rows, cols])).astype(BF16)


def _fnet_mixer(proj, n_seq_total, seq, seqs_per_step, groups_per_step, name):
    t = proj.shape[0]
    width = proj.shape[1] // 2
    gc = width // FN_GROUPS
    cc, sc = _dft_tables(gc)
    cl, sl = _dft_tables(seq)
    rows = seq * seqs_per_step
    wide = gc * groups_per_step
    n_wide = width // wide
    const = pl.Buffered(1)
    return pl.pallas_call(
        functools.partial(_fnet_body, seq=seq, n_seq=seqs_per_step, gc=gc),
        grid=(n_seq_total // seqs_per_step, n_wide),
        in_specs=[
            pl.BlockSpec((rows, wide), lambda i, g: (i, g)),
            pl.BlockSpec((rows, wide), lambda i, g: (i, n_wide + g)),
            pl.BlockSpec((gc, gc), lambda i, g: (0, 0), pipeline_mode=const),
            pl.BlockSpec((gc, gc), lambda i, g: (0, 0), pipeline_mode=const),
            pl.BlockSpec((seq, seq), lambda i, g: (0, 0), pipeline_mode=const),
            pl.BlockSpec((seq, seq), lambda i, g: (0, 0), pipeline_mode=const),
        ],
        out_specs=pl.BlockSpec((rows, wide), lambda i, g: (i, g)),
        out_shape=jax.ShapeDtypeStruct((t, width), BF16),
        compiler_params=_params("parallel", "parallel"),
        name=name,
    )(proj, proj, cc, sc, cl, sl)


def _signal_tables(n):
    o = 2 * np.arange(n, dtype=np.int64) + 1
    ang = (np.outer(o, o) % (8 * n)).astype(np.float64) * (math.pi / (4 * n))
    cos, sin = _trig_bf16(ang)
    return cos, -sin


def _filter_tables(n, split):
    f = np.arange(n, dtype=np.int64)
    if split:
        f = np.concatenate([f[:n // 2], f[:n // 2 - 1:-1]])
    f = 2 * f + 1
    o = np.arange(n, dtype=np.int64)
    ang = (np.outer(f, o) % (4 * n)).astype(np.float64) * (math.pi / (2 * n))
    return _trig_bf16(ang)


def _hyena_positions(seq, bands, pad_to):
    t = np.linspace(0.0, 1.0, seq)[:, None]
    w = 2.0 * math.pi * np.arange(seq)[:, None] / seq
    f = np.linspace(1e-4, bands - 1, bands)[None, :]
    zpos = np.concatenate([t, np.cos(f * w), -np.sin(f * w)], axis=-1)
    zpos = np.pad(zpos, ((0, 0), (0, pad_to - zpos.shape[1])))
    return jnp.asarray(zpos, F32), jnp.asarray(t, F32)


def _hyfilt_body(zpos_ref, w1_ref, b1_ref, w2_ref, b2_ref, w3_ref, b3_ref, fr_ref, w4f_ref,
                 w4b_ref, t_ref, del_ref, ch_ref, sh_ref, *rest, split):
    if split:
        c2_ref, s2_ref = rest[:2]
        out_refs, a3_ref = rest[2:-1], rest[-1]
    else:
        out_refs, a3_ref = rest[:-1], rest[-1]

    @pl.when(pl.program_id(0) == 0)
    def _():
        fr = fr_ref[...]
        a = jnp.sin(fr * (_dot3(zpos_ref[...], w1_ref[...]) + b1_ref[...]))
        a = jnp.sin(fr * (_dot3(a, w2_ref[...]) + b2_ref[...]))
        a3_ref[...] = _dot3_lhs(jnp.sin(fr * (_dot3(a, w3_ref[...]) + b3_ref[...])))

    a3 = a3_ref[...]
    dec = jnp.exp(-t_ref[...] * del_ref[...]) * (1.0 / t_ref.shape[0])
    fwd = _dot(a3, _dot3_rhs(w4f_ref[...])) * dec
    bwd = _dot(a3, _dot3_rhs(w4b_ref[...])) * dec
    first = lax.broadcasted_iota(jnp.int32, bwd.shape, 0) == 0
    bwd = jnp.where(first, 0.0, bwd)
    hre = _dot(ch_ref[...], (fwd + bwd).astype(BF16))
    him = _dot(sh_ref[...], (bwd - fwd).astype(BF16))
    if not split:
        out_refs[0][...] = hre
        out_refs[1][...] = him
        return
    half = hre.shape[0] // 2
    plus_r, plus_i = hre[:half] + hre[half:], him[:half] - him[half:]
    minus_r, minus_i = hre[:half] - hre[half:], him[:half] + him[half:]
    c2, s2 = c2_ref[...], s2_ref[...]
    for ref, val in zip(out_refs, (plus_r, plus_i,
                                   c2 * minus_r + s2 * minus_i, c2 * minus_i - s2 * minus_r,
                                   c2 * minus_r - s2 * minus_i, c2 * minus_i + s2 * minus_r)):
        ref[...] = val


def _hyena_filters(seq, j, emb, w1p, b1, w2, b2, w3, b3, w4, freq, ce, split, name):
    ffn = w2.shape[-1]
    width = w4.shape[-1] // (2 * HY_ORDER)
    emb_pad = w1p.shape[1]
    zpos, tcol = _hyena_positions(seq, (emb - 1) // 2, emb_pad)
    deltas = jnp.asarray(np.abs(np.linspace(HY_MIN_DECAY, HY_MAX_DECAY, width, dtype=np.float32))[None, :])
    ch, sh = _filter_tables(seq, split)
    nb = width // ce
    const = pl.Buffered(1)
    vec = lambda: pl.BlockSpec((None, 1, ffn), lambda c: (j, 0, 0))
    out_rows, n_out = (seq // 2, 6) if split else (seq, 2)
    twiddles, twiddle_specs = [], []
    if split:
        ang = np.broadcast_to((math.pi * (2 * np.arange(out_rows) + 1) / (2 * seq))[:, None], (out_rows, ce))
        twiddles = [jnp.asarray(np.cos(ang), F32), jnp.asarray(np.sin(ang), F32)]
        twiddle_specs = [pl.BlockSpec((out_rows, ce), lambda c: (0, 0), pipeline_mode=const)] * 2
    return pl.pallas_call(
        functools.partial(_hyfilt_body, split=split),
        grid=(HY_ORDER * nb,),
        in_specs=[
            pl.BlockSpec((seq, emb_pad), lambda c: (0, 0)),
            pl.BlockSpec((None, emb_pad, ffn), lambda c: (j, 0, 0)),
            vec(),
            pl.BlockSpec((None, ffn, ffn), lambda c: (j, 0, 0)),
            vec(),
            pl.BlockSpec((None, ffn, ffn), lambda c: (j, 0, 0)),
            vec(),
            vec(),
            pl.BlockSpec((None, ffn, ce), lambda c: (j, 0, (c // nb) * 2 * nb + c % nb)),
            pl.BlockSpec((None, ffn, ce), lambda c: (j, 0, (c // nb) * 2 * nb + nb + c % nb)),
            pl.BlockSpec((seq, 1), lambda c: (0, 0)),
            pl.BlockSpec((1, ce), lambda c: (0, c % nb)),
            pl.BlockSpec((seq, seq), lambda c: (0, 0), pipeline_mode=const),
            pl.BlockSpec((seq, seq), lambda c: (0, 0), pipeline_mode=const),
        ] + twiddle_specs,
        out_specs=[pl.BlockSpec((out_rows, ce), lambda c: (0, c))] * n_out,
        out_shape=[jax.ShapeDtypeStruct((out_rows, HY_ORDER * width), F32)] * n_out,
        scratch_shapes=[pltpu.VMEM((seq, 3 * ffn), BF16)],
        compiler_params=_params("arbitrary"),
        name=name,
    )(zpos, w1p, b1, w2, b2, w3, b3, freq, w4, w4, tcol, deltas, ch, sh, *twiddles)


SUBLANES = 8


def _shift_down(u, first):
    rolled = pltpu.roll(u, 1, 0)
    row = lax.broadcasted_iota(jnp.int32, (SUBLANES, 1), 0)
    return jnp.concatenate([jnp.where(row == 0, first, rolled[:SUBLANES]), rolled[SUBLANES:]], axis=0)


def _shift_up(u, last):
    n = u.shape[0]
    rolled = pltpu.roll(u, n - 1, 0)
    row = lax.broadcasted_iota(jnp.int32, (SUBLANES, 1), 0)
    tail = jnp.where(row == SUBLANES - 1, last, rolled[n - SUBLANES:])
    return jnp.concatenate([rolled[:n - SUBLANES], tail], axis=0)


def _hyena_body(x1_ref, x2_ref, v_ref, z_ref, cw1_ref, cw2_ref, cwv_ref, cb1_ref, cb2_ref, cbv_ref,
                hre0_ref, him0_ref, hre1_ref, him1_ref, d_ref, cq_ref, sq_ref, y_ref,
                cur_ref, cb_ref, yr_ref, yi_ref, *, seq, rows):
    n_rb = seq // rows

    def block(rb):
        return pl.ds(pl.multiple_of(rb * rows, rows), rows)

    def short_conv(u_ref, w_ref, b_ref, rb):
        u = u_ref[block(rb), :]
        before = u_ref[pl.ds(jnp.maximum(rb * rows - 1, 0), 1), :]
        after = u_ref[pl.ds(jnp.minimum((rb + 1) * rows, seq - 1), 1), :]
        before = jnp.where(rb == 0, 0.0, before)
        after = jnp.where(rb == n_rb - 1, 0.0, after)
        prev, nxt = _shift_down(u, before), _shift_up(u, after)
        return prev * w_ref[0:1] + u * w_ref[1:2] + nxt * w_ref[2:3] + b_ref[...]

    def long_conv_gate(x_ref, cw_ref, cbias_ref, hre_ref, him_ref, dn):
        def fwd(rb, carry):
            sl = block(rb)
            ur = _dot(cq_ref[sl, :], cb_ref[...])
            ui = _dot(sq_ref[sl, :], cb_ref[...])
            hr = hre_ref[sl, :]
            hi = him_ref[sl, :]
            yr_ref[sl, :] = (ur * hr - ui * hi).astype(BF16)
            yi_ref[sl, :] = (ur * hi + ui * hr).astype(BF16)
            return carry

        lax.fori_loop(0, n_rb, fwd, 0)

        def inv(rb, carry):
            sl = block(rb)
            c = _dot(cq_ref[sl, :], yr_ref[...]) + _dot(sq_ref[sl, :], yi_ref[...])
            new = short_conv(x_ref, cw_ref, cbias_ref, rb) * (c + cur_ref[sl, :] * dn)
            cur_ref[sl, :] = new
            cb_ref[sl, :] = new.astype(BF16)
            return carry

        lax.fori_loop(0, n_rb, inv, 0)

    def load_v(rb, carry):
        v = short_conv(v_ref, cwv_ref, cbv_ref, rb)
        cur_ref[block(rb), :] = v
        cb_ref[block(rb), :] = v.astype(BF16)
        return carry

    lax.fori_loop(0, n_rb, load_v, 0)
    long_conv_gate(x1_ref, cw1_ref, cb1_ref, hre0_ref, him0_ref, d_ref[0:1])
    long_conv_gate(x2_ref, cw2_ref, cb2_ref, hre1_ref, him1_ref, d_ref[1:2])

    def store(rb, carry):
        sl = block(rb)
        y_ref[sl, :] = (cur_ref[sl, :] * _silu(z_ref[sl, :])).astype(BF16)
        return carry

    lax.fori_loop(0, n_rb, store, 0)


def _hyena_split_body(x1_refs, x2_refs, v_refs, z_refs, cw1_ref, cw2_ref, cwv_ref, cb1_ref, cb2_ref, cbv_ref,
                      k_refs, d_ref, cq_ref, sq_ref, y_refs, cur_ref, sig_ref, z_buf, *, seq, rows):
    half = seq // 2
    n_rb = half // rows

    def block(rb, shift=0):
        return slice(rb * rows + shift, (rb + 1) * rows + shift)

    def short_conv(u_refs, w_ref, b_ref, rb):
        e_ref, o_ref = u_refs
        even, odd = e_ref[block(rb), :], o_ref[block(rb), :]
        odd_before = o_ref[block(rb, -1), :] if rb > 0 else _shift_down(odd, 0.0)
        even_after = e_ref[block(rb, 1), :] if rb < n_rb - 1 else _shift_up(even, 0.0)
        w0, w1, w2, b = w_ref[0:1], w_ref[1:2], w_ref[2:3], b_ref[...]
        return (odd_before * w0 + even * w1 + odd * w2 + b, even * w0 + odd * w1 + even_after * w2 + b)

    def long_conv_gate(x_ref, cw_ref, cbias_ref, k, dn):
        k1r, k1i, k2r, k2i, k3r, k3i = k
        for fb in range(n_rb):
            sl = block(fb)
            c, s = cq_ref[sl, :], sq_ref[sl, :]
            er, ei = _dot(c, sig_ref[0]), _dot(s, sig_ref[0])
            orr, oi = _dot(c, sig_ref[1]), _dot(s, sig_ref[1])
            a1r, a1i, a2r, a2i, a3r, a3i = (r[sl, :] for r in (k1r, k1i, k2r, k2i, k3r, k3i))
            z_buf[0, sl, :] = (er * a1r - ei * a1i + orr * a2r - oi * a2i).astype(BF16)
            z_buf[1, sl, :] = (er * a1i + ei * a1r + orr * a2i + oi * a2r).astype(BF16)
            z_buf[2, sl, :] = (er * a3r - ei * a3i + orr * a1r - oi * a1i).astype(BF16)
            z_buf[3, sl, :] = (er * a3i + ei * a3r + orr * a1i + oi * a1r).astype(BF16)
        for tb in range(n_rb):
            sl = block(tb)
            c, s = cq_ref[sl, :], sq_ref[sl, :]
            conv = (_dot(c, z_buf[0]) + _dot(s, z_buf[1]), _dot(c, z_buf[2]) + _dot(s, z_buf[3]))
            gate = short_conv(x_ref, cw_ref, cbias_ref, tb)
            for p in range(2):
                new = gate[p] * (conv[p] + cur_ref[p, sl, :] * dn)
                cur_ref[p, sl, :] = new
                sig_ref[p, sl, :] = new.astype(BF16)

    for rb in range(n_rb):
        for p, val in enumerate(short_conv(v_refs, cwv_ref, cbv_ref, rb)):
            cur_ref[p, block(rb), :] = val
            sig_ref[p, block(rb), :] = val.astype(BF16)
    long_conv_gate(x1_refs, cw1_ref, cb1_ref, k_refs[0], d_ref[0:1])
    long_conv_gate(x2_refs, cw2_ref, cb2_ref, k_refs[1], d_ref[1:2])
    for rb in range(n_rb):
        for p in range(2):
            sl = block(rb)
            y_refs[p][sl, :] = (cur_ref[p, sl, :] * _silu(z_refs[p][sl, :])).astype(BF16)


def _hyena_split_entry(*refs, seq, rows):
    it = iter(refs)
    take = lambda n: tuple(next(it) for _ in range(n))
    x1, x2, v, z = take(2), take(2), take(2), take(2)
    conv_refs = take(6)
    k = tuple(take(6) for _ in range(HY_ORDER))
    d, cq, sq = take(3)
    y = take(2)
    _hyena_split_body(x1, x2, v, z, *conv_refs, k, d, cq, sq, y, *it, seq=seq, rows=rows)


def _hyena_mixer(proj, conv_w, conv_b, filt, d_skip, j, n_seq, seq, ce, split, name):
    phases = 2 if split else 1
    t = proj.shape[0] * phases
    width = proj.shape[1] // (4 * phases)
    nb = width // ce
    n_dft = seq // phases
    cq, sq = _signal_tables(n_dft)
    rows = min(n_dft, HY_ROW_BLOCK)
    const = pl.Buffered(1)
    seg = lambda s: pl.BlockSpec((seq, ce), lambda e, b: (b, s * nb + e))
    cw = lambda s: pl.BlockSpec((None, conv_w.shape[1], ce), lambda e, b: (j, 0, s * nb + e))
    cb = lambda s: pl.BlockSpec((None, 1, ce), lambda e, b: (j, 0, s * nb + e))
    hs = lambda n: pl.BlockSpec((n_dft, ce), lambda e, b: (0, n * nb + e),
                                pipeline_mode=const if nb == 1 else pl.Buffered(2))
    conv_b3 = conv_b.reshape(conv_b.shape[0], 1, -1)
    if split:
        body = functools.partial(_hyena_split_entry, seq=seq, rows=rows)
        phase = lambda s, p: pl.BlockSpec((n_dft, ce), lambda e, b: (b, p * 4 * nb + s * nb + e))
        x_specs = [phase(s, p) for s in range(4) for p in range(2)]
        out_specs = [pl.BlockSpec((n_dft, ce), lambda e, b: (b, e))] * 2
        out_shape = [jax.ShapeDtypeStruct((t // 2, width), BF16)] * 2
        scratch = [pltpu.VMEM((2, n_dft, ce), F32), pltpu.VMEM((2, n_dft, ce), BF16),
                   pltpu.VMEM((4, n_dft, ce), BF16)]
    else:
        body = functools.partial(_hyena_body, seq=seq, rows=rows)
        x_specs = [seg(0), seg(1), seg(2), seg(3)]
        out_specs = pl.BlockSpec((seq, ce), lambda e, b: (b, e))
        out_shape = jax.ShapeDtypeStruct((t, width), BF16)
        scratch = [pltpu.VMEM((seq, ce), F32), pltpu.VMEM((seq, ce), BF16),
                   pltpu.VMEM((seq, ce), BF16), pltpu.VMEM((seq, ce), BF16)]
    return pl.pallas_call(
        body,
        grid=(nb, n_seq),
        in_specs=x_specs + [cw(0), cw(1), cw(2), cb(0), cb(1), cb(2)]
        + [hs(n) for n in range(HY_ORDER) for _ in filt]
        + [pl.BlockSpec((None, HY_ORDER, ce), lambda e, b: (j, 0, e)),
           pl.BlockSpec((n_dft, n_dft), lambda e, b: (0, 0), pipeline_mode=const),
           pl.BlockSpec((n_dft, n_dft), lambda e, b: (0, 0), pipeline_mode=const)],
        out_specs=out_specs,
        out_shape=out_shape,
        scratch_shapes=scratch,
        compiler_params=_params("parallel", "parallel"),
        name=name,
    )(*([proj] * len(x_specs)), conv_w, conv_w, conv_w, conv_b3, conv_b3, conv_b3,
      *(list(filt) * HY_ORDER), d_skip, cq, sq)


def kernel(x_prompt, x_sample, state_gla, c, c_ctx, mod_w, mod_b, norm_g, final_norm_g, gla_w_in, gla_w_dec, gla_b_dec, gla_onorm_g, gla_w_out, fn_w_in, fn_w_out, hy_w_in, hy_conv_w, hy_conv_b, hy_ffn_w1, hy_ffn_b1, hy_ffn_w2, hy_ffn_b2, hy_ffn_w3, hy_ffn_b3, hy_ffn_w4, hy_freq, hy_d, hy_w_out):
    n_p, l_p, d = x_prompt.shape
    n_s, l_s, _ = x_sample.shape
    depth = mod_w.shape[0]
    key = gla_w_dec.shape[-1]
    n_main = gla_w_in.shape[-1] - 2 * GLA_RANK

    cvec = jnp.concatenate([c_ctx[None], c, jnp.zeros((MOD_ROWS - 1 - n_s, d), F32)], axis=0)
    mod4 = _modulation(cvec, mod_w, mod_b).reshape(depth, MOD_ROWS, 1, 3 * d)
    norm_g3 = norm_g.reshape(depth, 1, d)

    gla_w_main = gla_w_in.astype(BF16)
    gla_w_lr = gla_w_in[:, :, n_main:].astype(BF16)
    gla_w_out_b = gla_w_out.astype(BF16)
    fn_w_in_b = fn_w_in.astype(BF16)
    fn_w_out_b = fn_w_out.astype(BF16)
    hy_w_in_b = hy_w_in.astype(BF16)
    hy_w_out_b = hy_w_out.astype(BF16)
    emb = hy_ffn_w1.shape[1]
    emb_pad = -(-emb // 128) * 128
    hy_w1p = jnp.pad(hy_ffn_w1, ((0, 0), (0, emb_pad - emb), (0, 0)))
    ffn = hy_ffn_w2.shape[-1]
    hy_vecs = [a.reshape(a.shape[0], 1, ffn) for a in (hy_ffn_b1, hy_ffn_b2, hy_ffn_b3, hy_freq)]

    tm = 512
    tiles_per_sample = l_s // tm
    streams = {
        "p": dict(x=x_prompt.reshape(n_p * l_p, d), n=n_p, l=l_p, row=lambda i: 0),
        "s": dict(x=x_sample.reshape(n_s * l_s, d), n=n_s, l=l_s, row=lambda i: 1 + i // tiles_per_sample),
    }
    def in_weights(i, seq):
        kind, j = i % N_MIXERS, i // N_MIXERS
        if kind == 0:
            return [(gla_w_main, (None, d, n_main), (j, 0, 0)), (gla_w_lr, (None, d, 2 * GLA_RANK), (j, 0, 0))], False
        if kind == 1:
            return [(fn_w_in_b, (None,) + fn_w_in.shape[1:], (j, 0, 0))], False
        return [(hy_w_in_b, (None,) + hy_w_in.shape[1:], (j, 0, 0))], seq >= HY_SPLIT_MIN_SEQ

    for tag, st in streams.items():
        weights, split = in_weights(0, st["l"])
        st["proj"] = _proj(st["x"], mod4, st["row"], tm, f"proj_{tag}0", nxt=(norm_g3, 0, weights, split))
    new_states = None
    for i in range(depth):
        kind, j = i % N_MIXERS, i // N_MIXERS
        if kind == 2:
            filt = {}
            for tag, st in streams.items():
                ce = min(d, 256 if st["l"] > 256 else d)
                filt[tag] = _hyena_filters(st["l"], j, emb, hy_w1p, hy_vecs[0], hy_ffn_w2, hy_vecs[1], hy_ffn_w3,
                                           hy_vecs[2], hy_ffn_w4, hy_vecs[3], ce, st["l"] >= HY_SPLIT_MIN_SEQ,
                                           f"hyfilt_{tag}{i}")
        for tag, st in streams.items():
            n_seq, seq = st["n"], st["l"]
            if kind == 0:
                proj, lr = st["proj"]
                is_ctx = tag == "p"
                res = _gla_mixer(proj, lr, gla_w_dec, gla_b_dec, gla_onorm_g,
                                 None if is_ctx else state_gla, j, n_seq, seq, is_ctx, new_states,
                                 f"gla_{tag}{i}")
                ys = [res[0]]
                if is_ctx:
                    new_states = res[1]
                w_out = gla_w_out_b
            elif kind == 1:
                short = seq <= 256
                ys = [_fnet_mixer(st["proj"][0], n_seq, seq, 4 if short else 1, FN_GROUPS if short else 2,
                                  f"fnet_{tag}{i}")]
                w_out = fn_w_out_b
            else:
                split = seq >= HY_SPLIT_MIN_SEQ
                ce = min(d, 256 if seq > 256 else d)
                y = _hyena_mixer(st["proj"][0], hy_conv_w, hy_conv_b, filt[tag], hy_d, j, n_seq, seq, ce, split,
                                 f"hyena_{tag}{i}")
                ys = list(y) if split else [y]
                w_out = hy_w_out_b
            if i + 1 < depth:
                weights, split = in_weights(i + 1, seq)
                st["x"], *st["proj"] = _proj(st["x"], mod4, st["row"], tm, f"proj_{tag}{i + 1}",
                                             res=(ys, w_out, j, i), nxt=(norm_g3, i + 1, weights, split))
            else:
                (st["x"],) = _proj(st["x"], mod4, st["row"], tm, f"proj_{tag}{i + 1}",
                                   res=(ys, w_out, j, i), final_g=final_norm_g)
    y_prompt = streams["p"]["x"].reshape(n_p, l_p, d)
    y_sample = streams["s"]["x"].reshape(n_s, l_s, d)
    new_state_gla = new_states.astype(x_prompt.dtype)
    return (y_prompt, y_sample, new_state_gla)
```

```python
import functools
import math

import jax
import jax.numpy as jnp
import numpy as np
from jax import lax
from jax.experimental import pallas as pl
from jax.experimental.pallas import tpu as pltpu

F32 = jnp.float32
BF16 = jnp.bfloat16

EPS = 1e-6
N_MIXERS = 3
GLA_HEADS = 4
GLA_RANK = 16
GLA_GATE_NORM = 16.0
FN_GROUPS = 4
HY_ORDER = 2
HY_TARGET = 1e-2
HY_MIN_DECAY = math.log(HY_TARGET) / 1.5
HY_MAX_DECAY = math.log(HY_TARGET) / 0.3
HY_ROW_BLOCK = 512
HY_SPLIT_MIN_SEQ = 512

V7X_VMEM_BYTES = 64 * 1024 * 1024
VMEM_LIMIT_BYTES = V7X_VMEM_BYTES - 8 * 1024 * 1024
MOD_ROWS = 8

_NT = (((1,), (1,)), ((), ()))
_TN = (((0,), (0,)), ((), ()))


def _params(*sem):
    return pltpu.CompilerParams(dimension_semantics=sem, vmem_limit_bytes=VMEM_LIMIT_BYTES)


def _dot(a, b):
    return jnp.dot(a, b, preferred_element_type=F32)


def _split3(x):
    hi = x.astype(BF16)
    r1 = x - hi.astype(F32)
    mid = r1.astype(BF16)
    lo = (r1 - mid.astype(F32)).astype(BF16)
    return hi, mid, lo


def _dot3(a, b):
    return _dot(_dot3_lhs(a), _dot3_rhs(b))


def _dot3_lhs(a):
    ah = a.astype(BF16)
    al = (a - ah.astype(F32)).astype(BF16)
    return jnp.concatenate([ah, ah, al], axis=1)


def _dot3_rhs(b):
    bh = b.astype(BF16)
    bl = (b - bh.astype(F32)).astype(BF16)
    return jnp.concatenate([bh, bl, bh], axis=0)


def _dot_exact_lhs(a_bf16, b):
    return _dot(jnp.concatenate([a_bf16] * 3, axis=1), jnp.concatenate(_split3(b), axis=0))


def _trig_bf16(ang):
    return (jnp.asarray(np.cos(ang), F32).astype(BF16), jnp.asarray(np.sin(ang), F32).astype(BF16))


def _silu(x):
    return x / (1.0 + jnp.exp(-x))


def _log_sigmoid(x):
    return jnp.minimum(x, 0.0) - jnp.log(1.0 + jnp.exp(-jnp.abs(x)))


def _mod_body(c_ref, w_ref, b_ref, o_ref):
    o_ref[...] = _dot3(_silu(c_ref[...]), w_ref[...]) + b_ref[...]


def _modulation(cvec, mod_w, mod_b):
    depth, d, e = mod_w.shape
    tn = e
    return pl.pallas_call(
        _mod_body,
        grid=(depth, e // tn),
        in_specs=[
            pl.BlockSpec((MOD_ROWS, d), lambda l, j: (0, 0)),
            pl.BlockSpec((None, d, tn), lambda l, j: (l, 0, j)),
            pl.BlockSpec((None, 1, tn), lambda l, j: (l, 0, j)),
        ],
        out_specs=pl.BlockSpec((None, MOD_ROWS, tn), lambda l, j: (l, 0, j)),
        out_shape=jax.ShapeDtypeStruct((depth, MOD_ROWS, e), F32),
        compiler_params=_params("parallel", "parallel"),
        name="modulation",
    )(cvec, mod_w, mod_b.reshape(depth, 1, e))


def _phase_permutation(rows):
    order = np.concatenate([np.arange(0, rows, 2), np.arange(1, rows, 2)])
    return jnp.asarray(np.eye(rows, dtype=np.float32)[order], BF16)


def _proj_body(*refs, n_y, has_res, final, has_next, split, n_w):
    it = iter(refs)
    take = lambda n: [next(it) for _ in range(n)]
    if has_res:
        y_refs = take(n_y)
        unperm_ref = next(it) if n_y == 2 else None
        w_out_ref, gate_ref = take(2)
        fg_ref = next(it) if final else None
    x_ref = next(it)
    if has_next:
        g_ref, mod_ref = take(2)
        perm_ref = next(it) if split else None
        w_refs = take(n_w)
    x_out_ref = next(it) if has_res else None
    o_refs = take(n_w) if has_next else []

    x = x_ref[...]
    d = x.shape[-1]
    if has_res:
        if n_y == 2:
            y = _dot(unperm_ref[...], jnp.concatenate([r[...] for r in y_refs], axis=0)).astype(BF16)
        else:
            y = y_refs[0][...]
        x = x + gate_ref[...][:, 2 * d:] * _dot(y, w_out_ref[...])
        if final:
            x_out_ref[...] = x * lax.rsqrt(jnp.mean(x * x, axis=-1, keepdims=True) + EPS) * fg_ref[...]
        else:
            x_out_ref[...] = x
    if has_next:
        h = x * lax.rsqrt(jnp.mean(x * x, axis=-1, keepdims=True) + EPS) * g_ref[...]
        mod = mod_ref[...]
        hb = (h * (1.0 + mod[:, d:2 * d]) + mod[:, :d]).astype(BF16)
        if split:
            hb = _dot(perm_ref[...], hb).astype(BF16)
        for w_ref, o_ref in zip(w_refs, o_refs):
            res = _dot(hb, w_ref[...])
            if split:
                half, n = res.shape[0] // 2, res.shape[1]
                o_ref[:, :n] = res[:half]
                o_ref[:, n:] = res[half:]
            else:
                o_ref[...] = res


def _proj(x, mod4, mod_row, tm, name, res=None, nxt=None, final_g=None):
    t, d = x.shape
    has_res, has_next, final = res is not None, nxt is not None, final_g is not None
    mod_spec = lambda layer: pl.BlockSpec((None, None, 1, mod4.shape[-1]), lambda i: (layer, mod_row(i), 0, 0))
    in_specs, args, out_specs, out_shape = [], [], [], []
    n_y, split, n_w = 0, False, 0
    if has_res:
        ys, w_out, j, layer = res
        n_y = len(ys)
        in_specs += [pl.BlockSpec((tm // n_y, y.shape[-1]), lambda i: (i, 0)) for y in ys]
        args += list(ys)
        if n_y == 2:
            in_specs.append(pl.BlockSpec((tm, tm), lambda i: (0, 0)))
            args.append(_phase_permutation(tm).T)
        in_specs += [pl.BlockSpec((None,) + w_out.shape[1:], lambda i: (j, 0, 0), pipeline_mode=pl.Buffered(1)),
                     mod_spec(layer)]
        args += [w_out, mod4]
        if final:
            in_specs.append(pl.BlockSpec((1, d), lambda i: (0, 0)))
            args.append(final_g.reshape(1, d))
        out_specs.append(pl.BlockSpec((tm, d), lambda i: (i, 0)))
        out_shape.append(jax.ShapeDtypeStruct((t, d), F32))
    in_specs.append(pl.BlockSpec((tm, d), lambda i: (i, 0)))
    args.append(x)
    if has_next:
        norm_g3, next_layer, weights, split = nxt
        n_w = len(weights)
        in_specs += [pl.BlockSpec((None, 1, d), lambda i: (next_layer, 0, 0)), mod_spec(next_layer)]
        args += [norm_g3, mod4]
        if split:
            in_specs.append(pl.BlockSpec((tm, tm), lambda i: (0, 0)))
            args.append(_phase_permutation(tm))
        in_specs += [pl.BlockSpec(bs, functools.partial(lambda idx, i: idx, idx), pipeline_mode=pl.Buffered(1))
                     for _, bs, idx in weights]
        args += [w for w, _, _ in weights]
        p = 2 if split else 1
        widths = [bs[-1] for _, bs, _ in weights]
        out_specs += [pl.BlockSpec((tm // p, p * n), lambda i: (i, 0)) for n in widths]
        out_shape += [jax.ShapeDtypeStruct((t // p, p * n), F32) for n in widths]
    return pl.pallas_call(
        functools.partial(_proj_body, n_y=n_y, has_res=has_res, final=final, has_next=has_next,
                          split=split, n_w=n_w),
        grid=(t // tm,),
        in_specs=in_specs,
        out_specs=out_specs,
        out_shape=out_shape,
        compiler_params=_params("parallel"),
        name=name,
    )(*args)


GLA_CHUNK = 64
GLA_BLOCK = 256
GLA_HEADS_PER_STEP = 2
LOG2_E = math.log2(math.e)
GLA_CLAMP_SPLIT = 8.0 * LOG2_E
GLA_CLAMP_HALF = 80.0 * LOG2_E


def _gla_tables(rows, dk):
    c, hc = GLA_CHUNK, GLA_CHUNK // 2
    r = np.arange(rows)[:, None]
    s = np.arange(rows)[None, :]
    same_chunk = (r // c) == (s // c)
    same_half = (r // hc) == (s // hc)
    tri = np.stack([same_chunk & (s <= r), same_chunk & (s >= r)])
    m_half = np.stack([same_half & (s < r), same_half & (s > r)])
    sel = np.concatenate([s == (r // c) * c + hc, s == (r // hc) * hc + hc // 2], axis=0)
    second = np.broadcast_to((r // hc) % 2 == 1, (rows, dk))
    as_const = lambda m, dt: jnp.asarray(m.astype(np.float32), dt)
    return (as_const(tri, BF16), as_const(sel, BF16), as_const(same_chunk, F32), as_const(m_half, F32),
            as_const(second, F32), as_const(2.0 * (s == r), F32))


def _gla_body(q_ref, k_ref, v_ref, r_ref, lr_ref, wdec_ref, bdec_ref, og_ref, tri_ref, sel_ref,
              mchunk_ref, mhalf_ref, second_ref, diag2_ref, *rest, seq, layer, has_s0, want_state,
              has_states_in):
    rest = list(rest)
    s0_ref = rest.pop(0) if has_s0 else None
    if has_states_in:
        rest.pop(0)
    y_ref = rest.pop(0)
    sfin_ref = rest.pop(0) if want_state else None
    o_ref, qi_ref, cum_ref, st_ref = rest
    chunk = GLA_CHUNK
    n_chunks = seq // chunk
    blk = tri_ref.shape[-1]
    n_heads = wdec_ref.shape[0]
    dk = q_ref.shape[-1] // n_heads
    dv = v_ref.shape[-1] // n_heads
    scale = dk ** -0.5
    heads = range(n_heads)
    dirs = range(2)
    chains = [(hh, d) for hh in heads for d in dirs]

    def nt(a, b):
        return lax.dot_general(a, b, _NT, preferred_element_type=F32)

    def cols(hh, width):
        return slice(hh * width, (hh + 1) * width)

    def block_pass(bi, carry):
        rows = pl.ds(pl.multiple_of(bi * blk, blk), blk)
        q = [q_ref[rows, cols(hh, dk)] * scale for hh in heads]
        k = [k_ref[rows, cols(hh, dk)] for hh in heads]
        lr = lr_ref[rows, :]
        g = [_log_sigmoid(_dot3(lr, wdec_ref[hh]) + bdec_ref[hh]) * (LOG2_E / GLA_GATE_NORM) for hh in heads]
        cum = {(hh, d): _dot_exact_lhs(tri_ref[d], g[hh][:, cols(d, dk)]) for hh, d in chains}
        refs = [_dot(sel_ref[...], jnp.concatenate([cum[hh, 0], cum[hh, 1]], axis=1).astype(BF16))
                for hh in heads]
        second = second_ref[...]
        first = 1.0 - second
        p_split, p_half = [], {}
        for hh in heads:
            q_in = (q[hh] * second, q[hh] * first)
            k_in = (k[hh] * first, k[hh] * second)
            q1, k1 = [], []
            for d in dirs:
                c = (hh, d)
                d_split = cum[c] - refs[hh][:blk, cols(d, dk)]
                d_half = cum[c] - refs[hh][blk:, cols(d, dk)]
                q1.append((q_in[d] * jnp.exp2(jnp.minimum(d_split, GLA_CLAMP_SPLIT))).astype(BF16))
                k1.append((k_in[d] * jnp.exp2(jnp.minimum(-d_split, GLA_CLAMP_SPLIT))).astype(BF16))
                q0 = (q[hh] * jnp.exp2(jnp.minimum(d_half, GLA_CLAMP_HALF))).astype(BF16)
                k0 = (k[hh] * jnp.exp2(jnp.minimum(-d_half, GLA_CLAMP_HALF))).astype(BF16)
                p_half[c] = nt(q0, k0)
                qi_ref[hh, d, rows, :] = (q[hh] * jnp.exp2(cum[c])).astype(BF16)
                cum_ref[hh, d, rows, :] = cum[c]
            p_split.append(nt(jnp.concatenate(q1, axis=1), jnp.concatenate(k1, axis=1)))
        for hh in heads:
            att = p_split[hh] * mchunk_ref[...] + diag2_ref[...] * jnp.sum(q[hh] * k[hh], axis=-1, keepdims=True)
            for d in dirs:
                att = att + jnp.where(mhalf_ref[d] > 0.5, p_half[hh, d], 0.0)
            o_ref[rows, cols(hh, dv)] = _dot(att.astype(BF16), v_ref[rows, cols(hh, dv)].astype(BF16))
        return carry

    lax.fori_loop(0, seq // blk, block_pass, 0)

    for hh, d in chains:
        st_ref[hh, d] = s0_ref[d, hh] if has_s0 else jnp.zeros((dk, dv), F32)

    def chunk_step(i, carry):
        for hh, d in chains:
            c = i if d == 0 else n_chunks - 1 - i
            sl = pl.ds(pl.multiple_of(c * chunk, chunk), chunk)
            cum = cum_ref[hh, d, sl, :]
            blast = cum[chunk - 1:chunk] if d == 0 else cum[0:1]
            kh = (k_ref[sl, cols(hh, dk)] * jnp.exp2(blast - cum)).astype(BF16)
            vb = v_ref[sl, cols(hh, dv)].astype(BF16)
            st = st_ref[hh, d]
            o_ref[sl, cols(hh, dv)] += _dot(qi_ref[hh, d, sl, :], st.astype(BF16))
            decay = jnp.broadcast_to(jnp.exp2(blast), (dk, dk)).T
            st_ref[hh, d] = (jnp.concatenate([decay] * (dv // dk), axis=1) * st
                             + lax.dot_general(kh, vb, _TN, preferred_element_type=F32))
        return carry

    lax.fori_loop(0, n_chunks, chunk_step, 0, unroll=min(n_chunks, 8))

    def finish(jb, carry):
        sl = pl.ds(pl.multiple_of(jb * blk, blk), blk)
        for hh in heads:
            o = o_ref[sl, cols(hh, dv)]
            o = o * lax.rsqrt(jnp.mean(o * o, axis=-1, keepdims=True) + EPS) * og_ref[...]
            y_ref[sl, cols(hh, dv)] = (o * _silu(r_ref[sl, cols(hh, dv)])).astype(BF16)
        return carry

    lax.fori_loop(0, seq // blk, finish, 0)
    if want_state:
        own = sfin_ref if has_states_in else sfin_ref.at[layer]
        for hh, d in chains:
            own[d, hh] = st_ref[hh, d]
        if not has_states_in:
            for other in range(sfin_ref.shape[0]):
                if other != layer:
                    sfin_ref[other] = jnp.zeros(sfin_ref.shape[1:], F32)


def _gla_mixer(proj, lr, w_dec, b_dec, onorm_g, s0, j, n_seq, seq, want_state, states, name):
    t = proj.shape[0]
    h = GLA_HEADS
    key = w_dec.shape[-1]
    dk = key // h
    val = (proj.shape[1] - 2 * key) // 2
    dv = val // h
    has_s0 = s0 is not None
    tables = _gla_tables(min(seq, GLA_BLOCK), dk)
    fixed = lambda a: pl.BlockSpec(a.shape, lambda b, hp: (0,) * a.ndim)
    hps = h if seq <= GLA_BLOCK else GLA_HEADS_PER_STEP
    wk, wv = hps * dk, hps * dv
    in_specs = [
        pl.BlockSpec((seq, wk), lambda b, hp: (b, hp)),
        pl.BlockSpec((seq, wk), lambda b, hp: (b, key // wk + hp)),
        pl.BlockSpec((seq, wv), lambda b, hp: (b, (2 * key) // wv + hp)),
        pl.BlockSpec((seq, wv), lambda b, hp: (b, (2 * key + val) // wv + hp)),
        pl.BlockSpec((seq, 2 * GLA_RANK), lambda b, hp: (b, 0)),
        pl.BlockSpec((None, hps, 2 * GLA_RANK, 2 * dk), lambda b, hp: (j, hp, 0, 0)),
        pl.BlockSpec((None, hps, 1, 2 * dk), lambda b, hp: (j, hp, 0, 0)),
        pl.BlockSpec((None, 1, dv), lambda b, hp: (j, 0, 0)),
    ] + [fixed(a) for a in tables]
    n_gla = w_dec.shape[0]
    w5 = w_dec.reshape(n_gla, 2, GLA_RANK, h, dk)
    zero = jnp.zeros_like(w5[:, 0])
    w_blk = jnp.concatenate([jnp.concatenate([w5[:, 0], zero], axis=-1),
                             jnp.concatenate([zero, w5[:, 1]], axis=-1)], axis=1).transpose(0, 2, 1, 3)
    b_cat = b_dec.reshape(n_gla, 2, h, dk).transpose(0, 2, 1, 3).reshape(n_gla, h, 1, 2 * dk)
    args = [proj, proj, proj, proj, lr, w_blk, b_cat, onorm_g.reshape(n_gla, 1, dv), *tables]
    if has_s0:
        in_specs.append(pl.BlockSpec((None, None, 2, hps, dk, dv), lambda b, hp: (b, j, 0, hp, 0, 0)))
        args.append(s0)
    out_specs = [pl.BlockSpec((seq, wv), lambda b, hp: (b, hp))]
    out_shape = [jax.ShapeDtypeStruct((t, val), BF16)]
    aliases = {}
    if want_state:
        out_shape.append(jax.ShapeDtypeStruct((n_seq, n_gla, 2, h, dk, dv), F32))
        if states is None:
            out_specs.append(pl.BlockSpec((None, n_gla, 2, hps, dk, dv), lambda b, hp: (b, 0, 0, hp, 0, 0)))
        else:
            out_specs.append(pl.BlockSpec((None, None, 2, hps, dk, dv), lambda b, hp: (b, j, 0, hp, 0, 0)))
            in_specs.append(pl.BlockSpec(memory_space=pl.ANY))
            args.append(states)
            aliases = {len(args) - 1: 1}
    return pl.pallas_call(
        functools.partial(_gla_body, seq=seq, layer=j, has_s0=has_s0, want_state=want_state,
                          has_states_in=bool(aliases)),
        grid=(n_seq, h // hps),
        in_specs=in_specs,
        out_specs=out_specs,
        out_shape=out_shape,
        input_output_aliases=aliases,
        scratch_shapes=[pltpu.VMEM((seq, wv), F32), pltpu.VMEM((hps, 2, seq, dk), BF16),
                        pltpu.VMEM((hps, 2, seq, dk), F32), pltpu.VMEM((hps, 2, dk, dv), F32)],
        compiler_params=_params("parallel", "parallel"),
        name=name,
    )(*args)


def _dft_tables(n):
    f = np.arange(n, dtype=np.int64)
    ang = (np.outer(f, f) % n).astype(np.float64) * (2.0 * math.pi / n)
    return _trig_bf16(ang)


def _fnet_body(x_ref, z_ref, cc_ref, sc_ref, cl_ref, sl_ref, y_ref, *, seq, n_seq, gc):
    inv = 1.0 / math.sqrt(seq * gc)
    for g in range(x_ref.shape[-1] // gc):
        cols = slice(g * gc, (g + 1) * gc)
        xb = x_ref[:, cols].astype(BF16)
        a = _dot(xb, cc_ref[...]).astype(BF16)
        bm = _dot(xb, sc_ref[...]).astype(BF16)
        for i in range(n_seq):
            rows = slice(i * seq, (i + 1) * seq)
            f = _dot(cl_ref[...], a[rows]) - _dot(sl_ref[...], bm[rows])
            y_ref[rows, cols] = (f * inv * _silu(z_ref[rows, cols])).astype(BF16)


def _fnet_mixer(proj, n_seq_total, seq, seqs_per_step, groups_per_step, name):
    t = proj.shape[0]
    width = proj.shape[1] // 2
    gc = width // FN_GROUPS
    cc, sc = _dft_tables(gc)
    cl, sl = _dft_tables(seq)
    rows = seq * seqs_per_step
    wide = gc * groups_per_step
    n_wide = width // wide
    const = pl.Buffered(1)
    return pl.pallas_call(
        functools.partial(_fnet_body, seq=seq, n_seq=seqs_per_step, gc=gc),
        grid=(n_seq_total // seqs_per_step, n_wide),
        in_specs=[
            pl.BlockSpec((rows, wide), lambda i, g: (i, g)),
            pl.BlockSpec((rows, wide), lambda i, g: (i, n_wide + g)),
            pl.BlockSpec((gc, gc), lambda i, g: (0, 0), pipeline_mode=const),
            pl.BlockSpec((gc, gc), lambda i, g: (0, 0), pipeline_mode=const),
            pl.BlockSpec((seq, seq), lambda i, g: (0, 0), pipeline_mode=const),
            pl.BlockSpec((seq, seq), lambda i, g: (0, 0), pipeline_mode=const),
        ],
        out_specs=pl.BlockSpec((rows, wide), lambda i, g: (i, g)),
        out_shape=jax.ShapeDtypeStruct((t, width), BF16),
        compiler_params=_params("parallel", "parallel"),
        name=name,
    )(proj, proj, cc, sc, cl, sl)


def _signal_tables(n):
    o = 2 * np.arange(n, dtype=np.int64) + 1
    ang = (np.outer(o, o) % (8 * n)).astype(np.float64) * (math.pi / (4 * n))
    cos, sin = _trig_bf16(ang)
    return cos, -sin


def _filter_tables(n, split):
    f = np.arange(n, dtype=np.int64)
    if split:
        f = np.concatenate([f[:n // 2], f[:n // 2 - 1:-1]])
    f = 2 * f + 1
    o = np.arange(n, dtype=np.int64)
    ang = (np.outer(f, o) % (4 * n)).astype(np.float64) * (math.pi / (2 * n))
    return _trig_bf16(ang)


def _hyena_positions(seq, bands, pad_to):
    t = np.linspace(0.0, 1.0, seq)[:, None]
    w = 2.0 * math.pi * np.arange(seq)[:, None] / seq
    f = np.linspace(1e-4, bands - 1, bands)[None, :]
    zpos = np.concatenate([t, np.cos(f * w), -np.sin(f * w)], axis=-1)
    zpos = np.pad(zpos, ((0, 0), (0, pad_to - zpos.shape[1])))
    return jnp.asarray(zpos, F32), jnp.asarray(t, F32)


def _hyfilt_body(zpos_ref, w1_ref, b1_ref, w2_ref, b2_ref, w3_ref, b3_ref, fr_ref, w4f_ref,
                 w4b_ref, t_ref, del_ref, ch_ref, sh_ref, *rest, split):
    if split:
        c2_ref, s2_ref = rest[:2]
        out_refs, a3_ref = rest[2:-1], rest[-1]
    else:
        out_refs, a3_ref = rest[:-1], rest[-1]

    @pl.when(pl.program_id(0) == 0)
    def _():
        fr = fr_ref[...]
        a = jnp.sin(fr * (_dot3(zpos_ref[...], w1_ref[...]) + b1_ref[...]))
        a = jnp.sin(fr * (_dot3(a, w2_ref[...]) + b2_ref[...]))
        a3_ref[...] = _dot3_lhs(jnp.sin(fr * (_dot3(a, w3_ref[...]) + b3_ref[...])))

    a3 = a3_ref[...]
    dec = jnp.exp(-t_ref[...] * del_ref[...]) * (1.0 / t_ref.shape[0])
    fwd = _dot(a3, _dot3_rhs(w4f_ref[...])) * dec
    bwd = _dot(a3, _dot3_rhs(w4b_ref[...])) * dec
    first = lax.broadcasted_iota(jnp.int32, bwd.shape, 0) == 0
    bwd = jnp.where(first, 0.0, bwd)
    hre = _dot(ch_ref[...], (fwd + bwd).astype(BF16))
    him = _dot(sh_ref[...], (bwd - fwd).astype(BF16))
    if not split:
        out_refs[0][...] = hre
        out_refs[1][...] = him
        return
    half = hre.shape[0] // 2
    plus_r, plus_i = hre[:half] + hre[half:], him[:half] - him[half:]
    minus_r, minus_i = hre[:half] - hre[half:], him[:half] + him[half:]
    c2, s2 = c2_ref[...], s2_ref[...]
    for ref, val in zip(out_refs, (plus_r, plus_i,
                                   c2 * minus_r + s2 * minus_i, c2 * minus_i - s2 * minus_r,
                                   c2 * minus_r - s2 * minus_i, c2 * minus_i + s2 * minus_r)):
        ref[...] = val


def _hyena_filters(seq, j, emb, w1p, b1, w2, b2, w3, b3, w4, freq, ce, split, name):
    ffn = w2.shape[-1]
    width = w4.shape[-1] // (2 * HY_ORDER)
    emb_pad = w1p.shape[1]
    zpos, tcol = _hyena_positions(seq, (emb - 1) // 2, emb_pad)
    deltas = jnp.asarray(np.abs(np.linspace(HY_MIN_DECAY, HY_MAX_DECAY, width, dtype=np.float32))[None, :])
    ch, sh = _filter_tables(seq, split)
    nb = width // ce
    const = pl.Buffered(1)
    vec = lambda: pl.BlockSpec((None, 1, ffn), lambda c: (j, 0, 0))
    out_rows, n_out = (seq // 2, 6) if split else (seq, 2)
    twiddles, twiddle_specs = [], []
    if split:
        ang = np.broadcast_to((math.pi * (2 * np.arange(out_rows) + 1) / (2 * seq))[:, None], (out_rows, ce))
        twiddles = [jnp.asarray(np.cos(ang), F32), jnp.asarray(np.sin(ang), F32)]
        twiddle_specs = [pl.BlockSpec((out_rows, ce), lambda c: (0, 0), pipeline_mode=const)] * 2
    return pl.pallas_call(
        functools.partial(_hyfilt_body, split=split),
        grid=(HY_ORDER * nb,),
        in_specs=[
            pl.BlockSpec((seq, emb_pad), lambda c: (0, 0)),
            pl.BlockSpec((None, emb_pad, ffn), lambda c: (j, 0, 0)),
            vec(),
            pl.BlockSpec((None, ffn, ffn), lambda c: (j, 0, 0)),
            vec(),
            pl.BlockSpec((None, ffn, ffn), lambda c: (j, 0, 0)),
            vec(),
            vec(),
            pl.BlockSpec((None, ffn, ce), lambda c: (j, 0, (c // nb) * 2 * nb + c % nb)),
            pl.BlockSpec((None, ffn, ce), lambda c: (j, 0, (c // nb) * 2 * nb + nb + c % nb)),
            pl.BlockSpec((seq, 1), lambda c: (0, 0)),
            pl.BlockSpec((1, ce), lambda c: (0, c % nb)),
            pl.BlockSpec((seq, seq), lambda c: (0, 0), pipeline_mode=const),
            pl.BlockSpec((seq, seq), lambda c: (0, 0), pipeline_mode=const),
        ] + twiddle_specs,
        out_specs=[pl.BlockSpec((out_rows, ce), lambda c: (0, c))] * n_out,
        out_shape=[jax.ShapeDtypeStruct((out_rows, HY_ORDER * width), F32)] * n_out,
        scratch_shapes=[pltpu.VMEM((seq, 3 * ffn), BF16)],
        compiler_params=_params("arbitrary"),
        name=name,
    )(zpos, w1p, b1, w2, b2, w3, b3, freq, w4, w4, tcol, deltas, ch, sh, *twiddles)


SUBLANES = 8


def _shift_down(u, first):
    rolled = pltpu.roll(u, 1, 0)
    row = lax.broadcasted_iota(jnp.int32, (SUBLANES, 1), 0)
    return jnp.concatenate([jnp.where(row == 0, first, rolled[:SUBLANES]), rolled[SUBLANES:]], axis=0)


def _shift_up(u, last):
    n = u.shape[0]
    rolled = pltpu.roll(u, n - 1, 0)
    row = lax.broadcasted_iota(jnp.int32, (SUBLANES, 1), 0)
    tail = jnp.where(row == SUBLANES - 1, last, rolled[n - SUBLANES:])
    return jnp.concatenate([rolled[:n - SUBLANES], tail], axis=0)


def _hyena_body(x1_ref, x2_ref, v_ref, z_ref, cw1_ref, cw2_ref, cwv_ref, cb1_ref, cb2_ref, cbv_ref,
                hre0_ref, him0_ref, hre1_ref, him1_ref, d_ref, cq_ref, sq_ref, y_ref,
                cur_ref, cb_ref, yr_ref, yi_ref, *, seq, rows):
    n_rb = seq // rows

    def block(rb):
        return pl.ds(pl.multiple_of(rb * rows, rows), rows)

    def short_conv(u_ref, w_ref, b_ref, rb):
        u = u_ref[block(rb), :]
        before = u_ref[pl.ds(jnp.maximum(rb * rows - 1, 0), 1), :]
        after = u_ref[pl.ds(jnp.minimum((rb + 1) * rows, seq - 1), 1), :]
        before = jnp.where(rb == 0, 0.0, before)
        after = jnp.where(rb == n_rb - 1, 0.0, after)
        prev, nxt = _shift_down(u, before), _shift_up(u, after)
        return prev * w_ref[0:1] + u * w_ref[1:2] + nxt * w_ref[2:3] + b_ref[...]

    def long_conv_gate(x_ref, cw_ref, cbias_ref, hre_ref, him_ref, dn):
        def fwd(rb, carry):
            sl = block(rb)
            ur = _dot(cq_ref[sl, :], cb_ref[...])
            ui = _dot(sq_ref[sl, :], cb_ref[...])
            hr = hre_ref[sl, :]
            hi = him_ref[sl, :]
            yr_ref[sl, :] = (ur * hr - ui * hi).astype(BF16)
            yi_ref[sl, :] = (ur * hi + ui * hr).astype(BF16)
            return carry

        lax.fori_loop(0, n_rb, fwd, 0)

        def inv(rb, carry):
            sl = block(rb)
            c = _dot(cq_ref[sl, :], yr_ref[...]) + _dot(sq_ref[sl, :], yi_ref[...])
            new = short_conv(x_ref, cw_ref, cbias_ref, rb) * (c + cur_ref[sl, :] * dn)
            cur_ref[sl, :] = new
            cb_ref[sl, :] = new.astype(BF16)
            return carry

        lax.fori_loop(0, n_rb, inv, 0)

    def load_v(rb, carry):
        v = short_conv(v_ref, cwv_ref, cbv_ref, rb)
        cur_ref[block(rb), :] = v
        cb_ref[block(rb), :] = v.astype(BF16)
        return carry

    lax.fori_loop(0, n_rb, load_v, 0)
    long_conv_gate(x1_ref, cw1_ref, cb1_ref, hre0_ref, him0_ref, d_ref[0:1])
    long_conv_gate(x2_ref, cw2_ref, cb2_ref, hre1_ref, him1_ref, d_ref[1:2])

    def store(rb, carry):
        sl = block(rb)
        y_ref[sl, :] = (cur_ref[sl, :] * _silu(z_ref[sl, :])).astype(BF16)
        return carry

    lax.fori_loop(0, n_rb, store, 0)


def _hyena_split_body(x1_refs, x2_refs, v_refs, z_refs, cw1_ref, cw2_ref, cwv_ref, cb1_ref, cb2_ref, cbv_ref,
                      k_refs, d_ref, cq_ref, sq_ref, y_refs, cur_ref, sig_ref, z_buf, *, seq, rows):
    half = seq // 2
    n_rb = half // rows

    def block(rb, shift=0):
        return slice(rb * rows + shift, (rb + 1) * rows + shift)

    def short_conv(u_refs, w_ref, b_ref, rb):
        e_ref, o_ref = u_refs
        even, odd = e_ref[block(rb), :], o_ref[block(rb), :]
        odd_before = o_ref[block(rb, -1), :] if rb > 0 else _shift_down(odd, 0.0)
        even_after = e_ref[block(rb, 1), :] if rb < n_rb - 1 else _shift_up(even, 0.0)
        w0, w1, w2, b = w_ref[0:1], w_ref[1:2], w_ref[2:3], b_ref[...]
        return (odd_before * w0 + even * w1 + odd * w2 + b, even * w0 + odd * w1 + even_after * w2 + b)

    def long_conv_gate(x_ref, cw_ref, cbias_ref, k, dn):
        k1r, k1i, k2r, k2i, k3r, k3i = k
        for fb in range(n_rb):
            sl = block(fb)
            c, s = cq_ref[sl, :], sq_ref[sl, :]
            er, ei = _dot(c, sig_ref[0]), _dot(s, sig_ref[0])
            orr, oi = _dot(c, sig_ref[1]), _dot(s, sig_ref[1])
            a1r, a1i, a2r, a2i, a3r, a3i = (r[sl, :] for r in (k1r, k1i, k2r, k2i, k3r, k3i))
            z_buf[0, sl, :] = (er * a1r - ei * a1i + orr * a2r - oi * a2i).astype(BF16)
            z_buf[1, sl, :] = (er * a1i + ei * a1r + orr * a2i + oi * a2r).astype(BF16)
            z_buf[2, sl, :] = (er * a3r - ei * a3i + orr * a1r - oi * a1i).astype(BF16)
            z_buf[3, sl, :] = (er * a3i + ei * a3r + orr * a1i + oi * a1r).astype(BF16)
        for tb in range(n_rb):
            sl = block(tb)
            c, s = cq_ref[sl, :], sq_ref[sl, :]
            conv = (_dot(c, z_buf[0]) + _dot(s, z_buf[1]), _dot(c, z_buf[2]) + _dot(s, z_buf[3]))
            gate = short_conv(x_ref, cw_ref, cbias_ref, tb)
            for p in range(2):
                new = gate[p] * (conv[p] + cur_ref[p, sl, :] * dn)
                cur_ref[p, sl, :] = new
                sig_ref[p, sl, :] = new.astype(BF16)

    for rb in range(n_rb):
        for p, val in enumerate(short_conv(v_refs, cwv_ref, cbv_ref, rb)):
            cur_ref[p, block(rb), :] = val
            sig_ref[p, block(rb), :] = val.astype(BF16)
    long_conv_gate(x1_refs, cw1_ref, cb1_ref, k_refs[0], d_ref[0:1])
    long_conv_gate(x2_refs, cw2_ref, cb2_ref, k_refs[1], d_ref[1:2])
    for rb in range(n_rb):
        for p in range(2):
            sl = block(rb)
            y_refs[p][sl, :] = (cur_ref[p, sl, :] * _silu(z_refs[p][sl, :])).astype(BF16)


def _hyena_split_entry(*refs, seq, rows):
    it = iter(refs)
    take = lambda n: tuple(next(it) for _ in range(n))
    x1, x2, v, z = take(2), take(2), take(2), take(2)
    conv_refs = take(6)
    k = tuple(take(6) for _ in range(HY_ORDER))
    d, cq, sq = take(3)
    y = take(2)
    _hyena_split_body(x1, x2, v, z, *conv_refs, k, d, cq, sq, y, *it, seq=seq, rows=rows)


def _hyena_mixer(proj, conv_w, conv_b, filt, d_skip, j, n_seq, seq, ce, split, name):
    phases = 2 if split else 1
    t = proj.shape[0] * phases
    width = proj.shape[1] // (4 * phases)
    nb = width // ce
    n_dft = seq // phases
    cq, sq = _signal_tables(n_dft)
    rows = min(n_dft, HY_ROW_BLOCK)
    const = pl.Buffered(1)
    seg = lambda s: pl.BlockSpec((seq, ce), lambda e, b: (b, s * nb + e))
    cw = lambda s: pl.BlockSpec((None, conv_w.shape[1], ce), lambda e, b: (j, 0, s * nb + e))
    cb = lambda s: pl.BlockSpec((None, 1, ce), lambda e, b: (j, 0, s * nb + e))
    hs = lambda n: pl.BlockSpec((n_dft, ce), lambda e, b: (0, n * nb + e),
                                pipeline_mode=const if nb == 1 else pl.Buffered(2))
    conv_b3 = conv_b.reshape(conv_b.shape[0], 1, -1)
    if split:
        body = functools.partial(_hyena_split_entry, seq=seq, rows=rows)
        phase = lambda s, p: pl.BlockSpec((n_dft, ce), lambda e, b: (b, p * 4 * nb + s * nb + e))
        x_specs = [phase(s, p) for s in range(4) for p in range(2)]
        out_specs = [pl.BlockSpec((n_dft, ce), lambda e, b: (b, e))] * 2
        out_shape = [jax.ShapeDtypeStruct((t // 2, width), BF16)] * 2
        scratch = [pltpu.VMEM((2, n_dft, ce), F32), pltpu.VMEM((2, n_dft, ce), BF16),
                   pltpu.VMEM((4, n_dft, ce), BF16)]
    else:
        body = functools.partial(_hyena_body, seq=seq, rows=rows)
        x_specs = [seg(0), seg(1), seg(2), seg(3)]
        out_specs = pl.BlockSpec((seq, ce), lambda e, b: (b, e))
        out_shape = jax.ShapeDtypeStruct((t, width), BF16)
        scratch = [pltpu.VMEM((seq, ce), F32), pltpu.VMEM((seq, ce), BF16),
                   pltpu.VMEM((seq, ce), BF16), pltpu.VMEM((seq, ce), BF16)]
    return pl.pallas_call(
        body,
        grid=(nb, n_seq),
        in_specs=x_specs + [cw(0), cw(1), cw(2), cb(0), cb(1), cb(2)]
        + [hs(n) for n in range(HY_ORDER) for _ in filt]
        + [pl.BlockSpec((None, HY_ORDER, ce), lambda e, b: (j, 0, e)),
           pl.BlockSpec((n_dft, n_dft), lambda e, b: (0, 0), pipeline_mode=const),
           pl.BlockSpec((n_dft, n_dft), lambda e, b: (0, 0), pipeline_mode=const)],
        out_specs=out_specs,
        out_shape=out_shape,
        scratch_shapes=scratch,
        compiler_params=_params("parallel", "parallel"),
        name=name,
    )(*([proj] * len(x_specs)), conv_w, conv_w, conv_w, conv_b3, conv_b3, conv_b3,
      *(list(filt) * HY_ORDER), d_skip, cq, sq)


def kernel(x_prompt, x_sample, state_gla, c, c_ctx, mod_w, mod_b, norm_g, final_norm_g, gla_w_in, gla_w_dec, gla_b_dec, gla_onorm_g, gla_w_out, fn_w_in, fn_w_out, hy_w_in, hy_conv_w, hy_conv_b, hy_ffn_w1, hy_ffn_b1, hy_ffn_w2, hy_ffn_b2, hy_ffn_w3, hy_ffn_b3, hy_ffn_w4, hy_freq, hy_d, hy_w_out):
    n_p, l_p, d = x_prompt.shape
    n_s, l_s, _ = x_sample.shape
    depth = mod_w.shape[0]
    key = gla_w_dec.shape[-1]
    n_main = gla_w_in.shape[-1] - 2 * GLA_RANK

    cvec = jnp.concatenate([c_ctx[None], c, jnp.zeros((MOD_ROWS - 1 - n_s, d), F32)], axis=0)
    mod4 = _modulation(cvec, mod_w, mod_b).reshape(depth, MOD_ROWS, 1, 3 * d)
    norm_g3 = norm_g.reshape(depth, 1, d)

    gla_w_main = gla_w_in.astype(BF16)
    gla_w_lr = gla_w_in[:, :, n_main:].astype(BF16)
    gla_w_out_b = gla_w_out.astype(BF16)
    fn_w_in_b = fn_w_in.astype(BF16)
    fn_w_out_b = fn_w_out.astype(BF16)
    hy_w_in_b = hy_w_in.astype(BF16)
    hy_w_out_b = hy_w_out.astype(BF16)
    emb = hy_ffn_w1.shape[1]
    emb_pad = -(-emb // 128) * 128
    hy_w1p = jnp.pad(hy_ffn_w1, ((0, 0), (0, emb_pad - emb), (0, 0)))
    ffn = hy_ffn_w2.shape[-1]
    hy_vecs = [a.reshape(a.shape[0], 1, ffn) for a in (hy_ffn_b1, hy_ffn_b2, hy_ffn_b3, hy_freq)]

    tm = 512
    tiles_per_sample = l_s // tm
    streams = {
        "p": dict(x=x_prompt.reshape(n_p * l_p, d), n=n_p, l=l_p, row=lambda i: 0),
        "s": dict(x=x_sample.reshape(n_s * l_s, d), n=n_s, l=l_s, row=lambda i: 1 + i // tiles_per_sample),
    }
    def in_weights(i, seq):
        kind, j = i % N_MIXERS, i // N_MIXERS
        if kind == 0:
            return [(gla_w_main, (None, d, n_main), (j, 0, 0)), (gla_w_lr, (None, d, 2 * GLA_RANK), (j, 0, 0))], False
        if kind == 1:
            return [(fn_w_in_b, (None,) + fn_w_in.shape[1:], (j, 0, 0))], False
        return [(hy_w_in_b, (None,) + hy_w_in.shape[1:], (j, 0, 0))], seq >= HY_SPLIT_MIN_SEQ

    for tag, st in streams.items():
        weights, split = in_weights(0, st["l"])
        st["proj"] = _proj(st["x"], mod4, st["row"], tm, f"proj_{tag}0", nxt=(norm_g3, 0, weights, split))
    new_states = None
    for i in range(depth):
        kind, j = i % N_MIXERS, i // N_MIXERS
        if kind == 2:
            filt = {}
            for tag, st in streams.items():
                ce = min(d, 256 if st["l"] > 256 else d)
                filt[tag] = _hyena_filters(st["l"], j, emb, hy_w1p, hy_vecs[0], hy_ffn_w2, hy_vecs[1], hy_ffn_w3,
                                           hy_vecs[2], hy_ffn_w4, hy_vecs[3], ce, st["l"] >= HY_SPLIT_MIN_SEQ,
                                           f"hyfilt_{tag}{i}")
        for tag, st in streams.items():
            n_seq, seq = st["n"], st["l"]
            if kind == 0:
                proj, lr = st["proj"]
                is_ctx = tag == "p"
                res = _gla_mixer(proj, lr, gla_w_dec, gla_b_dec, gla_onorm_g,
                                 None if is_ctx else state_gla, j, n_seq, seq, is_ctx, new_states,
                                 f"gla_{tag}{i}")
                ys = [res[0]]
                if is_ctx:
                    new_states = res[1]
                w_out = gla_w_out_b
            elif kind == 1:
                short = seq <= 256
                ys = [_fnet_mixer(st["proj"][0], n_seq, seq, 4 if short else 1, FN_GROUPS if short else 2,
                                  f"fnet_{tag}{i}")]
                w_out = fn_w_out_b
            else:
                split = seq >= HY_SPLIT_MIN_SEQ
                ce = min(d, 256 if seq > 256 else d)
                y = _hyena_mixer(st["proj"][0], hy_conv_w, hy_conv_b, filt[tag], hy_d, j, n_seq, seq, ce, split,
                                 f"hyena_{tag}{i}")
                ys = list(y) if split else [y]
                w_out = hy_w_out_b
            if i + 1 < depth:
                weights, split = in_weights(i + 1, seq)
                st["x"], *st["proj"] = _proj(st["x"], mod4, st["row"], tm, f"proj_{tag}{i + 1}",
                                             res=(ys, w_out, j, i), nxt=(norm_g3, i + 1, weights, split))
            else:
                (st["x"],) = _proj(st["x"], mod4, st["row"], tm, f"proj_{tag}{i + 1}",
                                   res=(ys, w_out, j, i), final_g=final_norm_g)
    y_prompt = streams["p"]["x"].reshape(n_p, l_p, d)
    y_sample = streams["s"]["x"].reshape(n_s, l_s, d)
    new_state_gla = new_states.astype(x_prompt.dtype)
    return (y_prompt, y_sample, new_state_gla)
```

```python
import functools
import math

import jax
import jax.numpy as jnp
import numpy as np
from jax import lax
from jax.experimental import pallas as pl
from jax.experimental.pallas import tpu as pltpu

F32 = jnp.float32
BF16 = jnp.bfloat16

EPS = 1e-6
N_MIXERS = 3
GLA_HEADS = 4
GLA_RANK = 16
GLA_GATE_NORM = 16.0
FN_GROUPS = 4
HY_ORDER = 2
HY_TARGET = 1e-2
HY_MIN_DECAY = math.log(HY_TARGET) / 1.5
HY_MAX_DECAY = math.log(HY_TARGET) / 0.3
HY_ROW_BLOCK = 512
HY_SPLIT_MIN_SEQ = 512

V7X_VMEM_BYTES = 64 * 1024 * 1024
VMEM_LIMIT_BYTES = V7X_VMEM_BYTES - 8 * 1024 * 1024
MOD_ROWS = 8

_NT = (((1,), (1,)), ((), ()))
_TN = (((0,), (0,)), ((), ()))


def _params(*sem):
    return pltpu.CompilerParams(dimension_semantics=sem, vmem_limit_bytes=VMEM_LIMIT_BYTES)


def _dot(a, b):
    return jnp.dot(a, b, preferred_element_type=F32)


def _split3(x):
    hi = x.astype(BF16)
    r1 = x - hi.astype(F32)
    mid = r1.astype(BF16)
    lo = (r1 - mid.astype(F32)).astype(BF16)
    return hi, mid, lo


def _dot3(a, b):
    return _dot(_dot3_lhs(a), _dot3_rhs(b))


def _dot3_lhs(a):
    ah = a.astype(BF16)
    al = (a - ah.astype(F32)).astype(BF16)
    return jnp.concatenate([ah, ah, al], axis=1)


def _dot3_rhs(b):
    bh = b.astype(BF16)
    bl = (b - bh.astype(F32)).astype(BF16)
    return jnp.concatenate([bh, bl, bh], axis=0)


def _dot_exact_lhs(a_bf16, b):
    return _dot(jnp.concatenate([a_bf16] * 3, axis=1), jnp.concatenate(_split3(b), axis=0))


def _trig_bf16(ang):
    return (jnp.asarray(np.cos(ang), F32).astype(BF16), jnp.asarray(np.sin(ang), F32).astype(BF16))


def _silu(x):
    return x / (1.0 + jnp.exp(-x))


def _log_sigmoid(x):
    return jnp.minimum(x, 0.0) - jnp.log(1.0 + jnp.exp(-jnp.abs(x)))


def _mod_body(c_ref, w_ref, b_ref, o_ref):
    o_ref[...] = _dot3(_silu(c_ref[...]), w_ref[...]) + b_ref[...]


def _modulation(cvec, mod_w, mod_b):
    depth, d, e = mod_w.shape
    tn = e
    return pl.pallas_call(
        _mod_body,
        grid=(depth, e // tn),
        in_specs=[
            pl.BlockSpec((MOD_ROWS, d), lambda l, j: (0, 0)),
            pl.BlockSpec((None, d, tn), lambda l, j: (l, 0, j)),
            pl.BlockSpec((None, 1, tn), lambda l, j: (l, 0, j)),
        ],
        out_specs=pl.BlockSpec((None, MOD_ROWS, tn), lambda l, j: (l, 0, j)),
        out_shape=jax.ShapeDtypeStruct((depth, MOD_ROWS, e), F32),
        compiler_params=_params("parallel", "parallel"),
        name="modulation",
    )(cvec, mod_w, mod_b.reshape(depth, 1, e))


def _phase_permutation(rows):
    order = np.concatenate([np.arange(0, rows, 2), np.arange(1, rows, 2)])
    return jnp.asarray(np.eye(rows, dtype=np.float32)[order], BF16)


def _proj_body(*refs, n_y, has_res, final, has_next, split, n_w, conv_seq):
    it = iter(refs)
    take = lambda n: [next(it) for _ in range(n)]
    if has_res:
        y_refs = take(n_y)
        unperm_ref = next(it) if n_y == 2 else None
        w_out_ref, gate_ref = take(2)
        fg_ref = next(it) if final else None
    x_ref = next(it)
    if has_next:
        g_ref, mod_ref = take(2)
        perm_ref = next(it) if split else None
        w_refs = take(n_w)
        convw_ref, convb_ref = take(2) if conv_seq else (None, None)
    x_out_ref = next(it) if has_res else None
    o_refs = take(n_w) if has_next else []

    x = x_ref[...]
    d = x.shape[-1]
    if has_res:
        if n_y == 2:
            y = _dot(unperm_ref[...], jnp.concatenate([r[...] for r in y_refs], axis=0)).astype(BF16)
        else:
            y = y_refs[0][...]
        x = x + gate_ref[...][:, 2 * d:] * _dot(y, w_out_ref[...])
        if final:
            x_out_ref[...] = x * lax.rsqrt(jnp.mean(x * x, axis=-1, keepdims=True) + EPS) * fg_ref[...]
        else:
            x_out_ref[...] = x
    if has_next:
        h = x * lax.rsqrt(jnp.mean(x * x, axis=-1, keepdims=True) + EPS) * g_ref[...]
        mod = mod_ref[...]
        hb = (h * (1.0 + mod[:, d:2 * d]) + mod[:, :d]).astype(BF16)
        if split:
            hb = _dot(perm_ref[...], hb).astype(BF16)
        for w_ref, o_ref in zip(w_refs, o_refs):
            res = _dot(hb, w_ref[...])
            if split:
                half, n = res.shape[0] // 2, res.shape[1]
                o_ref[:, :n] = res[:half]
                o_ref[:, n:] = res[half:]
            elif conv_seq:
                n_conv = convw_ref.shape[-1]
                w0, w1, w2, b = convw_ref[0:1], convw_ref[1:2], convw_ref[2:3], convb_ref[...]
                for s0 in range(0, res.shape[0], conv_seq):
                    u = res[s0:s0 + conv_seq, :n_conv]
                    o_ref[s0:s0 + conv_seq, :n_conv] = (
                        _shift_down(u, 0.0) * w0 + u * w1 + _shift_up(u, 0.0) * w2 + b)
                o_ref[:, n_conv:] = res[:, n_conv:]
            else:
                o_ref[...] = res


def _proj(x, mod4, mod_row, tm, name, res=None, nxt=None, final_g=None, conv=None):
    t, d = x.shape
    has_res, has_next, final = res is not None, nxt is not None, final_g is not None
    mod_spec = lambda layer: pl.BlockSpec((None, None, 1, mod4.shape[-1]), lambda i: (layer, mod_row(i), 0, 0))
    in_specs, args, out_specs, out_shape = [], [], [], []
    n_y, split, n_w, conv_seq = 0, False, 0, 0
    if has_res:
        ys, w_out, j, layer = res
        n_y = len(ys)
        in_specs += [pl.BlockSpec((tm // n_y, y.shape[-1]), lambda i: (i, 0)) for y in ys]
        args += list(ys)
        if n_y == 2:
            in_specs.append(pl.BlockSpec((tm, tm), lambda i: (0, 0)))
            args.append(_phase_permutation(tm).T)
        in_specs += [pl.BlockSpec((None,) + w_out.shape[1:], lambda i: (j, 0, 0), pipeline_mode=pl.Buffered(1)),
                     mod_spec(layer)]
        args += [w_out, mod4]
        if final:
            in_specs.append(pl.BlockSpec((1, d), lambda i: (0, 0)))
            args.append(final_g.reshape(1, d))
        out_specs.append(pl.BlockSpec((tm, d), lambda i: (i, 0)))
        out_shape.append(jax.ShapeDtypeStruct((t, d), F32))
    in_specs.append(pl.BlockSpec((tm, d), lambda i: (i, 0)))
    args.append(x)
    if has_next:
        norm_g3, next_layer, weights, split = nxt
        n_w = len(weights)
        in_specs += [pl.BlockSpec((None, 1, d), lambda i: (next_layer, 0, 0)), mod_spec(next_layer)]
        args += [norm_g3, mod4]
        if split:
            in_specs.append(pl.BlockSpec((tm, tm), lambda i: (0, 0)))
            args.append(_phase_permutation(tm))
        in_specs += [pl.BlockSpec(bs, functools.partial(lambda idx, i: idx, idx), pipeline_mode=pl.Buffered(1))
                     for _, bs, idx in weights]
        args += [w for w, _, _ in weights]
        if conv is not None:
            conv_w, conv_b3, conv_j, conv_seq = conv
            assert n_w == 1 and not split and tm % conv_seq == 0
            in_specs += [pl.BlockSpec((None,) + conv_w.shape[1:], lambda i: (conv_j, 0, 0)),
                         pl.BlockSpec((None,) + conv_b3.shape[1:], lambda i: (conv_j, 0, 0))]
            args += [conv_w, conv_b3]
        p = 2 if split else 1
        widths = [bs[-1] for _, bs, _ in weights]
        out_specs += [pl.BlockSpec((tm // p, p * n), lambda i: (i, 0)) for n in widths]
        out_shape += [jax.ShapeDtypeStruct((t // p, p * n), F32) for n in widths]
    return pl.pallas_call(
        functools.partial(_proj_body, n_y=n_y, has_res=has_res, final=final, has_next=has_next,
                          split=split, n_w=n_w, conv_seq=conv_seq),
        grid=(t // tm,),
        in_specs=in_specs,
        out_specs=out_specs,
        out_shape=out_shape,
        compiler_params=_params("parallel"),
        name=name,
    )(*args)


GLA_CHUNK = 64
GLA_BLOCK = 256
GLA_HEADS_PER_STEP = 2
LOG2_E = math.log2(math.e)
GLA_CLAMP_SPLIT = 8.0 * LOG2_E
GLA_CLAMP_HALF = 80.0 * LOG2_E


def _gla_tables(rows, dk):
    c, hc = GLA_CHUNK, GLA_CHUNK // 2
    r = np.arange(rows)[:, None]
    s = np.arange(rows)[None, :]
    same_chunk = (r // c) == (s // c)
    same_half = (r // hc) == (s // hc)
    tri = np.stack([same_chunk & (s <= r), same_chunk & (s >= r)])
    m_half = np.stack([same_half & (s < r), same_half & (s > r)])
    sel = np.concatenate([s == (r // c) * c + hc, s == (r // hc) * hc + hc // 2], axis=0)
    second = np.broadcast_to((r // hc) % 2 == 1, (rows, dk))
    as_const = lambda m, dt: jnp.asarray(m.astype(np.float32), dt)
    return (as_const(tri, BF16), as_const(sel, BF16), as_const(same_chunk, F32), as_const(m_half, F32),
            as_const(second, F32), as_const(2.0 * (s == r), F32))


def _gla_body(q_ref, k_ref, v_ref, r_ref, lr_ref, wdec_ref, bdec_ref, og_ref, tri_ref, sel_ref,
              mchunk_ref, mhalf_ref, second_ref, diag2_ref, *rest, seq, layer, has_s0, want_state,
              has_states_in):
    rest = list(rest)
    s0_ref = rest.pop(0) if has_s0 else None
    if has_states_in:
        rest.pop(0)
    y_ref = rest.pop(0)
    sfin_ref = rest.pop(0) if want_state else None
    o_ref, qi_ref, cum_ref, st_ref = rest
    chunk = GLA_CHUNK
    n_chunks = seq // chunk
    blk = tri_ref.shape[-1]
    n_heads = wdec_ref.shape[0]
    dk = q_ref.shape[-1] // n_heads
    dv = v_ref.shape[-1] // n_heads
    scale = dk ** -0.5
    heads = range(n_heads)
    dirs = range(2)
    chains = [(hh, d) for hh in heads for d in dirs]

    def nt(a, b):
        return lax.dot_general(a, b, _NT, preferred_element_type=F32)

    def cols(hh, width):
        return slice(hh * width, (hh + 1) * width)

    def block_pass(bi, carry):
        rows = pl.ds(pl.multiple_of(bi * blk, blk), blk)
        q = [q_ref[rows, cols(hh, dk)] * scale for hh in heads]
        k = [k_ref[rows, cols(hh, dk)] for hh in heads]
        lr = lr_ref[rows, :]
        g = [_log_sigmoid(_dot3(lr, wdec_ref[hh]) + bdec_ref[hh]) * (LOG2_E / GLA_GATE_NORM) for hh in heads]
        cum = {(hh, d): _dot_exact_lhs(tri_ref[d], g[hh][:, cols(d, dk)]) for hh, d in chains}
        refs = [_dot(sel_ref[...], jnp.concatenate([cum[hh, 0], cum[hh, 1]], axis=1).astype(BF16))
                for hh in heads]
        second = second_ref[...]
        first = 1.0 - second
        p_split, p_half = [], {}
        for hh in heads:
            q_in = (q[hh] * second, q[hh] * first)
            k_in = (k[hh] * first, k[hh] * second)
            q1, k1 = [], []
            for d in dirs:
                c = (hh, d)
                d_split = cum[c] - refs[hh][:blk, cols(d, dk)]
                d_half = cum[c] - refs[hh][blk:, cols(d, dk)]
                q1.append((q_in[d] * jnp.exp2(jnp.minimum(d_split, GLA_CLAMP_SPLIT))).astype(BF16))
                k1.append((k_in[d] * jnp.exp2(jnp.minimum(-d_split, GLA_CLAMP_SPLIT))).astype(BF16))
                q0 = (q[hh] * jnp.exp2(jnp.minimum(d_half, GLA_CLAMP_HALF))).astype(BF16)
                k0 = (k[hh] * jnp.exp2(jnp.minimum(-d_half, GLA_CLAMP_HALF))).astype(BF16)
                p_half[c] = nt(q0, k0)
                qi_ref[hh, d, rows, :] = (q[hh] * jnp.exp2(cum[c])).astype(BF16)
                cum_ref[hh, d, rows, :] = cum[c]
            p_split.append(nt(jnp.concatenate(q1, axis=1), jnp.concatenate(k1, axis=1)))
        for hh in heads:
            att = p_split[hh] * mchunk_ref[...] + diag2_ref[...] * jnp.sum(q[hh] * k[hh], axis=-1, keepdims=True)
            for d in dirs:
                att = att + jnp.where(mhalf_ref[d] > 0.5, p_half[hh, d], 0.0)
            o_ref[rows, cols(hh, dv)] = _dot(att.astype(BF16), v_ref[rows, cols(hh, dv)].astype(BF16))
        return carry

    lax.fori_loop(0, seq // blk, block_pass, 0)

    for hh, d in chains:
        st_ref[hh, d] = s0_ref[d, hh] if has_s0 else jnp.zeros((dk, dv), F32)

    def chunk_step(i, carry):
        for hh, d in chains:
            c = i if d == 0 else n_chunks - 1 - i
            sl = pl.ds(pl.multiple_of(c * chunk, chunk), chunk)
            cum = cum_ref[hh, d, sl, :]
            blast = cum[chunk - 1:chunk] if d == 0 else cum[0:1]
            kh = (k_ref[sl, cols(hh, dk)] * jnp.exp2(blast - cum)).astype(BF16)
            vb = v_ref[sl, cols(hh, dv)].astype(BF16)
            st = st_ref[hh, d]
            o_ref[sl, cols(hh, dv)] += _dot(qi_ref[hh, d, sl, :], st.astype(BF16))
            decay = jnp.broadcast_to(jnp.exp2(blast), (dk, dk)).T
            st_ref[hh, d] = (jnp.concatenate([decay] * (dv // dk), axis=1) * st
                             + lax.dot_general(kh, vb, _TN, preferred_element_type=F32))
        return carry

    lax.fori_loop(0, n_chunks, chunk_step, 0, unroll=min(n_chunks, 8))

    def finish(jb, carry):
        sl = pl.ds(pl.multiple_of(jb * blk, blk), blk)
        for hh in heads:
            o = o_ref[sl, cols(hh, dv)]
            o = o * lax.rsqrt(jnp.mean(o * o, axis=-1, keepdims=True) + EPS) * og_ref[...]
            y_ref[sl, cols(hh, dv)] = (o * _silu(r_ref[sl, cols(hh, dv)])).astype(BF16)
        return carry

    lax.fori_loop(0, seq // blk, finish, 0)
    if want_state:
        own = sfin_ref if has_states_in else sfin_ref.at[layer]
        for hh, d in chains:
            own[d, hh] = st_ref[hh, d]
        if not has_states_in:
            for other in range(sfin_ref.shape[0]):
                if other != layer:
                    sfin_ref[other] = jnp.zeros(sfin_ref.shape[1:], F32)


def _gla_mixer(proj, lr, w_dec, b_dec, onorm_g, s0, j, n_seq, seq, want_state, states, name):
    t = proj.shape[0]
    h = GLA_HEADS
    key = w_dec.shape[-1]
    dk = key // h
    val = (proj.shape[1] - 2 * key) // 2
    dv = val // h
    has_s0 = s0 is not None
    tables = _gla_tables(min(seq, GLA_BLOCK), dk)
    fixed = lambda a: pl.BlockSpec(a.shape, lambda b, hp: (0,) * a.ndim)
    hps = h if seq <= GLA_BLOCK else GLA_HEADS_PER_STEP
    wk, wv = hps * dk, hps * dv
    in_specs = [
        pl.BlockSpec((seq, wk), lambda b, hp: (b, hp)),
        pl.BlockSpec((seq, wk), lambda b, hp: (b, key // wk + hp)),
        pl.BlockSpec((seq, wv), lambda b, hp: (b, (2 * key) // wv + hp)),
        pl.BlockSpec((seq, wv), lambda b, hp: (b, (2 * key + val) // wv + hp)),
        pl.BlockSpec((seq, 2 * GLA_RANK), lambda b, hp: (b, 0)),
        pl.BlockSpec((None, hps, 2 * GLA_RANK, 2 * dk), lambda b, hp: (j, hp, 0, 0)),
        pl.BlockSpec((None, hps, 1, 2 * dk), lambda b, hp: (j, hp, 0, 0)),
        pl.BlockSpec((None, 1, dv), lambda b, hp: (j, 0, 0)),
    ] + [fixed(a) for a in tables]
    n_gla = w_dec.shape[0]
    w5 = w_dec.reshape(n_gla, 2, GLA_RANK, h, dk)
    zero = jnp.zeros_like(w5[:, 0])
    w_blk = jnp.concatenate([jnp.concatenate([w5[:, 0], zero], axis=-1),
                             jnp.concatenate([zero, w5[:, 1]], axis=-1)], axis=1).transpose(0, 2, 1, 3)
    b_cat = b_dec.reshape(n_gla, 2, h, dk).transpose(0, 2, 1, 3).reshape(n_gla, h, 1, 2 * dk)
    args = [proj, proj, proj, proj, lr, w_blk, b_cat, onorm_g.reshape(n_gla, 1, dv), *tables]
    if has_s0:
        in_specs.append(pl.BlockSpec((None, None, 2, hps, dk, dv), lambda b, hp: (b, j, 0, hp, 0, 0)))
        args.append(s0)
    out_specs = [pl.BlockSpec((seq, wv), lambda b, hp: (b, hp))]
    out_shape = [jax.ShapeDtypeStruct((t, val), BF16)]
    aliases = {}
    if want_state:
        out_shape.append(jax.ShapeDtypeStruct((n_seq, n_gla, 2, h, dk, dv), F32))
        if states is None:
            out_specs.append(pl.BlockSpec((None, n_gla, 2, hps, dk, dv), lambda b, hp: (b, 0, 0, hp, 0, 0)))
        else:
            out_specs.append(pl.BlockSpec((None, None, 2, hps, dk, dv), lambda b, hp: (b, j, 0, hp, 0, 0)))
            in_specs.append(pl.BlockSpec(memory_space=pl.ANY))
            args.append(states)
            aliases = {len(args) - 1: 1}
    return pl.pallas_call(
        functools.partial(_gla_body, seq=seq, layer=j, has_s0=has_s0, want_state=want_state,
                          has_states_in=bool(aliases)),
        grid=(n_seq, h // hps),
        in_specs=in_specs,
        out_specs=out_specs,
        out_shape=out_shape,
        input_output_aliases=aliases,
        scratch_shapes=[pltpu.VMEM((seq, wv), F32), pltpu.VMEM((hps, 2, seq, dk), BF16),
                        pltpu.VMEM((hps, 2, seq, dk), F32), pltpu.VMEM((hps, 2, dk, dv), F32)],
        compiler_params=_params("parallel", "parallel"),
        name=name,
    )(*args)


def _dft_tables(n):
    f = np.arange(n, dtype=np.int64)
    ang = (np.outer(f, f) % n).astype(np.float64) * (2.0 * math.pi / n)
    return _trig_bf16(ang)


def _fnet_body(x_ref, z_ref, cc_ref, sc_ref, cl_ref, sl_ref, y_ref, *, seq, n_seq, gc):
    inv = 1.0 / math.sqrt(seq * gc)
    for g in range(x_ref.shape[-1] // gc):
        cols = slice(g * gc, (g + 1) * gc)
        xb = x_ref[:, cols].astype(BF16)
        a = _dot(xb, cc_ref[...]).astype(BF16)
        bm = _dot(xb, sc_ref[...]).astype(BF16)
        for i in range(n_seq):
            rows = slice(i * seq, (i + 1) * seq)
            f = _dot(cl_ref[...], a[rows]) - _dot(sl_ref[...], bm[rows])
            y_ref[rows, cols] = (f * inv * _silu(z_ref[rows, cols])).astype(BF16)


def _fnet_mixer(proj, n_seq_total, seq, seqs_per_step, groups_per_step, name):
    t = proj.shape[0]
    width = proj.shape[1] // 2
    gc = width // FN_GROUPS
    cc, sc = _dft_tables(gc)
    cl, sl = _dft_tables(seq)
    rows = seq * seqs_per_step
    wide = gc * groups_per_step
    n_wide = width // wide
    const = pl.Buffered(1)
    return pl.pallas_call(
        functools.partial(_fnet_body, seq=seq, n_seq=seqs_per_step, gc=gc),
        grid=(n_seq_total // seqs_per_step, n_wide),
        in_specs=[
            pl.BlockSpec((rows, wide), lambda i, g: (i, g)),
            pl.BlockSpec((rows, wide), lambda i, g: (i, n_wide + g)),
            pl.BlockSpec((gc, gc), lambda i, g: (0, 0), pipeline_mode=const),
            pl.BlockSpec((gc, gc), lambda i, g: (0, 0), pipeline_mode=const),
            pl.BlockSpec((seq, seq), lambda i, g: (0, 0), pipeline_mode=const),
            pl.BlockSpec((seq, seq), lambda i, g: (0, 0), pipeline_mode=const),
        ],
        out_specs=pl.BlockSpec((rows, wide), lambda i, g: (i, g)),
        out_shape=jax.ShapeDtypeStruct((t, width), BF16),
        compiler_params=_params("parallel", "parallel"),
        name=name,
    )(proj, proj, cc, sc, cl, sl)


def _signal_tables(n):
    o = 2 * np.arange(n, dtype=np.int64) + 1
    ang = (np.outer(o, o) % (8 * n)).astype(np.float64) * (math.pi / (4 * n))
    cos, sin = _trig_bf16(ang)
    return cos, -sin


def _filter_tables(n, split):
    f = np.arange(n, dtype=np.int64)
    if split:
        f = np.concatenate([f[:n // 2], f[:n // 2 - 1:-1]])
    f = 2 * f + 1
    o = np.arange(n, dtype=np.int64)
    ang = (np.outer(f, o) % (4 * n)).astype(np.float64) * (math.pi / (2 * n))
    return _trig_bf16(ang)


def _hyena_positions(seq, bands, pad_to):
    t = np.linspace(0.0, 1.0, seq)[:, None]
    w = 2.0 * math.pi * np.arange(seq)[:, None] / seq
    f = np.linspace(1e-4, bands - 1, bands)[None, :]
    zpos = np.concatenate([t, np.cos(f * w), -np.sin(f * w)], axis=-1)
    zpos = np.pad(zpos, ((0, 0), (0, pad_to - zpos.shape[1])))
    return jnp.asarray(zpos, F32), jnp.asarray(t, F32)


def _hyfilt_body(zpos_ref, w1_ref, b1_ref, w2_ref, b2_ref, w3_ref, b3_ref, fr_ref, w4f_ref,
                 w4b_ref, t_ref, del_ref, ch_ref, sh_ref, *rest, split):
    if split:
        c2_ref, s2_ref = rest[:2]
        out_refs, a3_ref = rest[2:-1], rest[-1]
    else:
        out_refs, a3_ref = rest[:-1], rest[-1]

    @pl.when(pl.program_id(0) == 0)
    def _():
        fr = fr_ref[...]
        a = jnp.sin(fr * (_dot3(zpos_ref[...], w1_ref[...]) + b1_ref[...]))
        a = jnp.sin(fr * (_dot3(a, w2_ref[...]) + b2_ref[...]))
        a3_ref[...] = _dot3_lhs(jnp.sin(fr * (_dot3(a, w3_ref[...]) + b3_ref[...])))

    a3 = a3_ref[...]
    dec = jnp.exp(-t_ref[...] * del_ref[...]) * (1.0 / t_ref.shape[0])
    fwd = _dot(a3, _dot3_rhs(w4f_ref[...])) * dec
    bwd = _dot(a3, _dot3_rhs(w4b_ref[...])) * dec
    first = lax.broadcasted_iota(jnp.int32, bwd.shape, 0) == 0
    bwd = jnp.where(first, 0.0, bwd)
    hre = _dot(ch_ref[...], (fwd + bwd).astype(BF16))
    him = _dot(sh_ref[...], (bwd - fwd).astype(BF16))
    if not split:
        out_refs[0][...] = hre
        out_refs[1][...] = him
        return
    half = hre.shape[0] // 2
    plus_r, plus_i = hre[:half] + hre[half:], him[:half] - him[half:]
    minus_r, minus_i = hre[:half] - hre[half:], him[:half] + him[half:]
    c2, s2 = c2_ref[...], s2_ref[...]
    for ref, val in zip(out_refs, (plus_r, plus_i,
                                   c2 * minus_r + s2 * minus_i, c2 * minus_i - s2 * minus_r,
                                   c2 * minus_r - s2 * minus_i, c2 * minus_i + s2 * minus_r)):
        ref[...] = val


def _hyena_filters(seq, j, emb, w1p, b1, w2, b2, w3, b3, w4, freq, ce, split, name):
    ffn = w2.shape[-1]
    width = w4.shape[-1] // (2 * HY_ORDER)
    emb_pad = w1p.shape[1]
    zpos, tcol = _hyena_positions(seq, (emb - 1) // 2, emb_pad)
    deltas = jnp.asarray(np.abs(np.linspace(HY_MIN_DECAY, HY_MAX_DECAY, width, dtype=np.float32))[None, :])
    ch, sh = _filter_tables(seq, split)
    nb = width // ce
    const = pl.Buffered(1)
    vec = lambda: pl.BlockSpec((None, 1, ffn), lambda c: (j, 0, 0))
    out_rows, n_out = (seq // 2, 6) if split else (seq, 2)
    twiddles, twiddle_specs = [], []
    if split:
        ang = np.broadcast_to((math.pi * (2 * np.arange(out_rows) + 1) / (2 * seq))[:, None], (out_rows, ce))
        twiddles = [jnp.asarray(np.cos(ang), F32), jnp.asarray(np.sin(ang), F32)]
        twiddle_specs = [pl.BlockSpec((out_rows, ce), lambda c: (0, 0), pipeline_mode=const)] * 2
    return pl.pallas_call(
        functools.partial(_hyfilt_body, split=split),
        grid=(HY_ORDER * nb,),
        in_specs=[
            pl.BlockSpec((seq, emb_pad), lambda c: (0, 0)),
            pl.BlockSpec((None, emb_pad, ffn), lambda c: (j, 0, 0)),
            vec(),
            pl.BlockSpec((None, ffn, ffn), lambda c: (j, 0, 0)),
            vec(),
            pl.BlockSpec((None, ffn, ffn), lambda c: (j, 0, 0)),
            vec(),
            vec(),
            pl.BlockSpec((None, ffn, ce), lambda c: (j, 0, (c // nb) * 2 * nb + c % nb)),
            pl.BlockSpec((None, ffn, ce), lambda c: (j, 0, (c // nb) * 2 * nb + nb + c % nb)),
            pl.BlockSpec((seq, 1), lambda c: (0, 0)),
            pl.BlockSpec((1, ce), lambda c: (0, c % nb)),
            pl.BlockSpec((seq, seq), lambda c: (0, 0), pipeline_mode=const),
            pl.BlockSpec((seq, seq), lambda c: (0, 0), pipeline_mode=const),
        ] + twiddle_specs,
        out_specs=[pl.BlockSpec((out_rows, ce), lambda c: (0, c))] * n_out,
        out_shape=[jax.ShapeDtypeStruct((out_rows, HY_ORDER * width), F32)] * n_out,
        scratch_shapes=[pltpu.VMEM((seq, 3 * ffn), BF16)],
        compiler_params=_params("arbitrary"),
        name=name,
    )(zpos, w1p, b1, w2, b2, w3, b3, freq, w4, w4, tcol, deltas, ch, sh, *twiddles)


SUBLANES = 8


def _shift_down(u, first):
    rolled = pltpu.roll(u, 1, 0)
    row = lax.broadcasted_iota(jnp.int32, (SUBLANES, 1), 0)
    return jnp.concatenate([jnp.where(row == 0, first, rolled[:SUBLANES]), rolled[SUBLANES:]], axis=0)


def _shift_up(u, last):
    n = u.shape[0]
    rolled = pltpu.roll(u, n - 1, 0)
    row = lax.broadcasted_iota(jnp.int32, (SUBLANES, 1), 0)
    tail = jnp.where(row == SUBLANES - 1, last, rolled[n - SUBLANES:])
    return jnp.concatenate([rolled[:n - SUBLANES], tail], axis=0)


def _hyena_body(x1_ref, x2_ref, v_ref, z_ref, cw1_ref, cw2_ref, cwv_ref, cb1_ref, cb2_ref, cbv_ref,
                hre0_ref, him0_ref, hre1_ref, him1_ref, d_ref, cq_ref, sq_ref, y_ref,
                cur_ref, cb_ref, yr_ref, yi_ref, *, seq, rows, conv_done):
    n_rb = seq // rows

    def block(rb):
        return pl.ds(pl.multiple_of(rb * rows, rows), rows)

    def short_conv(u_ref, w_ref, b_ref, rb):
        u = u_ref[block(rb), :]
        if conv_done:
            return u
        before = u_ref[pl.ds(jnp.maximum(rb * rows - 1, 0), 1), :]
        after = u_ref[pl.ds(jnp.minimum((rb + 1) * rows, seq - 1), 1), :]
        before = jnp.where(rb == 0, 0.0, before)
        after = jnp.where(rb == n_rb - 1, 0.0, after)
        prev, nxt = _shift_down(u, before), _shift_up(u, after)
        return prev * w_ref[0:1] + u * w_ref[1:2] + nxt * w_ref[2:3] + b_ref[...]

    def long_conv_gate(x_ref, cw_ref, cbias_ref, hre_ref, him_ref, dn):
        def fwd(rb, carry):
            sl = block(rb)
            ur = _dot(cq_ref[sl, :], cb_ref[...])
            ui = _dot(sq_ref[sl, :], cb_ref[...])
            hr = hre_ref[sl, :]
            hi = him_ref[sl, :]
            yr_ref[sl, :] = (ur * hr - ui * hi).astype(BF16)
            yi_ref[sl, :] = (ur * hi + ui * hr).astype(BF16)
            return carry

        lax.fori_loop(0, n_rb, fwd, 0)

        def inv(rb, carry):
            sl = block(rb)
            c = _dot(cq_ref[sl, :], yr_ref[...]) + _dot(sq_ref[sl, :], yi_ref[...])
            new = short_conv(x_ref, cw_ref, cbias_ref, rb) * (c + cur_ref[sl, :] * dn)
            cur_ref[sl, :] = new
            cb_ref[sl, :] = new.astype(BF16)
            return carry

        lax.fori_loop(0, n_rb, inv, 0)

    def load_v(rb, carry):
        v = short_conv(v_ref, cwv_ref, cbv_ref, rb)
        cur_ref[block(rb), :] = v
        cb_ref[block(rb), :] = v.astype(BF16)
        return carry

    lax.fori_loop(0, n_rb, load_v, 0)
    long_conv_gate(x1_ref, cw1_ref, cb1_ref, hre0_ref, him0_ref, d_ref[0:1])
    long_conv_gate(x2_ref, cw2_ref, cb2_ref, hre1_ref, him1_ref, d_ref[1:2])

    def store(rb, carry):
        sl = block(rb)
        y_ref[sl, :] = (cur_ref[sl, :] * _silu(z_ref[sl, :])).astype(BF16)
        return carry

    lax.fori_loop(0, n_rb, store, 0)


def _hyena_split_body(x1_refs, x2_refs, v_refs, z_refs, cw1_ref, cw2_ref, cwv_ref, cb1_ref, cb2_ref, cbv_ref,
                      k_refs, d_ref, cq_ref, sq_ref, y_refs, cur_ref, sig_ref, z_buf, *, seq, rows):
    half = seq // 2
    n_rb = half // rows

    def block(rb, shift=0):
        return slice(rb * rows + shift, (rb + 1) * rows + shift)

    def short_conv(u_refs, w_ref, b_ref, rb):
        e_ref, o_ref = u_refs
        even, odd = e_ref[block(rb), :], o_ref[block(rb), :]
        odd_before = o_ref[block(rb, -1), :] if rb > 0 else _shift_down(odd, 0.0)
        even_after = e_ref[block(rb, 1), :] if rb < n_rb - 1 else _shift_up(even, 0.0)
        w0, w1, w2, b = w_ref[0:1], w_ref[1:2], w_ref[2:3], b_ref[...]
        return (odd_before * w0 + even * w1 + odd * w2 + b, even * w0 + odd * w1 + even_after * w2 + b)

    def long_conv_gate(x_ref, cw_ref, cbias_ref, k, dn):
        k1r, k1i, k2r, k2i, k3r, k3i = k
        for fb in range(n_rb):
            sl = block(fb)
            c, s = cq_ref[sl, :], sq_ref[sl, :]
            er, ei = _dot(c, sig_ref[0]), _dot(s, sig_ref[0])
            orr, oi = _dot(c, sig_ref[1]), _dot(s, sig_ref[1])
            a1r, a1i, a2r, a2i, a3r, a3i = (r[sl, :] for r in (k1r, k1i, k2r, k2i, k3r, k3i))
            z_buf[0, sl, :] = (er * a1r - ei * a1i + orr * a2r - oi * a2i).astype(BF16)
            z_buf[1, sl, :] = (er * a1i + ei * a1r + orr * a2i + oi * a2r).astype(BF16)
            z_buf[2, sl, :] = (er * a3r - ei * a3i + orr * a1r - oi * a1i).astype(BF16)
            z_buf[3, sl, :] = (er * a3i + ei * a3r + orr * a1i + oi * a1r).astype(BF16)
        for tb in range(n_rb):
            sl = block(tb)
            c, s = cq_ref[sl, :], sq_ref[sl, :]
            conv = (_dot(c, z_buf[0]) + _dot(s, z_buf[1]), _dot(c, z_buf[2]) + _dot(s, z_buf[3]))
            gate = short_conv(x_ref, cw_ref, cbias_ref, tb)
            for p in range(2):
                new = gate[p] * (conv[p] + cur_ref[p, sl, :] * dn)
                cur_ref[p, sl, :] = new
                sig_ref[p, sl, :] = new.astype(BF16)

    for rb in range(n_rb):
        for p, val in enumerate(short_conv(v_refs, cwv_ref, cbv_ref, rb)):
            cur_ref[p, block(rb), :] = val
            sig_ref[p, block(rb), :] = val.astype(BF16)
    long_conv_gate(x1_refs, cw1_ref, cb1_ref, k_refs[0], d_ref[0:1])
    long_conv_gate(x2_refs, cw2_ref, cb2_ref, k_refs[1], d_ref[1:2])
    for rb in range(n_rb):
        for p in range(2):
            sl = block(rb)
            y_refs[p][sl, :] = (cur_ref[p, sl, :] * _silu(z_refs[p][sl, :])).astype(BF16)


def _hyena_split_entry(*refs, seq, rows):
    it = iter(refs)
    take = lambda n: tuple(next(it) for _ in range(n))
    x1, x2, v, z = take(2), take(2), take(2), take(2)
    conv_refs = take(6)
    k = tuple(take(6) for _ in range(HY_ORDER))
    d, cq, sq = take(3)
    y = take(2)
    _hyena_split_body(x1, x2, v, z, *conv_refs, k, d, cq, sq, y, *it, seq=seq, rows=rows)


def _hyena_mixer(proj, conv_w, conv_b, filt, d_skip, j, n_seq, seq, ce, split, conv_done, name):
    phases = 2 if split else 1
    t = proj.shape[0] * phases
    width = proj.shape[1] // (4 * phases)
    nb = width // ce
    n_dft = seq // phases
    cq, sq = _signal_tables(n_dft)
    rows = min(n_dft, HY_ROW_BLOCK)
    const = pl.Buffered(1)
    seg = lambda s: pl.BlockSpec((seq, ce), lambda e, b: (b, s * nb + e))
    cw = lambda s: pl.BlockSpec((None, conv_w.shape[1], ce), lambda e, b: (j, 0, s * nb + e))
    cb = lambda s: pl.BlockSpec((None, 1, ce), lambda e, b: (j, 0, s * nb + e))
    hs = lambda n: pl.BlockSpec((n_dft, ce), lambda e, b: (0, n * nb + e),
                                pipeline_mode=const if nb == 1 else pl.Buffered(2))
    conv_b3 = conv_b.reshape(conv_b.shape[0], 1, -1)
    if split:
        body = functools.partial(_hyena_split_entry, seq=seq, rows=rows)
        phase = lambda s, p: pl.BlockSpec((n_dft, ce), lambda e, b: (b, p * 4 * nb + s * nb + e))
        x_specs = [phase(s, p) for s in range(4) for p in range(2)]
        out_specs = [pl.BlockSpec((n_dft, ce), lambda e, b: (b, e))] * 2
        out_shape = [jax.ShapeDtypeStruct((t // 2, width), BF16)] * 2
        scratch = [pltpu.VMEM((2, n_dft, ce), F32), pltpu.VMEM((2, n_dft, ce), BF16),
                   pltpu.VMEM((4, n_dft, ce), BF16)]
    else:
        body = functools.partial(_hyena_body, seq=seq, rows=rows, conv_done=conv_done)
        x_specs = [seg(0), seg(1), seg(2), seg(3)]
        out_specs = pl.BlockSpec((seq, ce), lambda e, b: (b, e))
        out_shape = jax.ShapeDtypeStruct((t, width), BF16)
        scratch = [pltpu.VMEM((seq, ce), F32), pltpu.VMEM((seq, ce), BF16),
                   pltpu.VMEM((seq, ce), BF16), pltpu.VMEM((seq, ce), BF16)]
    return pl.pallas_call(
        body,
        grid=(nb, n_seq),
        in_specs=x_specs + [cw(0), cw(1), cw(2), cb(0), cb(1), cb(2)]
        + [hs(n) for n in range(HY_ORDER) for _ in filt]
        + [pl.BlockSpec((None, HY_ORDER, ce), lambda e, b: (j, 0, e)),
           pl.BlockSpec((n_dft, n_dft), lambda e, b: (0, 0), pipeline_mode=const),
           pl.BlockSpec((n_dft, n_dft), lambda e, b: (0, 0), pipeline_mode=const)],
        out_specs=out_specs,
        out_shape=out_shape,
        scratch_shapes=scratch,
        compiler_params=_params("parallel", "parallel"),
        name=name,
    )(*([proj] * len(x_specs)), conv_w, conv_w, conv_w, conv_b3, conv_b3, conv_b3,
      *(list(filt) * HY_ORDER), d_skip, cq, sq)


def kernel(x_prompt, x_sample, state_gla, c, c_ctx, mod_w, mod_b, norm_g, final_norm_g, gla_w_in, gla_w_dec, gla_b_dec, gla_onorm_g, gla_w_out, fn_w_in, fn_w_out, hy_w_in, hy_conv_w, hy_conv_b, hy_ffn_w1, hy_ffn_b1, hy_ffn_w2, hy_ffn_b2, hy_ffn_w3, hy_ffn_b3, hy_ffn_w4, hy_freq, hy_d, hy_w_out):
    n_p, l_p, d = x_prompt.shape
    n_s, l_s, _ = x_sample.shape
    depth = mod_w.shape[0]
    key = gla_w_dec.shape[-1]
    n_main = gla_w_in.shape[-1] - 2 * GLA_RANK

    cvec = jnp.concatenate([c_ctx[None], c, jnp.zeros((MOD_ROWS - 1 - n_s, d), F32)], axis=0)
    mod4 = _modulation(cvec, mod_w, mod_b).reshape(depth, MOD_ROWS, 1, 3 * d)
    norm_g3 = norm_g.reshape(depth, 1, d)

    gla_w_main = gla_w_in.astype(BF16)
    gla_w_lr = gla_w_in[:, :, n_main:].astype(BF16)
    gla_w_out_b = gla_w_out.astype(BF16)
    fn_w_in_b = fn_w_in.astype(BF16)
    fn_w_out_b = fn_w_out.astype(BF16)
    hy_w_in_b = hy_w_in.astype(BF16)
    hy_w_out_b = hy_w_out.astype(BF16)
    emb = hy_ffn_w1.shape[1]
    emb_pad = -(-emb // 128) * 128
    hy_w1p = jnp.pad(hy_ffn_w1, ((0, 0), (0, emb_pad - emb), (0, 0)))
    ffn = hy_ffn_w2.shape[-1]
    hy_vecs = [a.reshape(a.shape[0], 1, ffn) for a in (hy_ffn_b1, hy_ffn_b2, hy_ffn_b3, hy_freq)]

    tm = 512
    tiles_per_sample = l_s // tm
    streams = {
        "p": dict(x=x_prompt.reshape(n_p * l_p, d), n=n_p, l=l_p, row=lambda i: 0),
        "s": dict(x=x_sample.reshape(n_s * l_s, d), n=n_s, l=l_s, row=lambda i: 1 + i // tiles_per_sample),
    }
    def in_weights(i, seq):
        kind, j = i % N_MIXERS, i // N_MIXERS
        if kind == 0:
            return [(gla_w_main, (None, d, n_main), (j, 0, 0)), (gla_w_lr, (None, d, 2 * GLA_RANK), (j, 0, 0))], False
        if kind == 1:
            return [(fn_w_in_b, (None,) + fn_w_in.shape[1:], (j, 0, 0))], False
        return [(hy_w_in_b, (None,) + hy_w_in.shape[1:], (j, 0, 0))], seq >= HY_SPLIT_MIN_SEQ

    hy_conv_b3 = hy_conv_b.reshape(hy_conv_b.shape[0], 1, -1)

    def in_conv(i, seq):
        if i % N_MIXERS == 2 and seq < HY_SPLIT_MIN_SEQ and tm % seq == 0:
            return (hy_conv_w, hy_conv_b3, i // N_MIXERS, seq)
        return None

    for tag, st in streams.items():
        weights, split = in_weights(0, st["l"])
        st["proj"] = _proj(st["x"], mod4, st["row"], tm, f"proj_{tag}0", nxt=(norm_g3, 0, weights, split),
                           conv=in_conv(0, st["l"]))
    new_states = None
    for i in range(depth):
        kind, j = i % N_MIXERS, i // N_MIXERS
        if kind == 2:
            filt = {}
            for tag, st in streams.items():
                ce = min(d, 256 if st["l"] > 256 else d)
                filt[tag] = _hyena_filters(st["l"], j, emb, hy_w1p, hy_vecs[0], hy_ffn_w2, hy_vecs[1], hy_ffn_w3,
                                           hy_vecs[2], hy_ffn_w4, hy_vecs[3], ce, st["l"] >= HY_SPLIT_MIN_SEQ,
                                           f"hyfilt_{tag}{i}")
        for tag, st in streams.items():
            n_seq, seq = st["n"], st["l"]
            if kind == 0:
                proj, lr = st["proj"]
                is_ctx = tag == "p"
                res = _gla_mixer(proj, lr, gla_w_dec, gla_b_dec, gla_onorm_g,
                                 None if is_ctx else state_gla, j, n_seq, seq, is_ctx, new_states,
                                 f"gla_{tag}{i}")
                ys = [res[0]]
                if is_ctx:
                    new_states = res[1]
                w_out = gla_w_out_b
            elif kind == 1:
                short = seq <= 256
                ys = [_fnet_mixer(st["proj"][0], n_seq, seq, 4 if short else 1, FN_GROUPS if short else 2,
                                  f"fnet_{tag}{i}")]
                w_out = fn_w_out_b
            else:
                split = seq >= HY_SPLIT_MIN_SEQ
                ce = min(d, 256 if seq > 256 else d)
                y = _hyena_mixer(st["proj"][0], hy_conv_w, hy_conv_b, filt[tag], hy_d, j, n_seq, seq, ce, split,
                                 in_conv(i, seq) is not None, f"hyena_{tag}{i}")
                ys = list(y) if split else [y]
                w_out = hy_w_out_b
            if i + 1 < depth:
                weights, split = in_weights(i + 1, seq)
                st["x"], *st["proj"] = _proj(st["x"], mod4, st["row"], tm, f"proj_{tag}{i + 1}",
                                             res=(ys, w_out, j, i), nxt=(norm_g3, i + 1, weights, split),
                                             conv=in_conv(i + 1, seq))
            else:
                (st["x"],) = _proj(st["x"], mod4, st["row"], tm, f"proj_{tag}{i + 1}",
                                   res=(ys, w_out, j, i), final_g=final_norm_g)
    y_prompt = streams["p"]["x"].reshape(n_p, l_p, d)
    y_sample = streams["s"]["x"].reshape(n_s, l_s, d)
    new_state_gla = new_states.astype(x_prompt.dtype)
    return (y_prompt, y_sample, new_state_gla)
```

```python
import functools
import math

import jax
import jax.numpy as jnp
import numpy as np
from jax import lax
from jax.experimental import pallas as pl
from jax.experimental.pallas import tpu as pltpu

F32 = jnp.float32
BF16 = jnp.bfloat16

EPS = 1e-6
N_MIXERS = 3
GLA_HEADS = 4
GLA_RANK = 16
GLA_GATE_NORM = 16.0
FN_GROUPS = 4
HY_ORDER = 2
HY_TARGET = 1e-2
HY_MIN_DECAY = math.log(HY_TARGET) / 1.5
HY_MAX_DECAY = math.log(HY_TARGET) / 0.3
HY_ROW_BLOCK = 512
HY_SPLIT_MIN_SEQ = 512

V7X_VMEM_BYTES = 64 * 1024 * 1024
VMEM_LIMIT_BYTES = V7X_VMEM_BYTES - 8 * 1024 * 1024
MOD_ROWS = 8

_NT = (((1,), (1,)), ((), ()))
_TN = (((0,), (0,)), ((), ()))


def _params(*sem):
    return pltpu.CompilerParams(dimension_semantics=sem, vmem_limit_bytes=VMEM_LIMIT_BYTES)


def _dot(a, b):
    return jnp.dot(a, b, preferred_element_type=F32)


def _split3(x):
    hi = x.astype(BF16)
    r1 = x - hi.astype(F32)
    mid = r1.astype(BF16)
    lo = (r1 - mid.astype(F32)).astype(BF16)
    return hi, mid, lo


def _dot3(a, b):
    return _dot(_dot3_lhs(a), _dot3_rhs(b))


def _dot3_lhs(a):
    ah = a.astype(BF16)
    al = (a - ah.astype(F32)).astype(BF16)
    return jnp.concatenate([ah, ah, al], axis=1)


def _dot3_rhs(b):
    bh = b.astype(BF16)
    bl = (b - bh.astype(F32)).astype(BF16)
    return jnp.concatenate([bh, bl, bh], axis=0)


def _dot_exact_lhs(a_bf16, b):
    return _dot(jnp.concatenate([a_bf16] * 3, axis=1), jnp.concatenate(_split3(b), axis=0))


def _trig_bf16(ang):
    return (jnp.asarray(np.cos(ang), F32).astype(BF16), jnp.asarray(np.sin(ang), F32).astype(BF16))


def _silu(x):
    return x / (1.0 + jnp.exp(-x))


def _log_sigmoid(x):
    return jnp.minimum(x, 0.0) - jnp.log(1.0 + jnp.exp(-jnp.abs(x)))


def _mod_body(c_ref, w_ref, b_ref, o_ref):
    o_ref[...] = _dot3(_silu(c_ref[...]), w_ref[...]) + b_ref[...]


def _modulation(cvec, mod_w, mod_b):
    depth, d, e = mod_w.shape
    tn = e
    return pl.pallas_call(
        _mod_body,
        grid=(depth, e // tn),
        in_specs=[
            pl.BlockSpec((MOD_ROWS, d), lambda l, j: (0, 0)),
            pl.BlockSpec((None, d, tn), lambda l, j: (l, 0, j)),
            pl.BlockSpec((None, 1, tn), lambda l, j: (l, 0, j)),
        ],
        out_specs=pl.BlockSpec((None, MOD_ROWS, tn), lambda l, j: (l, 0, j)),
        out_shape=jax.ShapeDtypeStruct((depth, MOD_ROWS, e), F32),
        compiler_params=_params("parallel", "parallel"),
        name="modulation",
    )(cvec, mod_w, mod_b.reshape(depth, 1, e))


def _phase_permutation(rows):
    order = np.concatenate([np.arange(0, rows, 2), np.arange(1, rows, 2)])
    return jnp.asarray(np.eye(rows, dtype=np.float32)[order], BF16)


def _proj_body(*refs, n_y, has_res, final, has_next, split, n_w, conv_seq):
    it = iter(refs)
    take = lambda n: [next(it) for _ in range(n)]
    if has_res:
        y_refs = take(n_y)
        unperm_ref = next(it) if n_y == 2 else None
        w_out_ref, gate_ref = take(2)
        fg_ref = next(it) if final else None
    x_ref = next(it)
    if has_next:
        g_ref, mod_ref = take(2)
        perm_ref = next(it) if split else None
        w_refs = take(n_w)
        convw_ref, convb_ref = take(2) if conv_seq else (None, None)
    x_out_ref = next(it) if has_res else None
    o_refs = take(n_w) if has_next else []

    x = x_ref[...]
    d = x.shape[-1]
    if has_res:
        if n_y == 2:
            y = _dot(unperm_ref[...], jnp.concatenate([r[...] for r in y_refs], axis=0)).astype(BF16)
        else:
            y = y_refs[0][...]
        x = x + gate_ref[...][:, 2 * d:] * _dot(y, w_out_ref[...])
        if final:
            x_out_ref[...] = x * lax.rsqrt(jnp.mean(x * x, axis=-1, keepdims=True) + EPS) * fg_ref[...]
        else:
            x_out_ref[...] = x
    if has_next:
        h = x * lax.rsqrt(jnp.mean(x * x, axis=-1, keepdims=True) + EPS) * g_ref[...]
        mod = mod_ref[...]
        hb = (h * (1.0 + mod[:, d:2 * d]) + mod[:, :d]).astype(BF16)
        if split:
            hb = _dot(perm_ref[...], hb).astype(BF16)
        for w_ref, o_ref in zip(w_refs, o_refs):
            res = _dot(hb, w_ref[...])
            if split:
                half, n = res.shape[0] // 2, res.shape[1]
                o_ref[:, :n] = res[:half]
                o_ref[:, n:] = res[half:]
            elif conv_seq:
                n_conv = convw_ref.shape[-1]
                w0, w1, w2, b = convw_ref[0:1], convw_ref[1:2], convw_ref[2:3], convb_ref[...]
                for s0 in range(0, res.shape[0], conv_seq):
                    u = res[s0:s0 + conv_seq, :n_conv]
                    o_ref[s0:s0 + conv_seq, :n_conv] = (
                        _shift_down(u, 0.0) * w0 + u * w1 + _shift_up(u, 0.0) * w2 + b)
                o_ref[:, n_conv:] = res[:, n_conv:]
            else:
                o_ref[...] = res.astype(o_ref.dtype)


def _proj(x, mod4, mod_row, tm, name, res=None, nxt=None, final_g=None, conv=None):
    t, d = x.shape
    has_res, has_next, final = res is not None, nxt is not None, final_g is not None
    mod_spec = lambda layer: pl.BlockSpec((None, None, 1, mod4.shape[-1]), lambda i: (layer, mod_row(i), 0, 0))
    in_specs, args, out_specs, out_shape = [], [], [], []
    n_y, split, n_w, conv_seq = 0, False, 0, 0
    if has_res:
        ys, w_out, j, layer = res
        n_y = len(ys)
        in_specs += [pl.BlockSpec((tm // n_y, y.shape[-1]), lambda i: (i, 0)) for y in ys]
        args += list(ys)
        if n_y == 2:
            in_specs.append(pl.BlockSpec((tm, tm), lambda i: (0, 0)))
            args.append(_phase_permutation(tm).T)
        in_specs += [pl.BlockSpec((None,) + w_out.shape[1:], lambda i: (j, 0, 0), pipeline_mode=pl.Buffered(1)),
                     mod_spec(layer)]
        args += [w_out, mod4]
        if final:
            in_specs.append(pl.BlockSpec((1, d), lambda i: (0, 0)))
            args.append(final_g.reshape(1, d))
        out_specs.append(pl.BlockSpec((tm, d), lambda i: (i, 0)))
        out_shape.append(jax.ShapeDtypeStruct((t, d), F32))
    in_specs.append(pl.BlockSpec((tm, d), lambda i: (i, 0)))
    args.append(x)
    if has_next:
        norm_g3, next_layer, weights, split = nxt
        n_w = len(weights)
        in_specs += [pl.BlockSpec((None, 1, d), lambda i: (next_layer, 0, 0)), mod_spec(next_layer)]
        args += [norm_g3, mod4]
        if split:
            in_specs.append(pl.BlockSpec((tm, tm), lambda i: (0, 0)))
            args.append(_phase_permutation(tm))
        in_specs += [pl.BlockSpec(bs, functools.partial(lambda idx, i: idx, idx), pipeline_mode=pl.Buffered(1))
                     for _, bs, idx, _ in weights]
        args += [w for w, _, _, _ in weights]
        if conv is not None:
            conv_w, conv_b3, conv_j, conv_seq = conv
            assert n_w == 1 and not split and tm % conv_seq == 0
            in_specs += [pl.BlockSpec((None,) + conv_w.shape[1:], lambda i: (conv_j, 0, 0)),
                         pl.BlockSpec((None,) + conv_b3.shape[1:], lambda i: (conv_j, 0, 0))]
            args += [conv_w, conv_b3]
        p = 2 if split else 1
        out_specs += [pl.BlockSpec((tm // p, p * bs[-1]), lambda i: (i, 0)) for _, bs, _, _ in weights]
        out_shape += [jax.ShapeDtypeStruct((t // p, p * bs[-1]), dt) for _, bs, _, dt in weights]
    return pl.pallas_call(
        functools.partial(_proj_body, n_y=n_y, has_res=has_res, final=final, has_next=has_next,
                          split=split, n_w=n_w, conv_seq=conv_seq),
        grid=(t // tm,),
        in_specs=in_specs,
        out_specs=out_specs,
        out_shape=out_shape,
        compiler_params=_params("parallel"),
        name=name,
    )(*args)


GLA_CHUNK = 64
GLA_BLOCK = 256
GLA_HEADS_PER_STEP = 2
LOG2_E = math.log2(math.e)
GLA_CLAMP_SPLIT = 8.0 * LOG2_E
GLA_CLAMP_HALF = 80.0 * LOG2_E


def _gla_tables(rows, dk):
    c, hc = GLA_CHUNK, GLA_CHUNK // 2
    r = np.arange(rows)[:, None]
    s = np.arange(rows)[None, :]
    same_chunk = (r // c) == (s // c)
    same_half = (r // hc) == (s // hc)
    tri = np.stack([same_chunk & (s <= r), same_chunk & (s >= r)])
    m_half = np.stack([same_half & (s < r), same_half & (s > r)])
    sel = np.concatenate([s == (r // c) * c + hc, s == (r // hc) * hc + hc // 2], axis=0)
    second = np.broadcast_to((r // hc) % 2 == 1, (rows, dk))
    as_const = lambda m, dt: jnp.asarray(m.astype(np.float32), dt)
    return (as_const(tri, BF16), as_const(sel, BF16), as_const(same_chunk, F32), as_const(m_half, F32),
            as_const(second, F32), as_const(2.0 * (s == r), F32))


def _gla_body(q_ref, k_ref, v_ref, r_ref, lr_ref, wdec_ref, bdec_ref, og_ref, tri_ref, sel_ref,
              mchunk_ref, mhalf_ref, second_ref, diag2_ref, *rest, seq, layer, has_s0, want_state,
              has_states_in):
    rest = list(rest)
    s0_ref = rest.pop(0) if has_s0 else None
    if has_states_in:
        rest.pop(0)
    y_ref = rest.pop(0)
    sfin_ref = rest.pop(0) if want_state else None
    o_ref, qi_ref, cum_ref, st_ref = rest
    chunk = GLA_CHUNK
    n_chunks = seq // chunk
    blk = tri_ref.shape[-1]
    n_heads = wdec_ref.shape[0]
    dk = q_ref.shape[-1] // n_heads
    dv = v_ref.shape[-1] // n_heads
    scale = dk ** -0.5
    heads = range(n_heads)
    dirs = range(2)
    chains = [(hh, d) for hh in heads for d in dirs]

    def nt(a, b):
        return lax.dot_general(a, b, _NT, preferred_element_type=F32)

    def cols(hh, width):
        return slice(hh * width, (hh + 1) * width)

    def block_pass(bi, carry):
        rows = pl.ds(pl.multiple_of(bi * blk, blk), blk)
        q = [q_ref[rows, cols(hh, dk)] * scale for hh in heads]
        k = [k_ref[rows, cols(hh, dk)] for hh in heads]
        lr = lr_ref[rows, :]
        g = [_log_sigmoid(_dot3(lr, wdec_ref[hh]) + bdec_ref[hh]) * (LOG2_E / GLA_GATE_NORM) for hh in heads]
        cum = {(hh, d): _dot_exact_lhs(tri_ref[d], g[hh][:, cols(d, dk)]) for hh, d in chains}
        refs = [_dot(sel_ref[...], jnp.concatenate([cum[hh, 0], cum[hh, 1]], axis=1).astype(BF16))
                for hh in heads]
        second = second_ref[...]
        first = 1.0 - second
        p_split, p_half = [], {}
        for hh in heads:
            q_in = (q[hh] * second, q[hh] * first)
            k_in = (k[hh] * first, k[hh] * second)
            q1, k1 = [], []
            for d in dirs:
                c = (hh, d)
                d_split = cum[c] - refs[hh][:blk, cols(d, dk)]
                d_half = cum[c] - refs[hh][blk:, cols(d, dk)]
                q1.append((q_in[d] * jnp.exp2(jnp.minimum(d_split, GLA_CLAMP_SPLIT))).astype(BF16))
                k1.append((k_in[d] * jnp.exp2(jnp.minimum(-d_split, GLA_CLAMP_SPLIT))).astype(BF16))
                q0 = (q[hh] * jnp.exp2(jnp.minimum(d_half, GLA_CLAMP_HALF))).astype(BF16)
                k0 = (k[hh] * jnp.exp2(jnp.minimum(-d_half, GLA_CLAMP_HALF))).astype(BF16)
                p_half[c] = nt(q0, k0)
                qi_ref[hh, d, rows, :] = (q[hh] * jnp.exp2(cum[c])).astype(BF16)
                cum_ref[hh, d, rows, :] = cum[c]
            p_split.append(nt(jnp.concatenate(q1, axis=1), jnp.concatenate(k1, axis=1)))
        for hh in heads:
            att = p_split[hh] * mchunk_ref[...] + diag2_ref[...] * jnp.sum(q[hh] * k[hh], axis=-1, keepdims=True)
            for d in dirs:
                att = att + jnp.where(mhalf_ref[d] > 0.5, p_half[hh, d], 0.0)
            o_ref[rows, cols(hh, dv)] = _dot(att.astype(BF16), v_ref[rows, cols(hh, dv)].astype(BF16))
        return carry

    lax.fori_loop(0, seq // blk, block_pass, 0)

    for hh, d in chains:
        st_ref[hh, d] = s0_ref[d, hh] if has_s0 else jnp.zeros((dk, dv), F32)

    def chunk_step(i, carry):
        for hh, d in chains:
            c = i if d == 0 else n_chunks - 1 - i
            sl = pl.ds(pl.multiple_of(c * chunk, chunk), chunk)
            cum = cum_ref[hh, d, sl, :]
            blast = cum[chunk - 1:chunk] if d == 0 else cum[0:1]
            kh = (k_ref[sl, cols(hh, dk)] * jnp.exp2(blast - cum)).astype(BF16)
            vb = v_ref[sl, cols(hh, dv)].astype(BF16)
            st = st_ref[hh, d]
            o_ref[sl, cols(hh, dv)] += _dot(qi_ref[hh, d, sl, :], st.astype(BF16))
            decay = jnp.broadcast_to(jnp.exp2(blast), (dk, dk)).T
            st_ref[hh, d] = (jnp.concatenate([decay] * (dv // dk), axis=1) * st
                             + lax.dot_general(kh, vb, _TN, preferred_element_type=F32))
        return carry

    lax.fori_loop(0, n_chunks, chunk_step, 0, unroll=min(n_chunks, 8))

    def finish(jb, carry):
        sl = pl.ds(pl.multiple_of(jb * blk, blk), blk)
        for hh in heads:
            o = o_ref[sl, cols(hh, dv)]
            o = o * lax.rsqrt(jnp.mean(o * o, axis=-1, keepdims=True) + EPS) * og_ref[...]
            y_ref[sl, cols(hh, dv)] = (o * _silu(r_ref[sl, cols(hh, dv)])).astype(BF16)
        return carry

    lax.fori_loop(0, seq // blk, finish, 0)
    if want_state:
        own = sfin_ref if has_states_in else sfin_ref.at[layer]
        for hh, d in chains:
            own[d, hh] = st_ref[hh, d]
        if not has_states_in:
            for other in range(sfin_ref.shape[0]):
                if other != layer:
                    sfin_ref[other] = jnp.zeros(sfin_ref.shape[1:], F32)


def _gla_mixer(proj, lr, w_dec, b_dec, onorm_g, s0, j, n_seq, seq, want_state, states, name):
    t = proj.shape[0]
    h = GLA_HEADS
    key = w_dec.shape[-1]
    dk = key // h
    val = (proj.shape[1] - 2 * key) // 2
    dv = val // h
    has_s0 = s0 is not None
    tables = _gla_tables(min(seq, GLA_BLOCK), dk)
    fixed = lambda a: pl.BlockSpec(a.shape, lambda b, hp: (0,) * a.ndim)
    hps = h if seq <= GLA_BLOCK else GLA_HEADS_PER_STEP
    wk, wv = hps * dk, hps * dv
    in_specs = [
        pl.BlockSpec((seq, wk), lambda b, hp: (b, hp)),
        pl.BlockSpec((seq, wk), lambda b, hp: (b, key // wk + hp)),
        pl.BlockSpec((seq, wv), lambda b, hp: (b, (2 * key) // wv + hp)),
        pl.BlockSpec((seq, wv), lambda b, hp: (b, (2 * key + val) // wv + hp)),
        pl.BlockSpec((seq, 2 * GLA_RANK), lambda b, hp: (b, 0)),
        pl.BlockSpec((None, hps, 2 * GLA_RANK, 2 * dk), lambda b, hp: (j, hp, 0, 0)),
        pl.BlockSpec((None, hps, 1, 2 * dk), lambda b, hp: (j, hp, 0, 0)),
        pl.BlockSpec((None, 1, dv), lambda b, hp: (j, 0, 0)),
    ] + [fixed(a) for a in tables]
    n_gla = w_dec.shape[0]
    w5 = w_dec.reshape(n_gla, 2, GLA_RANK, h, dk)
    zero = jnp.zeros_like(w5[:, 0])
    w_blk = jnp.concatenate([jnp.concatenate([w5[:, 0], zero], axis=-1),
                             jnp.concatenate([zero, w5[:, 1]], axis=-1)], axis=1).transpose(0, 2, 1, 3)
    b_cat = b_dec.reshape(n_gla, 2, h, dk).transpose(0, 2, 1, 3).reshape(n_gla, h, 1, 2 * dk)
    args = [proj, proj, proj, proj, lr, w_blk, b_cat, onorm_g.reshape(n_gla, 1, dv), *tables]
    if has_s0:
        in_specs.append(pl.BlockSpec((None, None, 2, hps, dk, dv), lambda b, hp: (b, j, 0, hp, 0, 0)))
        args.append(s0)
    out_specs = [pl.BlockSpec((seq, wv), lambda b, hp: (b, hp))]
    out_shape = [jax.ShapeDtypeStruct((t, val), BF16)]
    aliases = {}
    if want_state:
        out_shape.append(jax.ShapeDtypeStruct((n_seq, n_gla, 2, h, dk, dv), F32))
        if states is None:
            out_specs.append(pl.BlockSpec((None, n_gla, 2, hps, dk, dv), lambda b, hp: (b, 0, 0, hp, 0, 0)))
        else:
            out_specs.append(pl.BlockSpec((None, None, 2, hps, dk, dv), lambda b, hp: (b, j, 0, hp, 0, 0)))
            in_specs.append(pl.BlockSpec(memory_space=pl.ANY))
            args.append(states)
            aliases = {len(args) - 1: 1}
    return pl.pallas_call(
        functools.partial(_gla_body, seq=seq, layer=j, has_s0=has_s0, want_state=want_state,
                          has_states_in=bool(aliases)),
        grid=(n_seq, h // hps),
        in_specs=in_specs,
        out_specs=out_specs,
        out_shape=out_shape,
        input_output_aliases=aliases,
        scratch_shapes=[pltpu.VMEM((seq, wv), F32), pltpu.VMEM((hps, 2, seq, dk), BF16),
                        pltpu.VMEM((hps, 2, seq, dk), F32), pltpu.VMEM((hps, 2, dk, dv), F32)],
        compiler_params=_params("parallel", "parallel"),
        name=name,
    )(*args)


def _dft_tables(n):
    f = np.arange(n, dtype=np.int64)
    ang = (np.outer(f, f) % n).astype(np.float64) * (2.0 * math.pi / n)
    return _trig_bf16(ang)


def _fnet_body(x_ref, z_ref, cc_ref, sc_ref, cl_ref, sl_ref, y_ref, *, seq, n_seq, gc):
    inv = 1.0 / math.sqrt(seq * gc)
    for g in range(x_ref.shape[-1] // gc):
        cols = slice(g * gc, (g + 1) * gc)
        xb = x_ref[:, cols].astype(BF16)
        a = _dot(xb, cc_ref[...]).astype(BF16)
        bm = _dot(xb, sc_ref[...]).astype(BF16)
        for i in range(n_seq):
            rows = slice(i * seq, (i + 1) * seq)
            f = _dot(cl_ref[...], a[rows]) - _dot(sl_ref[...], bm[rows])
            y_ref[rows, cols] = (f * inv * _silu(z_ref[rows, cols])).astype(BF16)


def _fnet_mixer(u, z, n_seq_total, seq, seqs_per_step, groups_per_step, name):
    t, width = u.shape
    gc = width // FN_GROUPS
    cc, sc = _dft_tables(gc)
    cl, sl = _dft_tables(seq)
    rows = seq * seqs_per_step
    wide = gc * groups_per_step
    n_wide = width // wide
    const = pl.Buffered(1)
    return pl.pallas_call(
        functools.partial(_fnet_body, seq=seq, n_seq=seqs_per_step, gc=gc),
        grid=(n_seq_total // seqs_per_step, n_wide),
        in_specs=[
            pl.BlockSpec((rows, wide), lambda i, g: (i, g)),
            pl.BlockSpec((rows, wide), lambda i, g: (i, g)),
            pl.BlockSpec((gc, gc), lambda i, g: (0, 0), pipeline_mode=const),
            pl.BlockSpec((gc, gc), lambda i, g: (0, 0), pipeline_mode=const),
            pl.BlockSpec((seq, seq), lambda i, g: (0, 0), pipeline_mode=const),
            pl.BlockSpec((seq, seq), lambda i, g: (0, 0), pipeline_mode=const),
        ],
        out_specs=pl.BlockSpec((rows, wide), lambda i, g: (i, g)),
        out_shape=jax.ShapeDtypeStruct((t, width), BF16),
        compiler_params=_params("parallel", "parallel"),
        name=name,
    )(u, z, cc, sc, cl, sl)


def _signal_tables(n):
    o = 2 * np.arange(n, dtype=np.int64) + 1
    ang = (np.outer(o, o) % (8 * n)).astype(np.float64) * (math.pi / (4 * n))
    cos, sin = _trig_bf16(ang)
    return cos, -sin


def _filter_tables(n, split):
    f = np.arange(n, dtype=np.int64)
    if split:
        f = np.concatenate([f[:n // 2], f[:n // 2 - 1:-1]])
    f = 2 * f + 1
    o = np.arange(n, dtype=np.int64)
    ang = (np.outer(f, o) % (4 * n)).astype(np.float64) * (math.pi / (2 * n))
    return _trig_bf16(ang)


def _hyena_positions(seq, bands, pad_to):
    t = np.linspace(0.0, 1.0, seq)[:, None]
    w = 2.0 * math.pi * np.arange(seq)[:, None] / seq
    f = np.linspace(1e-4, bands - 1, bands)[None, :]
    zpos = np.concatenate([t, np.cos(f * w), -np.sin(f * w)], axis=-1)
    zpos = np.pad(zpos, ((0, 0), (0, pad_to - zpos.shape[1])))
    return jnp.asarray(zpos, F32), jnp.asarray(t, F32)


def _hyfilt_body(zpos_ref, w1_ref, b1_ref, w2_ref, b2_ref, w3_ref, b3_ref, fr_ref, w4f_ref,
                 w4b_ref, t_ref, del_ref, ch_ref, sh_ref, *rest, split):
    if split:
        c2_ref, s2_ref = rest[:2]
        out_refs, a3_ref = rest[2:-1], rest[-1]
    else:
        out_refs, a3_ref = rest[:-1], rest[-1]

    @pl.when(pl.program_id(0) == 0)
    def _():
        fr = fr_ref[...]
        a = jnp.sin(fr * (_dot3(zpos_ref[...], w1_ref[...]) + b1_ref[...]))
        a = jnp.sin(fr * (_dot3(a, w2_ref[...]) + b2_ref[...]))
        a3_ref[...] = _dot3_lhs(jnp.sin(fr * (_dot3(a, w3_ref[...]) + b3_ref[...])))

    a3 = a3_ref[...]
    dec = jnp.exp(-t_ref[...] * del_ref[...]) * (1.0 / t_ref.shape[0])
    fwd = _dot(a3, _dot3_rhs(w4f_ref[...])) * dec
    bwd = _dot(a3, _dot3_rhs(w4b_ref[...])) * dec
    first = lax.broadcasted_iota(jnp.int32, bwd.shape, 0) == 0
    bwd = jnp.where(first, 0.0, bwd)
    hre = _dot(ch_ref[...], (fwd + bwd).astype(BF16))
    him = _dot(sh_ref[...], (bwd - fwd).astype(BF16))
    if not split:
        out_refs[0][...] = hre
        out_refs[1][...] = him
        return
    half = hre.shape[0] // 2
    plus_r, plus_i = hre[:half] + hre[half:], him[:half] - him[half:]
    minus_r, minus_i = hre[:half] - hre[half:], him[:half] + him[half:]
    c2, s2 = c2_ref[...], s2_ref[...]
    for ref, val in zip(out_refs, (plus_r, plus_i,
                                   c2 * minus_r + s2 * minus_i, c2 * minus_i - s2 * minus_r,
                                   c2 * minus_r - s2 * minus_i, c2 * minus_i + s2 * minus_r)):
        ref[...] = val


def _hyena_filters(seq, j, emb, w1p, b1, w2, b2, w3, b3, w4, freq, ce, split, name):
    ffn = w2.shape[-1]
    width = w4.shape[-1] // (2 * HY_ORDER)
    emb_pad = w1p.shape[1]
    zpos, tcol = _hyena_positions(seq, (emb - 1) // 2, emb_pad)
    deltas = jnp.asarray(np.abs(np.linspace(HY_MIN_DECAY, HY_MAX_DECAY, width, dtype=np.float32))[None, :])
    ch, sh = _filter_tables(seq, split)
    nb = width // ce
    const = pl.Buffered(1)
    vec = lambda: pl.BlockSpec((None, 1, ffn), lambda c: (j, 0, 0))
    out_rows, n_out = (seq // 2, 6) if split else (seq, 2)
    twiddles, twiddle_specs = [], []
    if split:
        ang = np.broadcast_to((math.pi * (2 * np.arange(out_rows) + 1) / (2 * seq))[:, None], (out_rows, ce))
        twiddles = [jnp.asarray(np.cos(ang), F32), jnp.asarray(np.sin(ang), F32)]
        twiddle_specs = [pl.BlockSpec((out_rows, ce), lambda c: (0, 0), pipeline_mode=const)] * 2
    return pl.pallas_call(
        functools.partial(_hyfilt_body, split=split),
        grid=(HY_ORDER * nb,),
        in_specs=[
            pl.BlockSpec((seq, emb_pad), lambda c: (0, 0)),
            pl.BlockSpec((None, emb_pad, ffn), lambda c: (j, 0, 0)),
            vec(),
            pl.BlockSpec((None, ffn, ffn), lambda c: (j, 0, 0)),
            vec(),
            pl.BlockSpec((None, ffn, ffn), lambda c: (j, 0, 0)),
            vec(),
            vec(),
            pl.BlockSpec((None, ffn, ce), lambda c: (j, 0, (c // nb) * 2 * nb + c % nb)),
            pl.BlockSpec((None, ffn, ce), lambda c: (j, 0, (c // nb) * 2 * nb + nb + c % nb)),
            pl.BlockSpec((seq, 1), lambda c: (0, 0)),
            pl.BlockSpec((1, ce), lambda c: (0, c % nb)),
            pl.BlockSpec((seq, seq), lambda c: (0, 0), pipeline_mode=const),
            pl.BlockSpec((seq, seq), lambda c: (0, 0), pipeline_mode=const),
        ] + twiddle_specs,
        out_specs=[pl.BlockSpec((out_rows, ce), lambda c: (0, c))] * n_out,
        out_shape=[jax.ShapeDtypeStruct((out_rows, HY_ORDER * width), F32)] * n_out,
        scratch_shapes=[pltpu.VMEM((seq, 3 * ffn), BF16)],
        compiler_params=_params("arbitrary"),
        name=name,
    )(zpos, w1p, b1, w2, b2, w3, b3, freq, w4, w4, tcol, deltas, ch, sh, *twiddles)


SUBLANES = 8


def _shift_down(u, first):
    rolled = pltpu.roll(u, 1, 0)
    row = lax.broadcasted_iota(jnp.int32, (SUBLANES, 1), 0)
    return jnp.concatenate([jnp.where(row == 0, first, rolled[:SUBLANES]), rolled[SUBLANES:]], axis=0)


def _shift_up(u, last):
    n = u.shape[0]
    rolled = pltpu.roll(u, n - 1, 0)
    row = lax.broadcasted_iota(jnp.int32, (SUBLANES, 1), 0)
    tail = jnp.where(row == SUBLANES - 1, last, rolled[n - SUBLANES:])
    return jnp.concatenate([rolled[:n - SUBLANES], tail], axis=0)


def _hyena_body(x1_ref, x2_ref, v_ref, z_ref, cw1_ref, cw2_ref, cwv_ref, cb1_ref, cb2_ref, cbv_ref,
                hre0_ref, him0_ref, hre1_ref, him1_ref, d_ref, cq_ref, sq_ref, y_ref,
                cur_ref, cb_ref, yr_ref, yi_ref, *, seq, rows, conv_done):
    n_rb = seq // rows

    def block(rb):
        return pl.ds(pl.multiple_of(rb * rows, rows), rows)

    def short_conv(u_ref, w_ref, b_ref, rb):
        u = u_ref[block(rb), :]
        if conv_done:
            return u
        before = u_ref[pl.ds(jnp.maximum(rb * rows - 1, 0), 1), :]
        after = u_ref[pl.ds(jnp.minimum((rb + 1) * rows, seq - 1), 1), :]
        before = jnp.where(rb == 0, 0.0, before)
        after = jnp.where(rb == n_rb - 1, 0.0, after)
        prev, nxt = _shift_down(u, before), _shift_up(u, after)
        return prev * w_ref[0:1] + u * w_ref[1:2] + nxt * w_ref[2:3] + b_ref[...]

    def long_conv_gate(x_ref, cw_ref, cbias_ref, hre_ref, him_ref, dn):
        def fwd(rb, carry):
            sl = block(rb)
            ur = _dot(cq_ref[sl, :], cb_ref[...])
            ui = _dot(sq_ref[sl, :], cb_ref[...])
            hr = hre_ref[sl, :]
            hi = him_ref[sl, :]
            yr_ref[sl, :] = (ur * hr - ui * hi).astype(BF16)
            yi_ref[sl, :] = (ur * hi + ui * hr).astype(BF16)
            return carry

        lax.fori_loop(0, n_rb, fwd, 0)

        def inv(rb, carry):
            sl = block(rb)
            c = _dot(cq_ref[sl, :], yr_ref[...]) + _dot(sq_ref[sl, :], yi_ref[...])
            new = short_conv(x_ref, cw_ref, cbias_ref, rb) * (c + cur_ref[sl, :] * dn)
            cur_ref[sl, :] = new
            cb_ref[sl, :] = new.astype(BF16)
            return carry

        lax.fori_loop(0, n_rb, inv, 0)

    def load_v(rb, carry):
        v = short_conv(v_ref, cwv_ref, cbv_ref, rb)
        cur_ref[block(rb), :] = v
        cb_ref[block(rb), :] = v.astype(BF16)
        return carry

    lax.fori_loop(0, n_rb, load_v, 0)
    long_conv_gate(x1_ref, cw1_ref, cb1_ref, hre0_ref, him0_ref, d_ref[0:1])
    long_conv_gate(x2_ref, cw2_ref, cb2_ref, hre1_ref, him1_ref, d_ref[1:2])

    def store(rb, carry):
        sl = block(rb)
        y_ref[sl, :] = (cur_ref[sl, :] * _silu(z_ref[sl, :])).astype(BF16)
        return carry

    lax.fori_loop(0, n_rb, store, 0)


def _hyena_split_body(x1_refs, x2_refs, v_refs, z_refs, cw1_ref, cw2_ref, cwv_ref, cb1_ref, cb2_ref, cbv_ref,
                      k_refs, d_ref, cq_ref, sq_ref, y_refs, cur_ref, sig_ref, z_buf, *, seq, rows):
    half = seq // 2
    n_rb = half // rows

    def block(rb, shift=0):
        return slice(rb * rows + shift, (rb + 1) * rows + shift)

    def short_conv(u_refs, w_ref, b_ref, rb):
        e_ref, o_ref = u_refs
        even, odd = e_ref[block(rb), :], o_ref[block(rb), :]
        odd_before = o_ref[block(rb, -1), :] if rb > 0 else _shift_down(odd, 0.0)
        even_after = e_ref[block(rb, 1), :] if rb < n_rb - 1 else _shift_up(even, 0.0)
        w0, w1, w2, b = w_ref[0:1], w_ref[1:2], w_ref[2:3], b_ref[...]
        return (odd_before * w0 + even * w1 + odd * w2 + b, even * w0 + odd * w1 + even_after * w2 + b)

    def long_conv_gate(x_ref, cw_ref, cbias_ref, k, dn):
        k1r, k1i, k2r, k2i, k3r, k3i = k
        for fb in range(n_rb):
            sl = block(fb)
            c, s = cq_ref[sl, :], sq_ref[sl, :]
            er, ei = _dot(c, sig_ref[0]), _dot(s, sig_ref[0])
            orr, oi = _dot(c, sig_ref[1]), _dot(s, sig_ref[1])
            a1r, a1i, a2r, a2i, a3r, a3i = (r[sl, :] for r in (k1r, k1i, k2r, k2i, k3r, k3i))
            z_buf[0, sl, :] = (er * a1r - ei * a1i + orr * a2r - oi * a2i).astype(BF16)
            z_buf[1, sl, :] = (er * a1i + ei * a1r + orr * a2i + oi * a2r).astype(BF16)
            z_buf[2, sl, :] = (er * a3r - ei * a3i + orr * a1r - oi * a1i).astype(BF16)
            z_buf[3, sl, :] = (er * a3i + ei * a3r + orr * a1i + oi * a1r).astype(BF16)
        for tb in range(n_rb):
            sl = block(tb)
            c, s = cq_ref[sl, :], sq_ref[sl, :]
            conv = (_dot(c, z_buf[0]) + _dot(s, z_buf[1]), _dot(c, z_buf[2]) + _dot(s, z_buf[3]))
            gate = short_conv(x_ref, cw_ref, cbias_ref, tb)
            for p in range(2):
                new = gate[p] * (conv[p] + cur_ref[p, sl, :] * dn)
                cur_ref[p, sl, :] = new
                sig_ref[p, sl, :] = new.astype(BF16)

    for rb in range(n_rb):
        for p, val in enumerate(short_conv(v_refs, cwv_ref, cbv_ref, rb)):
            cur_ref[p, block(rb), :] = val
            sig_ref[p, block(rb), :] = val.astype(BF16)
    long_conv_gate(x1_refs, cw1_ref, cb1_ref, k_refs[0], d_ref[0:1])
    long_conv_gate(x2_refs, cw2_ref, cb2_ref, k_refs[1], d_ref[1:2])
    for rb in range(n_rb):
        for p in range(2):
            sl = block(rb)
            y_refs[p][sl, :] = (cur_ref[p, sl, :] * _silu(z_refs[p][sl, :])).astype(BF16)


def _hyena_split_entry(*refs, seq, rows):
    it = iter(refs)
    take = lambda n: tuple(next(it) for _ in range(n))
    x1, x2, v, z = take(2), take(2), take(2), take(2)
    conv_refs = take(6)
    k = tuple(take(6) for _ in range(HY_ORDER))
    d, cq, sq = take(3)
    y = take(2)
    _hyena_split_body(x1, x2, v, z, *conv_refs, k, d, cq, sq, y, *it, seq=seq, rows=rows)


def _hyena_mixer(proj, conv_w, conv_b, filt, d_skip, j, n_seq, seq, ce, split, conv_done, name):
    phases = 2 if split else 1
    t = proj.shape[0] * phases
    width = proj.shape[1] // (4 * phases)
    nb = width // ce
    n_dft = seq // phases
    cq, sq = _signal_tables(n_dft)
    rows = min(n_dft, HY_ROW_BLOCK)
    const = pl.Buffered(1)
    seg = lambda s: pl.BlockSpec((seq, ce), lambda e, b: (b, s * nb + e))
    cw = lambda s: pl.BlockSpec((None, conv_w.shape[1], ce), lambda e, b: (j, 0, s * nb + e))
    cb = lambda s: pl.BlockSpec((None, 1, ce), lambda e, b: (j, 0, s * nb + e))
    hs = lambda n: pl.BlockSpec((n_dft, ce), lambda e, b: (0, n * nb + e),
                                pipeline_mode=const if nb == 1 else pl.Buffered(2))
    conv_b3 = conv_b.reshape(conv_b.shape[0], 1, -1)
    if split:
        body = functools.partial(_hyena_split_entry, seq=seq, rows=rows)
        phase = lambda s, p: pl.BlockSpec((n_dft, ce), lambda e, b: (b, p * 4 * nb + s * nb + e))
        x_specs = [phase(s, p) for s in range(4) for p in range(2)]
        out_specs = [pl.BlockSpec((n_dft, ce), lambda e, b: (b, e))] * 2
        out_shape = [jax.ShapeDtypeStruct((t // 2, width), BF16)] * 2
        scratch = [pltpu.VMEM((2, n_dft, ce), F32), pltpu.VMEM((2, n_dft, ce), BF16),
                   pltpu.VMEM((4, n_dft, ce), BF16)]
    else:
        body = functools.partial(_hyena_body, seq=seq, rows=rows, conv_done=conv_done)
        x_specs = [seg(0), seg(1), seg(2), seg(3)]
        out_specs = pl.BlockSpec((seq, ce), lambda e, b: (b, e))
        out_shape = jax.ShapeDtypeStruct((t, width), BF16)
        scratch = [pltpu.VMEM((seq, ce), F32), pltpu.VMEM((seq, ce), BF16),
                   pltpu.VMEM((seq, ce), BF16), pltpu.VMEM((seq, ce), BF16)]
    return pl.pallas_call(
        body,
        grid=(nb, n_seq),
        in_specs=x_specs + [cw(0), cw(1), cw(2), cb(0), cb(1), cb(2)]
        + [hs(n) for n in range(HY_ORDER) for _ in filt]
        + [pl.BlockSpec((None, HY_ORDER, ce), lambda e, b: (j, 0, e)),
           pl.BlockSpec((n_dft, n_dft), lambda e, b: (0, 0), pipeline_mode=const),
           pl.BlockSpec((n_dft, n_dft), lambda e, b: (0, 0), pipeline_mode=const)],
        out_specs=out_specs,
        out_shape=out_shape,
        scratch_shapes=scratch,
        compiler_params=_params("parallel", "parallel"),
        name=name,
    )(*([proj] * len(x_specs)), conv_w, conv_w, conv_w, conv_b3, conv_b3, conv_b3,
      *(list(filt) * HY_ORDER), d_skip, cq, sq)


def kernel(x_prompt, x_sample, state_gla, c, c_ctx, mod_w, mod_b, norm_g, final_norm_g, gla_w_in, gla_w_dec, gla_b_dec, gla_onorm_g, gla_w_out, fn_w_in, fn_w_out, hy_w_in, hy_conv_w, hy_conv_b, hy_ffn_w1, hy_ffn_b1, hy_ffn_w2, hy_ffn_b2, hy_ffn_w3, hy_ffn_b3, hy_ffn_w4, hy_freq, hy_d, hy_w_out):
    n_p, l_p, d = x_prompt.shape
    n_s, l_s, _ = x_sample.shape
    depth = mod_w.shape[0]
    key = gla_w_dec.shape[-1]
    n_main = gla_w_in.shape[-1] - 2 * GLA_RANK

    cvec = jnp.concatenate([c_ctx[None], c, jnp.zeros((MOD_ROWS - 1 - n_s, d), F32)], axis=0)
    mod4 = _modulation(cvec, mod_w, mod_b).reshape(depth, MOD_ROWS, 1, 3 * d)
    norm_g3 = norm_g.reshape(depth, 1, d)

    gla_w_main = gla_w_in.astype(BF16)
    gla_w_lr = gla_w_in[:, :, n_main:].astype(BF16)
    gla_w_out_b = gla_w_out.astype(BF16)
    fn_w_in_b = fn_w_in.astype(BF16)
    fn_w_out_b = fn_w_out.astype(BF16)
    hy_w_in_b = hy_w_in.astype(BF16)
    hy_w_out_b = hy_w_out.astype(BF16)
    emb = hy_ffn_w1.shape[1]
    emb_pad = -(-emb // 128) * 128
    hy_w1p = jnp.pad(hy_ffn_w1, ((0, 0), (0, emb_pad - emb), (0, 0)))
    ffn = hy_ffn_w2.shape[-1]
    hy_vecs = [a.reshape(a.shape[0], 1, ffn) for a in (hy_ffn_b1, hy_ffn_b2, hy_ffn_b3, hy_freq)]

    tm = 512
    tiles_per_sample = l_s // tm
    streams = {
        "p": dict(x=x_prompt.reshape(n_p * l_p, d), n=n_p, l=l_p, row=lambda i: 0),
        "s": dict(x=x_sample.reshape(n_s * l_s, d), n=n_s, l=l_s, row=lambda i: 1 + i // tiles_per_sample),
    }
    def in_weights(i, seq):
        kind, j = i % N_MIXERS, i // N_MIXERS
        if kind == 0:
            return [(gla_w_main, (None, d, n_main), (j, 0, 0), F32),
                    (gla_w_lr, (None, d, 2 * GLA_RANK), (j, 0, 0), F32)], False
        if kind == 1:
            half_block = (None, d, fn_w_in.shape[-1] // 2)
            return [(fn_w_in_b, half_block, (j, 0, 0), BF16), (fn_w_in_b, half_block, (j, 0, 1), F32)], False
        return [(hy_w_in_b, (None,) + hy_w_in.shape[1:], (j, 0, 0), F32)], seq >= HY_SPLIT_MIN_SEQ

    hy_conv_b3 = hy_conv_b.reshape(hy_conv_b.shape[0], 1, -1)

    def in_conv(i, seq):
        if i % N_MIXERS == 2 and seq < HY_SPLIT_MIN_SEQ and tm % seq == 0:
            return (hy_conv_w, hy_conv_b3, i // N_MIXERS, seq)
        return None

    for tag, st in streams.items():
        weights, split = in_weights(0, st["l"])
        st["proj"] = _proj(st["x"], mod4, st["row"], tm, f"proj_{tag}0", nxt=(norm_g3, 0, weights, split),
                           conv=in_conv(0, st["l"]))
    new_states = None
    for i in range(depth):
        kind, j = i % N_MIXERS, i // N_MIXERS
        if kind == 2:
            filt = {}
            for tag, st in streams.items():
                ce = min(d, 256 if st["l"] > 256 else d)
                filt[tag] = _hyena_filters(st["l"], j, emb, hy_w1p, hy_vecs[0], hy_ffn_w2, hy_vecs[1], hy_ffn_w3,
                                           hy_vecs[2], hy_ffn_w4, hy_vecs[3], ce, st["l"] >= HY_SPLIT_MIN_SEQ,
                                           f"hyfilt_{tag}{i}")
        for tag, st in streams.items():
            n_seq, seq = st["n"], st["l"]
            if kind == 0:
                proj, lr = st["proj"]
                is_ctx = tag == "p"
                res = _gla_mixer(proj, lr, gla_w_dec, gla_b_dec, gla_onorm_g,
                                 None if is_ctx else state_gla, j, n_seq, seq, is_ctx, new_states,
                                 f"gla_{tag}{i}")
                ys = [res[0]]
                if is_ctx:
                    new_states = res[1]
                w_out = gla_w_out_b
            elif kind == 1:
                short = seq <= 256
                u, z = st["proj"]
                ys = [_fnet_mixer(u, z, n_seq, seq, 4 if short else 1, FN_GROUPS if short else 2,
                                  f"fnet_{tag}{i}")]
                w_out = fn_w_out_b
            else:
                split = seq >= HY_SPLIT_MIN_SEQ
                ce = min(d, 256 if seq > 256 else d)
                y = _hyena_mixer(st["proj"][0], hy_conv_w, hy_conv_b, filt[tag], hy_d, j, n_seq, seq, ce, split,
                                 in_conv(i, seq) is not None, f"hyena_{tag}{i}")
                ys = list(y) if split else [y]
                w_out = hy_w_out_b
            if i + 1 < depth:
                weights, split = in_weights(i + 1, seq)
                st["x"], *st["proj"] = _proj(st["x"], mod4, st["row"], tm, f"proj_{tag}{i + 1}",
                                             res=(ys, w_out, j, i), nxt=(norm_g3, i + 1, weights, split),
                                             conv=in_conv(i + 1, seq))
            else:
                (st["x"],) = _proj(st["x"], mod4, st["row"], tm, f"proj_{tag}{i + 1}",
                                   res=(ys, w_out, j, i), final_g=final_norm_g)
    y_prompt = streams["p"]["x"].reshape(n_p, l_p, d)
    y_sample = streams["s"]["x"].reshape(n_s, l_s, d)
    new_state_gla = new_states.astype(x_prompt.dtype)
    return (y_prompt, y_sample, new_state_gla)
```

```python
import functools
import math

import jax
import jax.numpy as jnp
import numpy as np
from jax import lax
from jax.experimental import pallas as pl
from jax.experimental.pallas import tpu as pltpu

F32 = jnp.float32
BF16 = jnp.bfloat16

EPS = 1e-6
N_MIXERS = 3
GLA_HEADS = 4
GLA_RANK = 16
GLA_GATE_NORM = 16.0
FN_GROUPS = 4
HY_ORDER = 2
HY_TARGET = 1e-2
HY_MIN_DECAY = math.log(HY_TARGET) / 1.5
HY_MAX_DECAY = math.log(HY_TARGET) / 0.3
HY_ROW_BLOCK = 512
HY_SPLIT_MIN_SEQ = 512

V7X_VMEM_BYTES = 64 * 1024 * 1024
VMEM_LIMIT_BYTES = V7X_VMEM_BYTES - 8 * 1024 * 1024
MOD_ROWS = 8

_NT = (((1,), (1,)), ((), ()))
_TN = (((0,), (0,)), ((), ()))


def _params(*sem):
    return pltpu.CompilerParams(dimension_semantics=sem, vmem_limit_bytes=VMEM_LIMIT_BYTES)


def _dot(a, b):
    return jnp.dot(a, b, preferred_element_type=F32)


def _split3(x):
    hi = x.astype(BF16)
    r1 = x - hi.astype(F32)
    mid = r1.astype(BF16)
    lo = (r1 - mid.astype(F32)).astype(BF16)
    return hi, mid, lo


def _dot3(a, b):
    return _dot(_dot3_lhs(a), _dot3_rhs(b))


def _dot3_lhs(a):
    ah = a.astype(BF16)
    al = (a - ah.astype(F32)).astype(BF16)
    return jnp.concatenate([ah, ah, al], axis=1)


def _dot3_rhs(b):
    bh = b.astype(BF16)
    bl = (b - bh.astype(F32)).astype(BF16)
    return jnp.concatenate([bh, bl, bh], axis=0)


def _dot_exact_lhs(a_bf16, b):
    hi, mid, _ = _split3(b)
    return _dot(jnp.concatenate([a_bf16] * 2, axis=1), jnp.concatenate([hi, mid], axis=0))


def _trig_bf16(ang):
    return (jnp.asarray(np.cos(ang), F32).astype(BF16), jnp.asarray(np.sin(ang), F32).astype(BF16))


def _silu(x):
    return x / (1.0 + jnp.exp(-x))


def _log_sigmoid(x):
    return jnp.minimum(x, 0.0) - jnp.log(1.0 + jnp.exp(-jnp.abs(x)))


def _mod_body(c_ref, w_ref, b_ref, o_ref):
    o_ref[...] = _dot3(_silu(c_ref[...]), w_ref[...]) + b_ref[...]


def _modulation(cvec, mod_w, mod_b):
    depth, d, e = mod_w.shape
    tn = e
    return pl.pallas_call(
        _mod_body,
        grid=(depth, e // tn),
        in_specs=[
            pl.BlockSpec((MOD_ROWS, d), lambda l, j: (0, 0)),
            pl.BlockSpec((None, d, tn), lambda l, j: (l, 0, j)),
            pl.BlockSpec((None, 1, tn), lambda l, j: (l, 0, j)),
        ],
        out_specs=pl.BlockSpec((None, MOD_ROWS, tn), lambda l, j: (l, 0, j)),
        out_shape=jax.ShapeDtypeStruct((depth, MOD_ROWS, e), F32),
        compiler_params=_params("parallel", "parallel"),
        name="modulation",
    )(cvec, mod_w, mod_b.reshape(depth, 1, e))


def _phase_permutation(rows):
    order = np.concatenate([np.arange(0, rows, 2), np.arange(1, rows, 2)])
    return jnp.asarray(np.eye(rows, dtype=np.float32)[order], BF16)


def _proj_body(*refs, n_y, has_res, final, has_next, split, n_w, conv_seq):
    it = iter(refs)
    take = lambda n: [next(it) for _ in range(n)]
    if has_res:
        y_refs = take(n_y)
        unperm_ref = next(it) if n_y == 2 else None
        w_out_ref, gate_ref = take(2)
        fg_ref = next(it) if final else None
    x_ref = next(it)
    if has_next:
        g_ref, mod_ref = take(2)
        perm_ref = next(it) if split else None
        w_refs = take(n_w)
        convw_ref, convb_ref = take(2) if conv_seq else (None, None)
    x_out_ref = next(it) if has_res else None
    o_refs = take(n_w) if has_next else []

    x = x_ref[...]
    d = x.shape[-1]
    if has_res:
        if n_y == 2:
            y = _dot(unperm_ref[...], jnp.concatenate([r[...] for r in y_refs], axis=0)).astype(BF16)
        else:
            y = y_refs[0][...]
        x = x + gate_ref[...][:, 2 * d:] * _dot(y, w_out_ref[...])
        if final:
            x_out_ref[...] = x * lax.rsqrt(jnp.mean(x * x, axis=-1, keepdims=True) + EPS) * fg_ref[...]
        else:
            x_out_ref[...] = x
    if has_next:
        h = x * lax.rsqrt(jnp.mean(x * x, axis=-1, keepdims=True) + EPS) * g_ref[...]
        mod = mod_ref[...]
        hb = (h * (1.0 + mod[:, d:2 * d]) + mod[:, :d]).astype(BF16)
        if split:
            hb = _dot(perm_ref[...], hb).astype(BF16)
        for w_ref, o_ref in zip(w_refs, o_refs):
            res = _dot(hb, w_ref[...])
            if split:
                half, n = res.shape[0] // 2, res.shape[1]
                o_ref[:, :n] = res[:half]
                o_ref[:, n:] = res[half:]
            elif conv_seq:
                n_conv = convw_ref.shape[-1]
                w0, w1, w2, b = convw_ref[0:1], convw_ref[1:2], convw_ref[2:3], convb_ref[...]
                for s0 in range(0, res.shape[0], conv_seq):
                    u = res[s0:s0 + conv_seq, :n_conv]
                    o_ref[s0:s0 + conv_seq, :n_conv] = (
                        _shift_down(u, 0.0) * w0 + u * w1 + _shift_up(u, 0.0) * w2 + b)
                o_ref[:, n_conv:] = res[:, n_conv:]
            else:
                o_ref[...] = res


def _proj(x, mod4, mod_row, tm, name, res=None, nxt=None, final_g=None, conv=None):
    t, d = x.shape
    has_res, has_next, final = res is not None, nxt is not None, final_g is not None
    mod_spec = lambda layer: pl.BlockSpec((None, None, 1, mod4.shape[-1]), lambda i: (layer, mod_row(i), 0, 0))
    in_specs, args, out_specs, out_shape = [], [], [], []
    n_y, split, n_w, conv_seq = 0, False, 0, 0
    if has_res:
        ys, w_out, j, layer = res
        n_y = len(ys)
        in_specs += [pl.BlockSpec((tm // n_y, y.shape[-1]), lambda i: (i, 0)) for y in ys]
        args += list(ys)
        if n_y == 2:
            in_specs.append(pl.BlockSpec((tm, tm), lambda i: (0, 0)))
            args.append(_phase_permutation(tm).T)
        in_specs += [pl.BlockSpec((None,) + w_out.shape[1:], lambda i: (j, 0, 0), pipeline_mode=pl.Buffered(1)),
                     mod_spec(layer)]
        args += [w_out, mod4]
        if final:
            in_specs.append(pl.BlockSpec((1, d), lambda i: (0, 0)))
            args.append(final_g.reshape(1, d))
        out_specs.append(pl.BlockSpec((tm, d), lambda i: (i, 0)))
        out_shape.append(jax.ShapeDtypeStruct((t, d), F32))
    in_specs.append(pl.BlockSpec((tm, d), lambda i: (i, 0)))
    args.append(x)
    if has_next:
        norm_g3, next_layer, weights, split = nxt
        n_w = len(weights)
        in_specs += [pl.BlockSpec((None, 1, d), lambda i: (next_layer, 0, 0)), mod_spec(next_layer)]
        args += [norm_g3, mod4]
        if split:
            in_specs.append(pl.BlockSpec((tm, tm), lambda i: (0, 0)))
            args.append(_phase_permutation(tm))
        in_specs += [pl.BlockSpec(bs, functools.partial(lambda idx, i: idx, idx), pipeline_mode=pl.Buffered(1))
                     for _, bs, idx in weights]
        args += [w for w, _, _ in weights]
        if conv is not None:
            conv_w, conv_b3, conv_j, conv_seq = conv
            assert n_w == 1 and not split and tm % conv_seq == 0
            in_specs += [pl.BlockSpec((None,) + conv_w.shape[1:], lambda i: (conv_j, 0, 0)),
                         pl.BlockSpec((None,) + conv_b3.shape[1:], lambda i: (conv_j, 0, 0))]
            args += [conv_w, conv_b3]
        p = 2 if split else 1
        widths = [bs[-1] for _, bs, _ in weights]
        out_specs += [pl.BlockSpec((tm // p, p * n), lambda i: (i, 0)) for n in widths]
        out_shape += [jax.ShapeDtypeStruct((t // p, p * n), F32) for n in widths]
    return pl.pallas_call(
        functools.partial(_proj_body, n_y=n_y, has_res=has_res, final=final, has_next=has_next,
                          split=split, n_w=n_w, conv_seq=conv_seq),
        grid=(t // tm,),
        in_specs=in_specs,
        out_specs=out_specs,
        out_shape=out_shape,
        compiler_params=_params("parallel"),
        name=name,
    )(*args)


GLA_CHUNK = 64
GLA_BLOCK = 256
GLA_HEADS_PER_STEP = 2
LOG2_E = math.log2(math.e)
GLA_CLAMP_SPLIT = 8.0 * LOG2_E
GLA_CLAMP_HALF = 80.0 * LOG2_E


def _gla_tables(rows, dk):
    c, hc = GLA_CHUNK, GLA_CHUNK // 2
    r = np.arange(rows)[:, None]
    s = np.arange(rows)[None, :]
    same_chunk = (r // c) == (s // c)
    same_half = (r // hc) == (s // hc)
    tri = np.stack([same_chunk & (s <= r), same_chunk & (s >= r)])
    m_half = np.stack([same_half & (s < r), same_half & (s > r)])
    sel = np.concatenate([s == (r // c) * c + hc, s == (r // hc) * hc + hc // 2], axis=0)
    second = np.broadcast_to((r // hc) % 2 == 1, (rows, dk))
    as_const = lambda m, dt: jnp.asarray(m.astype(np.float32), dt)
    return (as_const(tri, BF16), as_const(sel, BF16), as_const(same_chunk, F32), as_const(m_half, F32),
            as_const(second, F32), as_const(2.0 * (s == r), F32))


def _gla_body(q_ref, k_ref, v_ref, r_ref, lr_ref, wdec_ref, bdec_ref, og_ref, tri_ref, sel_ref,
              mchunk_ref, mhalf_ref, second_ref, diag2_ref, *rest, seq, layer, has_s0, want_state,
              has_states_in):
    rest = list(rest)
    s0_ref = rest.pop(0) if has_s0 else None
    if has_states_in:
        rest.pop(0)
    y_ref = rest.pop(0)
    sfin_ref = rest.pop(0) if want_state else None
    o_ref, qi_ref, cum_ref, st_ref = rest
    chunk = GLA_CHUNK
    n_chunks = seq // chunk
    blk = tri_ref.shape[-1]
    n_heads = wdec_ref.shape[0]
    dk = q_ref.shape[-1] // n_heads
    dv = v_ref.shape[-1] // n_heads
    scale = dk ** -0.5
    heads = range(n_heads)
    dirs = range(2)
    chains = [(hh, d) for hh in heads for d in dirs]

    def nt(a, b):
        return lax.dot_general(a, b, _NT, preferred_element_type=F32)

    def cols(hh, width):
        return slice(hh * width, (hh + 1) * width)

    def block_pass(bi, carry):
        rows = pl.ds(pl.multiple_of(bi * blk, blk), blk)
        q = [q_ref[rows, cols(hh, dk)] * scale for hh in heads]
        k = [k_ref[rows, cols(hh, dk)] for hh in heads]
        lr = lr_ref[rows, :]
        g = [_log_sigmoid(_dot3(lr, wdec_ref[hh]) + bdec_ref[hh]) * (LOG2_E / GLA_GATE_NORM) for hh in heads]
        cum = {(hh, d): _dot_exact_lhs(tri_ref[d], g[hh][:, cols(d, dk)]) for hh, d in chains}
        refs = [_dot(sel_ref[...], jnp.concatenate([cum[hh, 0], cum[hh, 1]], axis=1).astype(BF16))
                for hh in heads]
        second = second_ref[...]
        first = 1.0 - second
        p_split, p_half = [], {}
        for hh in heads:
            q_in = (q[hh] * second, q[hh] * first)
            k_in = (k[hh] * first, k[hh] * second)
            q1, k1 = [], []
            for d in dirs:
                c = (hh, d)
                d_split = cum[c] - refs[hh][:blk, cols(d, dk)]
                d_half = cum[c] - refs[hh][blk:, cols(d, dk)]
                q1.append((q_in[d] * jnp.exp2(jnp.minimum(d_split, GLA_CLAMP_SPLIT))).astype(BF16))
                k1.append((k_in[d] * jnp.exp2(jnp.minimum(-d_split, GLA_CLAMP_SPLIT))).astype(BF16))
                q0 = (q[hh] * jnp.exp2(jnp.minimum(d_half, GLA_CLAMP_HALF))).astype(BF16)
                k0 = (k[hh] * jnp.exp2(jnp.minimum(-d_half, GLA_CLAMP_HALF))).astype(BF16)
                p_half[c] = nt(q0, k0)
                qi_ref[hh, d, rows, :] = (q[hh] * jnp.exp2(cum[c])).astype(BF16)
                cum_ref[hh, d, rows, :] = cum[c]
            p_split.append(nt(jnp.concatenate(q1, axis=1), jnp.concatenate(k1, axis=1)))
        for hh in heads:
            att = p_split[hh] * mchunk_ref[...] + diag2_ref[...] * jnp.sum(q[hh] * k[hh], axis=-1, keepdims=True)
            for d in dirs:
                att = att + jnp.where(mhalf_ref[d] > 0.5, p_half[hh, d], 0.0)
            o_ref[rows, cols(hh, dv)] = _dot(att.astype(BF16), v_ref[rows, cols(hh, dv)].astype(BF16))
        return carry

    lax.fori_loop(0, seq // blk, block_pass, 0)

    for hh, d in chains:
        st_ref[hh, d] = s0_ref[d, hh] if has_s0 else jnp.zeros((dk, dv), F32)

    def chunk_step(i, carry):
        for hh, d in chains:
            c = i if d == 0 else n_chunks - 1 - i
            sl = pl.ds(pl.multiple_of(c * chunk, chunk), chunk)
            cum = cum_ref[hh, d, sl, :]
            blast = cum[chunk - 1:chunk] if d == 0 else cum[0:1]
            kh = (k_ref[sl, cols(hh, dk)] * jnp.exp2(blast - cum)).astype(BF16)
            vb = v_ref[sl, cols(hh, dv)].astype(BF16)
            st = st_ref[hh, d]
            o_ref[sl, cols(hh, dv)] += _dot(qi_ref[hh, d, sl, :], st.astype(BF16))
            decay = jnp.broadcast_to(jnp.exp2(blast), (dk, dk)).T
            st_ref[hh, d] = (jnp.concatenate([decay] * (dv // dk), axis=1) * st
                             + lax.dot_general(kh, vb, _TN, preferred_element_type=F32))
        return carry

    lax.fori_loop(0, n_chunks, chunk_step, 0, unroll=min(n_chunks, 8))

    def finish(jb, carry):
        sl = pl.ds(pl.multiple_of(jb * blk, blk), blk)
        for hh in heads:
            o = o_ref[sl, cols(hh, dv)]
            o = o * lax.rsqrt(jnp.mean(o * o, axis=-1, keepdims=True) + EPS) * og_ref[...]
            y_ref[sl, cols(hh, dv)] = (o * _silu(r_ref[sl, cols(hh, dv)])).astype(BF16)
        return carry

    lax.fori_loop(0, seq // blk, finish, 0)
    if want_state:
        own = sfin_ref if has_states_in else sfin_ref.at[layer]
        for hh, d in chains:
            own[d, hh] = st_ref[hh, d]
        if not has_states_in:
            for other in range(sfin_ref.shape[0]):
                if other != layer:
                    sfin_ref[other] = jnp.zeros(sfin_ref.shape[1:], F32)


def _gla_mixer(proj, lr, w_dec, b_dec, onorm_g, s0, j, n_seq, seq, want_state, states, name):
    t = proj.shape[0]
    h = GLA_HEADS
    key = w_dec.shape[-1]
    dk = key // h
    val = (proj.shape[1] - 2 * key) // 2
    dv = val // h
    has_s0 = s0 is not None
    tables = _gla_tables(min(seq, GLA_BLOCK), dk)
    fixed = lambda a: pl.BlockSpec(a.shape, lambda b, hp: (0,) * a.ndim)
    hps = h if seq <= GLA_BLOCK else GLA_HEADS_PER_STEP
    wk, wv = hps * dk, hps * dv
    in_specs = [
        pl.BlockSpec((seq, wk), lambda b, hp: (b, hp)),
        pl.BlockSpec((seq, wk), lambda b, hp: (b, key // wk + hp)),
        pl.BlockSpec((seq, wv), lambda b, hp: (b, (2 * key) // wv + hp)),
        pl.BlockSpec((seq, wv), lambda b, hp: (b, (2 * key + val) // wv + hp)),
        pl.BlockSpec((seq, 2 * GLA_RANK), lambda b, hp: (b, 0)),
        pl.BlockSpec((None, hps, 2 * GLA_RANK, 2 * dk), lambda b, hp: (j, hp, 0, 0)),
        pl.BlockSpec((None, hps, 1, 2 * dk), lambda b, hp: (j, hp, 0, 0)),
        pl.BlockSpec((None, 1, dv), lambda b, hp: (j, 0, 0)),
    ] + [fixed(a) for a in tables]
    n_gla = w_dec.shape[0]
    w5 = w_dec.reshape(n_gla, 2, GLA_RANK, h, dk)
    zero = jnp.zeros_like(w5[:, 0])
    w_blk = jnp.concatenate([jnp.concatenate([w5[:, 0], zero], axis=-1),
                             jnp.concatenate([zero, w5[:, 1]], axis=-1)], axis=1).transpose(0, 2, 1, 3)
    b_cat = b_dec.reshape(n_gla, 2, h, dk).transpose(0, 2, 1, 3).reshape(n_gla, h, 1, 2 * dk)
    args = [proj, proj, proj, proj, lr, w_blk, b_cat, onorm_g.reshape(n_gla, 1, dv), *tables]
    if has_s0:
        in_specs.append(pl.BlockSpec((None, None, 2, hps, dk, dv), lambda b, hp: (b, j, 0, hp, 0, 0)))
        args.append(s0)
    out_specs = [pl.BlockSpec((seq, wv), lambda b, hp: (b, hp))]
    out_shape = [jax.ShapeDtypeStruct((t, val), BF16)]
    aliases = {}
    if want_state:
        out_shape.append(jax.ShapeDtypeStruct((n_seq, n_gla, 2, h, dk, dv), F32))
        if states is None:
            out_specs.append(pl.BlockSpec((None, n_gla, 2, hps, dk, dv), lambda b, hp: (b, 0, 0, hp, 0, 0)))
        else:
            out_specs.append(pl.BlockSpec((None, None, 2, hps, dk, dv), lambda b, hp: (b, j, 0, hp, 0, 0)))
            in_specs.append(pl.BlockSpec(memory_space=pl.ANY))
            args.append(states)
            aliases = {len(args) - 1: 1}
    return pl.pallas_call(
        functools.partial(_gla_body, seq=seq, layer=j, has_s0=has_s0, want_state=want_state,
                          has_states_in=bool(aliases)),
        grid=(n_seq, h // hps),
        in_specs=in_specs,
        out_specs=out_specs,
        out_shape=out_shape,
        input_output_aliases=aliases,
        scratch_shapes=[pltpu.VMEM((seq, wv), F32), pltpu.VMEM((hps, 2, seq, dk), BF16),
                        pltpu.VMEM((hps, 2, seq, dk), F32), pltpu.VMEM((hps, 2, dk, dv), F32)],
        compiler_params=_params("parallel", "parallel"),
        name=name,
    )(*args)


def _dft_tables(n):
    f = np.arange(n, dtype=np.int64)
    ang = (np.outer(f, f) % n).astype(np.float64) * (2.0 * math.pi / n)
    return _trig_bf16(ang)


def _fnet_body(x_ref, z_ref, cc_ref, sc_ref, cl_ref, sl_ref, y_ref, *, seq, n_seq, gc):
    inv = 1.0 / math.sqrt(seq * gc)
    for g in range(x_ref.shape[-1] // gc):
        cols = slice(g * gc, (g + 1) * gc)
        xb = x_ref[:, cols].astype(BF16)
        a = _dot(xb, cc_ref[...]).astype(BF16)
        bm = _dot(xb, sc_ref[...]).astype(BF16)
        for i in range(n_seq):
            rows = slice(i * seq, (i + 1) * seq)
            f = _dot(cl_ref[...], a[rows]) - _dot(sl_ref[...], bm[rows])
            y_ref[rows, cols] = (f * inv * _silu(z_ref[rows, cols])).astype(BF16)


def _fnet_mixer(proj, n_seq_total, seq, seqs_per_step, groups_per_step, name):
    t = proj.shape[0]
    width = proj.shape[1] // 2
    gc = width // FN_GROUPS
    cc, sc = _dft_tables(gc)
    cl, sl = _dft_tables(seq)
    rows = seq * seqs_per_step
    wide = gc * groups_per_step
    n_wide = width // wide
    const = pl.Buffered(1)
    return pl.pallas_call(
        functools.partial(_fnet_body, seq=seq, n_seq=seqs_per_step, gc=gc),
        grid=(n_seq_total // seqs_per_step, n_wide),
        in_specs=[
            pl.BlockSpec((rows, wide), lambda i, g: (i, g)),
            pl.BlockSpec((rows, wide), lambda i, g: (i, n_wide + g)),
            pl.BlockSpec((gc, gc), lambda i, g: (0, 0), pipeline_mode=const),
            pl.BlockSpec((gc, gc), lambda i, g: (0, 0), pipeline_mode=const),
            pl.BlockSpec((seq, seq), lambda i, g: (0, 0), pipeline_mode=const),
            pl.BlockSpec((seq, seq), lambda i, g: (0, 0), pipeline_mode=const),
        ],
        out_specs=pl.BlockSpec((rows, wide), lambda i, g: (i, g)),
        out_shape=jax.ShapeDtypeStruct((t, width), BF16),
        compiler_params=_params("parallel", "parallel"),
        name=name,
    )(proj, proj, cc, sc, cl, sl)


def _signal_tables(n):
    o = 2 * np.arange(n, dtype=np.int64) + 1
    ang = (np.outer(o, o) % (8 * n)).astype(np.float64) * (math.pi / (4 * n))
    cos, sin = _trig_bf16(ang)
    return cos, -sin


def _filter_tables(n, split):
    f = np.arange(n, dtype=np.int64)
    if split:
        f = np.concatenate([f[:n // 2], f[:n // 2 - 1:-1]])
    f = 2 * f + 1
    o = np.arange(n, dtype=np.int64)
    ang = (np.outer(f, o) % (4 * n)).astype(np.float64) * (math.pi / (2 * n))
    return _trig_bf16(ang)


def _hyena_positions(seq, bands, pad_to):
    t = np.linspace(0.0, 1.0, seq)[:, None]
    w = 2.0 * math.pi * np.arange(seq)[:, None] / seq
    f = np.linspace(1e-4, bands - 1, bands)[None, :]
    zpos = np.concatenate([t, np.cos(f * w), -np.sin(f * w)], axis=-1)
    zpos = np.pad(zpos, ((0, 0), (0, pad_to - zpos.shape[1])))
    return jnp.asarray(zpos, F32), jnp.asarray(t, F32)


def _hyfilt_body(zpos_ref, w1_ref, b1_ref, w2_ref, b2_ref, w3_ref, b3_ref, fr_ref, w4f_ref,
                 w4b_ref, t_ref, del_ref, ch_ref, sh_ref, *rest, split):
    if split:
        c2_ref, s2_ref = rest[:2]
        out_refs, a3_ref = rest[2:-1], rest[-1]
    else:
        out_refs, a3_ref = rest[:-1], rest[-1]

    @pl.when(pl.program_id(0) == 0)
    def _():
        fr = fr_ref[...]
        a = jnp.sin(fr * (_dot3(zpos_ref[...], w1_ref[...]) + b1_ref[...]))
        a = jnp.sin(fr * (_dot3(a, w2_ref[...]) + b2_ref[...]))
        a3_ref[...] = _dot3_lhs(jnp.sin(fr * (_dot3(a, w3_ref[...]) + b3_ref[...])))

    a3 = a3_ref[...]
    dec = jnp.exp(-t_ref[...] * del_ref[...]) * (1.0 / t_ref.shape[0])
    fwd = _dot(a3, _dot3_rhs(w4f_ref[...])) * dec
    bwd = _dot(a3, _dot3_rhs(w4b_ref[...])) * dec
    first = lax.broadcasted_iota(jnp.int32, bwd.shape, 0) == 0
    bwd = jnp.where(first, 0.0, bwd)
    hre = _dot(ch_ref[...], (fwd + bwd).astype(BF16))
    him = _dot(sh_ref[...], (bwd - fwd).astype(BF16))
    if not split:
        out_refs[0][...] = hre
        out_refs[1][...] = him
        return
    half = hre.shape[0] // 2
    plus_r, plus_i = hre[:half] + hre[half:], him[:half] - him[half:]
    minus_r, minus_i = hre[:half] - hre[half:], him[:half] + him[half:]
    c2, s2 = c2_ref[...], s2_ref[...]
    for ref, val in zip(out_refs, (plus_r, plus_i,
                                   c2 * minus_r + s2 * minus_i, c2 * minus_i - s2 * minus_r,
                                   c2 * minus_r - s2 * minus_i, c2 * minus_i + s2 * minus_r)):
        ref[...] = val


def _hyena_filters(seq, j, emb, w1p, b1, w2, b2, w3, b3, w4, freq, ce, split, name):
    ffn = w2.shape[-1]
    width = w4.shape[-1] // (2 * HY_ORDER)
    emb_pad = w1p.shape[1]
    zpos, tcol = _hyena_positions(seq, (emb - 1) // 2, emb_pad)
    deltas = jnp.asarray(np.abs(np.linspace(HY_MIN_DECAY, HY_MAX_DECAY, width, dtype=np.float32))[None, :])
    ch, sh = _filter_tables(seq, split)
    nb = width // ce
    const = pl.Buffered(1)
    vec = lambda: pl.BlockSpec((None, 1, ffn), lambda c: (j, 0, 0))
    out_rows, n_out = (seq // 2, 6) if split else (seq, 2)
    twiddles, twiddle_specs = [], []
    if split:
        ang = np.broadcast_to((math.pi * (2 * np.arange(out_rows) + 1) / (2 * seq))[:, None], (out_rows, ce))
        twiddles = [jnp.asarray(np.cos(ang), F32), jnp.asarray(np.sin(ang), F32)]
        twiddle_specs = [pl.BlockSpec((out_rows, ce), lambda c: (0, 0), pipeline_mode=const)] * 2
    return pl.pallas_call(
        functools.partial(_hyfilt_body, split=split),
        grid=(HY_ORDER * nb,),
        in_specs=[
            pl.BlockSpec((seq, emb_pad), lambda c: (0, 0)),
            pl.BlockSpec((None, emb_pad, ffn), lambda c: (j, 0, 0)),
            vec(),
            pl.BlockSpec((None, ffn, ffn), lambda c: (j, 0, 0)),
            vec(),
            pl.BlockSpec((None, ffn, ffn), lambda c: (j, 0, 0)),
            vec(),
            vec(),
            pl.BlockSpec((None, ffn, ce), lambda c: (j, 0, (c // nb) * 2 * nb + c % nb)),
            pl.BlockSpec((None, ffn, ce), lambda c: (j, 0, (c // nb) * 2 * nb + nb + c % nb)),
            pl.BlockSpec((seq, 1), lambda c: (0, 0)),
            pl.BlockSpec((1, ce), lambda c: (0, c % nb)),
            pl.BlockSpec((seq, seq), lambda c: (0, 0), pipeline_mode=const),
            pl.BlockSpec((seq, seq), lambda c: (0, 0), pipeline_mode=const),
        ] + twiddle_specs,
        out_specs=[pl.BlockSpec((out_rows, ce), lambda c: (0, c))] * n_out,
        out_shape=[jax.ShapeDtypeStruct((out_rows, HY_ORDER * width), F32)] * n_out,
        scratch_shapes=[pltpu.VMEM((seq, 3 * ffn), BF16)],
        compiler_params=_params("arbitrary"),
        name=name,
    )(zpos, w1p, b1, w2, b2, w3, b3, freq, w4, w4, tcol, deltas, ch, sh, *twiddles)


SUBLANES = 8


def _shift_down(u, first):
    rolled = pltpu.roll(u, 1, 0)
    row = lax.broadcasted_iota(jnp.int32, (SUBLANES, 1), 0)
    return jnp.concatenate([jnp.where(row == 0, first, rolled[:SUBLANES]), rolled[SUBLANES:]], axis=0)


def _shift_up(u, last):
    n = u.shape[0]
    rolled = pltpu.roll(u, n - 1, 0)
    row = lax.broadcasted_iota(jnp.int32, (SUBLANES, 1), 0)
    tail = jnp.where(row == SUBLANES - 1, last, rolled[n - SUBLANES:])
    return jnp.concatenate([rolled[:n - SUBLANES], tail], axis=0)


def _hyena_body(x1_ref, x2_ref, v_ref, z_ref, cw1_ref, cw2_ref, cwv_ref, cb1_ref, cb2_ref, cbv_ref,
                hre0_ref, him0_ref, hre1_ref, him1_ref, d_ref, cq_ref, sq_ref, y_ref,
                cur_ref, cb_ref, yr_ref, yi_ref, *, seq, rows, conv_done):
    n_rb = seq // rows

    def block(rb):
        return pl.ds(pl.multiple_of(rb * rows, rows), rows)

    def short_conv(u_ref, w_ref, b_ref, rb):
        u = u_ref[block(rb), :]
        if conv_done:
            return u
        before = u_ref[pl.ds(jnp.maximum(rb * rows - 1, 0), 1), :]
        after = u_ref[pl.ds(jnp.minimum((rb + 1) * rows, seq - 1), 1), :]
        before = jnp.where(rb == 0, 0.0, before)
        after = jnp.where(rb == n_rb - 1, 0.0, after)
        prev, nxt = _shift_down(u, before), _shift_up(u, after)
        return prev * w_ref[0:1] + u * w_ref[1:2] + nxt * w_ref[2:3] + b_ref[...]

    def long_conv_gate(x_ref, cw_ref, cbias_ref, hre_ref, him_ref, dn):
        def fwd(rb, carry):
            sl = block(rb)
            ur = _dot(cq_ref[sl, :], cb_ref[...])
            ui = _dot(sq_ref[sl, :], cb_ref[...])
            hr = hre_ref[sl, :]
            hi = him_ref[sl, :]
            yr_ref[sl, :] = (ur * hr - ui * hi).astype(BF16)
            yi_ref[sl, :] = (ur * hi + ui * hr).astype(BF16)
            return carry

        lax.fori_loop(0, n_rb, fwd, 0)

        def inv(rb, carry):
            sl = block(rb)
            c = _dot(cq_ref[sl, :], yr_ref[...]) + _dot(sq_ref[sl, :], yi_ref[...])
            new = short_conv(x_ref, cw_ref, cbias_ref, rb) * (c + cur_ref[sl, :] * dn)
            cur_ref[sl, :] = new
            cb_ref[sl, :] = new.astype(BF16)
            return carry

        lax.fori_loop(0, n_rb, inv, 0)

    def load_v(rb, carry):
        v = short_conv(v_ref, cwv_ref, cbv_ref, rb)
        cur_ref[block(rb), :] = v
        cb_ref[block(rb), :] = v.astype(BF16)
        return carry

    lax.fori_loop(0, n_rb, load_v, 0)
    long_conv_gate(x1_ref, cw1_ref, cb1_ref, hre0_ref, him0_ref, d_ref[0:1])
    long_conv_gate(x2_ref, cw2_ref, cb2_ref, hre1_ref, him1_ref, d_ref[1:2])

    def store(rb, carry):
        sl = block(rb)
        y_ref[sl, :] = (cur_ref[sl, :] * _silu(z_ref[sl, :])).astype(BF16)
        return carry

    lax.fori_loop(0, n_rb, store, 0)


def _hyena_split_body(x1_refs, x2_refs, v_refs, z_refs, cw1_ref, cw2_ref, cwv_ref, cb1_ref, cb2_ref, cbv_ref,
                      k_refs, d_ref, cq_ref, sq_ref, y_refs, cur_ref, sig_ref, z_buf, *, seq, rows):
    half = seq // 2
    n_rb = half // rows

    def block(rb, shift=0):
        return slice(rb * rows + shift, (rb + 1) * rows + shift)

    def short_conv(u_refs, w_ref, b_ref, rb):
        e_ref, o_ref = u_refs
        even, odd = e_ref[block(rb), :], o_ref[block(rb), :]
        odd_before = o_ref[block(rb, -1), :] if rb > 0 else _shift_down(odd, 0.0)
        even_after = e_ref[block(rb, 1), :] if rb < n_rb - 1 else _shift_up(even, 0.0)
        w0, w1, w2, b = w_ref[0:1], w_ref[1:2], w_ref[2:3], b_ref[...]
        return (odd_before * w0 + even * w1 + odd * w2 + b, even * w0 + odd * w1 + even_after * w2 + b)

    def long_conv_gate(x_ref, cw_ref, cbias_ref, k, dn):
        k1r, k1i, k2r, k2i, k3r, k3i = k
        for fb in range(n_rb):
            sl = block(fb)
            c, s = cq_ref[sl, :], sq_ref[sl, :]
            er, ei = _dot(c, sig_ref[0]), _dot(s, sig_ref[0])
            orr, oi = _dot(c, sig_ref[1]), _dot(s, sig_ref[1])
            a1r, a1i, a2r, a2i, a3r, a3i = (r[sl, :] for r in (k1r, k1i, k2r, k2i, k3r, k3i))
            z_buf[0, sl, :] = (er * a1r - ei * a1i + orr * a2r - oi * a2i).astype(BF16)
            z_buf[1, sl, :] = (er * a1i + ei * a1r + orr * a2i + oi * a2r).astype(BF16)
            z_buf[2, sl, :] = (er * a3r - ei * a3i + orr * a1r - oi * a1i).astype(BF16)
            z_buf[3, sl, :] = (er * a3i + ei * a3r + orr * a1i + oi * a1r).astype(BF16)
        for tb in range(n_rb):
            sl = block(tb)
            c, s = cq_ref[sl, :], sq_ref[sl, :]
            conv = (_dot(c, z_buf[0]) + _dot(s, z_buf[1]), _dot(c, z_buf[2]) + _dot(s, z_buf[3]))
            gate = short_conv(x_ref, cw_ref, cbias_ref, tb)
            for p in range(2):
                new = gate[p] * (conv[p] + cur_ref[p, sl, :] * dn)
                cur_ref[p, sl, :] = new
                sig_ref[p, sl, :] = new.astype(BF16)

    for rb in range(n_rb):
        for p, val in enumerate(short_conv(v_refs, cwv_ref, cbv_ref, rb)):
            cur_ref[p, block(rb), :] = val
            sig_ref[p, block(rb), :] = val.astype(BF16)
    long_conv_gate(x1_refs, cw1_ref, cb1_ref, k_refs[0], d_ref[0:1])
    long_conv_gate(x2_refs, cw2_ref, cb2_ref, k_refs[1], d_ref[1:2])
    for rb in range(n_rb):
        for p in range(2):
            sl = block(rb)
            y_refs[p][sl, :] = (cur_ref[p, sl, :] * _silu(z_refs[p][sl, :])).astype(BF16)


def _hyena_split_entry(*refs, seq, rows):
    it = iter(refs)
    take = lambda n: tuple(next(it) for _ in range(n))
    x1, x2, v, z = take(2), take(2), take(2), take(2)
    conv_refs = take(6)
    k = tuple(take(6) for _ in range(HY_ORDER))
    d, cq, sq = take(3)
    y = take(2)
    _hyena_split_body(x1, x2, v, z, *conv_refs, k, d, cq, sq, y, *it, seq=seq, rows=rows)


def _hyena_mixer(proj, conv_w, conv_b, filt, d_skip, j, n_seq, seq, ce, split, conv_done, name):
    phases = 2 if split else 1
    t = proj.shape[0] * phases
    width = proj.shape[1] // (4 * phases)
    nb = width // ce
    n_dft = seq // phases
    cq, sq = _signal_tables(n_dft)
    rows = min(n_dft, HY_ROW_BLOCK)
    const = pl.Buffered(1)
    seg = lambda s: pl.BlockSpec((seq, ce), lambda e, b: (b, s * nb + e))
    cw = lambda s: pl.BlockSpec((None, conv_w.shape[1], ce), lambda e, b: (j, 0, s * nb + e))
    cb = lambda s: pl.BlockSpec((None, 1, ce), lambda e, b: (j, 0, s * nb + e))
    hs = lambda n: pl.BlockSpec((n_dft, ce), lambda e, b: (0, n * nb + e),
                                pipeline_mode=const if nb == 1 else pl.Buffered(2))
    conv_b3 = conv_b.reshape(conv_b.shape[0], 1, -1)
    if split:
        body = functools.partial(_hyena_split_entry, seq=seq, rows=rows)
        phase = lambda s, p: pl.BlockSpec((n_dft, ce), lambda e, b: (b, p * 4 * nb + s * nb + e))
        x_specs = [phase(s, p) for s in range(4) for p in range(2)]
        out_specs = [pl.BlockSpec((n_dft, ce), lambda e, b: (b, e))] * 2
        out_shape = [jax.ShapeDtypeStruct((t // 2, width), BF16)] * 2
        scratch = [pltpu.VMEM((2, n_dft, ce), F32), pltpu.VMEM((2, n_dft, ce), BF16),
                   pltpu.VMEM((4, n_dft, ce), BF16)]
    else:
        body = functools.partial(_hyena_body, seq=seq, rows=rows, conv_done=conv_done)
        x_specs = [seg(0), seg(1), seg(2), seg(3)]
        out_specs = pl.BlockSpec((seq, ce), lambda e, b: (b, e))
        out_shape = jax.ShapeDtypeStruct((t, width), BF16)
        scratch = [pltpu.VMEM((seq, ce), F32), pltpu.VMEM((seq, ce), BF16),
                   pltpu.VMEM((seq, ce), BF16), pltpu.VMEM((seq, ce), BF16)]
    return pl.pallas_call(
        body,
        grid=(nb, n_seq),
        in_specs=x_specs + [cw(0), cw(1), cw(2), cb(0), cb(1), cb(2)]
        + [hs(n) for n in range(HY_ORDER) for _ in filt]
        + [pl.BlockSpec((None, HY_ORDER, ce), lambda e, b: (j, 0, e)),
           pl.BlockSpec((n_dft, n_dft), lambda e, b: (0, 0), pipeline_mode=const),
           pl.BlockSpec((n_dft, n_dft), lambda e, b: (0, 0), pipeline_mode=const)],
        out_specs=out_specs,
        out_shape=out_shape,
        scratch_shapes=scratch,
        compiler_params=_params("parallel", "parallel"),
        name=name,
    )(*([proj] * len(x_specs)), conv_w, conv_w, conv_w, conv_b3, conv_b3, conv_b3,
      *(list(filt) * HY_ORDER), d_skip, cq, sq)


def kernel(x_prompt, x_sample, state_gla, c, c_ctx, mod_w, mod_b, norm_g, final_norm_g, gla_w_in, gla_w_dec, gla_b_dec, gla_onorm_g, gla_w_out, fn_w_in, fn_w_out, hy_w_in, hy_conv_w, hy_conv_b, hy_ffn_w1, hy_ffn_b1, hy_ffn_w2, hy_ffn_b2, hy_ffn_w3, hy_ffn_b3, hy_ffn_w4, hy_freq, hy_d, hy_w_out):
    n_p, l_p, d = x_prompt.shape
    n_s, l_s, _ = x_sample.shape
    depth = mod_w.shape[0]
    key = gla_w_dec.shape[-1]
    n_main = gla_w_in.shape[-1] - 2 * GLA_RANK

    cvec = jnp.concatenate([c_ctx[None], c, jnp.zeros((MOD_ROWS - 1 - n_s, d), F32)], axis=0)
    mod4 = _modulation(cvec, mod_w, mod_b).reshape(depth, MOD_ROWS, 1, 3 * d)
    norm_g3 = norm_g.reshape(depth, 1, d)

    gla_w_main = gla_w_in.astype(BF16)
    gla_w_lr = gla_w_in[:, :, n_main:].astype(BF16)
    gla_w_out_b = gla_w_out.astype(BF16)
    fn_w_in_b = fn_w_in.astype(BF16)
    fn_w_out_b = fn_w_out.astype(BF16)
    hy_w_in_b = hy_w_in.astype(BF16)
    hy_w_out_b = hy_w_out.astype(BF16)
    emb = hy_ffn_w1.shape[1]
    emb_pad = -(-emb // 128) * 128
    hy_w1p = jnp.pad(hy_ffn_w1, ((0, 0), (0, emb_pad - emb), (0, 0)))
    ffn = hy_ffn_w2.shape[-1]
    hy_vecs = [a.reshape(a.shape[0], 1, ffn) for a in (hy_ffn_b1, hy_ffn_b2, hy_ffn_b3, hy_freq)]

    tm = 512
    tiles_per_sample = l_s // tm
    streams = {
        "p": dict(x=x_prompt.reshape(n_p * l_p, d), n=n_p, l=l_p, row=lambda i: 0),
        "s": dict(x=x_sample.reshape(n_s * l_s, d), n=n_s, l=l_s, row=lambda i: 1 + i // tiles_per_sample),
    }
    def in_weights(i, seq):
        kind, j = i % N_MIXERS, i // N_MIXERS
        if kind == 0:
            return [(gla_w_main, (None, d, n_main), (j, 0, 0)), (gla_w_lr, (None, d, 2 * GLA_RANK), (j, 0, 0))], False
        if kind == 1:
            return [(fn_w_in_b, (None,) + fn_w_in.shape[1:], (j, 0, 0))], False
        return [(hy_w_in_b, (None,) + hy_w_in.shape[1:], (j, 0, 0))], seq >= HY_SPLIT_MIN_SEQ

    hy_conv_b3 = hy_conv_b.reshape(hy_conv_b.shape[0], 1, -1)

    def in_conv(i, seq):
        if i % N_MIXERS == 2 and seq < HY_SPLIT_MIN_SEQ and tm % seq == 0:
            return (hy_conv_w, hy_conv_b3, i // N_MIXERS, seq)
        return None

    for tag, st in streams.items():
        weights, split = in_weights(0, st["l"])
        st["proj"] = _proj(st["x"], mod4, st["row"], tm, f"proj_{tag}0", nxt=(norm_g3, 0, weights, split),
                           conv=in_conv(0, st["l"]))
    new_states = None
    for i in range(depth):
        kind, j = i % N_MIXERS, i // N_MIXERS
        if kind == 2:
            filt = {}
            for tag, st in streams.items():
                ce = min(d, 256 if st["l"] > 256 else d)
                filt[tag] = _hyena_filters(st["l"], j, emb, hy_w1p, hy_vecs[0], hy_ffn_w2, hy_vecs[1], hy_ffn_w3,
                                           hy_vecs[2], hy_ffn_w4, hy_vecs[3], ce, st["l"] >= HY_SPLIT_MIN_SEQ,
                                           f"hyfilt_{tag}{i}")
        for tag, st in streams.items():
            n_seq, seq = st["n"], st["l"]
            if kind == 0:
                proj, lr = st["proj"]
                is_ctx = tag == "p"
                res = _gla_mixer(proj, lr, gla_w_dec, gla_b_dec, gla_onorm_g,
                                 None if is_ctx else state_gla, j, n_seq, seq, is_ctx, new_states,
                                 f"gla_{tag}{i}")
                ys = [res[0]]
                if is_ctx:
                    new_states = res[1]
                w_out = gla_w_out_b
            elif kind == 1:
                short = seq <= 256
                ys = [_fnet_mixer(st["proj"][0], n_seq, seq, 4 if short else 1, FN_GROUPS if short else 2,
                                  f"fnet_{tag}{i}")]
                w_out = fn_w_out_b
            else:
                split = seq >= HY_SPLIT_MIN_SEQ
                ce = min(d, 256 if seq > 256 else d)
                y = _hyena_mixer(st["proj"][0], hy_conv_w, hy_conv_b, filt[tag], hy_d, j, n_seq, seq, ce, split,
                                 in_conv(i, seq) is not None, f"hyena_{tag}{i}")
                ys = list(y) if split else [y]
                w_out = hy_w_out_b
            if i + 1 < depth:
                weights, split = in_weights(i + 1, seq)
                st["x"], *st["proj"] = _proj(st["x"], mod4, st["row"], tm, f"proj_{tag}{i + 1}",
                                             res=(ys, w_out, j, i), nxt=(norm_g3, i + 1, weights, split),
                                             conv=in_conv(i + 1, seq))
            else:
                (st["x"],) = _proj(st["x"], mod4, st["row"], tm, f"proj_{tag}{i + 1}",
                                   res=(ys, w_out, j, i), final_g=final_norm_g)
    y_prompt = streams["p"]["x"].reshape(n_p, l_p, d)
    y_sample = streams["s"]["x"].reshape(n_s, l_s, d)
    new_state_gla = new_states.astype(x_prompt.dtype)
    return (y_prompt, y_sample, new_state_gla)
```
